```python
import jax
import jax.numpy as jnp
from jax import lax
import numpy as np

D_MODEL = 1024
BATCH = 16
SEQ = 2048
DEPTH = 2

GRID_W = 64
CTX_LEN = 256
N_MOD = 6
EPS = 1e-6
NEG_INF = -1e30

HEAD_DIM = 64
NA_WIDTH = D_MODEL // 2
NA_HEADS = NA_WIDTH // HEAD_DIM
LRU_WIDTH = D_MODEL // 4
LRU_HEADS = 4
LRU_BLOCK = LRU_WIDTH // LRU_HEADS
SC_WIDTH = D_MODEL - NA_WIDTH - LRU_WIDTH

COL_Q = 0
COL_K = COL_Q + NA_WIDTH
COL_V = COL_K + NA_WIDTH
COL_RX = COL_V + NA_WIDTH
COL_RG = COL_RX + LRU_WIDTH
COL_SB = COL_RG + LRU_WIDTH
COL_SC = COL_SB + SC_WIDTH
COL_SX = COL_SC + SC_WIDTH
IN_WIDTH = COL_SX + SC_WIDTH

LRU_CONV = 4
LRU_CONV_LEFT = 2
RG_C = 8.0
SC_CONV = 3
SC_CONV_LEFT = 1
WIN_R = 8
WIN_C = 16
Q_COLS = 16
K_COLS = 32
N_EXPERTS = 16
N_GROUPS = 4
E_PER_GROUP = N_EXPERTS // N_GROUPS
TOP_K = 2
D_EXPERT = 512

kernel_name = 'hybrid_na_rglru_shortconv_moe_dit'


def rms_norm(x, g):
    xf = x.astype(jnp.float32)
    y = xf * lax.rsqrt(jnp.mean(xf * xf, axis=-1, keepdims=True) + EPS)
    return (y * g.astype(jnp.float32)).astype(x.dtype)


def modulate(x, shift, scale):
    return x * (1.0 + scale) + shift


def dwconv(x, w, b, left):
    width, length = w.shape[0], x.shape[1]
    xp = jnp.pad(x, ((0, 0), (left, width - 1 - left), (0, 0)))
    y = xp[:, :length] * w[0] + b
    for k in range(1, width):
        y = y + xp[:, k:k + length] * w[k]
    return y


def _affine_combine(left, right):
    a_l, h_l = left
    a_r, h_r = right
    return a_l * a_r, a_r * h_l + h_r


def linear_scan(a, b, h0, reverse):
    if reverse:
        a, b = jnp.flip(a, 1), jnp.flip(b, 1)
    b = b.at[:, 0].add(a[:, 0] * h0)
    _, h = lax.associative_scan(_affine_combine, (a, b), axis=1)
    return jnp.flip(h, 1) if reverse else h


def rglru_coeffs(xc, w, bias, lam):
    bsz, length, _ = xc.shape
    xb = xc.reshape(bsz, length, LRU_HEADS, LRU_BLOCK)
    pre = jnp.einsum('blnc,gncd->gblnd', xb, w.astype(jnp.float32)).reshape(2, bsz, length, LRU_WIDTH)
    gates = jax.nn.sigmoid(pre + bias.astype(jnp.float32)[:, None, None, :])
    r, i = gates[0], gates[1]
    log_a = -RG_C * r * jax.nn.softplus(-lam.astype(jnp.float32))
    a = jnp.exp(log_a)
    b = jnp.sqrt(-jnp.expm1(2.0 * log_a)) * (i * xc)
    return a, b


def rglru_bidir(x_lat, x_ctx, w, bias, lam, need_ctx):
    y_lat = jnp.zeros_like(x_lat)
    y_ctx = jnp.zeros_like(x_ctx) if need_ctx else None
    for d, rev in enumerate((False, True)):
        a_c, b_c = rglru_coeffs(x_ctx, w[d], bias[d], lam[d])
        h_c = linear_scan(a_c, b_c, jnp.zeros_like(x_ctx[:, 0]), rev)
        a_l, b_l = rglru_coeffs(x_lat, w[d], bias[d], lam[d])
        y_lat = y_lat + linear_scan(a_l, b_l, h_c[:, 0] if rev else h_c[:, -1], rev)
        if need_ctx:
            y_ctx = y_ctx + h_c
    return y_lat, y_ctx


def neighbourhood_attention(q, k, v, kc, vc, rpb):
    bsz, seq, nh, dh = q.shape
    rows = seq // GRID_W
    wr = min(WIN_R, rows)
    ncb = GRID_W // Q_COLS
    scale = dh ** -0.5
    qg = q.reshape(bsz, rows, ncb, Q_COLS, nh, dh)
    kg = k.reshape(bsz, rows, GRID_W, nh, dh)
    vg = v.reshape(bsz, rows, GRID_W, nh, dh)
    q_col = jnp.arange(GRID_W).reshape(ncb, Q_COLS)
    c_start = jnp.clip(q_col - WIN_C // 2, 0, GRID_W - WIN_C)
    k_col = jnp.clip(jnp.arange(ncb) * Q_COLS - WIN_C // 2, 0, GRID_W - K_COLS)[:, None] + jnp.arange(K_COLS)
    col_ok = (k_col[:, None, :] >= c_start[..., None]) & (k_col[:, None, :] < c_start[..., None] + WIN_C)
    rel_c = jnp.clip(k_col[:, None, :] - q_col[..., None] + WIN_C - 1, 0, 2 * WIN_C - 2)
    bias_c = rpb[:, :, rel_c]
    n_loc = wr * K_COLS

    def row_block(r):
        r0 = jnp.clip(r - wr // 2, 0, rows - wr)
        kb = lax.dynamic_slice_in_dim(kg, r0, wr, axis=1)[:, :, k_col]
        vb = lax.dynamic_slice_in_dim(vg, r0, wr, axis=1)[:, :, k_col]
        qr = lax.dynamic_index_in_dim(qg, r, axis=1, keepdims=False)
        bias = jnp.take(bias_c, r0 + jnp.arange(wr) - r + WIN_R - 1, axis=1)
        s_loc = jnp.einsum('bnqhd,brnkhd->bhnqrk', qr, kb, preferred_element_type=jnp.float32) * scale
        s_loc = jnp.where(col_ok[:, :, None, :], s_loc + bias.transpose(0, 2, 3, 1, 4), NEG_INF)
        s_ctx = jnp.einsum('bnqhd,bchd->bhnqc', qr, kc, preferred_element_type=jnp.float32) * scale
        s = jnp.concatenate([s_loc.reshape(bsz, nh, ncb, Q_COLS, n_loc), s_ctx], axis=-1)
        p = jax.nn.softmax(s, axis=-1).astype(v.dtype)
        p_loc = p[..., :n_loc].reshape(bsz, nh, ncb, Q_COLS, wr, K_COLS)
        return (jnp.einsum('bhnqrk,brnkhd->bnqhd', p_loc, vb)
                + jnp.einsum('bhnqc,bchd->bnqhd', p[..., n_loc:], vc))

    out = lax.map(row_block, jnp.arange(rows))
    return jnp.moveaxis(out, 0, 1).reshape(bsz, seq, nh, dh)


def context_attention(qc, kc, vc):
    s = jnp.einsum('bqhd,bkhd->bhqk', qc, kc, preferred_element_type=jnp.float32) * (HEAD_DIM ** -0.5)
    p = jax.nn.softmax(s, axis=-1).astype(vc.dtype)
    return jnp.einsum('bhqk,bkhd->bqhd', p, vc)


def merge_groups(y_na, y_lru, y_sc, g):
    return jnp.concatenate([
        rms_norm(y_na, g[:NA_WIDTH]),
        rms_norm(y_lru, g[NA_WIDTH:NA_WIDTH + LRU_WIDTH]),
        rms_norm(y_sc, g[NA_WIDTH + LRU_WIDTH:])], axis=-1)


def hybrid_mixer(h, hc, w_in, lru_conv_w, lru_conv_b, rg_w, rg_b, rg_lam, rpb,
                 sc_conv_w, sc_conv_b, out_g, w_out, need_ctx):
    bsz, seq, _ = h.shape
    n_ctx = hc.shape[1]
    u = h @ w_in
    c_lo, c_hi = (0, IN_WIDTH) if need_ctx else (COL_K, COL_RG)
    uc = hc @ w_in[:, c_lo:c_hi]

    def lat(lo, hi):
        return u[..., lo:hi]

    def cx(lo, hi):
        return uc[..., lo - c_lo:hi - c_lo]

    def heads(t):
        return t.reshape(t.shape[0], t.shape[1], NA_HEADS, HEAD_DIM)

    kc, vc = heads(cx(COL_K, COL_V)), heads(cx(COL_V, COL_RX))
    y_na = neighbourhood_attention(heads(lat(COL_Q, COL_K)), heads(lat(COL_K, COL_V)),
                                   heads(lat(COL_V, COL_RX)), kc, vc, rpb).reshape(bsz, seq, NA_WIDTH)
    rx = dwconv(lat(COL_RX, COL_RG), lru_conv_w, lru_conv_b, LRU_CONV_LEFT).astype(jnp.float32)
    rxc = dwconv(cx(COL_RX, COL_RG), lru_conv_w, lru_conv_b, LRU_CONV_LEFT).astype(jnp.float32)
    h_lru, h_lru_c = rglru_bidir(rx, rxc, rg_w, rg_b, rg_lam, need_ctx)
    y_lru = h_lru.astype(h.dtype) * jax.nn.gelu(lat(COL_RG, COL_SB))
    y_sc = lat(COL_SB, COL_SC) * dwconv(lat(COL_SC, COL_SX) * lat(COL_SX, IN_WIDTH), sc_conv_w, sc_conv_b, SC_CONV_LEFT)
    y = merge_groups(y_na, y_lru, y_sc, out_g) @ w_out
    if not need_ctx:
        return y, None
    y_na_c = context_attention(heads(cx(COL_Q, COL_K)), kc, vc).reshape(bsz, n_ctx, NA_WIDTH)
    y_lru_c = h_lru_c.astype(hc.dtype) * jax.nn.gelu(cx(COL_RG, COL_SB))
    y_sc_c = cx(COL_SB, COL_SC) * dwconv(cx(COL_SC, COL_SX) * cx(COL_SX, IN_WIDTH), sc_conv_w, sc_conv_b, SC_CONV_LEFT)
    yc = merge_groups(y_na_c, y_lru_c, y_sc_c, out_g) @ w_out
    return y, yc


def moe_ffn(h, w_router, b_router, w_gate, w_up, w_down):
    logits = jnp.einsum('btd,de->bte', h, w_router, preferred_element_type=jnp.float32) + b_router.astype(jnp.float32)
    probs = jax.nn.softmax(logits, axis=-1)
    pg = probs.reshape(*probs.shape[:-1], N_GROUPS, E_PER_GROUP)
    group_score = jnp.sum(lax.top_k(pg, TOP_K)[0], axis=-1)
    g_sel = jnp.argmax(group_score, axis=-1)
    p_in = jnp.take_along_axis(pg, g_sel[..., None, None], axis=-2)[..., 0, :]
    w_top, e_loc = lax.top_k(p_in, TOP_K)
    w_top = w_top / jnp.sum(w_top, axis=-1, keepdims=True)
    e_idx = g_sel[..., None] * E_PER_GROUP + e_loc
    combine = jnp.sum(jax.nn.one_hot(e_idx, N_EXPERTS, dtype=jnp.float32) * w_top[..., None], axis=-2).astype(h.dtype)
    out = jnp.zeros_like(h)
    for e in range(N_EXPERTS):
        hid = jax.nn.silu(h @ w_gate[e]) * (h @ w_up[e])
        out = out + combine[..., e:e + 1] * (hid @ w_down[e])
    return out


def setup_inputs(seed: int = 0) -> dict:
    key = jax.random.key(seed)
    ks = jax.random.split(key, 25)
    L = DEPTH

    def nrm(k, shape, fan_in, gain=1.0):
        return jax.random.normal(k, shape, jnp.float32) * (gain * fan_in ** -0.5)

    def small(k, shape):
        return 0.01 * jax.random.normal(k, shape, jnp.float32)

    def gain(k, shape):
        return 1.0 + 0.02 * jax.random.normal(k, shape, jnp.float32)

    a_pow = jax.random.uniform(ks[12], (L, 2, LRU_WIDTH), jnp.float32, 0.9, 0.999)
    a_base = a_pow ** (1.0 / RG_C)
    return {
        'x': jax.random.normal(ks[0], (BATCH, SEQ, D_MODEL), jnp.float32),
        'c': jax.random.normal(ks[1], (BATCH, D_MODEL), jnp.float32),
        'ctx': jax.random.normal(ks[2], (BATCH, CTX_LEN, D_MODEL), jnp.float32),
        'c_ctx': jax.random.normal(ks[3], (D_MODEL,), jnp.float32),
        'w_ada': nrm(ks[4], (L, D_MODEL, N_MOD * D_MODEL), D_MODEL, 0.5),
        'b_ada': small(ks[5], (L, N_MOD * D_MODEL)),
        'norm_mix_g': gain(ks[6], (L, D_MODEL)),
        'w_in': nrm(ks[7], (L, D_MODEL, IN_WIDTH), D_MODEL),
        'lru_conv_w': nrm(ks[8], (L, LRU_CONV, LRU_WIDTH), LRU_CONV),
        'lru_conv_b': small(ks[9], (L, LRU_WIDTH)),
        'rg_w': nrm(ks[10], (L, 2, 2, LRU_HEADS, LRU_BLOCK, LRU_BLOCK), LRU_BLOCK),
        'rg_b': small(ks[11], (L, 2, 2, LRU_WIDTH)),
        'rg_lam': jnp.log(a_base) - jnp.log1p(-a_base),
        'na_rpb': 0.02 * jax.random.normal(ks[13], (L, NA_HEADS, 2 * WIN_R - 1, 2 * WIN_C - 1), jnp.float32),
        'sc_conv_w': nrm(ks[14], (L, SC_CONV, SC_WIDTH), SC_CONV),
        'sc_conv_b': small(ks[15], (L, SC_WIDTH)),
        'mix_out_g': gain(ks[16], (L, D_MODEL)),
        'w_out': nrm(ks[17], (L, D_MODEL, D_MODEL), D_MODEL),
        'norm_ffn_g': gain(ks[18], (L, D_MODEL)),
        'w_router': nrm(ks[19], (D_MODEL, N_EXPERTS), D_MODEL),
        'b_router': small(ks[20], (N_EXPERTS,)),
        'w_gate': nrm(ks[21], (L, N_EXPERTS, D_MODEL, D_EXPERT), D_MODEL),
        'w_up': nrm(ks[22], (L, N_EXPERTS, D_MODEL, D_EXPERT), D_MODEL),
        'w_down': nrm(ks[23], (L, N_EXPERTS, D_EXPERT, D_MODEL), D_EXPERT),
        'final_g': gain(ks[24], (D_MODEL,)),
    }


def reference(x, c, ctx, c_ctx, w_ada, b_ada, norm_mix_g, w_in, lru_conv_w, lru_conv_b,
              rg_w, rg_b, rg_lam, na_rpb, sc_conv_w, sc_conv_b, mix_out_g, w_out,
              norm_ffn_g, w_router, b_router, w_gate, w_up, w_down, final_g):
    bsz = x.shape[0]
    n_ctx = ctx.shape[1]
    cond = jax.nn.silu(c)
    cond_ctx = jax.nn.silu(c_ctx)
    xc = ctx
    for l in range(DEPTH):
        need_ctx = l < DEPTH - 1
        mod = (cond @ w_ada[l] + b_ada[l]).reshape(bsz, N_MOD, 1, D_MODEL)
        mod_c = (cond_ctx @ w_ada[l] + b_ada[l]).reshape(N_MOD, 1, D_MODEL)
        h = modulate(rms_norm(x, norm_mix_g[l]), mod[:, 0], mod[:, 1])
        hc = modulate(rms_norm(xc, norm_mix_g[l]), mod_c[0], mod_c[1])
        y, yc = hybrid_mixer(h, hc, w_in[l], lru_conv_w[l], lru_conv_b[l], rg_w[l], rg_b[l], rg_lam[l],
                             na_rpb[l], sc_conv_w[l], sc_conv_b[l], mix_out_g[l], w_out[l], need_ctx)
        x = x + mod[:, 2] * y
        h = modulate(rms_norm(x, norm_ffn_g[l]), mod[:, 3], mod[:, 4])
        if need_ctx:
            xc = xc + mod_c[2] * yc
            hc = modulate(rms_norm(xc, norm_ffn_g[l]), mod_c[3], mod_c[4])
            f = moe_ffn(jnp.concatenate([hc, h], axis=1), w_router, b_router, w_gate[l], w_up[l], w_down[l])
            xc = xc + mod_c[5] * f[:, :n_ctx]
            x = x + mod[:, 5] * f[:, n_ctx:]
        else:
            x = x + mod[:, 5] * moe_ffn(h, w_router, b_router, w_gate[l], w_up[l], w_down[l])
    return rms_norm(x, final_g)
```

```python
import functools

import jax
import jax.numpy as jnp
from jax import lax
from jax.experimental import pallas as pl
from jax.experimental.pallas import tpu as pltpu

D_MODEL = 1024
SEQ = 2048
CTX_LEN = 256
ROWS_PER_B = CTX_LEN + SEQ
DEPTH = 2
N_MOD = 6
EPS = 1e-6
NEG_INF = -1e30

GRID_W = 64
GRID_ROWS = SEQ // GRID_W
HEAD_DIM = 64
NA_WIDTH = 512
NA_HEADS = 8
LRU_WIDTH = 256
LRU_HEADS = 4
LRU_BLOCK = 64
SC_WIDTH = 256
QKV_WIDTH = 3 * NA_WIDTH
REST_WIDTH = 2 * LRU_WIDTH + 3 * SC_WIDTH
IN_WIDTH = QKV_WIDTH + REST_WIDTH
RG_C = 8.0
WIN_R = 8
WIN_C = 16
N_EXPERTS = 16
N_GROUPS = 4
E_PER_GROUP = 4
N_PAIRS = 6
N_BUCKETS = N_GROUPS * N_PAIRS
D_EXPERT = 512

TM = 256
TILES_PER_B = ROWS_PER_B // TM
LAT_TILES_PER_B = SEQ // TM
LANES = 128
BUCKET_ROWS = 32
ROW_WIDTH = D_MODEL + LANES
HEADS_PER_STACK = 4
STACK_W = HEADS_PER_STACK * HEAD_DIM
SCAN_SEGS = 8
VMEM_LIMIT = 56 * 1024 * 1024

_HI = lax.Precision.HIGHEST
_NT = (((1,), (1,)), ((), ()))


def _cparams(sem):
    return pltpu.CompilerParams(dimension_semantics=sem, vmem_limit_bytes=VMEM_LIMIT)


def _rms(v, g):
    return v * lax.rsqrt(jnp.mean(v * v, axis=-1, keepdims=True) + EPS) * g


def _ada_kernel(c_ref, w_ref, b_ref, o_ref):
    cond = c_ref[...]
    cond = cond * jax.nn.sigmoid(cond)
    o_ref[0] = jnp.dot(cond, w_ref[0], precision=_HI, preferred_element_type=jnp.float32) + b_ref[0]


def _ada(c_all, w_ada, b_ada):
    depth, _, width = w_ada.shape
    rows = c_all.shape[0]
    tn = 1536
    return pl.pallas_call(
        _ada_kernel,
        out_shape=jax.ShapeDtypeStruct((depth, rows, width), jnp.float32),
        grid=(depth, width // tn),
        in_specs=[
            pl.BlockSpec((rows, D_MODEL), lambda l, n: (0, 0)),
            pl.BlockSpec((1, D_MODEL, tn), lambda l, n: (l, 0, n)),
            pl.BlockSpec((1, 1, tn), lambda l, n: (l, 0, n)),
        ],
        out_specs=pl.BlockSpec((1, rows, tn), lambda l, n: (l, 0, n)),
        compiler_params=_cparams(("arbitrary", "arbitrary")),
        name="ada",
    )(c_all, w_ada, b_ada.reshape(depth, 1, width))


def _tile_maps(bsz, with_ctx):
    per_b = TILES_PER_B if with_ctx else LAT_TILES_PER_B
    off = 0 if with_ctx else 1

    def bj(i):
        return i // per_b, i % per_b + off

    def mod_row(i):
        b, j = bj(i)
        return jnp.where(j == 0, bsz, b)

    return per_b * bsz, bj, mod_row


def _in_kernel(*refs, two_src):
    if two_src:
        ctx_ref, x_ref, g_ref, mod_ref, w_ref, qkv_ref, rest_ref = refs
        j = pl.program_id(0) % TILES_PER_B
        x = jnp.where(j == 0, ctx_ref[...], x_ref[...])
    else:
        x_ref, g_ref, mod_ref, w_ref, qkv_ref, rest_ref = refs
        x = x_ref[...]
    h = _rms(x, g_ref[...]) * (1.0 + mod_ref[1:2, :]) + mod_ref[0:1, :]
    hb = h.astype(jnp.bfloat16)
    qkv_ref[...] = jnp.dot(hb, w_ref[:, :QKV_WIDTH], preferred_element_type=jnp.float32).astype(jnp.bfloat16)
    rest_ref[...] = jnp.dot(hb, w_ref[:, QKV_WIDTH:], preferred_element_type=jnp.float32)


def _in_proj(srcs, g, mod, w_bf16, bsz):
    two_src = len(srcs) == 2
    n_tiles, bj, mod_row = _tile_maps(bsz, True)
    if two_src:
        src_specs = [
            pl.BlockSpec((None, TM, D_MODEL), lambda i: (bj(i)[0], 0, 0)),
            pl.BlockSpec((None, TM, D_MODEL), lambda i: (bj(i)[0], jnp.maximum(bj(i)[1] - 1, 0), 0)),
        ]
    else:
        src_specs = [pl.BlockSpec((None, TM, D_MODEL), lambda i: (bj(i)[0], bj(i)[1], 0))]
    return pl.pallas_call(
        functools.partial(_in_kernel, two_src=two_src),
        out_shape=[
            jax.ShapeDtypeStruct((bsz, ROWS_PER_B, QKV_WIDTH), jnp.bfloat16),
            jax.ShapeDtypeStruct((bsz, ROWS_PER_B, REST_WIDTH), jnp.float32),
        ],
        grid=(n_tiles,),
        in_specs=src_specs + [
            pl.BlockSpec((1, D_MODEL), lambda i: (0, 0)),
            pl.BlockSpec((None, N_MOD, D_MODEL), lambda i: (mod_row(i), 0, 0)),
            pl.BlockSpec((D_MODEL, IN_WIDTH), lambda i: (0, 0)),
        ],
        out_specs=[
            pl.BlockSpec((None, TM, QKV_WIDTH), lambda i: (bj(i)[0], bj(i)[1], 0)),
            pl.BlockSpec((None, TM, REST_WIDTH), lambda i: (bj(i)[0], bj(i)[1], 0)),
        ],
        compiler_params=_cparams(("arbitrary",)),
        name="in_proj",
    )(*srcs, g.reshape(1, D_MODEL), mod, w_bf16)


def _attn_kernel(q_ref, k_ref, v_ref, bias_ref, o_ref, *, need_ctx):
    lane_head = lax.broadcasted_iota(jnp.int32, (1, STACK_W), 1) // HEAD_DIM
    n_stacks = NA_WIDTH // STACK_W

    def stack_q(qg):
        zero = jnp.zeros_like(qg)
        return jnp.concatenate([jnp.where(lane_head == h, qg, zero) for h in range(HEADS_PER_STACK)], axis=0)

    def unstack(o):
        out = jnp.zeros((GRID_W, STACK_W), jnp.float32)
        for h in range(HEADS_PER_STACK):
            out = out + jnp.where(lane_head == h, o[h * GRID_W:(h + 1) * GRID_W], 0.0)
        return out

    def attend(q_rows, s, local):
        cols = slice(s * STACK_W, (s + 1) * STACK_W)
        qg = q_ref[pl.ds(q_rows, GRID_W), cols] * jnp.bfloat16(HEAD_DIM ** -0.5)
        qs = stack_q(qg)
        kc = k_ref[0:CTX_LEN, cols]
        vc = v_ref[0:CTX_LEN, cols]
        s_ctx = lax.dot_general(qs, kc, _NT, preferred_element_type=jnp.float32)
        m = jnp.max(s_ctx, axis=-1, keepdims=True)
        if local is not None:
            k_rows, delta = local
            kw = k_ref[pl.ds(k_rows, WIN_R * GRID_W), cols]
            vw = v_ref[pl.ds(k_rows, WIN_R * GRID_W), cols]
            bias = bias_ref[delta, s * HEADS_PER_STACK:(s + 1) * HEADS_PER_STACK]
            s_loc = lax.dot_general(qs, kw, _NT, preferred_element_type=jnp.float32)
            s_loc = s_loc + bias.reshape(HEADS_PER_STACK * GRID_W, WIN_R * GRID_W)
            m = jnp.maximum(m, jnp.max(s_loc, axis=-1, keepdims=True))
            p_loc = jnp.exp(s_loc - m)
        p_ctx = jnp.exp(s_ctx - m)
        denom = jnp.sum(p_ctx, axis=-1, keepdims=True)
        o = jnp.dot(p_ctx.astype(jnp.bfloat16), vc, preferred_element_type=jnp.float32)
        if local is not None:
            denom = denom + jnp.sum(p_loc, axis=-1, keepdims=True)
            o = o + jnp.dot(p_loc.astype(jnp.bfloat16), vw, preferred_element_type=jnp.float32)
        o_ref[pl.ds(q_rows, GRID_W), cols] = unstack(o / denom).astype(o_ref.dtype)

    def lat_row(r, carry):
        r0 = jnp.clip(r - WIN_R // 2, 0, GRID_ROWS - WIN_R)
        q_rows = pl.multiple_of(CTX_LEN + r * GRID_W, GRID_W)
        k_rows = pl.multiple_of(CTX_LEN + r0 * GRID_W, GRID_W)
        for s in range(n_stacks):
            attend(q_rows, s, (k_rows, r - r0))
        return carry

    lax.fori_loop(0, GRID_ROWS, lat_row, 0)

    if need_ctx:
        def ctx_chunk(cq, carry):
            q_rows = pl.multiple_of(cq * GRID_W, GRID_W)
            for s in range(n_stacks):
                attend(q_rows, s, None)
            return carry

        lax.fori_loop(0, CTX_LEN // GRID_W, ctx_chunk, 0)
    else:
        o_ref[0:CTX_LEN, :] = jnp.zeros((CTX_LEN, NA_WIDTH), o_ref.dtype)


def _attn_bias_table(rpb):
    delta = jnp.arange(WIN_R)[:, None]
    k_row = jnp.arange(WIN_R)[None, :]
    rel_r = k_row - delta + WIN_R - 1
    q_col = jnp.arange(GRID_W)[:, None]
    k_col = jnp.arange(GRID_W)[None, :]
    c_start = jnp.clip(q_col - WIN_C // 2, 0, GRID_W - WIN_C)
    ok = (k_col >= c_start) & (k_col < c_start + WIN_C)
    rel_c = jnp.clip(k_col - q_col + WIN_C - 1, 0, 2 * WIN_C - 2)
    tab = rpb[:, rel_r[:, :, None, None], rel_c[None, None, :, :]]
    tab = jnp.where(ok[None, None, None], tab, NEG_INF)
    tab = tab.transpose(1, 0, 3, 2, 4)
    return tab.reshape(WIN_R, NA_HEADS, GRID_W, WIN_R * GRID_W)


def _attention(qkv, bias, bsz, need_ctx):
    return pl.pallas_call(
        functools.partial(_attn_kernel, need_ctx=need_ctx),
        out_shape=jax.ShapeDtypeStruct((bsz, ROWS_PER_B, NA_WIDTH), jnp.bfloat16),
        grid=(bsz,),
        in_specs=[
            pl.BlockSpec((None, ROWS_PER_B, NA_WIDTH), lambda b: (b, 0, 0)),
            pl.BlockSpec((None, ROWS_PER_B, NA_WIDTH), lambda b: (b, 0, 1)),
            pl.BlockSpec((None, ROWS_PER_B, NA_WIDTH), lambda b: (b, 0, 2)),
            pl.BlockSpec((WIN_R, NA_HEADS, GRID_W, WIN_R * GRID_W), lambda b: (0, 0, 0, 0)),
        ],
        out_specs=pl.BlockSpec((None, ROWS_PER_B, NA_WIDTH), lambda b: (b, 0, 0)),
        compiler_params=_cparams(("arbitrary",)),
        name="attention",
    )(qkv, qkv, qkv, bias)


def _shift_rows(x, offset):
    length = x.shape[0]
    if offset == 0:
        return x
    rolled = pltpu.roll(x, (-offset) % length, axis=0)
    row = lax.broadcasted_iota(jnp.int32, x.shape, 0)
    ok = (row + offset >= 0) & (row + offset < length)
    return jnp.where(ok, rolled, 0.0)


def _dwconv(x, w_ref, b_ref, left):
    width = w_ref.shape[0]
    y = _shift_rows(x, -left) * w_ref[0:1, :] + b_ref[...]
    for k in range(1, width):
        y = y + _shift_rows(x, k - left) * w_ref[k:k + 1, :]
    return y


def _lru_kernel(rest_ref, lcw_ref, lcb_ref, wbd_ref, rgb_ref, lam_ref, scw_ref, scb_ref, o_ref,
                xc_s, a_s, b_s, hl_s, p_s, y_s, *, need_ctx):
    half = LRU_WIDTH // 2
    col_rx, col_rg, col_sb, col_sc, col_sx = (k * LRU_WIDTH for k in range(5))
    segments = ((0, CTX_LEN), (CTX_LEN, SEQ))

    for start, length in segments:
        xc_s[start:start + length, :] = _dwconv(rest_ref[start:start + length, col_rx:col_rx + LRU_WIDTH],
                                                lcw_ref, lcb_ref, 2)

    def coeffs(d, start, length):
        chunk = 256
        sp = jax.nn.softplus(-lam_ref[d:d + 1, :])
        for c0 in range(0, length, chunk):
            xc = xc_s[start + c0:start + c0 + chunk, :]
            pre = jnp.dot(xc.astype(jnp.bfloat16), wbd_ref[:, 2 * d * LRU_WIDTH:(2 * d + 2) * LRU_WIDTH],
                          preferred_element_type=jnp.float32) + rgb_ref[:, 2 * d * LRU_WIDTH:(2 * d + 2) * LRU_WIDTH]
            gate_r = jax.nn.sigmoid(pre[:, :LRU_WIDTH])
            gate_i = jax.nn.sigmoid(pre[:, LRU_WIDTH:])
            log_a = -RG_C * gate_r * sp
            a = jnp.exp(log_a)
            bb = jnp.sqrt(-jnp.tanh(log_a) * (a * a + 1.0)) * (gate_i * xc)
            for c in range(2):
                a_s[c, c0:c0 + chunk, :] = a[:, c * half:(c + 1) * half]
                b_s[c, c0:c0 + chunk, :] = bb[:, c * half:(c + 1) * half]

    def scan(length, reverse, h0):
        seg = length // SCAN_SEGS

        def step(i, carry):
            t = seg - 1 - i if reverse else i
            rows = pl.ds(t, SCAN_SEGS, stride=seg)
            new = []
            for c in range(2):
                h, p = carry[2 * c], carry[2 * c + 1]
                a_t = a_s[c, rows, :]
                h = a_t * h + b_s[c, rows, :]
                p = a_t * p
                hl_s[c, rows, :] = h
                p_s[c, rows, :] = p
                new += [h, p]
            return tuple(new)

        zero = jnp.zeros((SCAN_SEGS, half), jnp.float32)
        one = jnp.ones((SCAN_SEGS, half), jnp.float32)
        ends = lax.fori_loop(0, seg, step, (zero, one, zero, one))
        finals = []
        order = range(SCAN_SEGS - 1, -1, -1) if reverse else range(SCAN_SEGS)
        for c in range(2):
            h_end, p_end = ends[2 * c], ends[2 * c + 1]
            carry = h0[c]
            for j in order:
                rows = slice(j * seg, (j + 1) * seg)
                hl_s[c, rows, :] = hl_s[c, rows, :] + p_s[c, rows, :] * carry
                carry = h_end[j:j + 1, :] + p_end[j:j + 1, :] * carry
            finals.append(carry)
        return finals

    zero_state = [jnp.zeros((1, half), jnp.float32)] * 2
    for d, reverse in enumerate((False, True)):
        (c_start, c_len), (l_start, l_len) = segments
        coeffs(d, c_start, c_len)
        state = scan(c_len, reverse, zero_state)
        if need_ctx:
            for c in range(2):
                cols = slice(c * half, (c + 1) * half)
                prev = 0.0 if d == 0 else y_s[c_start:c_start + c_len, cols]
                y_s[c_start:c_start + c_len, cols] = prev + hl_s[c, 0:c_len, :]
        coeffs(d, l_start, l_len)
        scan(l_len, reverse, state)
        for c in range(2):
            cols = slice(c * half, (c + 1) * half)
            prev = 0.0 if d == 0 else y_s[l_start:l_start + l_len, cols]
            y_s[l_start:l_start + l_len, cols] = prev + hl_s[c, 0:l_len, :]

    out_segments = segments if need_ctx else segments[1:]
    for start, length in out_segments:
        rows = slice(start, start + length)
        y_lru = y_s[rows, :] * jax.nn.gelu(rest_ref[rows, col_rg:col_rg + LRU_WIDTH])
        o_ref[rows, 0:LRU_WIDTH] = y_lru.astype(o_ref.dtype)
        cx = rest_ref[rows, col_sc:col_sc + SC_WIDTH] * rest_ref[rows, col_sx:col_sx + SC_WIDTH]
        y_sc = rest_ref[rows, col_sb:col_sb + SC_WIDTH] * _dwconv(cx, scw_ref, scb_ref, 1)
        o_ref[rows, LRU_WIDTH:LRU_WIDTH + SC_WIDTH] = y_sc.astype(o_ref.dtype)
    if not need_ctx:
        o_ref[0:CTX_LEN, :] = jnp.zeros((CTX_LEN, LRU_WIDTH + SC_WIDTH), o_ref.dtype)


def _block_diag_gates(rg_w):
    eye = jnp.eye(LRU_HEADS, dtype=rg_w.dtype)
    full = jnp.einsum('dgncm,nk->dgnckm', rg_w, eye)
    full = full.reshape(2, 2, LRU_WIDTH, LRU_WIDTH)
    return full.transpose(2, 0, 1, 3).reshape(LRU_WIDTH, 4 * LRU_WIDTH)


def _lru_sconv(rest, lcw, lcb, rg_w, rg_b, rg_lam, scw, scb, bsz, need_ctx):
    wbd = _block_diag_gates(rg_w).astype(jnp.bfloat16)
    const2 = lambda b: (0, 0)
    half = LRU_WIDTH // 2
    return pl.pallas_call(
        functools.partial(_lru_kernel, need_ctx=need_ctx),
        out_shape=jax.ShapeDtypeStruct((bsz, ROWS_PER_B, LRU_WIDTH + SC_WIDTH), jnp.bfloat16),
        grid=(bsz,),
        in_specs=[
            pl.BlockSpec((None, ROWS_PER_B, REST_WIDTH), lambda b: (b, 0, 0)),
            pl.BlockSpec(lcw.shape, const2),
            pl.BlockSpec((1, LRU_WIDTH), const2),
            pl.BlockSpec((LRU_WIDTH, 4 * LRU_WIDTH), const2),
            pl.BlockSpec((1, 4 * LRU_WIDTH), const2),
            pl.BlockSpec((2, LRU_WIDTH), const2),
            pl.BlockSpec(scw.shape, const2),
            pl.BlockSpec((1, SC_WIDTH), const2),
        ],
        out_specs=pl.BlockSpec((None, ROWS_PER_B, LRU_WIDTH + SC_WIDTH), lambda b: (b, 0, 0)),
        scratch_shapes=[
            pltpu.VMEM((ROWS_PER_B, LRU_WIDTH), jnp.float32),
            pltpu.VMEM((2, SEQ, half), jnp.float32),
            pltpu.VMEM((2, SEQ, half), jnp.float32),
            pltpu.VMEM((2, SEQ, half), jnp.float32),
            pltpu.VMEM((2, SEQ, half), jnp.float32),
            pltpu.VMEM((ROWS_PER_B, LRU_WIDTH), jnp.float32),
        ],
        compiler_params=_cparams(("arbitrary",)),
        name="lru_sconv",
    )(rest, lcw, lcb.reshape(1, LRU_WIDTH), wbd, rg_b.reshape(1, 4 * LRU_WIDTH), rg_lam, scw,
      scb.reshape(1, SC_WIDTH))


def _out_kernel(*refs, two_src):
    if two_src:
        (ctx_ref, x_ref, yna_ref, yls_ref, mod_ref, og_ref, w_ref, fg_ref, wr_ref, wrt_ref, br_ref, brt_ref,
         xo_ref, row_ref, pos_ref, meta_ref, cnt_s, cur_s, alloc_s, tb_s) = refs
        j = pl.program_id(0) % TILES_PER_B
        x = jnp.where(j == 0, ctx_ref[...], x_ref[...])
    else:
        (x_ref, yna_ref, yls_ref, mod_ref, og_ref, w_ref, fg_ref, wr_ref, wrt_ref, br_ref, brt_ref,
         xo_ref, row_ref, pos_ref, meta_ref, cnt_s, cur_s, alloc_s, tb_s) = refs
        x = x_ref[...]
    step = pl.program_id(0)

    @pl.when(step == 0)
    def _():
        cnt_s[...] = jnp.zeros_like(cnt_s)
        cur_s[...] = jnp.zeros_like(cur_s)
        alloc_s[...] = jnp.zeros_like(alloc_s)
        tb_s[...] = jnp.zeros_like(tb_s)

    yna = yna_ref[...].astype(jnp.float32)
    yls = yls_ref[...].astype(jnp.float32)
    merged = jnp.concatenate([
        _rms(yna, og_ref[:, :NA_WIDTH]),
        _rms(yls[:, :LRU_WIDTH], og_ref[:, NA_WIDTH:NA_WIDTH + LRU_WIDTH]),
        _rms(yls[:, LRU_WIDTH:], og_ref[:, NA_WIDTH + LRU_WIDTH:]),
    ], axis=-1).astype(jnp.bfloat16)
    y = jnp.dot(merged, w_ref[...], preferred_element_type=jnp.float32)
    x_new = x + mod_ref[2:3, :] * y
    xo_ref[...] = x_new
    h2 = _rms(x_new, fg_ref[...]) * (1.0 + mod_ref[4:5, :]) + mod_ref[3:4, :]

    logits = jnp.dot(h2, wr_ref[...], precision=_HI, preferred_element_type=jnp.float32) + br_ref[...]
    ex = jnp.exp(logits - jnp.max(logits, axis=-1, keepdims=True))
    probs = ex / jnp.sum(ex, axis=-1, keepdims=True)
    row_ref[:, :D_MODEL] = h2
    row_ref[:, D_MODEL:] = probs

    lt = lax.dot_general(wrt_ref[...], h2, _NT, precision=_HI, preferred_element_type=jnp.float32) + brt_ref[...]
    et = jnp.exp(lt - jnp.max(lt, axis=0, keepdims=True))
    pe = [et[e:e + 1, :] for e in range(N_EXPERTS)]

    def top2_sum(v):
        best = v[0] + v[1]
        for a in range(E_PER_GROUP):
            for b in range(a + 1, E_PER_GROUP):
                if (a, b) != (0, 1):
                    best = jnp.maximum(best, v[a] + v[b])
        return best

    score = [top2_sum(pe[g * E_PER_GROUP:(g + 1) * E_PER_GROUP]) for g in range(N_GROUPS)]
    g_best, g_sel = score[0], jnp.zeros((1, TM), jnp.int32)
    for g in range(1, N_GROUPS):
        upd = score[g] > g_best
        g_sel = jnp.where(upd, g, g_sel)
        g_best = jnp.where(upd, score[g], g_best)
    p_in = []
    for k in range(E_PER_GROUP):
        v = pe[k]
        for g in range(1, N_GROUPS):
            v = jnp.where(g_sel == g, pe[g * E_PER_GROUP + k], v)
        p_in.append(v)
    m1, i1 = p_in[0], jnp.zeros((1, TM), jnp.int32)
    for k in range(1, E_PER_GROUP):
        upd = p_in[k] > m1
        i1 = jnp.where(upd, k, i1)
        m1 = jnp.where(upd, p_in[k], m1)
    m2, i2 = jnp.full((1, TM), -1.0, jnp.float32), jnp.zeros((1, TM), jnp.int32)
    for k in range(E_PER_GROUP):
        cand = jnp.where(i1 == k, -2.0, p_in[k])
        upd = cand > m2
        i2 = jnp.where(upd, k, i2)
        m2 = jnp.where(upd, cand, m2)
    lo, hi = jnp.minimum(i1, i2), jnp.maximum(i1, i2)
    pair = jnp.where(lo == 0, hi - 1, jnp.where(lo == 1, hi + 1, N_PAIRS - 1))
    bucket = g_sel * N_PAIRS + pair

    b_iota = lax.broadcasted_iota(jnp.int32, (BUCKET_ROWS, TM), 0)
    onehot = b_iota == bucket
    tri = (lax.broadcasted_iota(jnp.int32, (TM, TM), 0) <= lax.broadcasted_iota(jnp.int32, (TM, TM), 1))
    as_bf16 = lambda mask: jnp.where(mask, 1.0, 0.0).astype(jnp.bfloat16)
    cum = jnp.dot(as_bf16(onehot), as_bf16(tri), preferred_element_type=jnp.float32)
    cnt_new = cum[:, TM - 1:TM].astype(jnp.int32)
    cnt_old = cnt_s[:, 0:1]
    cur = cur_s[:, 0:1]
    alloc = alloc_s[0:1, 0:1]
    shift = TM.bit_length() - 1
    q_last = (cnt_old + cnt_new - 1) >> shift
    q_prev = (cnt_old - 1) >> shift
    opens = jnp.where(cnt_new > 0, q_last - q_prev, 0)
    lower = (lax.broadcasted_iota(jnp.int32, (BUCKET_ROWS, BUCKET_ROWS), 1)
             < lax.broadcasted_iota(jnp.int32, (BUCKET_ROWS, BUCKET_ROWS), 0))
    opens_b = jnp.broadcast_to(opens, (BUCKET_ROWS, LANES)).astype(jnp.float32).astype(jnp.bfloat16)
    before = jnp.dot(as_bf16(lower), opens_b, preferred_element_type=jnp.float32)[:, 0:1].astype(jnp.int32)
    new_id = alloc + before
    rank = cnt_old + cum.astype(jnp.int32) - 1
    tile_id = jnp.where((opens > 0) & ((rank >> shift) == q_last), new_id, cur)
    slot = tile_id * TM + (rank & (TM - 1))
    pos_ref[...] = jnp.sum(jnp.where(onehot, slot, 0).astype(jnp.float32), axis=0, keepdims=True).astype(jnp.int32)

    lane_id = lax.broadcasted_iota(jnp.int32, (BUCKET_ROWS, TM), 1)
    opened_here = (opens > 0) & (new_id == lane_id)
    opened_bucket = jnp.max(jnp.where(opened_here, b_iota, -1).astype(jnp.float32), axis=0,
                            keepdims=True).astype(jnp.int32)
    tb = jnp.where(opened_bucket >= 0, opened_bucket, tb_s[0:1, :])
    alloc_new = alloc + jnp.sum(opens.astype(jnp.float32), axis=0, keepdims=True).astype(jnp.int32)
    tb_s[...] = jnp.broadcast_to(tb, tb_s.shape)
    cnt_s[...] = jnp.broadcast_to(cnt_old + cnt_new, cnt_s.shape)
    cur_s[...] = jnp.broadcast_to(jnp.where(opens > 0, new_id, cur), cur_s.shape)
    alloc_s[...] = jnp.broadcast_to(alloc_new, alloc_s.shape)
    meta_ref[0:1, :] = tb
    meta_ref[1:2, :] = jnp.broadcast_to(alloc_new, (1, TM))
    meta_ref[2:8, :] = jnp.zeros((6, TM), jnp.int32)


def _out_proj(srcs, yna, yls, mod, out_g, w_bf16, ffn_g, w_router, b_router, bsz, with_ctx):
    two_src = len(srcs) == 2
    n_tiles, bj, mod_row = _tile_maps(bsz, with_ctx)
    tile = lambda width: pl.BlockSpec((None, TM, width), lambda i: (bj(i)[0], bj(i)[1], 0))
    const2 = lambda i: (0, 0)
    if two_src:
        src_specs = [
            pl.BlockSpec((None, TM, D_MODEL), lambda i: (bj(i)[0], 0, 0)),
            pl.BlockSpec((None, TM, D_MODEL), lambda i: (bj(i)[0], jnp.maximum(bj(i)[1] - 1, 0), 0)),
        ]
    else:
        src_specs = [tile(D_MODEL)]
    wr = jnp.zeros((D_MODEL, LANES), jnp.float32).at[:, :N_EXPERTS].set(w_router)
    br = jnp.full((1, LANES), NEG_INF, jnp.float32).at[0, :N_EXPERTS].set(b_router)
    return pl.pallas_call(
        functools.partial(_out_kernel, two_src=two_src),
        out_shape=[
            jax.ShapeDtypeStruct((bsz, ROWS_PER_B, D_MODEL), jnp.float32),
            jax.ShapeDtypeStruct((bsz, ROWS_PER_B, ROW_WIDTH), jnp.float32),
            jax.ShapeDtypeStruct((n_tiles, 1, TM), jnp.int32),
            jax.ShapeDtypeStruct((8, TM), jnp.int32),
        ],
        grid=(n_tiles,),
        in_specs=src_specs + [
            tile(NA_WIDTH),
            tile(LRU_WIDTH + SC_WIDTH),
            pl.BlockSpec((None, N_MOD, D_MODEL), lambda i: (mod_row(i), 0, 0)),
            pl.BlockSpec((1, D_MODEL), const2),
            pl.BlockSpec((D_MODEL, D_MODEL), const2),
            pl.BlockSpec((1, D_MODEL), const2),
            pl.BlockSpec((D_MODEL, LANES), const2),
            pl.BlockSpec((N_EXPERTS, D_MODEL), const2),
            pl.BlockSpec((1, LANES), const2),
            pl.BlockSpec((N_EXPERTS, 1), const2),
        ],
        out_specs=[
            tile(D_MODEL),
            tile(ROW_WIDTH),
            pl.BlockSpec((None, 1, TM), lambda i: (i, 0, 0)),
            pl.BlockSpec((8, TM), const2),
        ],
        scratch_shapes=[
            pltpu.VMEM((BUCKET_ROWS, LANES), jnp.int32),
            pltpu.VMEM((BUCKET_ROWS, LANES), jnp.int32),
            pltpu.VMEM((8, LANES), jnp.int32),
            pltpu.VMEM((8, TM), jnp.int32),
        ],
        compiler_params=_cparams(("arbitrary",)),
        name="out_proj_route",
    )(*srcs, yna, yls, mod, out_g.reshape(1, D_MODEL), w_bf16, ffn_g.reshape(1, D_MODEL), wr, w_router.T, br,
      b_router.reshape(N_EXPERTS, 1))


def _dispatch_kernel(pos_ref, rows_ref, buf_in_ref, buf_ref, sem, *, n_tiles, per_b, off):
    del buf_in_ref

    def copy(src_row, dst_row):
        return pltpu.make_async_copy(rows_ref.at[pl.ds(src_row, 1), :], buf_ref.at[pl.ds(dst_row, 1), :], sem)

    def drain(count):
        def body(k, c):
            copy(0, 0).wait()
            return c
        lax.fori_loop(0, count, body, 0, unroll=8)

    def tile_body(i, c):
        src0 = ((i // per_b) * TILES_PER_B + i % per_b + off) * TM

        def body(k, c2):
            copy(src0 + k, pos_ref[i * TM + k]).start()
            return c2
        lax.fori_loop(0, TM, body, 0, unroll=8)

        @pl.when(i > 0)
        def _():
            drain(TM)
        return c

    lax.fori_loop(0, n_tiles, tile_body, 0)
    drain(TM)


def _dispatch(pos, rows, buf, bsz, with_ctx):
    n_tiles, _, _ = _tile_maps(bsz, with_ctx)
    per_b = TILES_PER_B if with_ctx else LAT_TILES_PER_B
    return pl.pallas_call(
        functools.partial(_dispatch_kernel, n_tiles=n_tiles, per_b=per_b, off=0 if with_ctx else 1),
        out_shape=jax.ShapeDtypeStruct(buf.shape, buf.dtype),
        grid_spec=pltpu.PrefetchScalarGridSpec(
            num_scalar_prefetch=1,
            grid=(1,),
            in_specs=[pl.BlockSpec(memory_space=pl.ANY), pl.BlockSpec(memory_space=pl.ANY)],
            out_specs=pl.BlockSpec(memory_space=pl.ANY),
            scratch_shapes=[pltpu.SemaphoreType.DMA(())],
        ),
        input_output_aliases={2: 0},
        compiler_params=_cparams(("arbitrary",)),
        name="dispatch",
    )(pos.reshape(-1), rows.reshape(bsz * ROWS_PER_B, ROW_WIDTH), buf)


def _moe_kernel(src_ref, e0_ref, e1_ref, used_ref, xs_ref, wg0, wu0, wd0, wg1, wu1, wd1, o_ref, wg_s, wu_s, wd_s):
    n = pl.program_id(0)
    prev = jnp.maximum(n - 1, 0)
    e0, e1 = e0_ref[n], e1_ref[n]
    fresh = (n == 0) | (e0 != e0_ref[prev]) | (e1 != e1_ref[prev])

    @pl.when(fresh)
    def _():
        for k, (g, u, d) in enumerate(((wg0, wu0, wd0), (wg1, wu1, wd1))):
            wg_s[k] = g[...].astype(jnp.bfloat16)
            wu_s[k] = u[...].astype(jnp.bfloat16)
            wd_s[k] = d[...].astype(jnp.bfloat16)

    @pl.when(n < used_ref[0])
    def _():
        xb = xs_ref[:, :D_MODEL].astype(jnp.bfloat16)
        probs = xs_ref[:, D_MODEL:]
        lane = lax.broadcasted_iota(jnp.int32, probs.shape, 1)
        p0 = jnp.sum(jnp.where(lane == e0, probs, 0.0), axis=-1, keepdims=True)
        p1 = jnp.sum(jnp.where(lane == e1, probs, 0.0), axis=-1, keepdims=True)
        total = p0 + p1
        total = jnp.where(total > 0.0, total, 1.0)
        out = jnp.zeros((TM, D_MODEL), jnp.float32)
        for k, wk in enumerate((p0 / total, p1 / total)):
            gate = jnp.dot(xb, wg_s[k], preferred_element_type=jnp.float32)
            up = jnp.dot(xb, wu_s[k], preferred_element_type=jnp.float32)
            hid = (gate * jax.nn.sigmoid(gate)) * up
            out = out + wk * jnp.dot(hid.astype(jnp.bfloat16), wd_s[k], preferred_element_type=jnp.float32)
        o_ref[...] = out


def _moe(sched, xs, w_gate, w_up, w_down):
    n_tiles = xs.shape[0] // TM
    src, e0, e1, used = sched
    first = lambda n, s, a, b, u: (a[n], 0, 0)
    second = lambda n, s, a, b, u: (b[n], 0, 0)
    gate_spec = lambda m: pl.BlockSpec((None, D_MODEL, D_EXPERT), m)
    down_spec = lambda m: pl.BlockSpec((None, D_EXPERT, D_MODEL), m)
    return pl.pallas_call(
        _moe_kernel,
        out_shape=jax.ShapeDtypeStruct((n_tiles * TM, D_MODEL), jnp.float32),
        grid_spec=pltpu.PrefetchScalarGridSpec(
            num_scalar_prefetch=4,
            grid=(n_tiles,),
            in_specs=[
                pl.BlockSpec((TM, ROW_WIDTH), lambda n, s, a, b, u: (s[n], 0)),
                gate_spec(first), gate_spec(first), down_spec(first),
                gate_spec(second), gate_spec(second), down_spec(second),
            ],
            out_specs=pl.BlockSpec((TM, D_MODEL), lambda n, s, a, b, u: (s[n], 0)),
            scratch_shapes=[
                pltpu.VMEM((2, D_MODEL, D_EXPERT), jnp.bfloat16),
                pltpu.VMEM((2, D_MODEL, D_EXPERT), jnp.bfloat16),
                pltpu.VMEM((2, D_EXPERT, D_MODEL), jnp.bfloat16),
            ],
        ),
        compiler_params=_cparams(("arbitrary",)),
        name="moe_experts",
    )(src, e0, e1, used, xs, w_gate, w_up, w_down, w_gate, w_up, w_down)


def _schedule(meta, n_tiles):
    tile_bucket = meta[0, :n_tiles]
    used = meta[1, 0]
    ids = jnp.arange(n_tiles, dtype=jnp.int32)
    order = jnp.argsort(jnp.where(ids < used, tile_bucket, N_BUCKETS), stable=True).astype(jnp.int32)
    src = jnp.where(ids < used, order, order[jnp.maximum(used - 1, 0)])
    bucket = tile_bucket[src]
    group, pair = bucket // N_PAIRS, bucket % N_PAIRS
    lo = jnp.array([0, 0, 0, 1, 1, 2], jnp.int32)[pair]
    hi = jnp.array([1, 2, 3, 2, 3, 3], jnp.int32)[pair]
    return src, group * E_PER_GROUP + lo, group * E_PER_GROUP + hi, used.reshape(1)


def _combine_kernel(pos_ref, x_ref, mod_ref, fg_ref, ys_ref, o_ref, buf, sem, *, final):
    i = pl.program_id(0)
    n = pl.num_programs(0)

    def row_copy(src_row, slot, k):
        return pltpu.make_async_copy(ys_ref.at[pl.ds(src_row, 1), :], buf.at[slot, pl.ds(k, 1), :], sem.at[slot])

    def issue(tile, slot):
        def body(k, c):
            row_copy(pos_ref[tile * TM + k], slot, k).start()
            return c
        lax.fori_loop(0, TM, body, 0, unroll=8)

    @pl.when(i == 0)
    def _():
        issue(0, 0)

    @pl.when(i + 1 < n)
    def _():
        issue(i + 1, (i + 1) % 2)

    slot = i % 2

    def drain(k, c):
        row_copy(0, slot, 0).wait()
        return c
    lax.fori_loop(0, TM, drain, 0, unroll=8)

    x_new = x_ref[...] + mod_ref[5:6, :] * buf[slot]
    if final:
        x_new = _rms(x_new, fg_ref[...])
    o_ref[...] = x_new


def _combine(pos, x_all, mod, final_g, ys, bsz, with_ctx, final):
    n_tiles, bj, mod_row = _tile_maps(bsz, with_ctx)
    if final:
        out_shape = jax.ShapeDtypeStruct((bsz, SEQ, D_MODEL), jnp.float32)
        out_spec = pl.BlockSpec((None, TM, D_MODEL), lambda i, p: (bj(i)[0], bj(i)[1] - 1, 0))
    else:
        out_shape = jax.ShapeDtypeStruct((bsz, ROWS_PER_B, D_MODEL), jnp.float32)
        out_spec = pl.BlockSpec((None, TM, D_MODEL), lambda i, p: (bj(i)[0], bj(i)[1], 0))
    return pl.pallas_call(
        functools.partial(_combine_kernel, final=final),
        out_shape=out_shape,
        grid_spec=pltpu.PrefetchScalarGridSpec(
            num_scalar_prefetch=1,
            grid=(n_tiles,),
            in_specs=[
                pl.BlockSpec((None, TM, D_MODEL), lambda i, p: (bj(i)[0], bj(i)[1], 0)),
                pl.BlockSpec((None, N_MOD, D_MODEL), lambda i, p: (mod_row(i), 0, 0)),
                pl.BlockSpec((1, D_MODEL), lambda i, p: (0, 0)),
                pl.BlockSpec(memory_space=pl.ANY),
            ],
            out_specs=out_spec,
            scratch_shapes=[pltpu.VMEM((2, TM, D_MODEL), jnp.float32), pltpu.SemaphoreType.DMA((2,))],
        ),
        compiler_params=_cparams(("arbitrary",)),
        name="combine",
    )(pos.reshape(-1), x_all, mod, final_g.reshape(1, D_MODEL), ys)


def kernel(x, c, ctx, c_ctx, w_ada, b_ada, norm_mix_g, w_in, lru_conv_w, lru_conv_b, rg_w, rg_b, rg_lam, na_rpb,
           sc_conv_w, sc_conv_b, mix_out_g, w_out, norm_ffn_g, w_router, b_router, w_gate, w_up, w_down, final_g):
    bsz = x.shape[0]
    mod_rows = -(-(bsz + 1) // 8) * 8
    c_all = jnp.zeros((mod_rows, D_MODEL), jnp.float32).at[:bsz].set(c).at[bsz].set(c_ctx)
    mods = _ada(c_all, w_ada, b_ada).reshape(DEPTH, mod_rows, N_MOD, D_MODEL)

    x_all = None
    out = None
    for l in range(DEPTH):
        need_ctx = l < DEPTH - 1
        mod = mods[l]
        srcs = (ctx, x) if l == 0 else (x_all,)
        qkv, rest = _in_proj(srcs, norm_mix_g[l], mod, w_in[l].astype(jnp.bfloat16), bsz)
        yna = _attention(qkv, _attn_bias_table(na_rpb[l]), bsz, need_ctx)
        yls = _lru_sconv(rest, lru_conv_w[l], lru_conv_b[l], rg_w[l], rg_b[l], rg_lam[l], sc_conv_w[l],
                         sc_conv_b[l], bsz, need_ctx)
        x_mid, rows, pos, meta = _out_proj(srcs, yna, yls, mod, mix_out_g[l], w_out[l].astype(jnp.bfloat16),
                                           norm_ffn_g[l], w_router, b_router, bsz, need_ctx)
        n_tok_tiles = pos.shape[0]
        n_sorted_tiles = n_tok_tiles + N_BUCKETS
        xs = _dispatch(pos, rows, jnp.zeros((n_sorted_tiles * TM, ROW_WIDTH), jnp.float32), bsz, need_ctx)
        ys = _moe(_schedule(meta, n_sorted_tiles), xs, w_gate[l], w_up[l], w_down[l])
        res = _combine(pos, x_mid, mod, final_g, ys, bsz, need_ctx, final=not need_ctx)
        if need_ctx:
            x_all = res
        else:
            out = res
    return out
```

```python
import functools

import jax
import jax.numpy as jnp
from jax import lax
from jax.experimental import pallas as pl
from jax.experimental.pallas import tpu as pltpu

D_MODEL = 1024
SEQ = 2048
CTX_LEN = 256
ROWS_PER_B = CTX_LEN + SEQ
DEPTH = 2
N_MOD = 6
EPS = 1e-6
NEG_INF = -1e30

GRID_W = 64
GRID_ROWS = SEQ // GRID_W
HEAD_DIM = 64
NA_WIDTH = 512
NA_HEADS = 8
LRU_WIDTH = 256
LRU_HEADS = 4
LRU_BLOCK = 64
SC_WIDTH = 256
QKV_WIDTH = 3 * NA_WIDTH
REST_WIDTH = 2 * LRU_WIDTH + 3 * SC_WIDTH
IN_WIDTH = QKV_WIDTH + REST_WIDTH
RG_C = 8.0
WIN_R = 8
WIN_C = 16
N_EXPERTS = 16
N_GROUPS = 4
E_PER_GROUP = 4
N_PAIRS = 6
N_BUCKETS = N_GROUPS * N_PAIRS
D_EXPERT = 512

TM = 256
TILES_PER_B = ROWS_PER_B // TM
LAT_TILES_PER_B = SEQ // TM
LANES = 128
BUCKET_ROWS = 32
ROW_WIDTH = D_MODEL + LANES
HEADS_PER_STACK = 4
STACK_W = HEADS_PER_STACK * HEAD_DIM
SCAN_SEGS = 8
VMEM_LIMIT = 56 * 1024 * 1024

_HI = lax.Precision.HIGHEST
_NT = (((1,), (1,)), ((), ()))


def _cparams(sem):
    return pltpu.CompilerParams(dimension_semantics=sem, vmem_limit_bytes=VMEM_LIMIT)


def _rms(v, g):
    return v * lax.rsqrt(jnp.mean(v * v, axis=-1, keepdims=True) + EPS) * g


def _ada_kernel(c_ref, w_ref, b_ref, o_ref):
    cond = c_ref[...]
    cond = cond * jax.nn.sigmoid(cond)
    o_ref[0] = jnp.dot(cond, w_ref[0], precision=_HI, preferred_element_type=jnp.float32) + b_ref[0]


def _ada(c_all, w_ada, b_ada):
    depth, _, width = w_ada.shape
    rows = c_all.shape[0]
    tn = 1536
    return pl.pallas_call(
        _ada_kernel,
        out_shape=jax.ShapeDtypeStruct((depth, rows, width), jnp.float32),
        grid=(depth, width // tn),
        in_specs=[
            pl.BlockSpec((rows, D_MODEL), lambda l, n: (0, 0)),
            pl.BlockSpec((1, D_MODEL, tn), lambda l, n: (l, 0, n)),
            pl.BlockSpec((1, 1, tn), lambda l, n: (l, 0, n)),
        ],
        out_specs=pl.BlockSpec((1, rows, tn), lambda l, n: (l, 0, n)),
        compiler_params=_cparams(("arbitrary", "arbitrary")),
        name="ada",
    )(c_all, w_ada, b_ada.reshape(depth, 1, width))


def _tile_maps(bsz, with_ctx):
    per_b = TILES_PER_B if with_ctx else LAT_TILES_PER_B
    off = 0 if with_ctx else 1

    def bj(i):
        return i // per_b, i % per_b + off

    def mod_row(i):
        b, j = bj(i)
        return jnp.where(j == 0, bsz, b)

    return per_b * bsz, bj, mod_row


def _in_kernel(*refs, two_src):
    if two_src:
        ctx_ref, x_ref, g_ref, mod_ref, w_ref, qkv_ref, rest_ref = refs
        j = pl.program_id(0) % TILES_PER_B
        x = jnp.where(j == 0, ctx_ref[...], x_ref[...])
    else:
        x_ref, g_ref, mod_ref, w_ref, qkv_ref, rest_ref = refs
        x = x_ref[...]
    h = _rms(x, g_ref[...]) * (1.0 + mod_ref[1:2, :]) + mod_ref[0:1, :]
    hb = h.astype(jnp.bfloat16)
    qkv_ref[...] = jnp.dot(hb, w_ref[:, :QKV_WIDTH], preferred_element_type=jnp.float32).astype(jnp.bfloat16)
    rest_ref[...] = jnp.dot(hb, w_ref[:, QKV_WIDTH:], preferred_element_type=jnp.float32)


def _in_proj(srcs, g, mod, w_bf16, bsz):
    two_src = len(srcs) == 2
    n_tiles, bj, mod_row = _tile_maps(bsz, True)
    if two_src:
        src_specs = [
            pl.BlockSpec((None, TM, D_MODEL), lambda i: (bj(i)[0], 0, 0)),
            pl.BlockSpec((None, TM, D_MODEL), lambda i: (bj(i)[0], jnp.maximum(bj(i)[1] - 1, 0), 0)),
        ]
    else:
        src_specs = [pl.BlockSpec((None, TM, D_MODEL), lambda i: (bj(i)[0], bj(i)[1], 0))]
    return pl.pallas_call(
        functools.partial(_in_kernel, two_src=two_src),
        out_shape=[
            jax.ShapeDtypeStruct((bsz, ROWS_PER_B, QKV_WIDTH), jnp.bfloat16),
            jax.ShapeDtypeStruct((bsz, ROWS_PER_B, REST_WIDTH), jnp.float32),
        ],
        grid=(n_tiles,),
        in_specs=src_specs + [
            pl.BlockSpec((1, D_MODEL), lambda i: (0, 0)),
            pl.BlockSpec((None, N_MOD, D_MODEL), lambda i: (mod_row(i), 0, 0)),
            pl.BlockSpec((D_MODEL, IN_WIDTH), lambda i: (0, 0)),
        ],
        out_specs=[
            pl.BlockSpec((None, TM, QKV_WIDTH), lambda i: (bj(i)[0], bj(i)[1], 0)),
            pl.BlockSpec((None, TM, REST_WIDTH), lambda i: (bj(i)[0], bj(i)[1], 0)),
        ],
        compiler_params=_cparams(("arbitrary",)),
        name="in_proj",
    )(*srcs, g.reshape(1, D_MODEL), mod, w_bf16)


def _attn_kernel(q_ref, k_ref, v_ref, bias_ref, o_ref, *, need_ctx):
    lane_head = lax.broadcasted_iota(jnp.int32, (1, STACK_W), 1) // HEAD_DIM
    n_stacks = NA_WIDTH // STACK_W

    def stack_q(qg):
        zero = jnp.zeros_like(qg)
        return jnp.concatenate([jnp.where(lane_head == h, qg, zero) for h in range(HEADS_PER_STACK)], axis=0)

    def unstack(o):
        out = jnp.zeros((GRID_W, STACK_W), jnp.float32)
        for h in range(HEADS_PER_STACK):
            out = out + jnp.where(lane_head == h, o[h * GRID_W:(h + 1) * GRID_W], 0.0)
        return out

    def attend(q_rows, s, local):
        cols = slice(s * STACK_W, (s + 1) * STACK_W)
        qg = q_ref[pl.ds(q_rows, GRID_W), cols] * jnp.bfloat16(HEAD_DIM ** -0.5)
        qs = stack_q(qg)
        kc = k_ref[0:CTX_LEN, cols]
        vc = v_ref[0:CTX_LEN, cols]
        s_ctx = lax.dot_general(qs, kc, _NT, preferred_element_type=jnp.float32)
        m = jnp.max(s_ctx, axis=-1, keepdims=True)
        if local is not None:
            k_rows, delta = local
            kw = k_ref[pl.ds(k_rows, WIN_R * GRID_W), cols]
            vw = v_ref[pl.ds(k_rows, WIN_R * GRID_W), cols]
            bias = bias_ref[delta, s * HEADS_PER_STACK:(s + 1) * HEADS_PER_STACK]
            s_loc = lax.dot_general(qs, kw, _NT, preferred_element_type=jnp.float32)
            s_loc = s_loc + bias.reshape(HEADS_PER_STACK * GRID_W, WIN_R * GRID_W)
            m = jnp.maximum(m, jnp.max(s_loc, axis=-1, keepdims=True))
            p_loc = jnp.exp(s_loc - m)
        p_ctx = jnp.exp(s_ctx - m)
        denom = jnp.sum(p_ctx, axis=-1, keepdims=True)
        o = jnp.dot(p_ctx.astype(jnp.bfloat16), vc, preferred_element_type=jnp.float32)
        if local is not None:
            denom = denom + jnp.sum(p_loc, axis=-1, keepdims=True)
            o = o + jnp.dot(p_loc.astype(jnp.bfloat16), vw, preferred_element_type=jnp.float32)
        o_ref[pl.ds(q_rows, GRID_W), cols] = unstack(o / denom).astype(o_ref.dtype)

    def lat_row(r, carry):
        r0 = jnp.clip(r - WIN_R // 2, 0, GRID_ROWS - WIN_R)
        q_rows = pl.multiple_of(CTX_LEN + r * GRID_W, GRID_W)
        k_rows = pl.multiple_of(CTX_LEN + r0 * GRID_W, GRID_W)
        for s in range(n_stacks):
            attend(q_rows, s, (k_rows, r - r0))
        return carry

    lax.fori_loop(0, GRID_ROWS, lat_row, 0)

    if need_ctx:
        def ctx_chunk(cq, carry):
            q_rows = pl.multiple_of(cq * GRID_W, GRID_W)
            for s in range(n_stacks):
                attend(q_rows, s, None)
            return carry

        lax.fori_loop(0, CTX_LEN // GRID_W, ctx_chunk, 0)
    else:
        o_ref[0:CTX_LEN, :] = jnp.zeros((CTX_LEN, NA_WIDTH), o_ref.dtype)


def _attn_bias_table(rpb):
    n_rel_c = 2 * WIN_C - 1
    lead = GRID_W - WIN_C
    padded = jnp.pad(rpb, ((0, 0), (0, 0), (lead, 2 * GRID_W - lead - n_rel_c)))
    skew = jnp.tile(padded, (1, 1, GRID_W))[..., :GRID_W * (2 * GRID_W - 1)]
    skew = skew.reshape(NA_HEADS, 2 * WIN_R - 1, GRID_W, 2 * GRID_W - 1)
    toeplitz = skew[..., GRID_W - 1:]
    q_col = jnp.arange(GRID_W)[:, None]
    k_col = jnp.arange(GRID_W)[None, :]
    c_start = jnp.clip(q_col - WIN_C // 2, 0, GRID_W - WIN_C)
    ok = (k_col >= c_start) & (k_col < c_start + WIN_C)
    toeplitz = jnp.where(ok, toeplitz, NEG_INF)
    per_delta = [toeplitz[:, WIN_R - 1 - d:2 * WIN_R - 1 - d] for d in range(WIN_R)]
    tab = jnp.stack(per_delta, axis=0)
    return tab.transpose(0, 1, 3, 2, 4).reshape(WIN_R, NA_HEADS, GRID_W, WIN_R * GRID_W)


def _attention(qkv, bias, bsz, need_ctx):
    return pl.pallas_call(
        functools.partial(_attn_kernel, need_ctx=need_ctx),
        out_shape=jax.ShapeDtypeStruct((bsz, ROWS_PER_B, NA_WIDTH), jnp.bfloat16),
        grid=(bsz,),
        in_specs=[
            pl.BlockSpec((None, ROWS_PER_B, NA_WIDTH), lambda b: (b, 0, 0)),
            pl.BlockSpec((None, ROWS_PER_B, NA_WIDTH), lambda b: (b, 0, 1)),
            pl.BlockSpec((None, ROWS_PER_B, NA_WIDTH), lambda b: (b, 0, 2)),
            pl.BlockSpec((WIN_R, NA_HEADS, GRID_W, WIN_R * GRID_W), lambda b: (0, 0, 0, 0)),
        ],
        out_specs=pl.BlockSpec((None, ROWS_PER_B, NA_WIDTH), lambda b: (b, 0, 0)),
        compiler_params=_cparams(("arbitrary",)),
        name="attention",
    )(qkv, qkv, qkv, bias)


def _shift_rows(x, offset):
    length = x.shape[0]
    if offset == 0:
        return x
    rolled = pltpu.roll(x, (-offset) % length, axis=0)
    row = lax.broadcasted_iota(jnp.int32, x.shape, 0)
    ok = (row + offset >= 0) & (row + offset < length)
    return jnp.where(ok, rolled, 0.0)


def _dwconv(x, w_ref, b_ref, left):
    width = w_ref.shape[0]
    y = _shift_rows(x, -left) * w_ref[0:1, :] + b_ref[...]
    for k in range(1, width):
        y = y + _shift_rows(x, k - left) * w_ref[k:k + 1, :]
    return y


def _lru_kernel(rest_ref, lcw_ref, lcb_ref, wbd_ref, rgb_ref, lam_ref, scw_ref, scb_ref, o_ref,
                xc_s, a_s, b_s, hl_s, p_s, y_s, *, need_ctx):
    half = LRU_WIDTH // 2
    col_rx, col_rg, col_sb, col_sc, col_sx = (k * LRU_WIDTH for k in range(5))
    segments = ((0, CTX_LEN), (CTX_LEN, SEQ))

    for start, length in segments:
        xc_s[start:start + length, :] = _dwconv(rest_ref[start:start + length, col_rx:col_rx + LRU_WIDTH],
                                                lcw_ref, lcb_ref, 2)

    def coeffs(d, start, length):
        chunk = 256
        sp = jax.nn.softplus(-lam_ref[d:d + 1, :])
        for c0 in range(0, length, chunk):
            xc = xc_s[start + c0:start + c0 + chunk, :]
            pre = jnp.dot(xc.astype(jnp.bfloat16), wbd_ref[:, 2 * d * LRU_WIDTH:(2 * d + 2) * LRU_WIDTH],
                          preferred_element_type=jnp.float32) + rgb_ref[:, 2 * d * LRU_WIDTH:(2 * d + 2) * LRU_WIDTH]
            gate_r = jax.nn.sigmoid(pre[:, :LRU_WIDTH])
            gate_i = jax.nn.sigmoid(pre[:, LRU_WIDTH:])
            log_a = -RG_C * gate_r * sp
            a = jnp.exp(log_a)
            bb = jnp.sqrt(-jnp.tanh(log_a) * (a * a + 1.0)) * (gate_i * xc)
            for c in range(2):
                a_s[c, c0:c0 + chunk, :] = a[:, c * half:(c + 1) * half]
                b_s[c, c0:c0 + chunk, :] = bb[:, c * half:(c + 1) * half]

    def scan(length, reverse, h0):
        seg = length // SCAN_SEGS

        def step(i, carry):
            t = seg - 1 - i if reverse else i
            rows = pl.ds(t, SCAN_SEGS, stride=seg)
            new = []
            for c in range(2):
                h, p = carry[2 * c], carry[2 * c + 1]
                a_t = a_s[c, rows, :]
                h = a_t * h + b_s[c, rows, :]
                p = a_t * p
                hl_s[c, rows, :] = h
                p_s[c, rows, :] = p
                new += [h, p]
            return tuple(new)

        zero = jnp.zeros((SCAN_SEGS, half), jnp.float32)
        one = jnp.ones((SCAN_SEGS, half), jnp.float32)
        ends = lax.fori_loop(0, seg, step, (zero, one, zero, one))
        finals = []
        order = range(SCAN_SEGS - 1, -1, -1) if reverse else range(SCAN_SEGS)
        for c in range(2):
            h_end, p_end = ends[2 * c], ends[2 * c + 1]
            carry = h0[c]
            for j in order:
                rows = slice(j * seg, (j + 1) * seg)
                hl_s[c, rows, :] = hl_s[c, rows, :] + p_s[c, rows, :] * carry
                carry = h_end[j:j + 1, :] + p_end[j:j + 1, :] * carry
            finals.append(carry)
        return finals

    zero_state = [jnp.zeros((1, half), jnp.float32)] * 2
    for d, reverse in enumerate((False, True)):
        (c_start, c_len), (l_start, l_len) = segments
        coeffs(d, c_start, c_len)
        state = scan(c_len, reverse, zero_state)
        if need_ctx:
            for c in range(2):
                cols = slice(c * half, (c + 1) * half)
                prev = 0.0 if d == 0 else y_s[c_start:c_start + c_len, cols]
                y_s[c_start:c_start + c_len, cols] = prev + hl_s[c, 0:c_len, :]
        coeffs(d, l_start, l_len)
        scan(l_len, reverse, state)
        for c in range(2):
            cols = slice(c * half, (c + 1) * half)
            prev = 0.0 if d == 0 else y_s[l_start:l_start + l_len, cols]
            y_s[l_start:l_start + l_len, cols] = prev + hl_s[c, 0:l_len, :]

    out_segments = segments if need_ctx else segments[1:]
    for start, length in out_segments:
        rows = slice(start, start + length)
        y_lru = y_s[rows, :] * jax.nn.gelu(rest_ref[rows, col_rg:col_rg + LRU_WIDTH])
        o_ref[rows, 0:LRU_WIDTH] = y_lru.astype(o_ref.dtype)
        cx = rest_ref[rows, col_sc:col_sc + SC_WIDTH] * rest_ref[rows, col_sx:col_sx + SC_WIDTH]
        y_sc = rest_ref[rows, col_sb:col_sb + SC_WIDTH] * _dwconv(cx, scw_ref, scb_ref, 1)
        o_ref[rows, LRU_WIDTH:LRU_WIDTH + SC_WIDTH] = y_sc.astype(o_ref.dtype)
    if not need_ctx:
        o_ref[0:CTX_LEN, :] = jnp.zeros((CTX_LEN, LRU_WIDTH + SC_WIDTH), o_ref.dtype)


def _block_diag_gates(rg_w):
    eye = jnp.eye(LRU_HEADS, dtype=rg_w.dtype)
    full = jnp.einsum('dgncm,nk->dgnckm', rg_w, eye)
    full = full.reshape(2, 2, LRU_WIDTH, LRU_WIDTH)
    return full.transpose(2, 0, 1, 3).reshape(LRU_WIDTH, 4 * LRU_WIDTH)


def _lru_sconv(rest, lcw, lcb, rg_w, rg_b, rg_lam, scw, scb, bsz, need_ctx):
    wbd = _block_diag_gates(rg_w).astype(jnp.bfloat16)
    const2 = lambda b: (0, 0)
    half = LRU_WIDTH // 2
    return pl.pallas_call(
        functools.partial(_lru_kernel, need_ctx=need_ctx),
        out_shape=jax.ShapeDtypeStruct((bsz, ROWS_PER_B, LRU_WIDTH + SC_WIDTH), jnp.bfloat16),
        grid=(bsz,),
        in_specs=[
            pl.BlockSpec((None, ROWS_PER_B, REST_WIDTH), lambda b: (b, 0, 0)),
            pl.BlockSpec(lcw.shape, const2),
            pl.BlockSpec((1, LRU_WIDTH), const2),
            pl.BlockSpec((LRU_WIDTH, 4 * LRU_WIDTH), const2),
            pl.BlockSpec((1, 4 * LRU_WIDTH), const2),
            pl.BlockSpec((2, LRU_WIDTH), const2),
            pl.BlockSpec(scw.shape, const2),
            pl.BlockSpec((1, SC_WIDTH), const2),
        ],
        out_specs=pl.BlockSpec((None, ROWS_PER_B, LRU_WIDTH + SC_WIDTH), lambda b: (b, 0, 0)),
        scratch_shapes=[
            pltpu.VMEM((ROWS_PER_B, LRU_WIDTH), jnp.float32),
            pltpu.VMEM((2, SEQ, half), jnp.float32),
            pltpu.VMEM((2, SEQ, half), jnp.float32),
            pltpu.VMEM((2, SEQ, half), jnp.float32),
            pltpu.VMEM((2, SEQ, half), jnp.float32),
            pltpu.VMEM((ROWS_PER_B, LRU_WIDTH), jnp.float32),
        ],
        compiler_params=_cparams(("arbitrary",)),
        name="lru_sconv",
    )(rest, lcw, lcb.reshape(1, LRU_WIDTH), wbd, rg_b.reshape(1, 4 * LRU_WIDTH), rg_lam, scw,
      scb.reshape(1, SC_WIDTH))


def _out_kernel(*refs, two_src):
    if two_src:
        (ctx_ref, x_ref, yna_ref, yls_ref, mod_ref, og_ref, w_ref, fg_ref, wr_ref, wrt_ref, br_ref, brt_ref, _,
         xo_ref, pos_ref, meta_ref, xs_ref, cnt_s, cur_s, alloc_s, tb_s, row_s, pos_v, pos_sm, sem_p, sem_r) = refs
        j = pl.program_id(0) % TILES_PER_B
        x = jnp.where(j == 0, ctx_ref[...], x_ref[...])
    else:
        (x_ref, yna_ref, yls_ref, mod_ref, og_ref, w_ref, fg_ref, wr_ref, wrt_ref, br_ref, brt_ref, _,
         xo_ref, pos_ref, meta_ref, xs_ref, cnt_s, cur_s, alloc_s, tb_s, row_s, pos_v, pos_sm, sem_p, sem_r) = refs
        x = x_ref[...]
    step = pl.program_id(0)
    n_steps = pl.num_programs(0)
    buf_slot = step % 2

    def row_copy(buf_slot, k, dst_row):
        return pltpu.make_async_copy(row_s.at[buf_slot, pl.ds(k, 1), :], xs_ref.at[pl.ds(dst_row, 1), :],
                                     sem_r.at[buf_slot])

    def drain(buf_slot):
        def body(k, c):
            row_copy(buf_slot, 0, 0).wait()
            return c
        lax.fori_loop(0, TM, body, 0, unroll=8)

    @pl.when(step == 0)
    def _():
        cnt_s[...] = jnp.zeros_like(cnt_s)
        cur_s[...] = jnp.zeros_like(cur_s)
        alloc_s[...] = jnp.zeros_like(alloc_s)
        tb_s[...] = jnp.zeros_like(tb_s)

    @pl.when(step >= 2)
    def _():
        drain(buf_slot)

    yna = yna_ref[...].astype(jnp.float32)
    yls = yls_ref[...].astype(jnp.float32)
    merged = jnp.concatenate([
        _rms(yna, og_ref[:, :NA_WIDTH]),
        _rms(yls[:, :LRU_WIDTH], og_ref[:, NA_WIDTH:NA_WIDTH + LRU_WIDTH]),
        _rms(yls[:, LRU_WIDTH:], og_ref[:, NA_WIDTH + LRU_WIDTH:]),
    ], axis=-1).astype(jnp.bfloat16)
    y = jnp.dot(merged, w_ref[...], preferred_element_type=jnp.float32)
    x_new = x + mod_ref[2:3, :] * y
    xo_ref[...] = x_new
    h2 = _rms(x_new, fg_ref[...]) * (1.0 + mod_ref[4:5, :]) + mod_ref[3:4, :]

    logits = jnp.dot(h2, wr_ref[...], precision=_HI, preferred_element_type=jnp.float32) + br_ref[...]
    ex = jnp.exp(logits - jnp.max(logits, axis=-1, keepdims=True))
    probs = ex / jnp.sum(ex, axis=-1, keepdims=True)
    row_s[buf_slot, :, :D_MODEL] = h2
    row_s[buf_slot, :, D_MODEL:] = probs

    lt = lax.dot_general(wrt_ref[...], h2, _NT, precision=_HI, preferred_element_type=jnp.float32) + brt_ref[...]
    et = jnp.exp(lt - jnp.max(lt, axis=0, keepdims=True))
    pe = [et[e:e + 1, :] for e in range(N_EXPERTS)]

    def top2_sum(v):
        best = v[0] + v[1]
        for a in range(E_PER_GROUP):
            for b in range(a + 1, E_PER_GROUP):
                if (a, b) != (0, 1):
                    best = jnp.maximum(best, v[a] + v[b])
        return best

    score = [top2_sum(pe[g * E_PER_GROUP:(g + 1) * E_PER_GROUP]) for g in range(N_GROUPS)]
    g_best, g_sel = score[0], jnp.zeros((1, TM), jnp.int32)
    for g in range(1, N_GROUPS):
        upd = score[g] > g_best
        g_sel = jnp.where(upd, g, g_sel)
        g_best = jnp.where(upd, score[g], g_best)
    p_in = []
    for k in range(E_PER_GROUP):
        v = pe[k]
        for g in range(1, N_GROUPS):
            v = jnp.where(g_sel == g, pe[g * E_PER_GROUP + k], v)
        p_in.append(v)
    m1, i1 = p_in[0], jnp.zeros((1, TM), jnp.int32)
    for k in range(1, E_PER_GROUP):
        upd = p_in[k] > m1
        i1 = jnp.where(upd, k, i1)
        m1 = jnp.where(upd, p_in[k], m1)
    m2, i2 = jnp.full((1, TM), -1.0, jnp.float32), jnp.zeros((1, TM), jnp.int32)
    for k in range(E_PER_GROUP):
        cand = jnp.where(i1 == k, -2.0, p_in[k])
        upd = cand > m2
        i2 = jnp.where(upd, k, i2)
        m2 = jnp.where(upd, cand, m2)
    lo, hi = jnp.minimum(i1, i2), jnp.maximum(i1, i2)
    pair = jnp.where(lo == 0, hi - 1, jnp.where(lo == 1, hi + 1, N_PAIRS - 1))
    bucket = g_sel * N_PAIRS + pair

    b_iota = lax.broadcasted_iota(jnp.int32, (BUCKET_ROWS, TM), 0)
    onehot = b_iota == bucket
    tri = (lax.broadcasted_iota(jnp.int32, (TM, TM), 0) <= lax.broadcasted_iota(jnp.int32, (TM, TM), 1))
    as_bf16 = lambda mask: jnp.where(mask, 1.0, 0.0).astype(jnp.bfloat16)
    cum = jnp.dot(as_bf16(onehot), as_bf16(tri), preferred_element_type=jnp.float32)
    cnt_new = cum[:, TM - 1:TM].astype(jnp.int32)
    cnt_old = cnt_s[:, 0:1]
    cur = cur_s[:, 0:1]
    alloc = alloc_s[0:1, 0:1]
    shift = TM.bit_length() - 1
    q_last = (cnt_old + cnt_new - 1) >> shift
    q_prev = (cnt_old - 1) >> shift
    opens = jnp.where(cnt_new > 0, q_last - q_prev, 0)
    lower = (lax.broadcasted_iota(jnp.int32, (BUCKET_ROWS, BUCKET_ROWS), 1)
             < lax.broadcasted_iota(jnp.int32, (BUCKET_ROWS, BUCKET_ROWS), 0))
    opens_b = jnp.broadcast_to(opens, (BUCKET_ROWS, LANES)).astype(jnp.float32).astype(jnp.bfloat16)
    before = jnp.dot(as_bf16(lower), opens_b, preferred_element_type=jnp.float32)[:, 0:1].astype(jnp.int32)
    new_id = alloc + before
    rank = cnt_old + cum.astype(jnp.int32) - 1
    tile_id = jnp.where((opens > 0) & ((rank >> shift) == q_last), new_id, cur)
    slot = tile_id * TM + (rank & (TM - 1))
    pos = jnp.sum(jnp.where(onehot, slot, 0).astype(jnp.float32), axis=0, keepdims=True).astype(jnp.int32)
    pos_ref[...] = pos

    pos_v[...] = jnp.broadcast_to(pos, pos_v.shape)
    to_smem = pltpu.make_async_copy(pos_v, pos_sm, sem_p)
    to_smem.start()
    to_smem.wait()

    def issue(k, c):
        row_copy(buf_slot, k, pos_sm[0, k]).start()
        return c
    lax.fori_loop(0, TM, issue, 0, unroll=8)

    @pl.when(step == n_steps - 1)
    def _():
        drain(buf_slot)
        drain(1 - buf_slot)

    lane_id = lax.broadcasted_iota(jnp.int32, (BUCKET_ROWS, TM), 1)
    opened_here = (opens > 0) & (new_id == lane_id)
    opened_bucket = jnp.max(jnp.where(opened_here, b_iota, -1).astype(jnp.float32), axis=0,
                            keepdims=True).astype(jnp.int32)
    tb = jnp.where(opened_bucket >= 0, opened_bucket, tb_s[0:1, :])
    alloc_new = alloc + jnp.sum(opens.astype(jnp.float32), axis=0, keepdims=True).astype(jnp.int32)
    tb_s[...] = jnp.broadcast_to(tb, tb_s.shape)
    cnt_s[...] = jnp.broadcast_to(cnt_old + cnt_new, cnt_s.shape)
    cur_s[...] = jnp.broadcast_to(jnp.where(opens > 0, new_id, cur), cur_s.shape)
    alloc_s[...] = jnp.broadcast_to(alloc_new, alloc_s.shape)
    meta_ref[0:1, :] = tb
    meta_ref[1:2, :] = jnp.broadcast_to(alloc_new, (1, TM))
    meta_ref[2:8, :] = jnp.zeros((6, TM), jnp.int32)


def _out_proj(srcs, yna, yls, mod, out_g, w_bf16, ffn_g, w_router, b_router, bsz, with_ctx):
    two_src = len(srcs) == 2
    n_tiles, bj, mod_row = _tile_maps(bsz, with_ctx)
    assert n_tiles >= 2
    n_sorted = n_tiles + N_BUCKETS
    tile = lambda width: pl.BlockSpec((None, TM, width), lambda i: (bj(i)[0], bj(i)[1], 0))
    const2 = lambda i: (0, 0)
    if two_src:
        src_specs = [
            pl.BlockSpec((None, TM, D_MODEL), lambda i: (bj(i)[0], 0, 0)),
            pl.BlockSpec((None, TM, D_MODEL), lambda i: (bj(i)[0], jnp.maximum(bj(i)[1] - 1, 0), 0)),
        ]
    else:
        src_specs = [tile(D_MODEL)]
    wr = jnp.zeros((D_MODEL, LANES), jnp.float32).at[:, :N_EXPERTS].set(w_router)
    br = jnp.full((1, LANES), NEG_INF, jnp.float32).at[0, :N_EXPERTS].set(b_router)
    xs_init = jnp.zeros((n_sorted * TM, ROW_WIDTH), jnp.float32)
    return pl.pallas_call(
        functools.partial(_out_kernel, two_src=two_src),
        out_shape=[
            jax.ShapeDtypeStruct((bsz, ROWS_PER_B, D_MODEL), jnp.float32),
            jax.ShapeDtypeStruct((n_tiles, 1, TM), jnp.int32),
            jax.ShapeDtypeStruct((8, TM), jnp.int32),
            jax.ShapeDtypeStruct(xs_init.shape, jnp.float32),
        ],
        grid=(n_tiles,),
        in_specs=src_specs + [
            tile(NA_WIDTH),
            tile(LRU_WIDTH + SC_WIDTH),
            pl.BlockSpec((None, N_MOD, D_MODEL), lambda i: (mod_row(i), 0, 0)),
            pl.BlockSpec((1, D_MODEL), const2),
            pl.BlockSpec((D_MODEL, D_MODEL), const2),
            pl.BlockSpec((1, D_MODEL), const2),
            pl.BlockSpec((D_MODEL, LANES), const2),
            pl.BlockSpec((N_EXPERTS, D_MODEL), const2),
            pl.BlockSpec((1, LANES), const2),
            pl.BlockSpec((N_EXPERTS, 1), const2),
            pl.BlockSpec(memory_space=pl.ANY),
        ],
        out_specs=[
            tile(D_MODEL),
            pl.BlockSpec((None, 1, TM), lambda i: (i, 0, 0)),
            pl.BlockSpec((8, TM), const2),
            pl.BlockSpec(memory_space=pl.ANY),
        ],
        scratch_shapes=[
            pltpu.VMEM((BUCKET_ROWS, LANES), jnp.int32),
            pltpu.VMEM((BUCKET_ROWS, LANES), jnp.int32),
            pltpu.VMEM((8, LANES), jnp.int32),
            pltpu.VMEM((8, TM), jnp.int32),
            pltpu.VMEM((2, TM, ROW_WIDTH), jnp.float32),
            pltpu.VMEM((8, TM), jnp.int32),
            pltpu.SMEM((8, TM), jnp.int32),
            pltpu.SemaphoreType.DMA(()),
            pltpu.SemaphoreType.DMA((2,)),
        ],
        input_output_aliases={len(srcs) + 10: 3},
        compiler_params=_cparams(("arbitrary",)),
        name="out_proj_route",
    )(*srcs, yna, yls, mod, out_g.reshape(1, D_MODEL), w_bf16, ffn_g.reshape(1, D_MODEL), wr, w_router.T, br,
      b_router.reshape(N_EXPERTS, 1), xs_init)


def _moe_kernel(src_ref, e0_ref, e1_ref, used_ref, xs_ref, wg0, wu0, wd0, wg1, wu1, wd1, o_ref, wg_s, wu_s, wd_s):
    n = pl.program_id(0)
    prev = jnp.maximum(n - 1, 0)
    e0, e1 = e0_ref[n], e1_ref[n]
    fresh = (n == 0) | (e0 != e0_ref[prev]) | (e1 != e1_ref[prev])

    @pl.when(fresh)
    def _():
        for k, (g, u, d) in enumerate(((wg0, wu0, wd0), (wg1, wu1, wd1))):
            wg_s[k] = g[...].astype(jnp.bfloat16)
            wu_s[k] = u[...].astype(jnp.bfloat16)
            wd_s[k] = d[...].astype(jnp.bfloat16)

    @pl.when(n < used_ref[0])
    def _():
        xb = xs_ref[:, :D_MODEL].astype(jnp.bfloat16)
        probs = xs_ref[:, D_MODEL:]
        lane = lax.broadcasted_iota(jnp.int32, probs.shape, 1)
        p0 = jnp.sum(jnp.where(lane == e0, probs, 0.0), axis=-1, keepdims=True)
        p1 = jnp.sum(jnp.where(lane == e1, probs, 0.0), axis=-1, keepdims=True)
        total = p0 + p1
        total = jnp.where(total > 0.0, total, 1.0)
        out = jnp.zeros((TM, D_MODEL), jnp.float32)
        for k, wk in enumerate((p0 / total, p1 / total)):
            gate = jnp.dot(xb, wg_s[k], preferred_element_type=jnp.float32)
            up = jnp.dot(xb, wu_s[k], preferred_element_type=jnp.float32)
            hid = (gate * jax.nn.sigmoid(gate)) * up
            out = out + wk * jnp.dot(hid.astype(jnp.bfloat16), wd_s[k], preferred_element_type=jnp.float32)
        o_ref[...] = out


def _moe(sched, xs, w_gate, w_up, w_down):
    n_tiles = xs.shape[0] // TM
    src, e0, e1, used = sched
    first = lambda n, s, a, b, u: (a[n], 0, 0)
    second = lambda n, s, a, b, u: (b[n], 0, 0)
    gate_spec = lambda m: pl.BlockSpec((None, D_MODEL, D_EXPERT), m)
    down_spec = lambda m: pl.BlockSpec((None, D_EXPERT, D_MODEL), m)
    return pl.pallas_call(
        _moe_kernel,
        out_shape=jax.ShapeDtypeStruct((n_tiles * TM, D_MODEL), jnp.float32),
        grid_spec=pltpu.PrefetchScalarGridSpec(
            num_scalar_prefetch=4,
            grid=(n_tiles,),
            in_specs=[
                pl.BlockSpec((TM, ROW_WIDTH), lambda n, s, a, b, u: (s[n], 0)),
                gate_spec(first), gate_spec(first), down_spec(first),
                gate_spec(second), gate_spec(second), down_spec(second),
            ],
            out_specs=pl.BlockSpec((TM, D_MODEL), lambda n, s, a, b, u: (s[n], 0)),
            scratch_shapes=[
                pltpu.VMEM((2, D_MODEL, D_EXPERT), jnp.bfloat16),
                pltpu.VMEM((2, D_MODEL, D_EXPERT), jnp.bfloat16),
                pltpu.VMEM((2, D_EXPERT, D_MODEL), jnp.bfloat16),
            ],
        ),
        compiler_params=_cparams(("arbitrary",)),
        name="moe_experts",
    )(src, e0, e1, used, xs, w_gate, w_up, w_down, w_gate, w_up, w_down)


def _schedule(meta, n_tiles):
    tile_bucket = meta[0, :n_tiles]
    used = meta[1, 0]
    ids = jnp.arange(n_tiles, dtype=jnp.int32)
    order = jnp.argsort(jnp.where(ids < used, tile_bucket, N_BUCKETS), stable=True).astype(jnp.int32)
    src = jnp.where(ids < used, order, order[jnp.maximum(used - 1, 0)])
    bucket = tile_bucket[src]
    group, pair = bucket // N_PAIRS, bucket % N_PAIRS
    lo = jnp.array([0, 0, 0, 1, 1, 2], jnp.int32)[pair]
    hi = jnp.array([1, 2, 3, 2, 3, 3], jnp.int32)[pair]
    return src, group * E_PER_GROUP + lo, group * E_PER_GROUP + hi, used.reshape(1)


def _combine_kernel(pos_ref, x_ref, mod_ref, fg_ref, ys_ref, o_ref, buf, sem, *, final):
    i = pl.program_id(0)
    n = pl.num_programs(0)

    def row_copy(src_row, slot, k):
        return pltpu.make_async_copy(ys_ref.at[pl.ds(src_row, 1), :], buf.at[slot, pl.ds(k, 1), :], sem.at[slot])

    def issue(tile, slot):
        def body(k, c):
            row_copy(pos_ref[tile * TM + k], slot, k).start()
            return c
        lax.fori_loop(0, TM, body, 0, unroll=8)

    @pl.when(i == 0)
    def _():
        issue(0, 0)

    @pl.when(i + 1 < n)
    def _():
        issue(i + 1, (i + 1) % 2)

    slot = i % 2

    def drain(k, c):
        row_copy(0, slot, 0).wait()
        return c
    lax.fori_loop(0, TM, drain, 0, unroll=8)

    x_new = x_ref[...] + mod_ref[5:6, :] * buf[slot]
    if final:
        x_new = _rms(x_new, fg_ref[...])
    o_ref[...] = x_new


def _combine(pos, x_all, mod, final_g, ys, bsz, with_ctx, final):
    n_tiles, bj, mod_row = _tile_maps(bsz, with_ctx)
    if final:
        out_shape = jax.ShapeDtypeStruct((bsz, SEQ, D_MODEL), jnp.float32)
        out_spec = pl.BlockSpec((None, TM, D_MODEL), lambda i, p: (bj(i)[0], bj(i)[1] - 1, 0))
    else:
        out_shape = jax.ShapeDtypeStruct((bsz, ROWS_PER_B, D_MODEL), jnp.float32)
        out_spec = pl.BlockSpec((None, TM, D_MODEL), lambda i, p: (bj(i)[0], bj(i)[1], 0))
    return pl.pallas_call(
        functools.partial(_combine_kernel, final=final),
        out_shape=out_shape,
        grid_spec=pltpu.PrefetchScalarGridSpec(
            num_scalar_prefetch=1,
            grid=(n_tiles,),
            in_specs=[
                pl.BlockSpec((None, TM, D_MODEL), lambda i, p: (bj(i)[0], bj(i)[1], 0)),
                pl.BlockSpec((None, N_MOD, D_MODEL), lambda i, p: (mod_row(i), 0, 0)),
                pl.BlockSpec((1, D_MODEL), lambda i, p: (0, 0)),
                pl.BlockSpec(memory_space=pl.ANY),
            ],
            out_specs=out_spec,
            scratch_shapes=[pltpu.VMEM((2, TM, D_MODEL), jnp.float32), pltpu.SemaphoreType.DMA((2,))],
        ),
        compiler_params=_cparams(("arbitrary",)),
        name="combine",
    )(pos.reshape(-1), x_all, mod, final_g.reshape(1, D_MODEL), ys)


def kernel(x, c, ctx, c_ctx, w_ada, b_ada, norm_mix_g, w_in, lru_conv_w, lru_conv_b, rg_w, rg_b, rg_lam, na_rpb,
           sc_conv_w, sc_conv_b, mix_out_g, w_out, norm_ffn_g, w_router, b_router, w_gate, w_up, w_down, final_g):
    bsz = x.shape[0]
    mod_rows = -(-(bsz + 1) // 8) * 8
    c_all = jnp.zeros((mod_rows, D_MODEL), jnp.float32).at[:bsz].set(c).at[bsz].set(c_ctx)
    mods = _ada(c_all, w_ada, b_ada).reshape(DEPTH, mod_rows, N_MOD, D_MODEL)

    x_all = None
    out = None
    for l in range(DEPTH):
        need_ctx = l < DEPTH - 1
        mod = mods[l]
        srcs = (ctx, x) if l == 0 else (x_all,)
        qkv, rest = _in_proj(srcs, norm_mix_g[l], mod, w_in[l].astype(jnp.bfloat16), bsz)
        yna = _attention(qkv, _attn_bias_table(na_rpb[l]), bsz, need_ctx)
        yls = _lru_sconv(rest, lru_conv_w[l], lru_conv_b[l], rg_w[l], rg_b[l], rg_lam[l], sc_conv_w[l],
                         sc_conv_b[l], bsz, need_ctx)
        x_mid, pos, meta, xs = _out_proj(srcs, yna, yls, mod, mix_out_g[l], w_out[l].astype(jnp.bfloat16),
                                         norm_ffn_g[l], w_router, b_router, bsz, need_ctx)
        ys = _moe(_schedule(meta, xs.shape[0] // TM), xs, w_gate[l], w_up[l], w_down[l])
        res = _combine(pos, x_mid, mod, final_g, ys, bsz, need_ctx, final=not need_ctx)
        if need_ctx:
            x_all = res
        else:
            out = res
    return out
```

```python
import functools

import jax
import jax.numpy as jnp
from jax import lax
from jax.experimental import pallas as pl
from jax.experimental.pallas import tpu as pltpu

D_MODEL = 1024
SEQ = 2048
CTX_LEN = 256
ROWS_PER_B = CTX_LEN + SEQ
DEPTH = 2
N_MOD = 6
EPS = 1e-6
NEG_INF = -1e30

GRID_W = 64
GRID_ROWS = SEQ // GRID_W
HEAD_DIM = 64
NA_WIDTH = 512
NA_HEADS = 8
LRU_WIDTH = 256
LRU_HEADS = 4
LRU_BLOCK = 64
SC_WIDTH = 256
QKV_WIDTH = 3 * NA_WIDTH
REST_WIDTH = 2 * LRU_WIDTH + 3 * SC_WIDTH
IN_WIDTH = QKV_WIDTH + REST_WIDTH
RG_C = 8.0
WIN_R = 8
WIN_C = 16
N_EXPERTS = 16
N_GROUPS = 4
E_PER_GROUP = 4
N_PAIRS = 6
N_BUCKETS = N_GROUPS * N_PAIRS
D_EXPERT = 512

TM = 256
TILES_PER_B = ROWS_PER_B // TM
LAT_TILES_PER_B = SEQ // TM
LANES = 128
BUCKET_ROWS = 32
ROW_WIDTH = D_MODEL + LANES
HEADS_PER_STACK = 4
STACK_W = HEADS_PER_STACK * HEAD_DIM
SCAN_SEGS = 8
VMEM_LIMIT = 56 * 1024 * 1024

_HI = lax.Precision.HIGHEST
_NT = (((1,), (1,)), ((), ()))


def _cparams(sem):
    return pltpu.CompilerParams(dimension_semantics=sem, vmem_limit_bytes=VMEM_LIMIT)


def _rms(v, g):
    return v * lax.rsqrt(jnp.mean(v * v, axis=-1, keepdims=True) + EPS) * g


def _ada_kernel(c_ref, w_ref, b_ref, o_ref):
    cond = c_ref[...]
    cond = cond * jax.nn.sigmoid(cond)
    o_ref[0] = jnp.dot(cond, w_ref[0], precision=_HI, preferred_element_type=jnp.float32) + b_ref[0]


def _ada(c_all, w_ada, b_ada):
    depth, _, width = w_ada.shape
    rows = c_all.shape[0]
    tn = 1536
    return pl.pallas_call(
        _ada_kernel,
        out_shape=jax.ShapeDtypeStruct((depth, rows, width), jnp.float32),
        grid=(depth, width // tn),
        in_specs=[
            pl.BlockSpec((rows, D_MODEL), lambda l, n: (0, 0)),
            pl.BlockSpec((1, D_MODEL, tn), lambda l, n: (l, 0, n)),
            pl.BlockSpec((1, 1, tn), lambda l, n: (l, 0, n)),
        ],
        out_specs=pl.BlockSpec((1, rows, tn), lambda l, n: (l, 0, n)),
        compiler_params=_cparams(("arbitrary", "arbitrary")),
        name="ada",
    )(c_all, w_ada, b_ada.reshape(depth, 1, width))


def _tile_maps(bsz, with_ctx):
    per_b = TILES_PER_B if with_ctx else LAT_TILES_PER_B
    off = 0 if with_ctx else 1

    def bj(i):
        return i // per_b, i % per_b + off

    def mod_row(i):
        b, j = bj(i)
        return jnp.where(j == 0, bsz, b)

    return per_b * bsz, bj, mod_row


def _in_kernel(*refs, two_src):
    if two_src:
        ctx_ref, x_ref, g_ref, mod_ref, w_ref, qkv_ref, rest_ref = refs
        j = pl.program_id(0) % TILES_PER_B
        x = jnp.where(j == 0, ctx_ref[...], x_ref[...])
    else:
        x_ref, g_ref, mod_ref, w_ref, qkv_ref, rest_ref = refs
        x = x_ref[...]
    h = _rms(x, g_ref[...]) * (1.0 + mod_ref[1:2, :]) + mod_ref[0:1, :]
    hb = h.astype(jnp.bfloat16)
    qkv_ref[...] = jnp.dot(hb, w_ref[:, :QKV_WIDTH], preferred_element_type=jnp.float32).astype(jnp.bfloat16)
    rest_ref[...] = jnp.dot(hb, w_ref[:, QKV_WIDTH:], preferred_element_type=jnp.float32)


def _in_proj(srcs, g, mod, w_bf16, bsz):
    two_src = len(srcs) == 2
    n_tiles, bj, mod_row = _tile_maps(bsz, True)
    if two_src:
        src_specs = [
            pl.BlockSpec((None, TM, D_MODEL), lambda i: (bj(i)[0], 0, 0)),
            pl.BlockSpec((None, TM, D_MODEL), lambda i: (bj(i)[0], jnp.maximum(bj(i)[1] - 1, 0), 0)),
        ]
    else:
        src_specs = [pl.BlockSpec((None, TM, D_MODEL), lambda i: (bj(i)[0], bj(i)[1], 0))]
    return pl.pallas_call(
        functools.partial(_in_kernel, two_src=two_src),
        out_shape=[
            jax.ShapeDtypeStruct((bsz, ROWS_PER_B, QKV_WIDTH), jnp.bfloat16),
            jax.ShapeDtypeStruct((bsz, ROWS_PER_B, REST_WIDTH), jnp.float32),
        ],
        grid=(n_tiles,),
        in_specs=src_specs + [
            pl.BlockSpec((1, D_MODEL), lambda i: (0, 0)),
            pl.BlockSpec((None, N_MOD, D_MODEL), lambda i: (mod_row(i), 0, 0)),
            pl.BlockSpec((D_MODEL, IN_WIDTH), lambda i: (0, 0)),
        ],
        out_specs=[
            pl.BlockSpec((None, TM, QKV_WIDTH), lambda i: (bj(i)[0], bj(i)[1], 0)),
            pl.BlockSpec((None, TM, REST_WIDTH), lambda i: (bj(i)[0], bj(i)[1], 0)),
        ],
        compiler_params=_cparams(("arbitrary",)),
        name="in_proj",
    )(*srcs, g.reshape(1, D_MODEL), mod, w_bf16)


def _attn_kernel(q_ref, k_ref, v_ref, bias_ref, o_ref, *, need_ctx):
    lane_head = lax.broadcasted_iota(jnp.int32, (1, STACK_W), 1) // HEAD_DIM
    n_stacks = NA_WIDTH // STACK_W

    def stack_q(qg):
        zero = jnp.zeros_like(qg)
        return jnp.concatenate([jnp.where(lane_head == h, qg, zero) for h in range(HEADS_PER_STACK)], axis=0)

    def unstack(o):
        out = jnp.zeros((GRID_W, STACK_W), jnp.float32)
        for h in range(HEADS_PER_STACK):
            out = out + jnp.where(lane_head == h, o[h * GRID_W:(h + 1) * GRID_W], 0.0)
        return out

    def attend(q_rows, s, local):
        cols = slice(s * STACK_W, (s + 1) * STACK_W)
        qg = q_ref[pl.ds(q_rows, GRID_W), cols] * jnp.bfloat16(HEAD_DIM ** -0.5)
        qs = stack_q(qg)
        kc = k_ref[0:CTX_LEN, cols]
        vc = v_ref[0:CTX_LEN, cols]
        s_ctx = lax.dot_general(qs, kc, _NT, preferred_element_type=jnp.float32)
        m = jnp.max(s_ctx, axis=-1, keepdims=True)
        if local is not None:
            k_rows, delta = local
            kw = k_ref[pl.ds(k_rows, WIN_R * GRID_W), cols]
            vw = v_ref[pl.ds(k_rows, WIN_R * GRID_W), cols]
            bias = bias_ref[delta, s * HEADS_PER_STACK:(s + 1) * HEADS_PER_STACK]
            s_loc = lax.dot_general(qs, kw, _NT, preferred_element_type=jnp.float32)
            s_loc = s_loc + bias.reshape(HEADS_PER_STACK * GRID_W, WIN_R * GRID_W)
            m = jnp.maximum(m, jnp.max(s_loc, axis=-1, keepdims=True))
            p_loc = jnp.exp(s_loc - m)
        p_ctx = jnp.exp(s_ctx - m)
        denom = jnp.sum(p_ctx, axis=-1, keepdims=True)
        o = jnp.dot(p_ctx.astype(jnp.bfloat16), vc, preferred_element_type=jnp.float32)
        if local is not None:
            denom = denom + jnp.sum(p_loc, axis=-1, keepdims=True)
            o = o + jnp.dot(p_loc.astype(jnp.bfloat16), vw, preferred_element_type=jnp.float32)
        o_ref[pl.ds(q_rows, GRID_W), cols] = unstack(o / denom).astype(o_ref.dtype)

    def lat_row(r, carry):
        r0 = jnp.clip(r - WIN_R // 2, 0, GRID_ROWS - WIN_R)
        q_rows = pl.multiple_of(CTX_LEN + r * GRID_W, GRID_W)
        k_rows = pl.multiple_of(CTX_LEN + r0 * GRID_W, GRID_W)
        for s in range(n_stacks):
            attend(q_rows, s, (k_rows, r - r0))
        return carry

    lax.fori_loop(0, GRID_ROWS, lat_row, 0)

    if need_ctx:
        def ctx_chunk(cq, carry):
            q_rows = pl.multiple_of(cq * GRID_W, GRID_W)
            for s in range(n_stacks):
                attend(q_rows, s, None)
            return carry

        lax.fori_loop(0, CTX_LEN // GRID_W, ctx_chunk, 0)
    else:
        o_ref[0:CTX_LEN, :] = jnp.zeros((CTX_LEN, NA_WIDTH), o_ref.dtype)


def _attn_bias_table(rpb):
    n_rel_c = 2 * WIN_C - 1
    lead = GRID_W - WIN_C
    padded = jnp.pad(rpb, ((0, 0), (0, 0), (lead, 2 * GRID_W - lead - n_rel_c)))
    skew = jnp.tile(padded, (1, 1, GRID_W))[..., :GRID_W * (2 * GRID_W - 1)]
    skew = skew.reshape(NA_HEADS, 2 * WIN_R - 1, GRID_W, 2 * GRID_W - 1)
    toeplitz = skew[..., GRID_W - 1:]
    q_col = jnp.arange(GRID_W)[:, None]
    k_col = jnp.arange(GRID_W)[None, :]
    c_start = jnp.clip(q_col - WIN_C // 2, 0, GRID_W - WIN_C)
    ok = (k_col >= c_start) & (k_col < c_start + WIN_C)
    toeplitz = jnp.where(ok, toeplitz, NEG_INF)
    per_delta = [toeplitz[:, WIN_R - 1 - d:2 * WIN_R - 1 - d] for d in range(WIN_R)]
    tab = jnp.stack(per_delta, axis=0)
    return tab.transpose(0, 1, 3, 2, 4).reshape(WIN_R, NA_HEADS, GRID_W, WIN_R * GRID_W)


def _attention(qkv, bias, bsz, need_ctx):
    return pl.pallas_call(
        functools.partial(_attn_kernel, need_ctx=need_ctx),
        out_shape=jax.ShapeDtypeStruct((bsz, ROWS_PER_B, NA_WIDTH), jnp.bfloat16),
        grid=(bsz,),
        in_specs=[
            pl.BlockSpec((None, ROWS_PER_B, NA_WIDTH), lambda b: (b, 0, 0)),
            pl.BlockSpec((None, ROWS_PER_B, NA_WIDTH), lambda b: (b, 0, 1)),
            pl.BlockSpec((None, ROWS_PER_B, NA_WIDTH), lambda b: (b, 0, 2)),
            pl.BlockSpec((WIN_R, NA_HEADS, GRID_W, WIN_R * GRID_W), lambda b: (0, 0, 0, 0)),
        ],
        out_specs=pl.BlockSpec((None, ROWS_PER_B, NA_WIDTH), lambda b: (b, 0, 0)),
        compiler_params=_cparams(("arbitrary",)),
        name="attention",
    )(qkv, qkv, qkv, bias)


def _shift_rows(x, offset):
    length = x.shape[0]
    if offset == 0:
        return x
    rolled = pltpu.roll(x, (-offset) % length, axis=0)
    row = lax.broadcasted_iota(jnp.int32, x.shape, 0)
    ok = (row + offset >= 0) & (row + offset < length)
    return jnp.where(ok, rolled, 0.0)


def _dwconv(x, w_ref, b_ref, left):
    width = w_ref.shape[0]
    y = _shift_rows(x, -left) * w_ref[0:1, :] + b_ref[...]
    for k in range(1, width):
        y = y + _shift_rows(x, k - left) * w_ref[k:k + 1, :]
    return y


def _lru_kernel(rest_ref, lcw_ref, lcb_ref, wbd_ref, rgb_ref, lam_ref, scw_ref, scb_ref, o_ref,
                xc_s, a_s, b_s, hl_s, p_s, y_s, *, need_ctx):
    half = LRU_WIDTH // 2
    col_rx, col_rg, col_sb, col_sc, col_sx = (k * LRU_WIDTH for k in range(5))
    segments = ((0, CTX_LEN), (CTX_LEN, SEQ))

    for start, length in segments:
        xc_s[start:start + length, :] = _dwconv(rest_ref[start:start + length, col_rx:col_rx + LRU_WIDTH],
                                                lcw_ref, lcb_ref, 2)

    def coeffs(d, start, length):
        chunk = 256
        sp = jax.nn.softplus(-lam_ref[d:d + 1, :])
        for c0 in range(0, length, chunk):
            xc = xc_s[start + c0:start + c0 + chunk, :]
            pre = jnp.dot(xc.astype(jnp.bfloat16), wbd_ref[:, 2 * d * LRU_WIDTH:(2 * d + 2) * LRU_WIDTH],
                          preferred_element_type=jnp.float32) + rgb_ref[:, 2 * d * LRU_WIDTH:(2 * d + 2) * LRU_WIDTH]
            gate_r = jax.nn.sigmoid(pre[:, :LRU_WIDTH])
            gate_i = jax.nn.sigmoid(pre[:, LRU_WIDTH:])
            log_a = -RG_C * gate_r * sp
            a = jnp.exp(log_a)
            bb = jnp.sqrt(-jnp.tanh(log_a) * (a * a + 1.0)) * (gate_i * xc)
            for c in range(2):
                a_s[c, c0:c0 + chunk, :] = a[:, c * half:(c + 1) * half]
                b_s[c, c0:c0 + chunk, :] = bb[:, c * half:(c + 1) * half]

    def scan(length, reverse, h0):
        seg = length // SCAN_SEGS

        def step(i, carry):
            t = seg - 1 - i if reverse else i
            rows = pl.ds(t, SCAN_SEGS, stride=seg)
            new = []
            for c in range(2):
                h, p = carry[2 * c], carry[2 * c + 1]
                a_t = a_s[c, rows, :]
                h = a_t * h + b_s[c, rows, :]
                p = a_t * p
                hl_s[c, rows, :] = h
                p_s[c, rows, :] = p
                new += [h, p]
            return tuple(new)

        zero = jnp.zeros((SCAN_SEGS, half), jnp.float32)
        one = jnp.ones((SCAN_SEGS, half), jnp.float32)
        ends = lax.fori_loop(0, seg, step, (zero, one, zero, one))
        finals = []
        order = range(SCAN_SEGS - 1, -1, -1) if reverse else range(SCAN_SEGS)
        for c in range(2):
            h_end, p_end = ends[2 * c], ends[2 * c + 1]
            carry = h0[c]
            for j in order:
                rows = slice(j * seg, (j + 1) * seg)
                hl_s[c, rows, :] = hl_s[c, rows, :] + p_s[c, rows, :] * carry
                carry = h_end[j:j + 1, :] + p_end[j:j + 1, :] * carry
            finals.append(carry)
        return finals

    zero_state = [jnp.zeros((1, half), jnp.float32)] * 2
    for d, reverse in enumerate((False, True)):
        (c_start, c_len), (l_start, l_len) = segments
        coeffs(d, c_start, c_len)
        state = scan(c_len, reverse, zero_state)
        if need_ctx:
            for c in range(2):
                cols = slice(c * half, (c + 1) * half)
                prev = 0.0 if d == 0 else y_s[c_start:c_start + c_len, cols]
                y_s[c_start:c_start + c_len, cols] = prev + hl_s[c, 0:c_len, :]
        coeffs(d, l_start, l_len)
        scan(l_len, reverse, state)
        for c in range(2):
            cols = slice(c * half, (c + 1) * half)
            prev = 0.0 if d == 0 else y_s[l_start:l_start + l_len, cols]
            y_s[l_start:l_start + l_len, cols] = prev + hl_s[c, 0:l_len, :]

    out_segments = segments if need_ctx else segments[1:]
    for start, length in out_segments:
        rows = slice(start, start + length)
        y_lru = y_s[rows, :] * jax.nn.gelu(rest_ref[rows, col_rg:col_rg + LRU_WIDTH])
        o_ref[rows, 0:LRU_WIDTH] = y_lru.astype(o_ref.dtype)
        cx = rest_ref[rows, col_sc:col_sc + SC_WIDTH] * rest_ref[rows, col_sx:col_sx + SC_WIDTH]
        y_sc = rest_ref[rows, col_sb:col_sb + SC_WIDTH] * _dwconv(cx, scw_ref, scb_ref, 1)
        o_ref[rows, LRU_WIDTH:LRU_WIDTH + SC_WIDTH] = y_sc.astype(o_ref.dtype)
    if not need_ctx:
        o_ref[0:CTX_LEN, :] = jnp.zeros((CTX_LEN, LRU_WIDTH + SC_WIDTH), o_ref.dtype)


def _block_diag_gates(rg_w):
    eye = jnp.eye(LRU_HEADS, dtype=rg_w.dtype)
    full = jnp.einsum('dgncm,nk->dgnckm', rg_w, eye)
    full = full.reshape(2, 2, LRU_WIDTH, LRU_WIDTH)
    return full.transpose(2, 0, 1, 3).reshape(LRU_WIDTH, 4 * LRU_WIDTH)


def _lru_sconv(rest, lcw, lcb, rg_w, rg_b, rg_lam, scw, scb, bsz, need_ctx):
    wbd = _block_diag_gates(rg_w).astype(jnp.bfloat16)
    const2 = lambda b: (0, 0)
    half = LRU_WIDTH // 2
    return pl.pallas_call(
        functools.partial(_lru_kernel, need_ctx=need_ctx),
        out_shape=jax.ShapeDtypeStruct((bsz, ROWS_PER_B, LRU_WIDTH + SC_WIDTH), jnp.bfloat16),
        grid=(bsz,),
        in_specs=[
            pl.BlockSpec((None, ROWS_PER_B, REST_WIDTH), lambda b: (b, 0, 0)),
            pl.BlockSpec(lcw.shape, const2),
            pl.BlockSpec((1, LRU_WIDTH), const2),
            pl.BlockSpec((LRU_WIDTH, 4 * LRU_WIDTH), const2),
            pl.BlockSpec((1, 4 * LRU_WIDTH), const2),
            pl.BlockSpec((2, LRU_WIDTH), const2),
            pl.BlockSpec(scw.shape, const2),
            pl.BlockSpec((1, SC_WIDTH), const2),
        ],
        out_specs=pl.BlockSpec((None, ROWS_PER_B, LRU_WIDTH + SC_WIDTH), lambda b: (b, 0, 0)),
        scratch_shapes=[
            pltpu.VMEM((ROWS_PER_B, LRU_WIDTH), jnp.float32),
            pltpu.VMEM((2, SEQ, half), jnp.float32),
            pltpu.VMEM((2, SEQ, half), jnp.float32),
            pltpu.VMEM((2, SEQ, half), jnp.float32),
            pltpu.VMEM((2, SEQ, half), jnp.float32),
            pltpu.VMEM((ROWS_PER_B, LRU_WIDTH), jnp.float32),
        ],
        compiler_params=_cparams(("arbitrary",)),
        name="lru_sconv",
    )(rest, lcw, lcb.reshape(1, LRU_WIDTH), wbd, rg_b.reshape(1, 4 * LRU_WIDTH), rg_lam, scw,
      scb.reshape(1, SC_WIDTH))


def _route(et):
    pe = [et[e:e + 1, :] for e in range(N_EXPERTS)]

    def top2_sum(v):
        best = v[0] + v[1]
        for a in range(E_PER_GROUP):
            for b in range(a + 1, E_PER_GROUP):
                if (a, b) != (0, 1):
                    best = jnp.maximum(best, v[a] + v[b])
        return best

    score = [top2_sum(pe[g * E_PER_GROUP:(g + 1) * E_PER_GROUP]) for g in range(N_GROUPS)]
    g_best, g_sel = score[0], jnp.zeros((1, TM), jnp.int32)
    for g in range(1, N_GROUPS):
        upd = score[g] > g_best
        g_sel = jnp.where(upd, g, g_sel)
        g_best = jnp.where(upd, score[g], g_best)
    p_in = []
    for k in range(E_PER_GROUP):
        v = pe[k]
        for g in range(1, N_GROUPS):
            v = jnp.where(g_sel == g, pe[g * E_PER_GROUP + k], v)
        p_in.append(v)
    m1, i1 = p_in[0], jnp.zeros((1, TM), jnp.int32)
    for k in range(1, E_PER_GROUP):
        upd = p_in[k] > m1
        i1 = jnp.where(upd, k, i1)
        m1 = jnp.where(upd, p_in[k], m1)
    m2, i2 = jnp.full((1, TM), -1.0, jnp.float32), jnp.zeros((1, TM), jnp.int32)
    for k in range(E_PER_GROUP):
        cand = jnp.where(i1 == k, -2.0, p_in[k])
        upd = cand > m2
        i2 = jnp.where(upd, k, i2)
        m2 = jnp.where(upd, cand, m2)
    lo, hi = jnp.minimum(i1, i2), jnp.maximum(i1, i2)
    pair = jnp.where(lo == 0, hi - 1, jnp.where(lo == 1, hi + 1, N_PAIRS - 1))
    bucket = g_sel * N_PAIRS + pair
    w1 = m1 / (m1 + m2)
    w2 = m2 / (m1 + m2)
    return bucket, jnp.where(i1 < i2, w1, w2), jnp.where(i1 < i2, w2, w1)


def _out_kernel(*refs, two_src):
    if two_src:
        ctx_ref, x_ref, *rest = refs
    else:
        x_ref, *rest = refs
    (yna_ref, yls_ref, mod_ref, og_ref, w_ref, fg_ref, wrt_ref, brt_ref, _, xo_ref, pos_ref, meta_ref, xs_ref,
     cnt_s, cur_s, alloc_s, tb_s, row_s, pos_v, pos_sm, sem_p, sem_r) = rest
    step = pl.program_id(0)
    n_steps = pl.num_programs(0)
    cur = step % 2

    def row_copy(buf, k, dst_row):
        return pltpu.make_async_copy(row_s.at[buf, pl.ds(k, 1), :], xs_ref.at[pl.ds(dst_row, 1), :], sem_r.at[buf])

    def pos_copy(buf):
        return pltpu.make_async_copy(pos_v.at[buf], pos_sm.at[buf], sem_p.at[buf])

    def scatter_rows(buf):
        pos_copy(buf).wait()
        for k in range(TM):
            row_copy(buf, k, pos_sm[buf, 0, k]).start()

    def drain(buf):
        def body(k, c):
            row_copy(buf, 0, 0).wait()
            return c
        lax.fori_loop(0, TM, body, 0, unroll=8)

    @pl.when(step == 0)
    def _():
        cnt_s[...] = jnp.zeros_like(cnt_s)
        cur_s[...] = jnp.zeros_like(cur_s)
        alloc_s[...] = jnp.zeros_like(alloc_s)
        tb_s[...] = jnp.zeros_like(tb_s)

    def tile_body(first):
        if two_src:
            x = jnp.where(step % TILES_PER_B == 0, ctx_ref[...], x_ref[...])
        else:
            x = x_ref[...]
        yna = yna_ref[...].astype(jnp.float32)
        yls = yls_ref[...].astype(jnp.float32)
        merged = jnp.concatenate([
            _rms(yna, og_ref[:, :NA_WIDTH]),
            _rms(yls[:, :LRU_WIDTH], og_ref[:, NA_WIDTH:NA_WIDTH + LRU_WIDTH]),
            _rms(yls[:, LRU_WIDTH:], og_ref[:, NA_WIDTH + LRU_WIDTH:]),
        ], axis=-1).astype(jnp.bfloat16)
        y = jnp.dot(merged, w_ref[...], preferred_element_type=jnp.float32)
        x_new = x + mod_ref[2:3, :] * y
        xo_ref[...] = x_new
        h2 = _rms(x_new, fg_ref[...]) * (1.0 + mod_ref[4:5, :]) + mod_ref[3:4, :]

        if not first:
            scatter_rows(1 - cur)

        lt = lax.dot_general(wrt_ref[...], h2, _NT, precision=_HI, preferred_element_type=jnp.float32) + brt_ref[...]
        bucket, w_lo, w_hi = _route(jnp.exp(lt - jnp.max(lt, axis=0, keepdims=True)))

        b_iota = lax.broadcasted_iota(jnp.int32, (BUCKET_ROWS, TM), 0)
        onehot = b_iota == bucket
        tri = (lax.broadcasted_iota(jnp.int32, (TM, TM), 0) <= lax.broadcasted_iota(jnp.int32, (TM, TM), 1))
        as_bf16 = lambda mask: jnp.where(mask, 1.0, 0.0).astype(jnp.bfloat16)
        cum = jnp.dot(as_bf16(onehot), as_bf16(tri), preferred_element_type=jnp.float32)
        cnt_new = cum[:, TM - 1:TM].astype(jnp.int32)
        cnt_old = cnt_s[:, 0:1]
        open_id = cur_s[:, 0:1]
        alloc = alloc_s[0:1, 0:1]
        shift = TM.bit_length() - 1
        q_last = (cnt_old + cnt_new - 1) >> shift
        q_prev = (cnt_old - 1) >> shift
        opens = jnp.where(cnt_new > 0, q_last - q_prev, 0)
        lower = (lax.broadcasted_iota(jnp.int32, (BUCKET_ROWS, BUCKET_ROWS), 1)
                 < lax.broadcasted_iota(jnp.int32, (BUCKET_ROWS, BUCKET_ROWS), 0))
        opens_b = jnp.broadcast_to(opens, (BUCKET_ROWS, LANES)).astype(jnp.float32).astype(jnp.bfloat16)
        before = jnp.dot(as_bf16(lower), opens_b, preferred_element_type=jnp.float32)[:, 0:1].astype(jnp.int32)
        new_id = alloc + before
        rank = cnt_old + cum.astype(jnp.int32) - 1
        tile_id = jnp.where((opens > 0) & ((rank >> shift) == q_last), new_id, open_id)
        slot = tile_id * TM + (rank & (TM - 1))
        pos = jnp.sum(jnp.where(onehot, slot, 0).astype(jnp.float32), axis=0, keepdims=True).astype(jnp.int32)
        pos_ref[...] = pos

        lane_id = lax.broadcasted_iota(jnp.int32, (BUCKET_ROWS, TM), 1)
        opened_here = (opens > 0) & (new_id == lane_id)
        opened_bucket = jnp.max(jnp.where(opened_here, b_iota, -1).astype(jnp.float32), axis=0,
                                keepdims=True).astype(jnp.int32)
        tb = jnp.where(opened_bucket >= 0, opened_bucket, tb_s[0:1, :])
        alloc_new = alloc + jnp.sum(opens.astype(jnp.float32), axis=0, keepdims=True).astype(jnp.int32)
        tb_s[...] = jnp.broadcast_to(tb, tb_s.shape)
        cnt_s[...] = jnp.broadcast_to(cnt_old + cnt_new, cnt_s.shape)
        cur_s[...] = jnp.broadcast_to(jnp.where(opens > 0, new_id, open_id), cur_s.shape)
        alloc_s[...] = jnp.broadcast_to(alloc_new, alloc_s.shape)
        meta_ref[0:1, :] = tb
        meta_ref[1:2, :] = jnp.broadcast_to(alloc_new, (1, TM))
        meta_ref[2:8, :] = jnp.zeros((6, TM), jnp.int32)

        @pl.when(step >= 2)
        def _():
            drain(cur)
        payload = jnp.concatenate([w_lo, w_hi, jnp.zeros((LANES - 2, TM), jnp.float32)], axis=0)
        row_s[cur, :, :D_MODEL] = h2
        row_s[cur, :, D_MODEL:] = payload.T
        pos_v[cur] = jnp.broadcast_to(pos, (8, TM))
        pos_copy(cur).start()

    @pl.when(step == 0)
    def _():
        tile_body(True)

    @pl.when(step > 0)
    def _():
        tile_body(False)

    @pl.when(step == n_steps - 1)
    def _():
        scatter_rows(cur)
        drain(1 - cur)
        drain(cur)


def _out_proj(srcs, yna, yls, mod, out_g, w_bf16, ffn_g, w_router, b_router, bsz, with_ctx):
    two_src = len(srcs) == 2
    n_tiles, bj, mod_row = _tile_maps(bsz, with_ctx)
    assert n_tiles >= 2
    n_sorted = n_tiles + N_BUCKETS
    tile = lambda width: pl.BlockSpec((None, TM, width), lambda i: (bj(i)[0], bj(i)[1], 0))
    const2 = lambda i: (0, 0)
    if two_src:
        src_specs = [
            pl.BlockSpec((None, TM, D_MODEL), lambda i: (bj(i)[0], 0, 0)),
            pl.BlockSpec((None, TM, D_MODEL), lambda i: (bj(i)[0], jnp.maximum(bj(i)[1] - 1, 0), 0)),
        ]
    else:
        src_specs = [tile(D_MODEL)]
    xs_init = jnp.zeros((n_sorted * TM, ROW_WIDTH), jnp.float32)
    return pl.pallas_call(
        functools.partial(_out_kernel, two_src=two_src),
        out_shape=[
            jax.ShapeDtypeStruct((bsz, ROWS_PER_B, D_MODEL), jnp.float32),
            jax.ShapeDtypeStruct((n_tiles, 1, TM), jnp.int32),
            jax.ShapeDtypeStruct((8, TM), jnp.int32),
            jax.ShapeDtypeStruct(xs_init.shape, jnp.float32),
        ],
        grid=(n_tiles,),
        in_specs=src_specs + [
            tile(NA_WIDTH),
            tile(LRU_WIDTH + SC_WIDTH),
            pl.BlockSpec((None, N_MOD, D_MODEL), lambda i: (mod_row(i), 0, 0)),
            pl.BlockSpec((1, D_MODEL), const2),
            pl.BlockSpec((D_MODEL, D_MODEL), const2),
            pl.BlockSpec((1, D_MODEL), const2),
            pl.BlockSpec((N_EXPERTS, D_MODEL), const2),
            pl.BlockSpec((N_EXPERTS, 1), const2),
            pl.BlockSpec(memory_space=pl.ANY),
        ],
        out_specs=[
            tile(D_MODEL),
            pl.BlockSpec((None, 1, TM), lambda i: (i, 0, 0)),
            pl.BlockSpec((8, TM), const2),
            pl.BlockSpec(memory_space=pl.ANY),
        ],
        scratch_shapes=[
            pltpu.VMEM((BUCKET_ROWS, LANES), jnp.int32),
            pltpu.VMEM((BUCKET_ROWS, LANES), jnp.int32),
            pltpu.VMEM((8, LANES), jnp.int32),
            pltpu.VMEM((8, TM), jnp.int32),
            pltpu.VMEM((2, TM, ROW_WIDTH), jnp.float32),
            pltpu.VMEM((2, 8, TM), jnp.int32),
            pltpu.SMEM((2, 8, TM), jnp.int32),
            pltpu.SemaphoreType.DMA((2,)),
            pltpu.SemaphoreType.DMA((2,)),
        ],
        input_output_aliases={len(srcs) + 8: 3},
        compiler_params=_cparams(("arbitrary",)),
        name="out_proj_route",
    )(*srcs, yna, yls, mod, out_g.reshape(1, D_MODEL), w_bf16, ffn_g.reshape(1, D_MODEL), w_router.T,
      b_router.reshape(N_EXPERTS, 1), xs_init)


def _moe_kernel(src_ref, e0_ref, e1_ref, used_ref, xs_ref, wg0, wu0, wd0, wg1, wu1, wd1, o_ref, wg_s, wu_s, wd_s):
    n = pl.program_id(0)
    prev = jnp.maximum(n - 1, 0)
    e0, e1 = e0_ref[n], e1_ref[n]
    fresh = (n == 0) | (e0 != e0_ref[prev]) | (e1 != e1_ref[prev])

    @pl.when(fresh)
    def _():
        for k, (g, u, d) in enumerate(((wg0, wu0, wd0), (wg1, wu1, wd1))):
            wg_s[k] = g[...].astype(jnp.bfloat16)
            wu_s[k] = u[...].astype(jnp.bfloat16)
            wd_s[k] = d[...].astype(jnp.bfloat16)

    @pl.when(n < used_ref[0])
    def _():
        xb = xs_ref[:, :D_MODEL].astype(jnp.bfloat16)
        weights = (xs_ref[:, D_MODEL:D_MODEL + 1], xs_ref[:, D_MODEL + 1:D_MODEL + 2])
        out = jnp.zeros((TM, D_MODEL), jnp.float32)
        for k, wk in enumerate(weights):
            gate = jnp.dot(xb, wg_s[k], preferred_element_type=jnp.float32)
            up = jnp.dot(xb, wu_s[k], preferred_element_type=jnp.float32)
            hid = (gate * jax.nn.sigmoid(gate)) * up
            out = out + wk * jnp.dot(hid.astype(jnp.bfloat16), wd_s[k], preferred_element_type=jnp.float32)
        o_ref[...] = out


def _moe(sched, xs, w_gate, w_up, w_down):
    n_tiles = xs.shape[0] // TM
    src, e0, e1, used = sched
    first = lambda n, s, a, b, u: (a[n], 0, 0)
    second = lambda n, s, a, b, u: (b[n], 0, 0)
    gate_spec = lambda m: pl.BlockSpec((None, D_MODEL, D_EXPERT), m)
    down_spec = lambda m: pl.BlockSpec((None, D_EXPERT, D_MODEL), m)
    return pl.pallas_call(
        _moe_kernel,
        out_shape=jax.ShapeDtypeStruct((n_tiles * TM, D_MODEL), jnp.float32),
        grid_spec=pltpu.PrefetchScalarGridSpec(
            num_scalar_prefetch=4,
            grid=(n_tiles,),
            in_specs=[
                pl.BlockSpec((TM, ROW_WIDTH), lambda n, s, a, b, u: (s[n], 0)),
                gate_spec(first), gate_spec(first), down_spec(first),
                gate_spec(second), gate_spec(second), down_spec(second),
            ],
            out_specs=pl.BlockSpec((TM, D_MODEL), lambda n, s, a, b, u: (s[n], 0)),
            scratch_shapes=[
                pltpu.VMEM((2, D_MODEL, D_EXPERT), jnp.bfloat16),
                pltpu.VMEM((2, D_MODEL, D_EXPERT), jnp.bfloat16),
                pltpu.VMEM((2, D_EXPERT, D_MODEL), jnp.bfloat16),
            ],
        ),
        compiler_params=_cparams(("arbitrary",)),
        name="moe_experts",
    )(src, e0, e1, used, xs, w_gate, w_up, w_down, w_gate, w_up, w_down)


def _schedule(meta, n_tiles):
    tile_bucket = meta[0, :n_tiles]
    used = meta[1, 0]
    ids = jnp.arange(n_tiles, dtype=jnp.int32)
    order = jnp.argsort(jnp.where(ids < used, tile_bucket, N_BUCKETS), stable=True).astype(jnp.int32)
    src = jnp.where(ids < used, order, order[jnp.maximum(used - 1, 0)])
    bucket = tile_bucket[src]
    group, pair = bucket // N_PAIRS, bucket % N_PAIRS
    lo = jnp.array([0, 0, 0, 1, 1, 2], jnp.int32)[pair]
    hi = jnp.array([1, 2, 3, 2, 3, 3], jnp.int32)[pair]
    return src, group * E_PER_GROUP + lo, group * E_PER_GROUP + hi, used.reshape(1)


def _combine_kernel(pos_ref, x_ref, mod_ref, fg_ref, ys_ref, o_ref, buf, sem, *, final):
    i = pl.program_id(0)
    n = pl.num_programs(0)

    def row_copy(src_row, slot, k):
        return pltpu.make_async_copy(ys_ref.at[pl.ds(src_row, 1), :], buf.at[slot, pl.ds(k, 1), :], sem.at[slot])

    def issue(tile, slot):
        def body(k, c):
            row_copy(pos_ref[tile * TM + k], slot, k).start()
            return c
        lax.fori_loop(0, TM, body, 0, unroll=8)

    @pl.when(i == 0)
    def _():
        issue(0, 0)

    @pl.when(i + 1 < n)
    def _():
        issue(i + 1, (i + 1) % 2)

    slot = i % 2

    def drain(k, c):
        row_copy(0, slot, 0).wait()
        return c
    lax.fori_loop(0, TM, drain, 0, unroll=8)

    x_new = x_ref[...] + mod_ref[5:6, :] * buf[slot]
    if final:
        x_new = _rms(x_new, fg_ref[...])
    o_ref[...] = x_new


def _combine(pos, x_all, mod, final_g, ys, bsz, with_ctx, final):
    n_tiles, bj, mod_row = _tile_maps(bsz, with_ctx)
    if final:
        out_shape = jax.ShapeDtypeStruct((bsz, SEQ, D_MODEL), jnp.float32)
        out_spec = pl.BlockSpec((None, TM, D_MODEL), lambda i, p: (bj(i)[0], bj(i)[1] - 1, 0))
    else:
        out_shape = jax.ShapeDtypeStruct((bsz, ROWS_PER_B, D_MODEL), jnp.float32)
        out_spec = pl.BlockSpec((None, TM, D_MODEL), lambda i, p: (bj(i)[0], bj(i)[1], 0))
    return pl.pallas_call(
        functools.partial(_combine_kernel, final=final),
        out_shape=out_shape,
        grid_spec=pltpu.PrefetchScalarGridSpec(
            num_scalar_prefetch=1,
            grid=(n_tiles,),
            in_specs=[
                pl.BlockSpec((None, TM, D_MODEL), lambda i, p: (bj(i)[0], bj(i)[1], 0)),
                pl.BlockSpec((None, N_MOD, D_MODEL), lambda i, p: (mod_row(i), 0, 0)),
                pl.BlockSpec((1, D_MODEL), lambda i, p: (0, 0)),
                pl.BlockSpec(memory_space=pl.ANY),
            ],
            out_specs=out_spec,
            scratch_shapes=[pltpu.VMEM((2, TM, D_MODEL), jnp.float32), pltpu.SemaphoreType.DMA((2,))],
        ),
        compiler_params=_cparams(("arbitrary",)),
        name="combine",
    )(pos.reshape(-1), x_all, mod, final_g.reshape(1, D_MODEL), ys)


def kernel(x, c, ctx, c_ctx, w_ada, b_ada, norm_mix_g, w_in, lru_conv_w, lru_conv_b, rg_w, rg_b, rg_lam, na_rpb,
           sc_conv_w, sc_conv_b, mix_out_g, w_out, norm_ffn_g, w_router, b_router, w_gate, w_up, w_down, final_g):
    bsz = x.shape[0]
    mod_rows = -(-(bsz + 1) // 8) * 8
    c_all = jnp.zeros((mod_rows, D_MODEL), jnp.float32).at[:bsz].set(c).at[bsz].set(c_ctx)
    mods = _ada(c_all, w_ada, b_ada).reshape(DEPTH, mod_rows, N_MOD, D_MODEL)

    x_all = None
    out = None
    for l in range(DEPTH):
        need_ctx = l < DEPTH - 1
        mod = mods[l]
        srcs = (ctx, x) if l == 0 else (x_all,)
        qkv, rest = _in_proj(srcs, norm_mix_g[l], mod, w_in[l].astype(jnp.bfloat16), bsz)
        yna = _attention(qkv, _attn_bias_table(na_rpb[l]), bsz, need_ctx)
        yls = _lru_sconv(rest, lru_conv_w[l], lru_conv_b[l], rg_w[l], rg_b[l], rg_lam[l], sc_conv_w[l],
                         sc_conv_b[l], bsz, need_ctx)
        x_mid, pos, meta, xs = _out_proj(srcs, yna, yls, mod, mix_out_g[l], w_out[l].astype(jnp.bfloat16),
                                         norm_ffn_g[l], w_router, b_router, bsz, need_ctx)
        ys = _moe(_schedule(meta, xs.shape[0] // TM), xs, w_gate[l], w_up[l], w_down[l])
        res = _combine(pos, x_mid, mod, final_g, ys, bsz, need_ctx, final=not need_ctx)
        if need_ctx:
            x_all = res
        else:
            out = res
    return out
```

```python
import functools

import jax
import jax.numpy as jnp
from jax import lax
from jax.experimental import pallas as pl
from jax.experimental.pallas import tpu as pltpu

D_MODEL = 1024
SEQ = 2048
CTX_LEN = 256
ROWS_PER_B = CTX_LEN + SEQ
DEPTH = 2
N_MOD = 6
EPS = 1e-6
NEG_INF = -1e30

GRID_W = 64
GRID_ROWS = SEQ // GRID_W
HEAD_DIM = 64
NA_WIDTH = 512
NA_HEADS = 8
LRU_WIDTH = 256
LRU_HEADS = 4
LRU_BLOCK = 64
SC_WIDTH = 256
QKV_WIDTH = 3 * NA_WIDTH
REST_WIDTH = 2 * LRU_WIDTH + 3 * SC_WIDTH
IN_WIDTH = QKV_WIDTH + REST_WIDTH
RG_C = 8.0
WIN_R = 8
WIN_C = 16
N_EXPERTS = 16
N_GROUPS = 4
E_PER_GROUP = 4
N_PAIRS = 6
N_BUCKETS = N_GROUPS * N_PAIRS
D_EXPERT = 512

TM = 256
TILES_PER_B = ROWS_PER_B // TM
LAT_TILES_PER_B = SEQ // TM
LANES = 128
BUCKET_ROWS = 32
ROW_WIDTH = D_MODEL + LANES
HEADS_PER_STACK = 4
STACK_W = HEADS_PER_STACK * HEAD_DIM
SCAN_SEGS = 8
VMEM_LIMIT = 56 * 1024 * 1024

_HI = lax.Precision.HIGHEST
_NT = (((1,), (1,)), ((), ()))


def _cparams(sem):
    return pltpu.CompilerParams(dimension_semantics=sem, vmem_limit_bytes=VMEM_LIMIT)


def _rms(v, g):
    return v * lax.rsqrt(jnp.mean(v * v, axis=-1, keepdims=True) + EPS) * g


def _ada_kernel(c_ref, w_ref, b_ref, o_ref):
    cond = c_ref[...]
    cond = cond * jax.nn.sigmoid(cond)
    o_ref[0] = jnp.dot(cond, w_ref[0], precision=_HI, preferred_element_type=jnp.float32) + b_ref[0]


def _ada(c_all, w_ada, b_ada):
    depth, _, width = w_ada.shape
    rows = c_all.shape[0]
    tn = 1536
    return pl.pallas_call(
        _ada_kernel,
        out_shape=jax.ShapeDtypeStruct((depth, rows, width), jnp.float32),
        grid=(depth, width // tn),
        in_specs=[
            pl.BlockSpec((rows, D_MODEL), lambda l, n: (0, 0)),
            pl.BlockSpec((1, D_MODEL, tn), lambda l, n: (l, 0, n)),
            pl.BlockSpec((1, 1, tn), lambda l, n: (l, 0, n)),
        ],
        out_specs=pl.BlockSpec((1, rows, tn), lambda l, n: (l, 0, n)),
        compiler_params=_cparams(("arbitrary", "arbitrary")),
        name="ada",
    )(c_all, w_ada, b_ada.reshape(depth, 1, width))


def _tile_maps(bsz, with_ctx):
    per_b = TILES_PER_B if with_ctx else LAT_TILES_PER_B
    off = 0 if with_ctx else 1

    def bj(i):
        return i // per_b, i % per_b + off

    def mod_row(i):
        b, j = bj(i)
        return jnp.where(j == 0, bsz, b)

    return per_b * bsz, bj, mod_row


def _in_kernel(*refs, two_src):
    if two_src:
        ctx_ref, x_ref, g_ref, mod_ref, w_ref, qkv_ref, rest_ref = refs
        j = pl.program_id(0) % TILES_PER_B
        x = jnp.where(j == 0, ctx_ref[...], x_ref[...])
    else:
        x_ref, g_ref, mod_ref, w_ref, qkv_ref, rest_ref = refs
        x = x_ref[...]
    h = _rms(x, g_ref[...]) * (1.0 + mod_ref[1:2, :]) + mod_ref[0:1, :]
    hb = h.astype(jnp.bfloat16)
    qkv_ref[...] = jnp.dot(hb, w_ref[:, :QKV_WIDTH], preferred_element_type=jnp.float32).astype(jnp.bfloat16)
    rest_ref[...] = jnp.dot(hb, w_ref[:, QKV_WIDTH:], preferred_element_type=jnp.float32)


def _in_proj(srcs, g, mod, w_bf16, bsz):
    two_src = len(srcs) == 2
    n_tiles, bj, mod_row = _tile_maps(bsz, True)
    if two_src:
        src_specs = [
            pl.BlockSpec((None, TM, D_MODEL), lambda i: (bj(i)[0], 0, 0)),
            pl.BlockSpec((None, TM, D_MODEL), lambda i: (bj(i)[0], jnp.maximum(bj(i)[1] - 1, 0), 0)),
        ]
    else:
        src_specs = [pl.BlockSpec((None, TM, D_MODEL), lambda i: (bj(i)[0], bj(i)[1], 0))]
    return pl.pallas_call(
        functools.partial(_in_kernel, two_src=two_src),
        out_shape=[
            jax.ShapeDtypeStruct((bsz, ROWS_PER_B, QKV_WIDTH), jnp.bfloat16),
            jax.ShapeDtypeStruct((bsz, ROWS_PER_B, REST_WIDTH), jnp.float32),
        ],
        grid=(n_tiles,),
        in_specs=src_specs + [
            pl.BlockSpec((1, D_MODEL), lambda i: (0, 0)),
            pl.BlockSpec((None, N_MOD, D_MODEL), lambda i: (mod_row(i), 0, 0)),
            pl.BlockSpec((D_MODEL, IN_WIDTH), lambda i: (0, 0)),
        ],
        out_specs=[
            pl.BlockSpec((None, TM, QKV_WIDTH), lambda i: (bj(i)[0], bj(i)[1], 0)),
            pl.BlockSpec((None, TM, REST_WIDTH), lambda i: (bj(i)[0], bj(i)[1], 0)),
        ],
        compiler_params=_cparams(("arbitrary",)),
        name="in_proj",
    )(*srcs, g.reshape(1, D_MODEL), mod, w_bf16)


def _attn_kernel(q_ref, k_ref, v_ref, bias_ref, o_ref, *, need_ctx):
    lane_head = lax.broadcasted_iota(jnp.int32, (1, STACK_W), 1) // HEAD_DIM
    n_stacks = NA_WIDTH // STACK_W

    def stack_q(qg):
        zero = jnp.zeros_like(qg)
        return jnp.concatenate([jnp.where(lane_head == h, qg, zero) for h in range(HEADS_PER_STACK)], axis=0)

    def unstack(o):
        out = jnp.zeros((GRID_W, STACK_W), jnp.float32)
        for h in range(HEADS_PER_STACK):
            out = out + jnp.where(lane_head == h, o[h * GRID_W:(h + 1) * GRID_W], 0.0)
        return out

    def attend(q_rows, s, local):
        cols = slice(s * STACK_W, (s + 1) * STACK_W)
        qg = q_ref[pl.ds(q_rows, GRID_W), cols] * jnp.bfloat16(HEAD_DIM ** -0.5)
        qs = stack_q(qg)
        kc = k_ref[0:CTX_LEN, cols]
        vc = v_ref[0:CTX_LEN, cols]
        s_ctx = lax.dot_general(qs, kc, _NT, preferred_element_type=jnp.float32)
        m = jnp.max(s_ctx, axis=-1, keepdims=True)
        if local is not None:
            k_rows, delta = local
            kw = k_ref[pl.ds(k_rows, WIN_R * GRID_W), cols]
            vw = v_ref[pl.ds(k_rows, WIN_R * GRID_W), cols]
            bias = bias_ref[delta, s * HEADS_PER_STACK:(s + 1) * HEADS_PER_STACK]
            s_loc = lax.dot_general(qs, kw, _NT, preferred_element_type=jnp.float32)
            s_loc = s_loc + bias.reshape(HEADS_PER_STACK * GRID_W, WIN_R * GRID_W)
            m = jnp.maximum(m, jnp.max(s_loc, axis=-1, keepdims=True))
            p_loc = jnp.exp(s_loc - m)
        p_ctx = jnp.exp(s_ctx - m)
        denom = jnp.sum(p_ctx, axis=-1, keepdims=True)
        o = jnp.dot(p_ctx.astype(jnp.bfloat16), vc, preferred_element_type=jnp.float32)
        if local is not None:
            denom = denom + jnp.sum(p_loc, axis=-1, keepdims=True)
            o = o + jnp.dot(p_loc.astype(jnp.bfloat16), vw, preferred_element_type=jnp.float32)
        o_ref[pl.ds(q_rows, GRID_W), cols] = unstack(o / denom).astype(o_ref.dtype)

    def lat_row(r, carry):
        r0 = jnp.clip(r - WIN_R // 2, 0, GRID_ROWS - WIN_R)
        q_rows = pl.multiple_of(CTX_LEN + r * GRID_W, GRID_W)
        k_rows = pl.multiple_of(CTX_LEN + r0 * GRID_W, GRID_W)
        for s in range(n_stacks):
            attend(q_rows, s, (k_rows, r - r0))
        return carry

    lax.fori_loop(0, GRID_ROWS, lat_row, 0)

    if need_ctx:
        def ctx_chunk(cq, carry):
            q_rows = pl.multiple_of(cq * GRID_W, GRID_W)
            for s in range(n_stacks):
                attend(q_rows, s, None)
            return carry

        lax.fori_loop(0, CTX_LEN // GRID_W, ctx_chunk, 0)
    else:
        o_ref[0:CTX_LEN, :] = jnp.zeros((CTX_LEN, NA_WIDTH), o_ref.dtype)


def _attn_bias_table(rpb):
    n_rel_c = 2 * WIN_C - 1
    lead = GRID_W - WIN_C
    padded = jnp.pad(rpb, ((0, 0), (0, 0), (lead, 2 * GRID_W - lead - n_rel_c)))
    skew = jnp.tile(padded, (1, 1, GRID_W))[..., :GRID_W * (2 * GRID_W - 1)]
    skew = skew.reshape(NA_HEADS, 2 * WIN_R - 1, GRID_W, 2 * GRID_W - 1)
    toeplitz = skew[..., GRID_W - 1:]
    q_col = jnp.arange(GRID_W)[:, None]
    k_col = jnp.arange(GRID_W)[None, :]
    c_start = jnp.clip(q_col - WIN_C // 2, 0, GRID_W - WIN_C)
    ok = (k_col >= c_start) & (k_col < c_start + WIN_C)
    toeplitz = jnp.where(ok, toeplitz, NEG_INF)
    per_delta = [toeplitz[:, WIN_R - 1 - d:2 * WIN_R - 1 - d] for d in range(WIN_R)]
    tab = jnp.stack(per_delta, axis=0)
    return tab.transpose(0, 1, 3, 2, 4).reshape(WIN_R, NA_HEADS, GRID_W, WIN_R * GRID_W)


def _attention(qkv, bias, bsz, need_ctx):
    return pl.pallas_call(
        functools.partial(_attn_kernel, need_ctx=need_ctx),
        out_shape=jax.ShapeDtypeStruct((bsz, ROWS_PER_B, NA_WIDTH), jnp.bfloat16),
        grid=(bsz,),
        in_specs=[
            pl.BlockSpec((None, ROWS_PER_B, NA_WIDTH), lambda b: (b, 0, 0)),
            pl.BlockSpec((None, ROWS_PER_B, NA_WIDTH), lambda b: (b, 0, 1)),
            pl.BlockSpec((None, ROWS_PER_B, NA_WIDTH), lambda b: (b, 0, 2)),
            pl.BlockSpec((WIN_R, NA_HEADS, GRID_W, WIN_R * GRID_W), lambda b: (0, 0, 0, 0)),
        ],
        out_specs=pl.BlockSpec((None, ROWS_PER_B, NA_WIDTH), lambda b: (b, 0, 0)),
        compiler_params=_cparams(("arbitrary",)),
        name="attention",
    )(qkv, qkv, qkv, bias)


CONV_PAD = 8


def _pad_base(start):
    return start + CONV_PAD * (1 if start == 0 else 2)


def _dwconv(pad_s, start, length, w_ref, b_ref, left):
    base = _pad_base(start)
    width = w_ref.shape[0]
    y = pad_s[base - left:base - left + length, :] * w_ref[0:1, :] + b_ref[...]
    for k in range(1, width):
        y = y + pad_s[base + k - left:base + k - left + length, :] * w_ref[k:k + 1, :]
    return y


def _lru_kernel(rest_ref, lcw_ref, lcb_ref, wbd_ref, rgb_ref, lam_ref, scw_ref, scb_ref, o_ref,
                xc_s, a_s, b_s, hl_s, p_s, y_s, pad_s, *, need_ctx):
    half = LRU_WIDTH // 2
    col_rx, col_rg, col_sb, col_sc, col_sx = (k * LRU_WIDTH for k in range(5))
    segments = ((0, CTX_LEN), (CTX_LEN, SEQ))

    for start, length in segments:
        base = _pad_base(start)
        pad_s[base - CONV_PAD:base, :] = jnp.zeros((CONV_PAD, LRU_WIDTH), jnp.float32)
    pad_s[pad_s.shape[0] - CONV_PAD:, :] = jnp.zeros((CONV_PAD, LRU_WIDTH), jnp.float32)

    for start, length in segments:
        base = _pad_base(start)
        pad_s[base:base + length, :] = rest_ref[start:start + length, col_rx:col_rx + LRU_WIDTH]
    for start, length in segments:
        xc_s[start:start + length, :] = _dwconv(pad_s, start, length, lcw_ref, lcb_ref, 2)

    def coeffs(d, start, length):
        chunk = 256
        sp = jax.nn.softplus(-lam_ref[d:d + 1, :])
        for c0 in range(0, length, chunk):
            xc = xc_s[start + c0:start + c0 + chunk, :]
            pre = jnp.dot(xc.astype(jnp.bfloat16), wbd_ref[:, 2 * d * LRU_WIDTH:(2 * d + 2) * LRU_WIDTH],
                          preferred_element_type=jnp.float32) + rgb_ref[:, 2 * d * LRU_WIDTH:(2 * d + 2) * LRU_WIDTH]
            gate_r = jax.nn.sigmoid(pre[:, :LRU_WIDTH])
            gate_i = jax.nn.sigmoid(pre[:, LRU_WIDTH:])
            log_a = -RG_C * gate_r * sp
            a = jnp.exp(log_a)
            bb = jnp.sqrt(1.0 - a * a) * (gate_i * xc)
            for c in range(2):
                a_s[c, c0:c0 + chunk, :] = a[:, c * half:(c + 1) * half]
                b_s[c, c0:c0 + chunk, :] = bb[:, c * half:(c + 1) * half]

    def scan(length, reverse, h0):
        seg = length // SCAN_SEGS

        def step(i, carry):
            t = seg - 1 - i if reverse else i
            rows = pl.ds(t, SCAN_SEGS, stride=seg)
            new = []
            for c in range(2):
                h, p = carry[2 * c], carry[2 * c + 1]
                a_t = a_s[c, rows, :]
                h = a_t * h + b_s[c, rows, :]
                p = a_t * p
                hl_s[c, rows, :] = h
                p_s[c, rows, :] = p
                new += [h, p]
            return tuple(new)

        zero = jnp.zeros((SCAN_SEGS, half), jnp.float32)
        one = jnp.ones((SCAN_SEGS, half), jnp.float32)
        ends = lax.fori_loop(0, seg, step, (zero, one, zero, one), unroll=8)
        finals = []
        order = range(SCAN_SEGS - 1, -1, -1) if reverse else range(SCAN_SEGS)
        for c in range(2):
            h_end, p_end = ends[2 * c], ends[2 * c + 1]
            carry = h0[c]
            for j in order:
                rows = slice(j * seg, (j + 1) * seg)
                hl_s[c, rows, :] = hl_s[c, rows, :] + p_s[c, rows, :] * carry
                carry = h_end[j:j + 1, :] + p_end[j:j + 1, :] * carry
            finals.append(carry)
        return finals

    zero_state = [jnp.zeros((1, half), jnp.float32)] * 2
    for d, reverse in enumerate((False, True)):
        (c_start, c_len), (l_start, l_len) = segments
        coeffs(d, c_start, c_len)
        state = scan(c_len, reverse, zero_state)
        if need_ctx:
            for c in range(2):
                cols = slice(c * half, (c + 1) * half)
                prev = 0.0 if d == 0 else y_s[c_start:c_start + c_len, cols]
                y_s[c_start:c_start + c_len, cols] = prev + hl_s[c, 0:c_len, :]
        coeffs(d, l_start, l_len)
        scan(l_len, reverse, state)
        for c in range(2):
            cols = slice(c * half, (c + 1) * half)
            prev = 0.0 if d == 0 else y_s[l_start:l_start + l_len, cols]
            y_s[l_start:l_start + l_len, cols] = prev + hl_s[c, 0:l_len, :]

    out_segments = segments if need_ctx else segments[1:]
    for start, length in out_segments:
        rows = slice(start, start + length)
        y_lru = y_s[rows, :] * jax.nn.gelu(rest_ref[rows, col_rg:col_rg + LRU_WIDTH])
        o_ref[rows, 0:LRU_WIDTH] = y_lru.astype(o_ref.dtype)
        base = _pad_base(start)
        pad_s[base:base + length, :] = (rest_ref[rows, col_sc:col_sc + SC_WIDTH]
                                        * rest_ref[rows, col_sx:col_sx + SC_WIDTH])
        y_sc = rest_ref[rows, col_sb:col_sb + SC_WIDTH] * _dwconv(pad_s, start, length, scw_ref, scb_ref, 1)
        o_ref[rows, LRU_WIDTH:LRU_WIDTH + SC_WIDTH] = y_sc.astype(o_ref.dtype)
    if not need_ctx:
        o_ref[0:CTX_LEN, :] = jnp.zeros((CTX_LEN, LRU_WIDTH + SC_WIDTH), o_ref.dtype)


def _block_diag_gates(rg_w):
    eye = jnp.eye(LRU_HEADS, dtype=rg_w.dtype)
    full = jnp.einsum('dgncm,nk->dgnckm', rg_w, eye)
    full = full.reshape(2, 2, LRU_WIDTH, LRU_WIDTH)
    return full.transpose(2, 0, 1, 3).reshape(LRU_WIDTH, 4 * LRU_WIDTH)


def _lru_sconv(rest, lcw, lcb, rg_w, rg_b, rg_lam, scw, scb, bsz, need_ctx):
    wbd = _block_diag_gates(rg_w).astype(jnp.bfloat16)
    const2 = lambda b: (0, 0)
    half = LRU_WIDTH // 2
    return pl.pallas_call(
        functools.partial(_lru_kernel, need_ctx=need_ctx),
        out_shape=jax.ShapeDtypeStruct((bsz, ROWS_PER_B, LRU_WIDTH + SC_WIDTH), jnp.bfloat16),
        grid=(bsz,),
        in_specs=[
            pl.BlockSpec((None, ROWS_PER_B, REST_WIDTH), lambda b: (b, 0, 0)),
            pl.BlockSpec(lcw.shape, const2),
            pl.BlockSpec((1, LRU_WIDTH), const2),
            pl.BlockSpec((LRU_WIDTH, 4 * LRU_WIDTH), const2),
            pl.BlockSpec((1, 4 * LRU_WIDTH), const2),
            pl.BlockSpec((2, LRU_WIDTH), const2),
            pl.BlockSpec(scw.shape, const2),
            pl.BlockSpec((1, SC_WIDTH), const2),
        ],
        out_specs=pl.BlockSpec((None, ROWS_PER_B, LRU_WIDTH + SC_WIDTH), lambda b: (b, 0, 0)),
        scratch_shapes=[
            pltpu.VMEM((ROWS_PER_B, LRU_WIDTH), jnp.float32),
            pltpu.VMEM((2, SEQ, half), jnp.float32),
            pltpu.VMEM((2, SEQ, half), jnp.float32),
            pltpu.VMEM((2, SEQ, half), jnp.float32),
            pltpu.VMEM((2, SEQ, half), jnp.float32),
            pltpu.VMEM((ROWS_PER_B, LRU_WIDTH), jnp.float32),
            pltpu.VMEM((ROWS_PER_B + 3 * CONV_PAD, LRU_WIDTH), jnp.float32),
        ],
        compiler_params=_cparams(("arbitrary",)),
        name="lru_sconv",
    )(rest, lcw, lcb.reshape(1, LRU_WIDTH), wbd, rg_b.reshape(1, 4 * LRU_WIDTH), rg_lam, scw,
      scb.reshape(1, SC_WIDTH))


def _route(et):
    pe = [et[e:e + 1, :] for e in range(N_EXPERTS)]

    def top2_sum(v):
        best = v[0] + v[1]
        for a in range(E_PER_GROUP):
            for b in range(a + 1, E_PER_GROUP):
                if (a, b) != (0, 1):
                    best = jnp.maximum(best, v[a] + v[b])
        return best

    score = [top2_sum(pe[g * E_PER_GROUP:(g + 1) * E_PER_GROUP]) for g in range(N_GROUPS)]
    g_best, g_sel = score[0], jnp.zeros((1, TM), jnp.int32)
    for g in range(1, N_GROUPS):
        upd = score[g] > g_best
        g_sel = jnp.where(upd, g, g_sel)
        g_best = jnp.where(upd, score[g], g_best)
    p_in = []
    for k in range(E_PER_GROUP):
        v = pe[k]
        for g in range(1, N_GROUPS):
            v = jnp.where(g_sel == g, pe[g * E_PER_GROUP + k], v)
        p_in.append(v)
    m1, i1 = p_in[0], jnp.zeros((1, TM), jnp.int32)
    for k in range(1, E_PER_GROUP):
        upd = p_in[k] > m1
        i1 = jnp.where(upd, k, i1)
        m1 = jnp.where(upd, p_in[k], m1)
    m2, i2 = jnp.full((1, TM), -1.0, jnp.float32), jnp.zeros((1, TM), jnp.int32)
    for k in range(E_PER_GROUP):
        cand = jnp.where(i1 == k, -2.0, p_in[k])
        upd = cand > m2
        i2 = jnp.where(upd, k, i2)
        m2 = jnp.where(upd, cand, m2)
    lo, hi = jnp.minimum(i1, i2), jnp.maximum(i1, i2)
    pair = jnp.where(lo == 0, hi - 1, jnp.where(lo == 1, hi + 1, N_PAIRS - 1))
    bucket = g_sel * N_PAIRS + pair
    w1 = m1 / (m1 + m2)
    w2 = m2 / (m1 + m2)
    return bucket, jnp.where(i1 < i2, w1, w2), jnp.where(i1 < i2, w2, w1)


def _out_kernel(*refs, two_src):
    if two_src:
        ctx_ref, x_ref, *rest = refs
    else:
        x_ref, *rest = refs
    (yna_ref, yls_ref, mod_ref, og_ref, w_ref, fg_ref, wrt_ref, brt_ref, _, xo_ref, pos_ref, meta_ref, xs_ref,
     cnt_s, cur_s, alloc_s, tb_s, row_s, pos_v, pos_sm, sem_p, sem_r) = rest
    step = pl.program_id(0)
    n_steps = pl.num_programs(0)
    cur = step % 2

    def row_copy(buf, k, dst_row):
        return pltpu.make_async_copy(row_s.at[buf, pl.ds(k, 1), :], xs_ref.at[pl.ds(dst_row, 1), :], sem_r.at[buf])

    def pos_copy(buf):
        return pltpu.make_async_copy(pos_v.at[buf], pos_sm.at[buf], sem_p.at[buf])

    def scatter_rows(buf):
        pos_copy(buf).wait()
        for k in range(TM):
            row_copy(buf, k, pos_sm[buf, 0, k]).start()

    def drain(buf):
        def body(k, c):
            row_copy(buf, 0, 0).wait()
            return c
        lax.fori_loop(0, TM, body, 0, unroll=8)

    @pl.when(step == 0)
    def _():
        cnt_s[...] = jnp.zeros_like(cnt_s)
        cur_s[...] = jnp.zeros_like(cur_s)
        alloc_s[...] = jnp.zeros_like(alloc_s)
        tb_s[...] = jnp.zeros_like(tb_s)

    def tile_body(first):
        if two_src:
            x = jnp.where(step % TILES_PER_B == 0, ctx_ref[...], x_ref[...])
        else:
            x = x_ref[...]
        yna = yna_ref[...].astype(jnp.float32)
        yls = yls_ref[...].astype(jnp.float32)
        merged = jnp.concatenate([
            _rms(yna, og_ref[:, :NA_WIDTH]),
            _rms(yls[:, :LRU_WIDTH], og_ref[:, NA_WIDTH:NA_WIDTH + LRU_WIDTH]),
            _rms(yls[:, LRU_WIDTH:], og_ref[:, NA_WIDTH + LRU_WIDTH:]),
        ], axis=-1).astype(jnp.bfloat16)
        y = jnp.dot(merged, w_ref[...], preferred_element_type=jnp.float32)
        x_new = x + mod_ref[2:3, :] * y
        xo_ref[...] = x_new
        h2 = _rms(x_new, fg_ref[...]) * (1.0 + mod_ref[4:5, :]) + mod_ref[3:4, :]

        if not first:
            scatter_rows(1 - cur)

        lt = lax.dot_general(wrt_ref[...], h2, _NT, precision=_HI, preferred_element_type=jnp.float32) + brt_ref[...]
        bucket, w_lo, w_hi = _route(jnp.exp(lt - jnp.max(lt, axis=0, keepdims=True)))

        b_iota = lax.broadcasted_iota(jnp.int32, (BUCKET_ROWS, TM), 0)
        onehot = b_iota == bucket
        tri = (lax.broadcasted_iota(jnp.int32, (TM, TM), 0) <= lax.broadcasted_iota(jnp.int32, (TM, TM), 1))
        as_bf16 = lambda mask: jnp.where(mask, 1.0, 0.0).astype(jnp.bfloat16)
        cum = jnp.dot(as_bf16(onehot), as_bf16(tri), preferred_element_type=jnp.float32)
        cnt_new = cum[:, TM - 1:TM].astype(jnp.int32)
        cnt_old = cnt_s[:, 0:1]
        open_id = cur_s[:, 0:1]
        alloc = alloc_s[0:1, 0:1]
        shift = TM.bit_length() - 1
        q_last = (cnt_old + cnt_new - 1) >> shift
        q_prev = (cnt_old - 1) >> shift
        opens = jnp.where(cnt_new > 0, q_last - q_prev, 0)
        lower = (lax.broadcasted_iota(jnp.int32, (BUCKET_ROWS, BUCKET_ROWS), 1)
                 < lax.broadcasted_iota(jnp.int32, (BUCKET_ROWS, BUCKET_ROWS), 0))
        opens_b = jnp.broadcast_to(opens, (BUCKET_ROWS, LANES)).astype(jnp.float32).astype(jnp.bfloat16)
        before = jnp.dot(as_bf16(lower), opens_b, preferred_element_type=jnp.float32)[:, 0:1].astype(jnp.int32)
        new_id = alloc + before
        rank = cnt_old + cum.astype(jnp.int32) - 1
        tile_id = jnp.where((opens > 0) & ((rank >> shift) == q_last), new_id, open_id)
        slot = tile_id * TM + (rank & (TM - 1))
        pos = jnp.sum(jnp.where(onehot, slot, 0).astype(jnp.float32), axis=0, keepdims=True).astype(jnp.int32)
        pos_ref[...] = pos

        lane_id = lax.broadcasted_iota(jnp.int32, (BUCKET_ROWS, TM), 1)
        opened_here = (opens > 0) & (new_id == lane_id)
        opened_bucket = jnp.max(jnp.where(opened_here, b_iota, -1).astype(jnp.float32), axis=0,
                                keepdims=True).astype(jnp.int32)
        tb = jnp.where(opened_bucket >= 0, opened_bucket, tb_s[0:1, :])
        alloc_new = alloc + jnp.sum(opens.astype(jnp.float32), axis=0, keepdims=True).astype(jnp.int32)
        tb_s[...] = jnp.broadcast_to(tb, tb_s.shape)
        cnt_s[...] = jnp.broadcast_to(cnt_old + cnt_new, cnt_s.shape)
        cur_s[...] = jnp.broadcast_to(jnp.where(opens > 0, new_id, open_id), cur_s.shape)
        alloc_s[...] = jnp.broadcast_to(alloc_new, alloc_s.shape)
        meta_ref[0:1, :] = tb
        meta_ref[1:2, :] = jnp.broadcast_to(alloc_new, (1, TM))
        meta_ref[2:8, :] = jnp.zeros((6, TM), jnp.int32)

        @pl.when(step >= 2)
        def _():
            drain(cur)
        payload = jnp.concatenate([w_lo, w_hi, jnp.zeros((LANES - 2, TM), jnp.float32)], axis=0)
        row_s[cur, :, :D_MODEL] = h2
        row_s[cur, :, D_MODEL:] = payload.T
        pos_v[cur] = jnp.broadcast_to(pos, (8, TM))
        pos_copy(cur).start()

    @pl.when(step == 0)
    def _():
        tile_body(True)

    @pl.when(step > 0)
    def _():
        tile_body(False)

    @pl.when(step == n_steps - 1)
    def _():
        scatter_rows(cur)
        drain(1 - cur)
        drain(cur)


def _out_proj(srcs, yna, yls, mod, out_g, w_bf16, ffn_g, w_router, b_router, bsz, with_ctx):
    two_src = len(srcs) == 2
    n_tiles, bj, mod_row = _tile_maps(bsz, with_ctx)
    assert n_tiles >= 2
    n_sorted = n_tiles + N_BUCKETS
    tile = lambda width: pl.BlockSpec((None, TM, width), lambda i: (bj(i)[0], bj(i)[1], 0))
    const2 = lambda i: (0, 0)
    if two_src:
        src_specs = [
            pl.BlockSpec((None, TM, D_MODEL), lambda i: (bj(i)[0], 0, 0)),
            pl.BlockSpec((None, TM, D_MODEL), lambda i: (bj(i)[0], jnp.maximum(bj(i)[1] - 1, 0), 0)),
        ]
    else:
        src_specs = [tile(D_MODEL)]
    xs_init = jnp.zeros((n_sorted * TM, ROW_WIDTH), jnp.float32)
    return pl.pallas_call(
        functools.partial(_out_kernel, two_src=two_src),
        out_shape=[
            jax.ShapeDtypeStruct((bsz, ROWS_PER_B, D_MODEL), jnp.float32),
            jax.ShapeDtypeStruct((n_tiles, 1, TM), jnp.int32),
            jax.ShapeDtypeStruct((8, TM), jnp.int32),
            jax.ShapeDtypeStruct(xs_init.shape, jnp.float32),
        ],
        grid=(n_tiles,),
        in_specs=src_specs + [
            tile(NA_WIDTH),
            tile(LRU_WIDTH + SC_WIDTH),
            pl.BlockSpec((None, N_MOD, D_MODEL), lambda i: (mod_row(i), 0, 0)),
            pl.BlockSpec((1, D_MODEL), const2),
            pl.BlockSpec((D_MODEL, D_MODEL), const2),
            pl.BlockSpec((1, D_MODEL), const2),
            pl.BlockSpec((N_EXPERTS, D_MODEL), const2),
            pl.BlockSpec((N_EXPERTS, 1), const2),
            pl.BlockSpec(memory_space=pl.ANY),
        ],
        out_specs=[
            tile(D_MODEL),
            pl.BlockSpec((None, 1, TM), lambda i: (i, 0, 0)),
            pl.BlockSpec((8, TM), const2),
            pl.BlockSpec(memory_space=pl.ANY),
        ],
        scratch_shapes=[
            pltpu.VMEM((BUCKET_ROWS, LANES), jnp.int32),
            pltpu.VMEM((BUCKET_ROWS, LANES), jnp.int32),
            pltpu.VMEM((8, LANES), jnp.int32),
            pltpu.VMEM((8, TM), jnp.int32),
            pltpu.VMEM((2, TM, ROW_WIDTH), jnp.float32),
            pltpu.VMEM((2, 8, TM), jnp.int32),
            pltpu.SMEM((2, 8, TM), jnp.int32),
            pltpu.SemaphoreType.DMA((2,)),
            pltpu.SemaphoreType.DMA((2,)),
        ],
        input_output_aliases={len(srcs) + 8: 3},
        compiler_params=_cparams(("arbitrary",)),
        name="out_proj_route",
    )(*srcs, yna, yls, mod, out_g.reshape(1, D_MODEL), w_bf16, ffn_g.reshape(1, D_MODEL), w_router.T,
      b_router.reshape(N_EXPERTS, 1), xs_init)


def _moe_kernel(src_ref, e0_ref, e1_ref, used_ref, xs_ref, wg0, wu0, wd0, wg1, wu1, wd1, o_ref, wg_s, wu_s, wd_s):
    n = pl.program_id(0)
    prev = jnp.maximum(n - 1, 0)
    e0, e1 = e0_ref[n], e1_ref[n]
    fresh = (n == 0) | (e0 != e0_ref[prev]) | (e1 != e1_ref[prev])

    @pl.when(fresh)
    def _():
        for k, (g, u, d) in enumerate(((wg0, wu0, wd0), (wg1, wu1, wd1))):
            wg_s[k] = g[...].astype(jnp.bfloat16)
            wu_s[k] = u[...].astype(jnp.bfloat16)
            wd_s[k] = d[...].astype(jnp.bfloat16)

    @pl.when(n < used_ref[0])
    def _():
        xb = xs_ref[:, :D_MODEL].astype(jnp.bfloat16)
        weights = (xs_ref[:, D_MODEL:D_MODEL + 1], xs_ref[:, D_MODEL + 1:D_MODEL + 2])
        out = jnp.zeros((TM, D_MODEL), jnp.float32)
        for k, wk in enumerate(weights):
            gate = jnp.dot(xb, wg_s[k], preferred_element_type=jnp.float32)
            up = jnp.dot(xb, wu_s[k], preferred_element_type=jnp.float32)
            hid = (gate * jax.nn.sigmoid(gate)) * up
            out = out + wk * jnp.dot(hid.astype(jnp.bfloat16), wd_s[k], preferred_element_type=jnp.float32)
        o_ref[...] = out


def _moe(sched, xs, w_gate, w_up, w_down, layer):
    n_tiles = xs.shape[0] // TM
    src, e0, e1, used = sched
    first = lambda n, s, a, b, u: (layer, a[n], 0, 0)
    second = lambda n, s, a, b, u: (layer, b[n], 0, 0)
    gate_spec = lambda m: pl.BlockSpec((None, None, D_MODEL, D_EXPERT), m)
    down_spec = lambda m: pl.BlockSpec((None, None, D_EXPERT, D_MODEL), m)
    return pl.pallas_call(
        _moe_kernel,
        out_shape=jax.ShapeDtypeStruct((n_tiles * TM, D_MODEL), jnp.float32),
        grid_spec=pltpu.PrefetchScalarGridSpec(
            num_scalar_prefetch=4,
            grid=(n_tiles,),
            in_specs=[
                pl.BlockSpec((TM, ROW_WIDTH), lambda n, s, a, b, u: (s[n], 0)),
                gate_spec(first), gate_spec(first), down_spec(first),
                gate_spec(second), gate_spec(second), down_spec(second),
            ],
            out_specs=pl.BlockSpec((TM, D_MODEL), lambda n, s, a, b, u: (s[n], 0)),
            scratch_shapes=[
                pltpu.VMEM((2, D_MODEL, D_EXPERT), jnp.bfloat16),
                pltpu.VMEM((2, D_MODEL, D_EXPERT), jnp.bfloat16),
                pltpu.VMEM((2, D_EXPERT, D_MODEL), jnp.bfloat16),
            ],
        ),
        compiler_params=_cparams(("arbitrary",)),
        name="moe_experts",
    )(src, e0, e1, used, xs, w_gate, w_up, w_down, w_gate, w_up, w_down)


def _schedule(meta, n_tiles):
    tile_bucket = meta[0, :n_tiles]
    used = meta[1, 0]
    ids = jnp.arange(n_tiles, dtype=jnp.int32)
    order = jnp.argsort(jnp.where(ids < used, tile_bucket, N_BUCKETS), stable=True).astype(jnp.int32)
    src = jnp.where(ids < used, order, order[jnp.maximum(used - 1, 0)])
    bucket = tile_bucket[src]
    group, pair = bucket // N_PAIRS, bucket % N_PAIRS
    lo = jnp.array([0, 0, 0, 1, 1, 2], jnp.int32)[pair]
    hi = jnp.array([1, 2, 3, 2, 3, 3], jnp.int32)[pair]
    return src, group * E_PER_GROUP + lo, group * E_PER_GROUP + hi, used.reshape(1)


def _combine_kernel(pos_ref, x_ref, mod_ref, fg_ref, ys_ref, o_ref, buf, sem, *, final):
    i = pl.program_id(0)
    n = pl.num_programs(0)

    def row_copy(src_row, slot, k):
        return pltpu.make_async_copy(ys_ref.at[pl.ds(src_row, 1), :], buf.at[slot, pl.ds(k, 1), :], sem.at[slot])

    def issue(tile, slot):
        for k in range(TM):
            row_copy(pos_ref[tile * TM + k], slot, k).start()

    @pl.when(i == 0)
    def _():
        issue(0, 0)

    @pl.when(i + 1 < n)
    def _():
        issue(i + 1, (i + 1) % 2)

    slot = i % 2

    def drain(k, c):
        row_copy(0, slot, 0).wait()
        return c
    lax.fori_loop(0, TM, drain, 0, unroll=8)

    x_new = x_ref[...] + mod_ref[5:6, :] * buf[slot]
    if final:
        x_new = _rms(x_new, fg_ref[...])
    o_ref[...] = x_new


def _combine(pos, x_all, mod, final_g, ys, bsz, with_ctx, final):
    n_tiles, bj, mod_row = _tile_maps(bsz, with_ctx)
    if final:
        out_shape = jax.ShapeDtypeStruct((bsz, SEQ, D_MODEL), jnp.float32)
        out_spec = pl.BlockSpec((None, TM, D_MODEL), lambda i, p: (bj(i)[0], bj(i)[1] - 1, 0))
    else:
        out_shape = jax.ShapeDtypeStruct((bsz, ROWS_PER_B, D_MODEL), jnp.float32)
        out_spec = pl.BlockSpec((None, TM, D_MODEL), lambda i, p: (bj(i)[0], bj(i)[1], 0))
    return pl.pallas_call(
        functools.partial(_combine_kernel, final=final),
        out_shape=out_shape,
        grid_spec=pltpu.PrefetchScalarGridSpec(
            num_scalar_prefetch=1,
            grid=(n_tiles,),
            in_specs=[
                pl.BlockSpec((None, TM, D_MODEL), lambda i, p: (bj(i)[0], bj(i)[1], 0)),
                pl.BlockSpec((None, N_MOD, D_MODEL), lambda i, p: (mod_row(i), 0, 0)),
                pl.BlockSpec((1, D_MODEL), lambda i, p: (0, 0)),
                pl.BlockSpec(memory_space=pl.ANY),
            ],
            out_specs=out_spec,
            scratch_shapes=[pltpu.VMEM((2, TM, D_MODEL), jnp.float32), pltpu.SemaphoreType.DMA((2,))],
        ),
        compiler_params=_cparams(("arbitrary",)),
        name="combine",
    )(pos.reshape(-1), x_all, mod, final_g.reshape(1, D_MODEL), ys)


def kernel(x, c, ctx, c_ctx, w_ada, b_ada, norm_mix_g, w_in, lru_conv_w, lru_conv_b, rg_w, rg_b, rg_lam, na_rpb,
           sc_conv_w, sc_conv_b, mix_out_g, w_out, norm_ffn_g, w_router, b_router, w_gate, w_up, w_down, final_g):
    bsz = x.shape[0]
    mod_rows = -(-(bsz + 1) // 8) * 8
    c_all = jnp.zeros((mod_rows, D_MODEL), jnp.float32).at[:bsz].set(c).at[bsz].set(c_ctx)
    mods = _ada(c_all, w_ada, b_ada).reshape(DEPTH, mod_rows, N_MOD, D_MODEL)

    x_all = None
    out = None
    for l in range(DEPTH):
        need_ctx = l < DEPTH - 1
        mod = mods[l]
        srcs = (ctx, x) if l == 0 else (x_all,)
        qkv, rest = _in_proj(srcs, norm_mix_g[l], mod, w_in[l].astype(jnp.bfloat16), bsz)
        yna = _attention(qkv, _attn_bias_table(na_rpb[l]), bsz, need_ctx)
        yls = _lru_sconv(rest, lru_conv_w[l], lru_conv_b[l], rg_w[l], rg_b[l], rg_lam[l], sc_conv_w[l],
                         sc_conv_b[l], bsz, need_ctx)
        x_mid, pos, meta, xs = _out_proj(srcs, yna, yls, mod, mix_out_g[l], w_out[l].astype(jnp.bfloat16),
                                         norm_ffn_g[l], w_router, b_router, bsz, need_ctx)
        ys = _moe(_schedule(meta, xs.shape[0] // TM), xs, w_gate, w_up, w_down, l)
        res = _combine(pos, x_mid, mod, final_g, ys, bsz, need_ctx, final=not need_ctx)
        if need_ctx:
            x_all = res
        else:
            out = res
    return out
```

```python
import functools

import jax
import jax.numpy as jnp
from jax import lax
from jax.experimental import pallas as pl
from jax.experimental.pallas import tpu as pltpu

D_MODEL = 1024
SEQ = 2048
CTX_LEN = 256
ROWS_PER_B = CTX_LEN + SEQ
DEPTH = 2
N_MOD = 6
EPS = 1e-6
NEG_INF = -1e30

GRID_W = 64
GRID_ROWS = SEQ // GRID_W
HEAD_DIM = 64
NA_WIDTH = 512
NA_HEADS = 8
LRU_WIDTH = 256
LRU_HEADS = 4
LRU_BLOCK = 64
SC_WIDTH = 256
QKV_WIDTH = 3 * NA_WIDTH
REST_WIDTH = 2 * LRU_WIDTH + 3 * SC_WIDTH
IN_WIDTH = QKV_WIDTH + REST_WIDTH
RG_C = 8.0
WIN_R = 8
WIN_C = 16
N_EXPERTS = 16
N_GROUPS = 4
E_PER_GROUP = 4
N_PAIRS = 6
N_BUCKETS = N_GROUPS * N_PAIRS
D_EXPERT = 512

TM = 256
TILES_PER_B = ROWS_PER_B // TM
LAT_TILES_PER_B = SEQ // TM
LANES = 128
BUCKET_ROWS = 32
ROW_WIDTH = D_MODEL + LANES
HEADS_PER_STACK = 4
STACK_W = HEADS_PER_STACK * HEAD_DIM
SCAN_SEGS = 8
VMEM_LIMIT = 56 * 1024 * 1024

_HI = lax.Precision.HIGHEST
_NT = (((1,), (1,)), ((), ()))


def _cparams(sem):
    return pltpu.CompilerParams(dimension_semantics=sem, vmem_limit_bytes=VMEM_LIMIT)


def _rms(v, g):
    return v * lax.rsqrt(jnp.mean(v * v, axis=-1, keepdims=True) + EPS) * g


def _ada_kernel(c_ref, w_ref, b_ref, o_ref):
    cond = c_ref[...]
    cond = cond * jax.nn.sigmoid(cond)
    o_ref[0] = jnp.dot(cond, w_ref[0], precision=_HI, preferred_element_type=jnp.float32) + b_ref[0]


def _ada(c_all, w_ada, b_ada):
    depth, _, width = w_ada.shape
    rows = c_all.shape[0]
    tn = 1536
    return pl.pallas_call(
        _ada_kernel,
        out_shape=jax.ShapeDtypeStruct((depth, rows, width), jnp.float32),
        grid=(depth, width // tn),
        in_specs=[
            pl.BlockSpec((rows, D_MODEL), lambda l, n: (0, 0)),
            pl.BlockSpec((1, D_MODEL, tn), lambda l, n: (l, 0, n)),
            pl.BlockSpec((1, 1, tn), lambda l, n: (l, 0, n)),
        ],
        out_specs=pl.BlockSpec((1, rows, tn), lambda l, n: (l, 0, n)),
        compiler_params=_cparams(("arbitrary", "arbitrary")),
        name="ada",
    )(c_all, w_ada, b_ada.reshape(depth, 1, width))


def _tile_maps(bsz, with_ctx):
    per_b = TILES_PER_B if with_ctx else LAT_TILES_PER_B
    off = 0 if with_ctx else 1

    def bj(i):
        return i // per_b, i % per_b + off

    def mod_row(i):
        b, j = bj(i)
        return jnp.where(j == 0, bsz, b)

    return per_b * bsz, bj, mod_row


def _in_kernel(*refs, two_src):
    if two_src:
        ctx_ref, x_ref, g_ref, mod_ref, w_ref, qkv_ref, rest_ref = refs
        j = pl.program_id(0) % TILES_PER_B
        x = jnp.where(j == 0, ctx_ref[...], x_ref[...])
    else:
        x_ref, g_ref, mod_ref, w_ref, qkv_ref, rest_ref = refs
        x = x_ref[...]
    h = _rms(x, g_ref[...]) * (1.0 + mod_ref[1:2, :]) + mod_ref[0:1, :]
    hb = h.astype(jnp.bfloat16)
    qkv_ref[...] = jnp.dot(hb, w_ref[:, :QKV_WIDTH], preferred_element_type=jnp.float32).astype(jnp.bfloat16)
    rest_ref[...] = jnp.dot(hb, w_ref[:, QKV_WIDTH:], preferred_element_type=jnp.float32)


def _in_proj(srcs, g, mod, w_bf16, bsz):
    two_src = len(srcs) == 2
    n_tiles, bj, mod_row = _tile_maps(bsz, True)
    if two_src:
        src_specs = [
            pl.BlockSpec((None, TM, D_MODEL), lambda i: (bj(i)[0], 0, 0)),
            pl.BlockSpec((None, TM, D_MODEL), lambda i: (bj(i)[0], jnp.maximum(bj(i)[1] - 1, 0), 0)),
        ]
    else:
        src_specs = [pl.BlockSpec((None, TM, D_MODEL), lambda i: (bj(i)[0], bj(i)[1], 0))]
    return pl.pallas_call(
        functools.partial(_in_kernel, two_src=two_src),
        out_shape=[
            jax.ShapeDtypeStruct((bsz, ROWS_PER_B, QKV_WIDTH), jnp.bfloat16),
            jax.ShapeDtypeStruct((bsz, ROWS_PER_B, REST_WIDTH), jnp.float32),
        ],
        grid=(n_tiles,),
        in_specs=src_specs + [
            pl.BlockSpec((1, D_MODEL), lambda i: (0, 0)),
            pl.BlockSpec((None, N_MOD, D_MODEL), lambda i: (mod_row(i), 0, 0)),
            pl.BlockSpec((D_MODEL, IN_WIDTH), lambda i: (0, 0)),
        ],
        out_specs=[
            pl.BlockSpec((None, TM, QKV_WIDTH), lambda i: (bj(i)[0], bj(i)[1], 0)),
            pl.BlockSpec((None, TM, REST_WIDTH), lambda i: (bj(i)[0], bj(i)[1], 0)),
        ],
        compiler_params=_cparams(("arbitrary",)),
        name="in_proj",
    )(*srcs, g.reshape(1, D_MODEL), mod, w_bf16)


def _attn_kernel(q_ref, k_ref, v_ref, bias_ref, o_ref, *, need_ctx):
    lane_head = lax.broadcasted_iota(jnp.int32, (1, STACK_W), 1) // HEAD_DIM
    n_stacks = NA_WIDTH // STACK_W

    def stack_q(qg):
        zero = jnp.zeros_like(qg)
        return jnp.concatenate([jnp.where(lane_head == h, qg, zero) for h in range(HEADS_PER_STACK)], axis=0)

    def unstack(o):
        out = jnp.zeros((GRID_W, STACK_W), jnp.float32)
        for h in range(HEADS_PER_STACK):
            out = out + jnp.where(lane_head == h, o[h * GRID_W:(h + 1) * GRID_W], 0.0)
        return out

    def attend(q_rows, s, local):
        cols = slice(s * STACK_W, (s + 1) * STACK_W)
        qg = q_ref[pl.ds(q_rows, GRID_W), cols] * jnp.bfloat16(HEAD_DIM ** -0.5)
        qs = stack_q(qg)
        kc = k_ref[0:CTX_LEN, cols]
        vc = v_ref[0:CTX_LEN, cols]
        s_ctx = lax.dot_general(qs, kc, _NT, preferred_element_type=jnp.float32)
        m = jnp.max(s_ctx, axis=-1, keepdims=True)
        if local is not None:
            k_rows, delta = local
            kw = k_ref[pl.ds(k_rows, WIN_R * GRID_W), cols]
            vw = v_ref[pl.ds(k_rows, WIN_R * GRID_W), cols]
            bias = bias_ref[delta, s * HEADS_PER_STACK:(s + 1) * HEADS_PER_STACK]
            s_loc = lax.dot_general(qs, kw, _NT, preferred_element_type=jnp.float32)
            s_loc = s_loc + bias.reshape(HEADS_PER_STACK * GRID_W, WIN_R * GRID_W)
            m = jnp.maximum(m, jnp.max(s_loc, axis=-1, keepdims=True))
            p_loc = jnp.exp(s_loc - m)
        p_ctx = jnp.exp(s_ctx - m)
        denom = jnp.sum(p_ctx, axis=-1, keepdims=True)
        o = jnp.dot(p_ctx.astype(jnp.bfloat16), vc, preferred_element_type=jnp.float32)
        if local is not None:
            denom = denom + jnp.sum(p_loc, axis=-1, keepdims=True)
            o = o + jnp.dot(p_loc.astype(jnp.bfloat16), vw, preferred_element_type=jnp.float32)
        o_ref[pl.ds(q_rows, GRID_W), cols] = unstack(o / denom).astype(o_ref.dtype)

    def lat_row(r, carry):
        r0 = jnp.clip(r - WIN_R // 2, 0, GRID_ROWS - WIN_R)
        q_rows = pl.multiple_of(CTX_LEN + r * GRID_W, GRID_W)
        k_rows = pl.multiple_of(CTX_LEN + r0 * GRID_W, GRID_W)
        for s in range(n_stacks):
            attend(q_rows, s, (k_rows, r - r0))
        return carry

    lax.fori_loop(0, GRID_ROWS, lat_row, 0, unroll=4)

    if need_ctx:
        def ctx_chunk(cq, carry):
            q_rows = pl.multiple_of(cq * GRID_W, GRID_W)
            for s in range(n_stacks):
                attend(q_rows, s, None)
            return carry

        lax.fori_loop(0, CTX_LEN // GRID_W, ctx_chunk, 0, unroll=2)
    else:
        o_ref[0:CTX_LEN, :] = jnp.zeros((CTX_LEN, NA_WIDTH), o_ref.dtype)


def _attn_bias_table(rpb):
    n_rel_c = 2 * WIN_C - 1
    lead = GRID_W - WIN_C
    padded = jnp.pad(rpb, ((0, 0), (0, 0), (lead, 2 * GRID_W - lead - n_rel_c)))
    skew = jnp.tile(padded, (1, 1, GRID_W))[..., :GRID_W * (2 * GRID_W - 1)]
    skew = skew.reshape(NA_HEADS, 2 * WIN_R - 1, GRID_W, 2 * GRID_W - 1)
    toeplitz = skew[..., GRID_W - 1:]
    q_col = jnp.arange(GRID_W)[:, None]
    k_col = jnp.arange(GRID_W)[None, :]
    c_start = jnp.clip(q_col - WIN_C // 2, 0, GRID_W - WIN_C)
    ok = (k_col >= c_start) & (k_col < c_start + WIN_C)
    toeplitz = jnp.where(ok, toeplitz, NEG_INF)
    per_delta = [toeplitz[:, WIN_R - 1 - d:2 * WIN_R - 1 - d] for d in range(WIN_R)]
    tab = jnp.stack(per_delta, axis=0)
    return tab.transpose(0, 1, 3, 2, 4).reshape(WIN_R, NA_HEADS, GRID_W, WIN_R * GRID_W)


def _attention(qkv, bias, bsz, need_ctx):
    return pl.pallas_call(
        functools.partial(_attn_kernel, need_ctx=need_ctx),
        out_shape=jax.ShapeDtypeStruct((bsz, ROWS_PER_B, NA_WIDTH), jnp.bfloat16),
        grid=(bsz,),
        in_specs=[
            pl.BlockSpec((None, ROWS_PER_B, NA_WIDTH), lambda b: (b, 0, 0)),
            pl.BlockSpec((None, ROWS_PER_B, NA_WIDTH), lambda b: (b, 0, 1)),
            pl.BlockSpec((None, ROWS_PER_B, NA_WIDTH), lambda b: (b, 0, 2)),
            pl.BlockSpec((WIN_R, NA_HEADS, GRID_W, WIN_R * GRID_W), lambda b: (0, 0, 0, 0)),
        ],
        out_specs=pl.BlockSpec((None, ROWS_PER_B, NA_WIDTH), lambda b: (b, 0, 0)),
        compiler_params=_cparams(("arbitrary",)),
        name="attention",
    )(qkv, qkv, qkv, bias)


CONV_PAD = 8


def _pad_base(start):
    return start + CONV_PAD * (1 if start == 0 else 2)


def _dwconv(pad_s, start, length, w_ref, b_ref, left):
    base = _pad_base(start)
    width = w_ref.shape[0]
    y = pad_s[base - left:base - left + length, :] * w_ref[0:1, :] + b_ref[...]
    for k in range(1, width):
        y = y + pad_s[base + k - left:base + k - left + length, :] * w_ref[k:k + 1, :]
    return y


def _lru_kernel(rest_ref, lcw_ref, lcb_ref, wbd_ref, rgb_ref, lam_ref, scw_ref, scb_ref, o_ref,
                xc_s, a_s, b_s, hl_s, p_s, y_s, pad_s, *, need_ctx):
    half = LRU_WIDTH // 2
    col_rx, col_rg, col_sb, col_sc, col_sx = (k * LRU_WIDTH for k in range(5))
    segments = ((0, CTX_LEN), (CTX_LEN, SEQ))

    for start, length in segments:
        base = _pad_base(start)
        pad_s[base - CONV_PAD:base, :] = jnp.zeros((CONV_PAD, LRU_WIDTH), jnp.float32)
    pad_s[pad_s.shape[0] - CONV_PAD:, :] = jnp.zeros((CONV_PAD, LRU_WIDTH), jnp.float32)

    for start, length in segments:
        base = _pad_base(start)
        pad_s[base:base + length, :] = rest_ref[start:start + length, col_rx:col_rx + LRU_WIDTH]
    for start, length in segments:
        xc_s[start:start + length, :] = _dwconv(pad_s, start, length, lcw_ref, lcb_ref, 2)

    def coeffs(d, start, length):
        chunk = 256
        sp = jax.nn.softplus(-lam_ref[d:d + 1, :])
        for c0 in range(0, length, chunk):
            xc = xc_s[start + c0:start + c0 + chunk, :]
            pre = jnp.dot(xc.astype(jnp.bfloat16), wbd_ref[:, 2 * d * LRU_WIDTH:(2 * d + 2) * LRU_WIDTH],
                          preferred_element_type=jnp.float32) + rgb_ref[:, 2 * d * LRU_WIDTH:(2 * d + 2) * LRU_WIDTH]
            gate_r = jax.nn.sigmoid(pre[:, :LRU_WIDTH])
            gate_i = jax.nn.sigmoid(pre[:, LRU_WIDTH:])
            log_a = -RG_C * gate_r * sp
            a = jnp.exp(log_a)
            bb = jnp.sqrt(1.0 - a * a) * (gate_i * xc)
            for c in range(2):
                a_s[c, c0:c0 + chunk, :] = a[:, c * half:(c + 1) * half]
                b_s[c, c0:c0 + chunk, :] = bb[:, c * half:(c + 1) * half]

    def scan(length, reverse, h0):
        seg = length // SCAN_SEGS

        def step(i, carry):
            t = seg - 1 - i if reverse else i
            rows = pl.ds(t, SCAN_SEGS, stride=seg)
            new = []
            for c in range(2):
                h, p = carry[2 * c], carry[2 * c + 1]
                a_t = a_s[c, rows, :]
                h = a_t * h + b_s[c, rows, :]
                p = a_t * p
                hl_s[c, rows, :] = h
                p_s[c, rows, :] = p
                new += [h, p]
            return tuple(new)

        zero = jnp.zeros((SCAN_SEGS, half), jnp.float32)
        one = jnp.ones((SCAN_SEGS, half), jnp.float32)
        ends = lax.fori_loop(0, seg, step, (zero, one, zero, one), unroll=8)
        finals = []
        order = range(SCAN_SEGS - 1, -1, -1) if reverse else range(SCAN_SEGS)
        for c in range(2):
            h_end, p_end = ends[2 * c], ends[2 * c + 1]
            carry = h0[c]
            for j in order:
                rows = slice(j * seg, (j + 1) * seg)
                hl_s[c, rows, :] = hl_s[c, rows, :] + p_s[c, rows, :] * carry
                carry = h_end[j:j + 1, :] + p_end[j:j + 1, :] * carry
            finals.append(carry)
        return finals

    zero_state = [jnp.zeros((1, half), jnp.float32)] * 2
    for d, reverse in enumerate((False, True)):
        (c_start, c_len), (l_start, l_len) = segments
        coeffs(d, c_start, c_len)
        state = scan(c_len, reverse, zero_state)
        if need_ctx:
            for c in range(2):
                cols = slice(c * half, (c + 1) * half)
                prev = 0.0 if d == 0 else y_s[c_start:c_start + c_len, cols]
                y_s[c_start:c_start + c_len, cols] = prev + hl_s[c, 0:c_len, :]
        coeffs(d, l_start, l_len)
        scan(l_len, reverse, state)
        for c in range(2):
            cols = slice(c * half, (c + 1) * half)
            prev = 0.0 if d == 0 else y_s[l_start:l_start + l_len, cols]
            y_s[l_start:l_start + l_len, cols] = prev + hl_s[c, 0:l_len, :]

    out_segments = segments if need_ctx else segments[1:]
    for start, length in out_segments:
        rows = slice(start, start + length)
        y_lru = y_s[rows, :] * jax.nn.gelu(rest_ref[rows, col_rg:col_rg + LRU_WIDTH])
        o_ref[rows, 0:LRU_WIDTH] = y_lru.astype(o_ref.dtype)
        base = _pad_base(start)
        pad_s[base:base + length, :] = (rest_ref[rows, col_sc:col_sc + SC_WIDTH]
                                        * rest_ref[rows, col_sx:col_sx + SC_WIDTH])
        y_sc = rest_ref[rows, col_sb:col_sb + SC_WIDTH] * _dwconv(pad_s, start, length, scw_ref, scb_ref, 1)
        o_ref[rows, LRU_WIDTH:LRU_WIDTH + SC_WIDTH] = y_sc.astype(o_ref.dtype)
    if not need_ctx:
        o_ref[0:CTX_LEN, :] = jnp.zeros((CTX_LEN, LRU_WIDTH + SC_WIDTH), o_ref.dtype)


def _block_diag_gates(rg_w):
    eye = jnp.eye(LRU_HEADS, dtype=rg_w.dtype)
    full = jnp.einsum('dgncm,nk->dgnckm', rg_w, eye)
    full = full.reshape(2, 2, LRU_WIDTH, LRU_WIDTH)
    return full.transpose(2, 0, 1, 3).reshape(LRU_WIDTH, 4 * LRU_WIDTH)


def _lru_sconv(rest, lcw, lcb, rg_w, rg_b, rg_lam, scw, scb, bsz, need_ctx):
    wbd = _block_diag_gates(rg_w).astype(jnp.bfloat16)
    const2 = lambda b: (0, 0)
    half = LRU_WIDTH // 2
    return pl.pallas_call(
        functools.partial(_lru_kernel, need_ctx=need_ctx),
        out_shape=jax.ShapeDtypeStruct((bsz, ROWS_PER_B, LRU_WIDTH + SC_WIDTH), jnp.bfloat16),
        grid=(bsz,),
        in_specs=[
            pl.BlockSpec((None, ROWS_PER_B, REST_WIDTH), lambda b: (b, 0, 0)),
            pl.BlockSpec(lcw.shape, const2),
            pl.BlockSpec((1, LRU_WIDTH), const2),
            pl.BlockSpec((LRU_WIDTH, 4 * LRU_WIDTH), const2),
            pl.BlockSpec((1, 4 * LRU_WIDTH), const2),
            pl.BlockSpec((2, LRU_WIDTH), const2),
            pl.BlockSpec(scw.shape, const2),
            pl.BlockSpec((1, SC_WIDTH), const2),
        ],
        out_specs=pl.BlockSpec((None, ROWS_PER_B, LRU_WIDTH + SC_WIDTH), lambda b: (b, 0, 0)),
        scratch_shapes=[
            pltpu.VMEM((ROWS_PER_B, LRU_WIDTH), jnp.float32),
            pltpu.VMEM((2, SEQ, half), jnp.float32),
            pltpu.VMEM((2, SEQ, half), jnp.float32),
            pltpu.VMEM((2, SEQ, half), jnp.float32),
            pltpu.VMEM((2, SEQ, half), jnp.float32),
            pltpu.VMEM((ROWS_PER_B, LRU_WIDTH), jnp.float32),
            pltpu.VMEM((ROWS_PER_B + 3 * CONV_PAD, LRU_WIDTH), jnp.float32),
        ],
        compiler_params=_cparams(("arbitrary",)),
        name="lru_sconv",
    )(rest, lcw, lcb.reshape(1, LRU_WIDTH), wbd, rg_b.reshape(1, 4 * LRU_WIDTH), rg_lam, scw,
      scb.reshape(1, SC_WIDTH))


def _route(et):
    pe = [et[e:e + 1, :] for e in range(N_EXPERTS)]

    def top2_sum(v):
        best = v[0] + v[1]
        for a in range(E_PER_GROUP):
            for b in range(a + 1, E_PER_GROUP):
                if (a, b) != (0, 1):
                    best = jnp.maximum(best, v[a] + v[b])
        return best

    score = [top2_sum(pe[g * E_PER_GROUP:(g + 1) * E_PER_GROUP]) for g in range(N_GROUPS)]
    g_best, g_sel = score[0], jnp.zeros((1, TM), jnp.int32)
    for g in range(1, N_GROUPS):
        upd = score[g] > g_best
        g_sel = jnp.where(upd, g, g_sel)
        g_best = jnp.where(upd, score[g], g_best)
    p_in = []
    for k in range(E_PER_GROUP):
        v = pe[k]
        for g in range(1, N_GROUPS):
            v = jnp.where(g_sel == g, pe[g * E_PER_GROUP + k], v)
        p_in.append(v)
    m1, i1 = p_in[0], jnp.zeros((1, TM), jnp.int32)
    for k in range(1, E_PER_GROUP):
        upd = p_in[k] > m1
        i1 = jnp.where(upd, k, i1)
        m1 = jnp.where(upd, p_in[k], m1)
    m2, i2 = jnp.full((1, TM), -1.0, jnp.float32), jnp.zeros((1, TM), jnp.int32)
    for k in range(E_PER_GROUP):
        cand = jnp.where(i1 == k, -2.0, p_in[k])
        upd = cand > m2
        i2 = jnp.where(upd, k, i2)
        m2 = jnp.where(upd, cand, m2)
    lo, hi = jnp.minimum(i1, i2), jnp.maximum(i1, i2)
    pair = jnp.where(lo == 0, hi - 1, jnp.where(lo == 1, hi + 1, N_PAIRS - 1))
    bucket = g_sel * N_PAIRS + pair
    w1 = m1 / (m1 + m2)
    w2 = m2 / (m1 + m2)
    return bucket, jnp.where(i1 < i2, w1, w2), jnp.where(i1 < i2, w2, w1)


def _out_kernel(*refs, two_src):
    if two_src:
        ctx_ref, x_ref, *rest = refs
    else:
        x_ref, *rest = refs
    (yna_ref, yls_ref, mod_ref, og_ref, w_ref, fg_ref, wrt_ref, brt_ref, _, xo_ref, pos_ref, meta_ref, xs_ref,
     cnt_s, cur_s, alloc_s, tb_s, row_s, pos_v, pos_sm, sem_p, sem_r) = rest
    step = pl.program_id(0)
    n_steps = pl.num_programs(0)
    cur = step % 2

    def row_copy(buf, k, dst_row):
        return pltpu.make_async_copy(row_s.at[buf, pl.ds(k, 1), :], xs_ref.at[pl.ds(dst_row, 1), :], sem_r.at[buf])

    def pos_copy(buf):
        return pltpu.make_async_copy(pos_v.at[buf], pos_sm.at[buf], sem_p.at[buf])

    def scatter_rows(buf):
        pos_copy(buf).wait()
        for k in range(TM):
            row_copy(buf, k, pos_sm[buf, 0, k]).start()

    def drain(buf):
        def body(k, c):
            row_copy(buf, 0, 0).wait()
            return c
        lax.fori_loop(0, TM, body, 0, unroll=8)

    @pl.when(step == 0)
    def _():
        cnt_s[...] = jnp.zeros_like(cnt_s)
        cur_s[...] = jnp.zeros_like(cur_s)
        alloc_s[...] = jnp.zeros_like(alloc_s)
        tb_s[...] = jnp.zeros_like(tb_s)

    def tile_body(first):
        if not first:
            scatter_rows(1 - cur)
        if two_src:
            x = jnp.where(step % TILES_PER_B == 0, ctx_ref[...], x_ref[...])
        else:
            x = x_ref[...]
        yna = yna_ref[...].astype(jnp.float32)
        yls = yls_ref[...].astype(jnp.float32)
        merged = jnp.concatenate([
            _rms(yna, og_ref[:, :NA_WIDTH]),
            _rms(yls[:, :LRU_WIDTH], og_ref[:, NA_WIDTH:NA_WIDTH + LRU_WIDTH]),
            _rms(yls[:, LRU_WIDTH:], og_ref[:, NA_WIDTH + LRU_WIDTH:]),
        ], axis=-1).astype(jnp.bfloat16)
        y = jnp.dot(merged, w_ref[...], preferred_element_type=jnp.float32)
        x_new = x + mod_ref[2:3, :] * y
        xo_ref[...] = x_new
        h2 = _rms(x_new, fg_ref[...]) * (1.0 + mod_ref[4:5, :]) + mod_ref[3:4, :]

        @pl.when(step >= 2)
        def _():
            drain(cur)

        lt = lax.dot_general(wrt_ref[...], h2, _NT, precision=_HI, preferred_element_type=jnp.float32) + brt_ref[...]
        bucket, w_lo, w_hi = _route(jnp.exp(lt - jnp.max(lt, axis=0, keepdims=True)))

        b_iota = lax.broadcasted_iota(jnp.int32, (BUCKET_ROWS, TM), 0)
        onehot = b_iota == bucket
        tri = (lax.broadcasted_iota(jnp.int32, (TM, TM), 0) <= lax.broadcasted_iota(jnp.int32, (TM, TM), 1))
        as_bf16 = lambda mask: jnp.where(mask, 1.0, 0.0).astype(jnp.bfloat16)
        cum = jnp.dot(as_bf16(onehot), as_bf16(tri), preferred_element_type=jnp.float32)
        cnt_new = cum[:, TM - 1:TM].astype(jnp.int32)
        cnt_old = cnt_s[:, 0:1]
        open_id = cur_s[:, 0:1]
        alloc = alloc_s[0:1, 0:1]
        shift = TM.bit_length() - 1
        q_last = (cnt_old + cnt_new - 1) >> shift
        q_prev = (cnt_old - 1) >> shift
        opens = jnp.where(cnt_new > 0, q_last - q_prev, 0)
        lower = (lax.broadcasted_iota(jnp.int32, (BUCKET_ROWS, BUCKET_ROWS), 1)
                 < lax.broadcasted_iota(jnp.int32, (BUCKET_ROWS, BUCKET_ROWS), 0))
        opens_b = jnp.broadcast_to(opens, (BUCKET_ROWS, LANES)).astype(jnp.float32).astype(jnp.bfloat16)
        before = jnp.dot(as_bf16(lower), opens_b, preferred_element_type=jnp.float32)[:, 0:1].astype(jnp.int32)
        new_id = alloc + before
        rank = cnt_old + cum.astype(jnp.int32) - 1
        tile_id = jnp.where((opens > 0) & ((rank >> shift) == q_last), new_id, open_id)
        slot = tile_id * TM + (rank & (TM - 1))
        pos = jnp.sum(jnp.where(onehot, slot, 0).astype(jnp.float32), axis=0, keepdims=True).astype(jnp.int32)
        pos_ref[...] = pos

        lane_id = lax.broadcasted_iota(jnp.int32, (BUCKET_ROWS, TM), 1)
        opened_here = (opens > 0) & (new_id == lane_id)
        opened_bucket = jnp.max(jnp.where(opened_here, b_iota, -1).astype(jnp.float32), axis=0,
                                keepdims=True).astype(jnp.int32)
        tb = jnp.where(opened_bucket >= 0, opened_bucket, tb_s[0:1, :])
        alloc_new = alloc + jnp.sum(opens.astype(jnp.float32), axis=0, keepdims=True).astype(jnp.int32)
        tb_s[...] = jnp.broadcast_to(tb, tb_s.shape)
        cnt_s[...] = jnp.broadcast_to(cnt_old + cnt_new, cnt_s.shape)
        cur_s[...] = jnp.broadcast_to(jnp.where(opens > 0, new_id, open_id), cur_s.shape)
        alloc_s[...] = jnp.broadcast_to(alloc_new, alloc_s.shape)
        meta_ref[0:1, :] = tb
        meta_ref[1:2, :] = jnp.broadcast_to(alloc_new, (1, TM))
        meta_ref[2:8, :] = jnp.zeros((6, TM), jnp.int32)

        payload = jnp.concatenate([w_lo, w_hi, jnp.zeros((LANES - 2, TM), jnp.float32)], axis=0)
        row_s[cur, :, :D_MODEL] = h2
        row_s[cur, :, D_MODEL:] = payload.T
        pos_v[cur] = jnp.broadcast_to(pos, (8, TM))
        pos_copy(cur).start()

    @pl.when(step == 0)
    def _():
        tile_body(True)

    @pl.when(step > 0)
    def _():
        tile_body(False)

    @pl.when(step == n_steps - 1)
    def _():
        scatter_rows(cur)
        drain(1 - cur)
        drain(cur)


def _out_proj(srcs, yna, yls, mod, out_g, w_bf16, ffn_g, w_router, b_router, bsz, with_ctx):
    two_src = len(srcs) == 2
    n_tiles, bj, mod_row = _tile_maps(bsz, with_ctx)
    assert n_tiles >= 2
    n_sorted = n_tiles + N_BUCKETS
    tile = lambda width: pl.BlockSpec((None, TM, width), lambda i: (bj(i)[0], bj(i)[1], 0))
    const2 = lambda i: (0, 0)
    if two_src:
        src_specs = [
            pl.BlockSpec((None, TM, D_MODEL), lambda i: (bj(i)[0], 0, 0)),
            pl.BlockSpec((None, TM, D_MODEL), lambda i: (bj(i)[0], jnp.maximum(bj(i)[1] - 1, 0), 0)),
        ]
    else:
        src_specs = [tile(D_MODEL)]
    xs_init = jnp.zeros((n_sorted * TM, ROW_WIDTH), jnp.float32)
    return pl.pallas_call(
        functools.partial(_out_kernel, two_src=two_src),
        out_shape=[
            jax.ShapeDtypeStruct((bsz, ROWS_PER_B, D_MODEL), jnp.float32),
            jax.ShapeDtypeStruct((n_tiles, 1, TM), jnp.int32),
            jax.ShapeDtypeStruct((8, TM), jnp.int32),
            jax.ShapeDtypeStruct(xs_init.shape, jnp.float32),
        ],
        grid=(n_tiles,),
        in_specs=src_specs + [
            tile(NA_WIDTH),
            tile(LRU_WIDTH + SC_WIDTH),
            pl.BlockSpec((None, N_MOD, D_MODEL), lambda i: (mod_row(i), 0, 0)),
            pl.BlockSpec((1, D_MODEL), const2),
            pl.BlockSpec((D_MODEL, D_MODEL), const2),
            pl.BlockSpec((1, D_MODEL), const2),
            pl.BlockSpec((N_EXPERTS, D_MODEL), const2),
            pl.BlockSpec((N_EXPERTS, 1), const2),
            pl.BlockSpec(memory_space=pl.ANY),
        ],
        out_specs=[
            tile(D_MODEL),
            pl.BlockSpec((None, 1, TM), lambda i: (i, 0, 0)),
            pl.BlockSpec((8, TM), const2),
            pl.BlockSpec(memory_space=pl.ANY),
        ],
        scratch_shapes=[
            pltpu.VMEM((BUCKET_ROWS, LANES), jnp.int32),
            pltpu.VMEM((BUCKET_ROWS, LANES), jnp.int32),
            pltpu.VMEM((8, LANES), jnp.int32),
            pltpu.VMEM((8, TM), jnp.int32),
            pltpu.VMEM((2, TM, ROW_WIDTH), jnp.float32),
            pltpu.VMEM((2, 8, TM), jnp.int32),
            pltpu.SMEM((2, 8, TM), jnp.int32),
            pltpu.SemaphoreType.DMA((2,)),
            pltpu.SemaphoreType.DMA((2,)),
        ],
        input_output_aliases={len(srcs) + 8: 3},
        compiler_params=_cparams(("arbitrary",)),
        name="out_proj_route",
    )(*srcs, yna, yls, mod, out_g.reshape(1, D_MODEL), w_bf16, ffn_g.reshape(1, D_MODEL), w_router.T,
      b_router.reshape(N_EXPERTS, 1), xs_init)


def _moe_kernel(src_ref, e0_ref, e1_ref, used_ref, xs_ref, wg0, wu0, wd0, wg1, wu1, wd1, o_ref, wg_s, wu_s, wd_s):
    n = pl.program_id(0)
    prev = jnp.maximum(n - 1, 0)
    e0, e1 = e0_ref[n], e1_ref[n]
    fresh = (n == 0) | (e0 != e0_ref[prev]) | (e1 != e1_ref[prev])

    @pl.when(fresh)
    def _():
        for k, (g, u, d) in enumerate(((wg0, wu0, wd0), (wg1, wu1, wd1))):
            wg_s[k] = g[...].astype(jnp.bfloat16)
            wu_s[k] = u[...].astype(jnp.bfloat16)
            wd_s[k] = d[...].astype(jnp.bfloat16)

    @pl.when(n < used_ref[0])
    def _():
        xb = xs_ref[:, :D_MODEL].astype(jnp.bfloat16)
        weights = (xs_ref[:, D_MODEL:D_MODEL + 1], xs_ref[:, D_MODEL + 1:D_MODEL + 2])
        out = jnp.zeros((TM, D_MODEL), jnp.float32)
        for k, wk in enumerate(weights):
            gate = jnp.dot(xb, wg_s[k], preferred_element_type=jnp.float32)
            up = jnp.dot(xb, wu_s[k], preferred_element_type=jnp.float32)
            hid = (gate * jax.nn.sigmoid(gate)) * up
            out = out + wk * jnp.dot(hid.astype(jnp.bfloat16), wd_s[k], preferred_element_type=jnp.float32)
        o_ref[...] = out


def _moe(sched, xs, w_gate, w_up, w_down, layer):
    n_tiles = xs.shape[0] // TM
    src, e0, e1, used = sched
    first = lambda n, s, a, b, u: (layer, a[n], 0, 0)
    second = lambda n, s, a, b, u: (layer, b[n], 0, 0)
    gate_spec = lambda m: pl.BlockSpec((None, None, D_MODEL, D_EXPERT), m)
    down_spec = lambda m: pl.BlockSpec((None, None, D_EXPERT, D_MODEL), m)
    return pl.pallas_call(
        _moe_kernel,
        out_shape=jax.ShapeDtypeStruct((n_tiles * TM, D_MODEL), jnp.float32),
        grid_spec=pltpu.PrefetchScalarGridSpec(
            num_scalar_prefetch=4,
            grid=(n_tiles,),
            in_specs=[
                pl.BlockSpec((TM, ROW_WIDTH), lambda n, s, a, b, u: (s[n], 0)),
                gate_spec(first), gate_spec(first), down_spec(first),
                gate_spec(second), gate_spec(second), down_spec(second),
            ],
            out_specs=pl.BlockSpec((TM, D_MODEL), lambda n, s, a, b, u: (s[n], 0)),
            scratch_shapes=[
                pltpu.VMEM((2, D_MODEL, D_EXPERT), jnp.bfloat16),
                pltpu.VMEM((2, D_MODEL, D_EXPERT), jnp.bfloat16),
                pltpu.VMEM((2, D_EXPERT, D_MODEL), jnp.bfloat16),
            ],
        ),
        compiler_params=_cparams(("arbitrary",)),
        name="moe_experts",
    )(src, e0, e1, used, xs, w_gate, w_up, w_down, w_gate, w_up, w_down)


def _schedule(meta, n_tiles):
    tile_bucket = meta[0, :n_tiles]
    used = meta[1, 0]
    ids = jnp.arange(n_tiles, dtype=jnp.int32)
    order = jnp.argsort(jnp.where(ids < used, tile_bucket, N_BUCKETS), stable=True).astype(jnp.int32)
    src = jnp.where(ids < used, order, order[jnp.maximum(used - 1, 0)])
    bucket = tile_bucket[src]
    group, pair = bucket // N_PAIRS, bucket % N_PAIRS
    lo = jnp.array([0, 0, 0, 1, 1, 2], jnp.int32)[pair]
    hi = jnp.array([1, 2, 3, 2, 3, 3], jnp.int32)[pair]
    return src, group * E_PER_GROUP + lo, group * E_PER_GROUP + hi, used.reshape(1)


def _combine_kernel(pos_ref, x_ref, mod_ref, fg_ref, ys_ref, o_ref, buf, sem, *, final):
    i = pl.program_id(0)
    n = pl.num_programs(0)

    def row_copy(src_row, slot, k):
        return pltpu.make_async_copy(ys_ref.at[pl.ds(src_row, 1), :], buf.at[slot, pl.ds(k, 1), :], sem.at[slot])

    def issue(tile, slot):
        for k in range(TM):
            row_copy(pos_ref[tile * TM + k], slot, k).start()

    @pl.when(i == 0)
    def _():
        issue(0, 0)

    @pl.when(i + 1 < n)
    def _():
        issue(i + 1, (i + 1) % 2)

    slot = i % 2

    def drain(k, c):
        row_copy(0, slot, 0).wait()
        return c
    lax.fori_loop(0, TM, drain, 0, unroll=8)

    x_new = x_ref[...] + mod_ref[5:6, :] * buf[slot]
    if final:
        x_new = _rms(x_new, fg_ref[...])
    o_ref[...] = x_new


def _combine(pos, x_all, mod, final_g, ys, bsz, with_ctx, final):
    n_tiles, bj, mod_row = _tile_maps(bsz, with_ctx)
    if final:
        out_shape = jax.ShapeDtypeStruct((bsz, SEQ, D_MODEL), jnp.float32)
        out_spec = pl.BlockSpec((None, TM, D_MODEL), lambda i, p: (bj(i)[0], bj(i)[1] - 1, 0))
    else:
        out_shape = jax.ShapeDtypeStruct((bsz, ROWS_PER_B, D_MODEL), jnp.float32)
        out_spec = pl.BlockSpec((None, TM, D_MODEL), lambda i, p: (bj(i)[0], bj(i)[1], 0))
    return pl.pallas_call(
        functools.partial(_combine_kernel, final=final),
        out_shape=out_shape,
        grid_spec=pltpu.PrefetchScalarGridSpec(
            num_scalar_prefetch=1,
            grid=(n_tiles,),
            in_specs=[
                pl.BlockSpec((None, TM, D_MODEL), lambda i, p: (bj(i)[0], bj(i)[1], 0)),
                pl.BlockSpec((None, N_MOD, D_MODEL), lambda i, p: (mod_row(i), 0, 0)),
                pl.BlockSpec((1, D_MODEL), lambda i, p: (0, 0)),
                pl.BlockSpec(memory_space=pl.ANY),
            ],
            out_specs=out_spec,
            scratch_shapes=[pltpu.VMEM((2, TM, D_MODEL), jnp.float32), pltpu.SemaphoreType.DMA((2,))],
        ),
        compiler_params=_cparams(("arbitrary",)),
        name="combine",
    )(pos.reshape(-1), x_all, mod, final_g.reshape(1, D_MODEL), ys)


def kernel(x, c, ctx, c_ctx, w_ada, b_ada, norm_mix_g, w_in, lru_conv_w, lru_conv_b, rg_w, rg_b, rg_lam, na_rpb,
           sc_conv_w, sc_conv_b, mix_out_g, w_out, norm_ffn_g, w_router, b_router, w_gate, w_up, w_down, final_g):
    bsz = x.shape[0]
    mod_rows = -(-(bsz + 1) // 8) * 8
    c_all = jnp.zeros((mod_rows, D_MODEL), jnp.float32).at[:bsz].set(c).at[bsz].set(c_ctx)
    mods = _ada(c_all, w_ada, b_ada).reshape(DEPTH, mod_rows, N_MOD, D_MODEL)

    x_all = None
    out = None
    for l in range(DEPTH):
        need_ctx = l < DEPTH - 1
        mod = mods[l]
        srcs = (ctx, x) if l == 0 else (x_all,)
        qkv, rest = _in_proj(srcs, norm_mix_g[l], mod, w_in[l].astype(jnp.bfloat16), bsz)
        yna = _attention(qkv, _attn_bias_table(na_rpb[l]), bsz, need_ctx)
        yls = _lru_sconv(rest, lru_conv_w[l], lru_conv_b[l], rg_w[l], rg_b[l], rg_lam[l], sc_conv_w[l],
                         sc_conv_b[l], bsz, need_ctx)
        x_mid, pos, meta, xs = _out_proj(srcs, yna, yls, mod, mix_out_g[l], w_out[l].astype(jnp.bfloat16),
                                         norm_ffn_g[l], w_router, b_router, bsz, need_ctx)
        ys = _moe(_schedule(meta, xs.shape[0] // TM), xs, w_gate, w_up, w_down, l)
        res = _combine(pos, x_mid, mod, final_g, ys, bsz, need_ctx, final=not need_ctx)
        if need_ctx:
            x_all = res
        else:
            out = res
    return out
```

```python
import functools

import jax
import jax.numpy as jnp
from jax import lax
from jax.experimental import pallas as pl
from jax.experimental.pallas import tpu as pltpu

D_MODEL = 1024
SEQ = 2048
CTX_LEN = 256
ROWS_PER_B = CTX_LEN + SEQ
DEPTH = 2
N_MOD = 6
EPS = 1e-6
NEG_INF = -1e30

GRID_W = 64
GRID_ROWS = SEQ // GRID_W
HEAD_DIM = 64
NA_WIDTH = 512
NA_HEADS = 8
LRU_WIDTH = 256
LRU_HEADS = 4
LRU_BLOCK = 64
SC_WIDTH = 256
QKV_WIDTH = 3 * NA_WIDTH
REST_WIDTH = 2 * LRU_WIDTH + 3 * SC_WIDTH
IN_WIDTH = QKV_WIDTH + REST_WIDTH
RG_C = 8.0
WIN_R = 8
WIN_C = 16
N_EXPERTS = 16
N_GROUPS = 4
E_PER_GROUP = 4
N_PAIRS = 6
N_BUCKETS = N_GROUPS * N_PAIRS
D_EXPERT = 512

TM = 256
TILES_PER_B = ROWS_PER_B // TM
LAT_TILES_PER_B = SEQ // TM
LANES = 128
BUCKET_ROWS = 32
ROW_WIDTH = D_MODEL + LANES
HEADS_PER_STACK = 4
STACK_W = HEADS_PER_STACK * HEAD_DIM
SCAN_SEGS = 8
VMEM_LIMIT = 56 * 1024 * 1024

_HI = lax.Precision.HIGHEST
_NT = (((1,), (1,)), ((), ()))


def _cparams(sem):
    return pltpu.CompilerParams(dimension_semantics=sem, vmem_limit_bytes=VMEM_LIMIT)


def _rms(v, g):
    return v * lax.rsqrt(jnp.mean(v * v, axis=-1, keepdims=True) + EPS) * g


def _ada_kernel(c_ref, w_ref, b_ref, o_ref):
    cond = c_ref[...]
    cond = cond * jax.nn.sigmoid(cond)
    o_ref[0] = jnp.dot(cond, w_ref[0], precision=_HI, preferred_element_type=jnp.float32) + b_ref[0]


def _ada(c_all, w_ada, b_ada):
    depth, _, width = w_ada.shape
    rows = c_all.shape[0]
    tn = 1536
    return pl.pallas_call(
        _ada_kernel,
        out_shape=jax.ShapeDtypeStruct((depth, rows, width), jnp.float32),
        grid=(depth, width // tn),
        in_specs=[
            pl.BlockSpec((rows, D_MODEL), lambda l, n: (0, 0)),
            pl.BlockSpec((1, D_MODEL, tn), lambda l, n: (l, 0, n)),
            pl.BlockSpec((1, 1, tn), lambda l, n: (l, 0, n)),
        ],
        out_specs=pl.BlockSpec((1, rows, tn), lambda l, n: (l, 0, n)),
        compiler_params=_cparams(("arbitrary", "arbitrary")),
        name="ada",
    )(c_all, w_ada, b_ada.reshape(depth, 1, width))


def _tile_maps(bsz, with_ctx):
    per_b = TILES_PER_B if with_ctx else LAT_TILES_PER_B
    off = 0 if with_ctx else 1

    def bj(i):
        return i // per_b, i % per_b + off

    def mod_row(i):
        b, j = bj(i)
        return jnp.where(j == 0, bsz, b)

    return per_b * bsz, bj, mod_row


def _in_kernel(*refs, two_src):
    if two_src:
        ctx_ref, x_ref, g_ref, mod_ref, w_ref, qkv_ref, rest_ref = refs
        j = pl.program_id(0) % TILES_PER_B
        x = jnp.where(j == 0, ctx_ref[...], x_ref[...])
    else:
        x_ref, g_ref, mod_ref, w_ref, qkv_ref, rest_ref = refs
        x = x_ref[...]
    h = _rms(x, g_ref[...]) * (1.0 + mod_ref[1:2, :]) + mod_ref[0:1, :]
    hb = h.astype(jnp.bfloat16)
    qkv_ref[...] = jnp.dot(hb, w_ref[:, :QKV_WIDTH], preferred_element_type=jnp.float32).astype(jnp.bfloat16)
    rest_ref[...] = jnp.dot(hb, w_ref[:, QKV_WIDTH:], preferred_element_type=jnp.float32)


def _in_proj(srcs, g, mod, w_bf16, bsz):
    two_src = len(srcs) == 2
    n_tiles, bj, mod_row = _tile_maps(bsz, True)
    if two_src:
        src_specs = [
            pl.BlockSpec((None, TM, D_MODEL), lambda i: (bj(i)[0], 0, 0)),
            pl.BlockSpec((None, TM, D_MODEL), lambda i: (bj(i)[0], jnp.maximum(bj(i)[1] - 1, 0), 0)),
        ]
    else:
        src_specs = [pl.BlockSpec((None, TM, D_MODEL), lambda i: (bj(i)[0], bj(i)[1], 0))]
    return pl.pallas_call(
        functools.partial(_in_kernel, two_src=two_src),
        out_shape=[
            jax.ShapeDtypeStruct((bsz, ROWS_PER_B, QKV_WIDTH), jnp.bfloat16),
            jax.ShapeDtypeStruct((bsz, ROWS_PER_B, REST_WIDTH), jnp.float32),
        ],
        grid=(n_tiles,),
        in_specs=src_specs + [
            pl.BlockSpec((1, D_MODEL), lambda i: (0, 0)),
            pl.BlockSpec((None, N_MOD, D_MODEL), lambda i: (mod_row(i), 0, 0)),
            pl.BlockSpec((D_MODEL, IN_WIDTH), lambda i: (0, 0)),
        ],
        out_specs=[
            pl.BlockSpec((None, TM, QKV_WIDTH), lambda i: (bj(i)[0], bj(i)[1], 0)),
            pl.BlockSpec((None, TM, REST_WIDTH), lambda i: (bj(i)[0], bj(i)[1], 0)),
        ],
        compiler_params=_cparams(("arbitrary",)),
        name="in_proj",
    )(*srcs, g.reshape(1, D_MODEL), mod, w_bf16)


def _attn_kernel(q_ref, k_ref, v_ref, bias_ref, o_ref, *, need_ctx):
    lane_head = lax.broadcasted_iota(jnp.int32, (1, STACK_W), 1) // HEAD_DIM
    n_stacks = NA_WIDTH // STACK_W

    def stack_q(qg):
        zero = jnp.zeros_like(qg)
        return jnp.concatenate([jnp.where(lane_head == h, qg, zero) for h in range(HEADS_PER_STACK)], axis=0)

    def unstack(o):
        out = jnp.zeros((GRID_W, STACK_W), jnp.float32)
        for h in range(HEADS_PER_STACK):
            out = out + jnp.where(lane_head == h, o[h * GRID_W:(h + 1) * GRID_W], 0.0)
        return out

    def attend(q_rows, s, local):
        cols = slice(s * STACK_W, (s + 1) * STACK_W)
        qg = q_ref[pl.ds(q_rows, GRID_W), cols] * jnp.bfloat16(HEAD_DIM ** -0.5)
        qs = stack_q(qg)
        kc = k_ref[0:CTX_LEN, cols]
        vc = v_ref[0:CTX_LEN, cols]
        s_ctx = lax.dot_general(qs, kc, _NT, preferred_element_type=jnp.float32)
        m = jnp.max(s_ctx, axis=-1, keepdims=True)
        if local is not None:
            k_rows, delta = local
            kw = k_ref[pl.ds(k_rows, WIN_R * GRID_W), cols]
            vw = v_ref[pl.ds(k_rows, WIN_R * GRID_W), cols]
            bias = bias_ref[delta, s * HEADS_PER_STACK:(s + 1) * HEADS_PER_STACK]
            s_loc = lax.dot_general(qs, kw, _NT, preferred_element_type=jnp.float32)
            s_loc = s_loc + bias.reshape(HEADS_PER_STACK * GRID_W, WIN_R * GRID_W)
            m = jnp.maximum(m, jnp.max(s_loc, axis=-1, keepdims=True))
            p_loc = jnp.exp(s_loc - m)
        p_ctx = jnp.exp(s_ctx - m)
        denom = jnp.sum(p_ctx, axis=-1, keepdims=True)
        o = jnp.dot(p_ctx.astype(jnp.bfloat16), vc, preferred_element_type=jnp.float32)
        if local is not None:
            denom = denom + jnp.sum(p_loc, axis=-1, keepdims=True)
            o = o + jnp.dot(p_loc.astype(jnp.bfloat16), vw, preferred_element_type=jnp.float32)
        o_ref[pl.ds(q_rows, GRID_W), cols] = unstack(o / denom).astype(o_ref.dtype)

    def lat_row(r, carry):
        r0 = jnp.clip(r - WIN_R // 2, 0, GRID_ROWS - WIN_R)
        q_rows = pl.multiple_of(CTX_LEN + r * GRID_W, GRID_W)
        k_rows = pl.multiple_of(CTX_LEN + r0 * GRID_W, GRID_W)
        for s in range(n_stacks):
            attend(q_rows, s, (k_rows, r - r0))
        return carry

    lax.fori_loop(0, GRID_ROWS, lat_row, 0, unroll=4)

    if need_ctx:
        def ctx_chunk(cq, carry):
            q_rows = pl.multiple_of(cq * GRID_W, GRID_W)
            for s in range(n_stacks):
                attend(q_rows, s, None)
            return carry

        lax.fori_loop(0, CTX_LEN // GRID_W, ctx_chunk, 0, unroll=2)
    else:
        o_ref[0:CTX_LEN, :] = jnp.zeros((CTX_LEN, NA_WIDTH), o_ref.dtype)


def _attn_bias_table(rpb):
    n_rel_c = 2 * WIN_C - 1
    lead = GRID_W - WIN_C
    padded = jnp.pad(rpb, ((0, 0), (0, 0), (lead, 2 * GRID_W - lead - n_rel_c)))
    skew = jnp.tile(padded, (1, 1, GRID_W))[..., :GRID_W * (2 * GRID_W - 1)]
    skew = skew.reshape(NA_HEADS, 2 * WIN_R - 1, GRID_W, 2 * GRID_W - 1)
    toeplitz = skew[..., GRID_W - 1:]
    q_col = jnp.arange(GRID_W)[:, None]
    k_col = jnp.arange(GRID_W)[None, :]
    c_start = jnp.clip(q_col - WIN_C // 2, 0, GRID_W - WIN_C)
    ok = (k_col >= c_start) & (k_col < c_start + WIN_C)
    toeplitz = jnp.where(ok, toeplitz, NEG_INF)
    per_delta = [toeplitz[:, WIN_R - 1 - d:2 * WIN_R - 1 - d] for d in range(WIN_R)]
    tab = jnp.stack(per_delta, axis=0)
    return tab.transpose(0, 1, 3, 2, 4).reshape(WIN_R, NA_HEADS, GRID_W, WIN_R * GRID_W)


def _attention(qkv, bias, bsz, need_ctx):
    return pl.pallas_call(
        functools.partial(_attn_kernel, need_ctx=need_ctx),
        out_shape=jax.ShapeDtypeStruct((bsz, ROWS_PER_B, NA_WIDTH), jnp.bfloat16),
        grid=(bsz,),
        in_specs=[
            pl.BlockSpec((None, ROWS_PER_B, NA_WIDTH), lambda b: (b, 0, 0)),
            pl.BlockSpec((None, ROWS_PER_B, NA_WIDTH), lambda b: (b, 0, 1)),
            pl.BlockSpec((None, ROWS_PER_B, NA_WIDTH), lambda b: (b, 0, 2)),
            pl.BlockSpec((WIN_R, NA_HEADS, GRID_W, WIN_R * GRID_W), lambda b: (0, 0, 0, 0)),
        ],
        out_specs=pl.BlockSpec((None, ROWS_PER_B, NA_WIDTH), lambda b: (b, 0, 0)),
        compiler_params=_cparams(("arbitrary",)),
        name="attention",
    )(qkv, qkv, qkv, bias)


CONV_PAD = 8


def _pad_base(start):
    return start + CONV_PAD * (1 if start == 0 else 2)


def _dwconv(pad_s, start, length, w_ref, b_ref, left):
    base = _pad_base(start)
    width = w_ref.shape[0]
    y = pad_s[base - left:base - left + length, :] * w_ref[0:1, :] + b_ref[...]
    for k in range(1, width):
        y = y + pad_s[base + k - left:base + k - left + length, :] * w_ref[k:k + 1, :]
    return y


def _lru_kernel(rest_ref, lcw_ref, lcb_ref, wbd_ref, rgb_ref, lam_ref, scw_ref, scb_ref, o_ref,
                xc_s, a_s, b_s, hl_s, p_s, y_s, pad_s, *, need_ctx):
    half = LRU_WIDTH // 2
    col_rx, col_rg, col_sb, col_sc, col_sx = (k * LRU_WIDTH for k in range(5))
    segments = ((0, CTX_LEN), (CTX_LEN, SEQ))

    for start, length in segments:
        base = _pad_base(start)
        pad_s[base - CONV_PAD:base, :] = jnp.zeros((CONV_PAD, LRU_WIDTH), jnp.float32)
    pad_s[pad_s.shape[0] - CONV_PAD:, :] = jnp.zeros((CONV_PAD, LRU_WIDTH), jnp.float32)

    for start, length in segments:
        base = _pad_base(start)
        pad_s[base:base + length, :] = rest_ref[start:start + length, col_rx:col_rx + LRU_WIDTH]
    for start, length in segments:
        xc_s[start:start + length, :] = _dwconv(pad_s, start, length, lcw_ref, lcb_ref, 2)

    def coeffs(d, start, length):
        chunk = 256
        sp = jax.nn.softplus(-lam_ref[d:d + 1, :])
        for c0 in range(0, length, chunk):
            xc = xc_s[start + c0:start + c0 + chunk, :]
            pre = jnp.dot(xc.astype(jnp.bfloat16), wbd_ref[:, 2 * d * LRU_WIDTH:(2 * d + 2) * LRU_WIDTH],
                          preferred_element_type=jnp.float32) + rgb_ref[:, 2 * d * LRU_WIDTH:(2 * d + 2) * LRU_WIDTH]
            gate_r = jax.nn.sigmoid(pre[:, :LRU_WIDTH])
            gate_i = jax.nn.sigmoid(pre[:, LRU_WIDTH:])
            log_a = -RG_C * gate_r * sp
            a = jnp.exp(log_a)
            bb = jnp.sqrt(1.0 - a * a) * (gate_i * xc)
            for c in range(2):
                a_s[c, c0:c0 + chunk, :] = a[:, c * half:(c + 1) * half]
                b_s[c, c0:c0 + chunk, :] = bb[:, c * half:(c + 1) * half]

    def scan(length, reverse, h0):
        seg = length // SCAN_SEGS

        def step(i, carry):
            t = seg - 1 - i if reverse else i
            rows = pl.ds(t, SCAN_SEGS, stride=seg)
            new = []
            for c in range(2):
                h, p = carry[2 * c], carry[2 * c + 1]
                a_t = a_s[c, rows, :]
                h = a_t * h + b_s[c, rows, :]
                p = a_t * p
                hl_s[c, rows, :] = h
                p_s[c, rows, :] = p
                new += [h, p]
            return tuple(new)

        zero = jnp.zeros((SCAN_SEGS, half), jnp.float32)
        one = jnp.ones((SCAN_SEGS, half), jnp.float32)
        ends = lax.fori_loop(0, seg, step, (zero, one, zero, one), unroll=8)
        finals = []
        order = range(SCAN_SEGS - 1, -1, -1) if reverse else range(SCAN_SEGS)
        for c in range(2):
            h_end, p_end = ends[2 * c], ends[2 * c + 1]
            carry = h0[c]
            for j in order:
                rows = slice(j * seg, (j + 1) * seg)
                hl_s[c, rows, :] = hl_s[c, rows, :] + p_s[c, rows, :] * carry
                carry = h_end[j:j + 1, :] + p_end[j:j + 1, :] * carry
            finals.append(carry)
        return finals

    zero_state = [jnp.zeros((1, half), jnp.float32)] * 2
    for d, reverse in enumerate((False, True)):
        (c_start, c_len), (l_start, l_len) = segments
        coeffs(d, c_start, c_len)
        state = scan(c_len, reverse, zero_state)
        if need_ctx:
            for c in range(2):
                cols = slice(c * half, (c + 1) * half)
                prev = 0.0 if d == 0 else y_s[c_start:c_start + c_len, cols]
                y_s[c_start:c_start + c_len, cols] = prev + hl_s[c, 0:c_len, :]
        coeffs(d, l_start, l_len)
        scan(l_len, reverse, state)
        for c in range(2):
            cols = slice(c * half, (c + 1) * half)
            prev = 0.0 if d == 0 else y_s[l_start:l_start + l_len, cols]
            y_s[l_start:l_start + l_len, cols] = prev + hl_s[c, 0:l_len, :]

    out_segments = segments if need_ctx else segments[1:]
    for start, length in out_segments:
        rows = slice(start, start + length)
        y_lru = y_s[rows, :] * jax.nn.gelu(rest_ref[rows, col_rg:col_rg + LRU_WIDTH])
        o_ref[rows, 0:LRU_WIDTH] = y_lru.astype(o_ref.dtype)
        base = _pad_base(start)
        pad_s[base:base + length, :] = (rest_ref[rows, col_sc:col_sc + SC_WIDTH]
                                        * rest_ref[rows, col_sx:col_sx + SC_WIDTH])
        y_sc = rest_ref[rows, col_sb:col_sb + SC_WIDTH] * _dwconv(pad_s, start, length, scw_ref, scb_ref, 1)
        o_ref[rows, LRU_WIDTH:LRU_WIDTH + SC_WIDTH] = y_sc.astype(o_ref.dtype)
    if not need_ctx:
        o_ref[0:CTX_LEN, :] = jnp.zeros((CTX_LEN, LRU_WIDTH + SC_WIDTH), o_ref.dtype)


def _block_diag_gates(rg_w):
    eye = jnp.eye(LRU_HEADS, dtype=rg_w.dtype)
    full = jnp.einsum('dgncm,nk->dgnckm', rg_w, eye)
    full = full.reshape(2, 2, LRU_WIDTH, LRU_WIDTH)
    return full.transpose(2, 0, 1, 3).reshape(LRU_WIDTH, 4 * LRU_WIDTH)


def _lru_sconv(rest, lcw, lcb, rg_w, rg_b, rg_lam, scw, scb, bsz, need_ctx):
    wbd = _block_diag_gates(rg_w).astype(jnp.bfloat16)
    const2 = lambda b: (0, 0)
    half = LRU_WIDTH // 2
    return pl.pallas_call(
        functools.partial(_lru_kernel, need_ctx=need_ctx),
        out_shape=jax.ShapeDtypeStruct((bsz, ROWS_PER_B, LRU_WIDTH + SC_WIDTH), jnp.bfloat16),
        grid=(bsz,),
        in_specs=[
            pl.BlockSpec((None, ROWS_PER_B, REST_WIDTH), lambda b: (b, 0, 0)),
            pl.BlockSpec(lcw.shape, const2),
            pl.BlockSpec((1, LRU_WIDTH), const2),
            pl.BlockSpec((LRU_WIDTH, 4 * LRU_WIDTH), const2),
            pl.BlockSpec((1, 4 * LRU_WIDTH), const2),
            pl.BlockSpec((2, LRU_WIDTH), const2),
            pl.BlockSpec(scw.shape, const2),
            pl.BlockSpec((1, SC_WIDTH), const2),
        ],
        out_specs=pl.BlockSpec((None, ROWS_PER_B, LRU_WIDTH + SC_WIDTH), lambda b: (b, 0, 0)),
        scratch_shapes=[
            pltpu.VMEM((ROWS_PER_B, LRU_WIDTH), jnp.float32),
            pltpu.VMEM((2, SEQ, half), jnp.float32),
            pltpu.VMEM((2, SEQ, half), jnp.float32),
            pltpu.VMEM((2, SEQ, half), jnp.float32),
            pltpu.VMEM((2, SEQ, half), jnp.float32),
            pltpu.VMEM((ROWS_PER_B, LRU_WIDTH), jnp.float32),
            pltpu.VMEM((ROWS_PER_B + 3 * CONV_PAD, LRU_WIDTH), jnp.float32),
        ],
        compiler_params=_cparams(("arbitrary",)),
        name="lru_sconv",
    )(rest, lcw, lcb.reshape(1, LRU_WIDTH), wbd, rg_b.reshape(1, 4 * LRU_WIDTH), rg_lam, scw,
      scb.reshape(1, SC_WIDTH))


def _route(et):
    pe = [et[e:e + 1, :] for e in range(N_EXPERTS)]

    def top2_sum(v):
        best = v[0] + v[1]
        for a in range(E_PER_GROUP):
            for b in range(a + 1, E_PER_GROUP):
                if (a, b) != (0, 1):
                    best = jnp.maximum(best, v[a] + v[b])
        return best

    score = [top2_sum(pe[g * E_PER_GROUP:(g + 1) * E_PER_GROUP]) for g in range(N_GROUPS)]
    g_best, g_sel = score[0], jnp.zeros((1, TM), jnp.int32)
    for g in range(1, N_GROUPS):
        upd = score[g] > g_best
        g_sel = jnp.where(upd, g, g_sel)
        g_best = jnp.where(upd, score[g], g_best)
    p_in = []
    for k in range(E_PER_GROUP):
        v = pe[k]
        for g in range(1, N_GROUPS):
            v = jnp.where(g_sel == g, pe[g * E_PER_GROUP + k], v)
        p_in.append(v)
    m1, i1 = p_in[0], jnp.zeros((1, TM), jnp.int32)
    for k in range(1, E_PER_GROUP):
        upd = p_in[k] > m1
        i1 = jnp.where(upd, k, i1)
        m1 = jnp.where(upd, p_in[k], m1)
    m2, i2 = jnp.full((1, TM), -1.0, jnp.float32), jnp.zeros((1, TM), jnp.int32)
    for k in range(E_PER_GROUP):
        cand = jnp.where(i1 == k, -2.0, p_in[k])
        upd = cand > m2
        i2 = jnp.where(upd, k, i2)
        m2 = jnp.where(upd, cand, m2)
    lo, hi = jnp.minimum(i1, i2), jnp.maximum(i1, i2)
    pair = jnp.where(lo == 0, hi - 1, jnp.where(lo == 1, hi + 1, N_PAIRS - 1))
    bucket = g_sel * N_PAIRS + pair
    w1 = m1 / (m1 + m2)
    w2 = m2 / (m1 + m2)
    return bucket, jnp.where(i1 < i2, w1, w2), jnp.where(i1 < i2, w2, w1)


ROW_BUFS = 3


def _out_kernel(*refs, two_src, n_tiles):
    if two_src:
        ctx_ref, x_ref, *rest = refs
    else:
        x_ref, *rest = refs
    (yna_ref, yls_ref, mod_ref, og_ref, w_ref, fg_ref, wrt_ref, brt_ref, _, xo_ref, pos_ref, meta_ref, xs_ref,
     cnt_s, cur_s, alloc_s, tb_s, row_s, pos_v, pos_sm, sem_p, sem_r) = rest
    step = pl.program_id(0)

    def row_copy(buf, k, dst_row):
        return pltpu.make_async_copy(row_s.at[buf, pl.ds(k, 1), :], xs_ref.at[pl.ds(dst_row, 1), :], sem_r.at[buf])

    def pos_copy(buf):
        return pltpu.make_async_copy(pos_v.at[buf], pos_sm.at[buf], sem_p.at[buf])

    def copy_out(tile):
        buf, pbuf = tile % ROW_BUFS, tile % 2
        pos_copy(pbuf).wait()
        for k in range(TM):
            row_copy(buf, k, pos_sm[pbuf, 0, k]).start()

    def drain(buf):
        def body(k, c):
            row_copy(buf, 0, 0).wait()
            return c
        lax.fori_loop(0, TM, body, 0, unroll=8)

    @pl.when(step == 0)
    def _():
        cnt_s[...] = jnp.zeros_like(cnt_s)
        cur_s[...] = jnp.zeros_like(cur_s)
        alloc_s[...] = jnp.zeros_like(alloc_s)
        tb_s[...] = jnp.zeros_like(tb_s)

    def project(tile):
        if two_src:
            x = jnp.where(tile % TILES_PER_B == 0, ctx_ref[...], x_ref[...])
        else:
            x = x_ref[...]
        yna = yna_ref[...].astype(jnp.float32)
        yls = yls_ref[...].astype(jnp.float32)
        merged = jnp.concatenate([
            _rms(yna, og_ref[:, :NA_WIDTH]),
            _rms(yls[:, :LRU_WIDTH], og_ref[:, NA_WIDTH:NA_WIDTH + LRU_WIDTH]),
            _rms(yls[:, LRU_WIDTH:], og_ref[:, NA_WIDTH + LRU_WIDTH:]),
        ], axis=-1).astype(jnp.bfloat16)
        y = jnp.dot(merged, w_ref[...], preferred_element_type=jnp.float32)
        x_new = x + mod_ref[2:3, :] * y
        xo_ref[...] = x_new
        h2 = _rms(x_new, fg_ref[...]) * (1.0 + mod_ref[4:5, :]) + mod_ref[3:4, :]
        row_s[tile % ROW_BUFS, :, :D_MODEL] = h2

    def route(tile):
        buf = tile % ROW_BUFS
        h2 = row_s[buf, :, :D_MODEL]
        h_hi = h2.astype(jnp.bfloat16)
        h_lo = (h2 - h_hi.astype(jnp.float32)).astype(jnp.bfloat16)
        part = lax.dot_general(wrt_ref[...], h_hi, _NT, preferred_element_type=jnp.float32)
        lt = (part[:N_EXPERTS] + part[N_EXPERTS:] + brt_ref[...]
              + lax.dot_general(wrt_ref[:N_EXPERTS, :], h_lo, _NT, preferred_element_type=jnp.float32))
        bucket, w_lo, w_hi = _route(jnp.exp(lt - jnp.max(lt, axis=0, keepdims=True)))

        b_iota = lax.broadcasted_iota(jnp.int32, (BUCKET_ROWS, TM), 0)
        onehot = b_iota == bucket
        tri = (lax.broadcasted_iota(jnp.int32, (TM, TM), 0) <= lax.broadcasted_iota(jnp.int32, (TM, TM), 1))
        as_bf16 = lambda mask: jnp.where(mask, 1.0, 0.0).astype(jnp.bfloat16)
        cum = jnp.dot(as_bf16(onehot), as_bf16(tri), preferred_element_type=jnp.float32)
        cnt_new = cum[:, TM - 1:TM].astype(jnp.int32)
        cnt_old = cnt_s[:, 0:1]
        open_id = cur_s[:, 0:1]
        alloc = alloc_s[0:1, 0:1]
        shift = TM.bit_length() - 1
        q_last = (cnt_old + cnt_new - 1) >> shift
        q_prev = (cnt_old - 1) >> shift
        opens = jnp.where(cnt_new > 0, q_last - q_prev, 0)
        lower = (lax.broadcasted_iota(jnp.int32, (BUCKET_ROWS, BUCKET_ROWS), 1)
                 < lax.broadcasted_iota(jnp.int32, (BUCKET_ROWS, BUCKET_ROWS), 0))
        opens_b = jnp.broadcast_to(opens, (BUCKET_ROWS, LANES)).astype(jnp.float32).astype(jnp.bfloat16)
        before = jnp.dot(as_bf16(lower), opens_b, preferred_element_type=jnp.float32)[:, 0:1].astype(jnp.int32)
        new_id = alloc + before
        rank = cnt_old + cum.astype(jnp.int32) - 1
        tile_id = jnp.where((opens > 0) & ((rank >> shift) == q_last), new_id, open_id)
        slot = tile_id * TM + (rank & (TM - 1))
        pos = jnp.sum(jnp.where(onehot, slot, 0).astype(jnp.float32), axis=0, keepdims=True).astype(jnp.int32)
        pos_ref[...] = pos

        lane_id = lax.broadcasted_iota(jnp.int32, (BUCKET_ROWS, TM), 1)
        opened_here = (opens > 0) & (new_id == lane_id)
        opened_bucket = jnp.max(jnp.where(opened_here, b_iota, -1).astype(jnp.float32), axis=0,
                                keepdims=True).astype(jnp.int32)
        tb = jnp.where(opened_bucket >= 0, opened_bucket, tb_s[0:1, :])
        alloc_new = alloc + jnp.sum(opens.astype(jnp.float32), axis=0, keepdims=True).astype(jnp.int32)
        tb_s[...] = jnp.broadcast_to(tb, tb_s.shape)
        cnt_s[...] = jnp.broadcast_to(cnt_old + cnt_new, cnt_s.shape)
        cur_s[...] = jnp.broadcast_to(jnp.where(opens > 0, new_id, open_id), cur_s.shape)
        alloc_s[...] = jnp.broadcast_to(alloc_new, alloc_s.shape)
        meta_ref[0:1, :] = tb
        meta_ref[1:2, :] = jnp.broadcast_to(alloc_new, (1, TM))
        meta_ref[2:8, :] = jnp.zeros((6, TM), jnp.int32)

        payload = jnp.concatenate([w_lo, w_hi, jnp.zeros((LANES - 2, TM), jnp.float32)], axis=0)
        row_s[buf, :, D_MODEL:] = payload.T
        pos_v[tile % 2] = jnp.broadcast_to(pos, (8, TM))

    @pl.when((step >= ROW_BUFS) & (step < n_tiles))
    def _():
        drain(step % ROW_BUFS)
        route(step - 1)
        project(step)
        pos_copy((step - 1) % 2).start()
        copy_out(step - 2)

    @pl.when(step < ROW_BUFS)
    def _():
        project(step)

    @pl.when((step >= 1) & (step < ROW_BUFS))
    def _():
        route(step - 1)
        pos_copy((step - 1) % 2).start()

    @pl.when(step == 2)
    def _():
        copy_out(step - 2)

    @pl.when(step == n_tiles)
    def _():
        route(step - 1)
        pos_copy((step - 1) % 2).start()
        copy_out(step - 2)

    @pl.when(step == n_tiles + 1)
    def _():
        copy_out(step - 2)
        for buf in range(ROW_BUFS):
            drain(buf)


def _out_proj(srcs, yna, yls, mod, out_g, w_bf16, ffn_g, w_router, b_router, bsz, with_ctx):
    two_src = len(srcs) == 2
    n_tiles, bj, mod_row = _tile_maps(bsz, with_ctx)
    assert n_tiles >= ROW_BUFS
    n_sorted = n_tiles + N_BUCKETS
    proj = lambda i: bj(jnp.minimum(i, n_tiles - 1))
    tile = lambda width: pl.BlockSpec((None, TM, width), lambda i: (proj(i)[0], proj(i)[1], 0))
    const2 = lambda i: (0, 0)
    if two_src:
        src_specs = [
            pl.BlockSpec((None, TM, D_MODEL), lambda i: (proj(i)[0], 0, 0)),
            pl.BlockSpec((None, TM, D_MODEL), lambda i: (proj(i)[0], jnp.maximum(proj(i)[1] - 1, 0), 0)),
        ]
    else:
        src_specs = [tile(D_MODEL)]
    xs_init = jnp.zeros((n_sorted * TM, ROW_WIDTH), jnp.float32)
    wr_hi = w_router.T.astype(jnp.bfloat16)
    wr_lo = (w_router.T - wr_hi.astype(jnp.float32)).astype(jnp.bfloat16)
    wr_split = jnp.concatenate([wr_hi, wr_lo], axis=0)
    return pl.pallas_call(
        functools.partial(_out_kernel, two_src=two_src, n_tiles=n_tiles),
        out_shape=[
            jax.ShapeDtypeStruct((bsz, ROWS_PER_B, D_MODEL), jnp.float32),
            jax.ShapeDtypeStruct((n_tiles, 1, TM), jnp.int32),
            jax.ShapeDtypeStruct((8, TM), jnp.int32),
            jax.ShapeDtypeStruct(xs_init.shape, jnp.float32),
        ],
        grid=(n_tiles + 2,),
        in_specs=src_specs + [
            tile(NA_WIDTH),
            tile(LRU_WIDTH + SC_WIDTH),
            pl.BlockSpec((None, N_MOD, D_MODEL), lambda i: (mod_row(jnp.minimum(i, n_tiles - 1)), 0, 0)),
            pl.BlockSpec((1, D_MODEL), const2),
            pl.BlockSpec((D_MODEL, D_MODEL), const2),
            pl.BlockSpec((1, D_MODEL), const2),
            pl.BlockSpec((2 * N_EXPERTS, D_MODEL), const2),
            pl.BlockSpec((N_EXPERTS, 1), const2),
            pl.BlockSpec(memory_space=pl.ANY),
        ],
        out_specs=[
            tile(D_MODEL),
            pl.BlockSpec((None, 1, TM), lambda i: (jnp.clip(i - 1, 0, n_tiles - 1), 0, 0)),
            pl.BlockSpec((8, TM), const2),
            pl.BlockSpec(memory_space=pl.ANY),
        ],
        scratch_shapes=[
            pltpu.VMEM((BUCKET_ROWS, LANES), jnp.int32),
            pltpu.VMEM((BUCKET_ROWS, LANES), jnp.int32),
            pltpu.VMEM((8, LANES), jnp.int32),
            pltpu.VMEM((8, TM), jnp.int32),
            pltpu.VMEM((ROW_BUFS, TM, ROW_WIDTH), jnp.float32),
            pltpu.VMEM((2, 8, TM), jnp.int32),
            pltpu.SMEM((2, 8, TM), jnp.int32),
            pltpu.SemaphoreType.DMA((2,)),
            pltpu.SemaphoreType.DMA((ROW_BUFS,)),
        ],
        input_output_aliases={len(srcs) + 8: 3},
        compiler_params=_cparams(("arbitrary",)),
        name="out_proj_route",
    )(*srcs, yna, yls, mod, out_g.reshape(1, D_MODEL), w_bf16, ffn_g.reshape(1, D_MODEL), wr_split,
      b_router.reshape(N_EXPERTS, 1), xs_init)


def _moe_kernel(src_ref, e0_ref, e1_ref, used_ref, xs_ref, wg0, wu0, wd0, wg1, wu1, wd1, o_ref, wg_s, wu_s, wd_s):
    n = pl.program_id(0)
    prev = jnp.maximum(n - 1, 0)
    e0, e1 = e0_ref[n], e1_ref[n]
    fresh = (n == 0) | (e0 != e0_ref[prev]) | (e1 != e1_ref[prev])

    @pl.when(fresh)
    def _():
        for k, (g, u, d) in enumerate(((wg0, wu0, wd0), (wg1, wu1, wd1))):
            wg_s[k] = g[...].astype(jnp.bfloat16)
            wu_s[k] = u[...].astype(jnp.bfloat16)
            wd_s[k] = d[...].astype(jnp.bfloat16)

    @pl.when(n < used_ref[0])
    def _():
        xb = xs_ref[:, :D_MODEL].astype(jnp.bfloat16)
        weights = (xs_ref[:, D_MODEL:D_MODEL + 1], xs_ref[:, D_MODEL + 1:D_MODEL + 2])
        out = jnp.zeros((TM, D_MODEL), jnp.float32)
        for k, wk in enumerate(weights):
            gate = jnp.dot(xb, wg_s[k], preferred_element_type=jnp.float32)
            up = jnp.dot(xb, wu_s[k], preferred_element_type=jnp.float32)
            hid = (gate * jax.nn.sigmoid(gate)) * up
            out = out + wk * jnp.dot(hid.astype(jnp.bfloat16), wd_s[k], preferred_element_type=jnp.float32)
        o_ref[...] = out


def _moe(sched, xs, w_gate, w_up, w_down, layer):
    n_tiles = xs.shape[0] // TM
    src, e0, e1, used = sched
    first = lambda n, s, a, b, u: (layer, a[n], 0, 0)
    second = lambda n, s, a, b, u: (layer, b[n], 0, 0)
    gate_spec = lambda m: pl.BlockSpec((None, None, D_MODEL, D_EXPERT), m)
    down_spec = lambda m: pl.BlockSpec((None, None, D_EXPERT, D_MODEL), m)
    return pl.pallas_call(
        _moe_kernel,
        out_shape=jax.ShapeDtypeStruct((n_tiles * TM, D_MODEL), jnp.float32),
        grid_spec=pltpu.PrefetchScalarGridSpec(
            num_scalar_prefetch=4,
            grid=(n_tiles,),
            in_specs=[
                pl.BlockSpec((TM, ROW_WIDTH), lambda n, s, a, b, u: (s[n], 0)),
                gate_spec(first), gate_spec(first), down_spec(first),
                gate_spec(second), gate_spec(second), down_spec(second),
            ],
            out_specs=pl.BlockSpec((TM, D_MODEL), lambda n, s, a, b, u: (s[n], 0)),
            scratch_shapes=[
                pltpu.VMEM((2, D_MODEL, D_EXPERT), jnp.bfloat16),
                pltpu.VMEM((2, D_MODEL, D_EXPERT), jnp.bfloat16),
                pltpu.VMEM((2, D_EXPERT, D_MODEL), jnp.bfloat16),
            ],
        ),
        compiler_params=_cparams(("arbitrary",)),
        name="moe_experts",
    )(src, e0, e1, used, xs, w_gate, w_up, w_down, w_gate, w_up, w_down)


def _schedule(meta, n_tiles):
    tile_bucket = meta[0, :n_tiles]
    used = meta[1, 0]
    ids = jnp.arange(n_tiles, dtype=jnp.int32)
    order = jnp.argsort(jnp.where(ids < used, tile_bucket, N_BUCKETS), stable=True).astype(jnp.int32)
    src = jnp.where(ids < used, order, order[jnp.maximum(used - 1, 0)])
    bucket = tile_bucket[src]
    group, pair = bucket // N_PAIRS, bucket % N_PAIRS
    lo = jnp.array([0, 0, 0, 1, 1, 2], jnp.int32)[pair]
    hi = jnp.array([1, 2, 3, 2, 3, 3], jnp.int32)[pair]
    return src, group * E_PER_GROUP + lo, group * E_PER_GROUP + hi, used.reshape(1)


def _combine_kernel(pos_ref, x_ref, mod_ref, fg_ref, ys_ref, o_ref, buf, sem, *, final):
    i = pl.program_id(0)
    n = pl.num_programs(0)

    def row_copy(src_row, slot, k):
        return pltpu.make_async_copy(ys_ref.at[pl.ds(src_row, 1), :], buf.at[slot, pl.ds(k, 1), :], sem.at[slot])

    def issue(tile, slot):
        for k in range(TM):
            row_copy(pos_ref[tile * TM + k], slot, k).start()

    @pl.when(i == 0)
    def _():
        issue(0, 0)

    @pl.when(i + 1 < n)
    def _():
        issue(i + 1, (i + 1) % 2)

    slot = i % 2

    def drain(k, c):
        row_copy(0, slot, 0).wait()
        return c
    lax.fori_loop(0, TM, drain, 0, unroll=8)

    x_new = x_ref[...] + mod_ref[5:6, :] * buf[slot]
    if final:
        x_new = _rms(x_new, fg_ref[...])
    o_ref[...] = x_new


def _combine(pos, x_all, mod, final_g, ys, bsz, with_ctx, final):
    n_tiles, bj, mod_row = _tile_maps(bsz, with_ctx)
    if final:
        out_shape = jax.ShapeDtypeStruct((bsz, SEQ, D_MODEL), jnp.float32)
        out_spec = pl.BlockSpec((None, TM, D_MODEL), lambda i, p: (bj(i)[0], bj(i)[1] - 1, 0))
    else:
        out_shape = jax.ShapeDtypeStruct((bsz, ROWS_PER_B, D_MODEL), jnp.float32)
        out_spec = pl.BlockSpec((None, TM, D_MODEL), lambda i, p: (bj(i)[0], bj(i)[1], 0))
    return pl.pallas_call(
        functools.partial(_combine_kernel, final=final),
        out_shape=out_shape,
        grid_spec=pltpu.PrefetchScalarGridSpec(
            num_scalar_prefetch=1,
            grid=(n_tiles,),
            in_specs=[
                pl.BlockSpec((None, TM, D_MODEL), lambda i, p: (bj(i)[0], bj(i)[1], 0)),
                pl.BlockSpec((None, N_MOD, D_MODEL), lambda i, p: (mod_row(i), 0, 0)),
                pl.BlockSpec((1, D_MODEL), lambda i, p: (0, 0)),
                pl.BlockSpec(memory_space=pl.ANY),
            ],
            out_specs=out_spec,
            scratch_shapes=[pltpu.VMEM((2, TM, D_MODEL), jnp.float32), pltpu.SemaphoreType.DMA((2,))],
        ),
        compiler_params=_cparams(("arbitrary",)),
        name="combine",
    )(pos.reshape(-1), x_all, mod, final_g.reshape(1, D_MODEL), ys)


def kernel(x, c, ctx, c_ctx, w_ada, b_ada, norm_mix_g, w_in, lru_conv_w, lru_conv_b, rg_w, rg_b, rg_lam, na_rpb,
           sc_conv_w, sc_conv_b, mix_out_g, w_out, norm_ffn_g, w_router, b_router, w_gate, w_up, w_down, final_g):
    bsz = x.shape[0]
    mod_rows = -(-(bsz + 1) // 8) * 8
    c_all = jnp.zeros((mod_rows, D_MODEL), jnp.float32).at[:bsz].set(c).at[bsz].set(c_ctx)
    mods = _ada(c_all, w_ada, b_ada).reshape(DEPTH, mod_rows, N_MOD, D_MODEL)

    x_all = None
    out = None
    for l in range(DEPTH):
        need_ctx = l < DEPTH - 1
        mod = mods[l]
        srcs = (ctx, x) if l == 0 else (x_all,)
        qkv, rest = _in_proj(srcs, norm_mix_g[l], mod, w_in[l].astype(jnp.bfloat16), bsz)
        yna = _attention(qkv, _attn_bias_table(na_rpb[l]), bsz, need_ctx)
        yls = _lru_sconv(rest, lru_conv_w[l], lru_conv_b[l], rg_w[l], rg_b[l], rg_lam[l], sc_conv_w[l],
                         sc_conv_b[l], bsz, need_ctx)
        x_mid, pos, meta, xs = _out_proj(srcs, yna, yls, mod, mix_out_g[l], w_out[l].astype(jnp.bfloat16),
                                         norm_ffn_g[l], w_router, b_router, bsz, need_ctx)
        ys = _moe(_schedule(meta, xs.shape[0] // TM), xs, w_gate, w_up, w_down, l)
        res = _combine(pos, x_mid, mod, final_g, ys, bsz, need_ctx, final=not need_ctx)
        if need_ctx:
            x_all = res
        else:
            out = res
    return out
```

```python
import functools

import jax
import jax.numpy as jnp
from jax import lax
from jax.experimental import pallas as pl
from jax.experimental.pallas import tpu as pltpu

D_MODEL = 1024
SEQ = 2048
CTX_LEN = 256
ROWS_PER_B = CTX_LEN + SEQ
DEPTH = 2
N_MOD = 6
EPS = 1e-6
NEG_INF = -1e30

GRID_W = 64
GRID_ROWS = SEQ // GRID_W
HEAD_DIM = 64
NA_WIDTH = 512
NA_HEADS = 8
LRU_WIDTH = 256
LRU_HEADS = 4
LRU_BLOCK = 64
SC_WIDTH = 256
QKV_WIDTH = 3 * NA_WIDTH
REST_WIDTH = 2 * LRU_WIDTH + 3 * SC_WIDTH
IN_WIDTH = QKV_WIDTH + REST_WIDTH
RG_C = 8.0
WIN_R = 8
WIN_C = 16
N_EXPERTS = 16
N_GROUPS = 4
E_PER_GROUP = 4
N_PAIRS = 6
N_BUCKETS = N_GROUPS * N_PAIRS
D_EXPERT = 512

TM = 256
TILES_PER_B = ROWS_PER_B // TM
LAT_TILES_PER_B = SEQ // TM
LANES = 128
BUCKET_ROWS = 32
ROW_WIDTH = D_MODEL + LANES
HEADS_PER_STACK = 4
STACK_W = HEADS_PER_STACK * HEAD_DIM
SCAN_ROWS = 8
VMEM_LIMIT = 56 * 1024 * 1024

_HI = lax.Precision.HIGHEST
_NT = (((1,), (1,)), ((), ()))


def _cparams(sem):
    return pltpu.CompilerParams(dimension_semantics=sem, vmem_limit_bytes=VMEM_LIMIT)


def _rms(v, g):
    return v * lax.rsqrt(jnp.mean(v * v, axis=-1, keepdims=True) + EPS) * g


def _ada_kernel(c_ref, w_ref, b_ref, o_ref):
    cond = c_ref[...]
    cond = cond * jax.nn.sigmoid(cond)
    o_ref[0] = jnp.dot(cond, w_ref[0], precision=_HI, preferred_element_type=jnp.float32) + b_ref[0]


def _ada(c_all, w_ada, b_ada):
    depth, _, width = w_ada.shape
    rows = c_all.shape[0]
    tn = 1536
    return pl.pallas_call(
        _ada_kernel,
        out_shape=jax.ShapeDtypeStruct((depth, rows, width), jnp.float32),
        grid=(depth, width // tn),
        in_specs=[
            pl.BlockSpec((rows, D_MODEL), lambda l, n: (0, 0)),
            pl.BlockSpec((1, D_MODEL, tn), lambda l, n: (l, 0, n)),
            pl.BlockSpec((1, 1, tn), lambda l, n: (l, 0, n)),
        ],
        out_specs=pl.BlockSpec((1, rows, tn), lambda l, n: (l, 0, n)),
        compiler_params=_cparams(("arbitrary", "arbitrary")),
        name="ada",
    )(c_all, w_ada, b_ada.reshape(depth, 1, width))


def _tile_maps(bsz, with_ctx):
    per_b = TILES_PER_B if with_ctx else LAT_TILES_PER_B
    off = 0 if with_ctx else 1

    def bj(i):
        return i // per_b, i % per_b + off

    def mod_row(i):
        b, j = bj(i)
        return jnp.where(j == 0, bsz, b)

    return per_b * bsz, bj, mod_row


def _in_kernel(*refs, two_src):
    if two_src:
        ctx_ref, x_ref, g_ref, mod_ref, w_ref, qkv_ref, rest_ref = refs
        j = pl.program_id(0) % TILES_PER_B
        x = jnp.where(j == 0, ctx_ref[...], x_ref[...])
    else:
        x_ref, g_ref, mod_ref, w_ref, qkv_ref, rest_ref = refs
        x = x_ref[...]
    h = _rms(x, g_ref[...]) * (1.0 + mod_ref[1:2, :]) + mod_ref[0:1, :]
    hb = h.astype(jnp.bfloat16)
    qkv_ref[...] = jnp.dot(hb, w_ref[:, :QKV_WIDTH], preferred_element_type=jnp.float32).astype(jnp.bfloat16)
    rest_ref[...] = jnp.dot(hb, w_ref[:, QKV_WIDTH:], preferred_element_type=jnp.float32)


def _in_proj(srcs, g, mod, w_bf16, bsz):
    two_src = len(srcs) == 2
    n_tiles, bj, mod_row = _tile_maps(bsz, True)
    if two_src:
        src_specs = [
            pl.BlockSpec((None, TM, D_MODEL), lambda i: (bj(i)[0], 0, 0)),
            pl.BlockSpec((None, TM, D_MODEL), lambda i: (bj(i)[0], jnp.maximum(bj(i)[1] - 1, 0), 0)),
        ]
    else:
        src_specs = [pl.BlockSpec((None, TM, D_MODEL), lambda i: (bj(i)[0], bj(i)[1], 0))]
    return pl.pallas_call(
        functools.partial(_in_kernel, two_src=two_src),
        out_shape=[
            jax.ShapeDtypeStruct((bsz, ROWS_PER_B, QKV_WIDTH), jnp.bfloat16),
            jax.ShapeDtypeStruct((bsz, ROWS_PER_B, REST_WIDTH), jnp.float32),
        ],
        grid=(n_tiles,),
        in_specs=src_specs + [
            pl.BlockSpec((1, D_MODEL), lambda i: (0, 0)),
            pl.BlockSpec((None, N_MOD, D_MODEL), lambda i: (mod_row(i), 0, 0)),
            pl.BlockSpec((D_MODEL, IN_WIDTH), lambda i: (0, 0)),
        ],
        out_specs=[
            pl.BlockSpec((None, TM, QKV_WIDTH), lambda i: (bj(i)[0], bj(i)[1], 0)),
            pl.BlockSpec((None, TM, REST_WIDTH), lambda i: (bj(i)[0], bj(i)[1], 0)),
        ],
        compiler_params=_cparams(("arbitrary",)),
        name="in_proj",
    )(*srcs, g.reshape(1, D_MODEL), mod, w_bf16)


def _attn_kernel(q_ref, k_ref, v_ref, bias_ref, o_ref, *, need_ctx):
    lane_head = lax.broadcasted_iota(jnp.int32, (1, STACK_W), 1) // HEAD_DIM
    n_stacks = NA_WIDTH // STACK_W

    def stack_q(qg):
        zero = jnp.zeros_like(qg)
        return jnp.concatenate([jnp.where(lane_head == h, qg, zero) for h in range(HEADS_PER_STACK)], axis=0)

    def unstack(o):
        out = jnp.zeros((GRID_W, STACK_W), jnp.float32)
        for h in range(HEADS_PER_STACK):
            out = out + jnp.where(lane_head == h, o[h * GRID_W:(h + 1) * GRID_W], 0.0)
        return out

    def attend(q_rows, s, local):
        cols = slice(s * STACK_W, (s + 1) * STACK_W)
        qg = q_ref[pl.ds(q_rows, GRID_W), cols] * jnp.bfloat16(HEAD_DIM ** -0.5)
        qs = stack_q(qg)
        kc = k_ref[0:CTX_LEN, cols]
        vc = v_ref[0:CTX_LEN, cols]
        s_ctx = lax.dot_general(qs, kc, _NT, preferred_element_type=jnp.float32)
        m = jnp.max(s_ctx, axis=-1, keepdims=True)
        if local is not None:
            k_rows, delta = local
            kw = k_ref[pl.ds(k_rows, WIN_R * GRID_W), cols]
            vw = v_ref[pl.ds(k_rows, WIN_R * GRID_W), cols]
            bias = bias_ref[delta, s * HEADS_PER_STACK:(s + 1) * HEADS_PER_STACK]
            s_loc = lax.dot_general(qs, kw, _NT, preferred_element_type=jnp.float32)
            s_loc = s_loc + bias.reshape(HEADS_PER_STACK * GRID_W, WIN_R * GRID_W)
            m = jnp.maximum(m, jnp.max(s_loc, axis=-1, keepdims=True))
            p_loc = jnp.exp(s_loc - m)
        p_ctx = jnp.exp(s_ctx - m)
        denom = jnp.sum(p_ctx, axis=-1, keepdims=True)
        o = jnp.dot(p_ctx.astype(jnp.bfloat16), vc, preferred_element_type=jnp.float32)
        if local is not None:
            denom = denom + jnp.sum(p_loc, axis=-1, keepdims=True)
            o = o + jnp.dot(p_loc.astype(jnp.bfloat16), vw, preferred_element_type=jnp.float32)
        o_ref[pl.ds(q_rows, GRID_W), cols] = unstack(o / denom).astype(o_ref.dtype)

    def lat_row(r, carry):
        r0 = jnp.clip(r - WIN_R // 2, 0, GRID_ROWS - WIN_R)
        q_rows = pl.multiple_of(CTX_LEN + r * GRID_W, GRID_W)
        k_rows = pl.multiple_of(CTX_LEN + r0 * GRID_W, GRID_W)
        for s in range(n_stacks):
            attend(q_rows, s, (k_rows, r - r0))
        return carry

    lax.fori_loop(0, GRID_ROWS, lat_row, 0, unroll=4)

    if need_ctx:
        def ctx_chunk(cq, carry):
            q_rows = pl.multiple_of(cq * GRID_W, GRID_W)
            for s in range(n_stacks):
                attend(q_rows, s, None)
            return carry

        lax.fori_loop(0, CTX_LEN // GRID_W, ctx_chunk, 0, unroll=2)
    else:
        o_ref[0:CTX_LEN, :] = jnp.zeros((CTX_LEN, NA_WIDTH), o_ref.dtype)


def _attn_bias_table(rpb):
    n_rel_c = 2 * WIN_C - 1
    lead = GRID_W - WIN_C
    padded = jnp.pad(rpb, ((0, 0), (0, 0), (lead, 2 * GRID_W - lead - n_rel_c)))
    skew = jnp.tile(padded, (1, 1, GRID_W))[..., :GRID_W * (2 * GRID_W - 1)]
    skew = skew.reshape(NA_HEADS, 2 * WIN_R - 1, GRID_W, 2 * GRID_W - 1)
    toeplitz = skew[..., GRID_W - 1:]
    q_col = jnp.arange(GRID_W)[:, None]
    k_col = jnp.arange(GRID_W)[None, :]
    c_start = jnp.clip(q_col - WIN_C // 2, 0, GRID_W - WIN_C)
    ok = (k_col >= c_start) & (k_col < c_start + WIN_C)
    toeplitz = jnp.where(ok, toeplitz, NEG_INF)
    per_delta = [toeplitz[:, WIN_R - 1 - d:2 * WIN_R - 1 - d] for d in range(WIN_R)]
    tab = jnp.stack(per_delta, axis=0)
    return tab.transpose(0, 1, 3, 2, 4).reshape(WIN_R, NA_HEADS, GRID_W, WIN_R * GRID_W)


def _attention(qkv, bias, bsz, need_ctx):
    return pl.pallas_call(
        functools.partial(_attn_kernel, need_ctx=need_ctx),
        out_shape=jax.ShapeDtypeStruct((bsz, ROWS_PER_B, NA_WIDTH), jnp.bfloat16),
        grid=(bsz,),
        in_specs=[
            pl.BlockSpec((None, ROWS_PER_B, NA_WIDTH), lambda b: (b, 0, 0)),
            pl.BlockSpec((None, ROWS_PER_B, NA_WIDTH), lambda b: (b, 0, 1)),
            pl.BlockSpec((None, ROWS_PER_B, NA_WIDTH), lambda b: (b, 0, 2)),
            pl.BlockSpec((WIN_R, NA_HEADS, GRID_W, WIN_R * GRID_W), lambda b: (0, 0, 0, 0)),
        ],
        out_specs=pl.BlockSpec((None, ROWS_PER_B, NA_WIDTH), lambda b: (b, 0, 0)),
        compiler_params=_cparams(("arbitrary",)),
        name="attention",
    )(qkv, qkv, qkv, bias)


CONV_PAD = 8


def _pad_base(start):
    return start + CONV_PAD * (1 if start == 0 else 2)


def _dwconv(pad_s, start, length, w_ref, b_ref, left):
    base = _pad_base(start)
    width = w_ref.shape[0]
    y = pad_s[base - left:base - left + length, :] * w_ref[0:1, :] + b_ref[...]
    for k in range(1, width):
        y = y + pad_s[base + k - left:base + k - left + length, :] * w_ref[k:k + 1, :]
    return y


def _lru_kernel(rest_ref, lcw_ref, lcb_ref, wbd_ref, rgb_ref, lam_ref, scw_ref, scb_ref, o_ref,
                xc_s, a_s, b_s, y_s, pad_s, *, need_ctx):
    col_rx, col_rg, col_sb, col_sc, col_sx = (k * LRU_WIDTH for k in range(5))
    segments = ((0, CTX_LEN), (CTX_LEN, SEQ))

    for start, length in segments:
        base = _pad_base(start)
        pad_s[base - CONV_PAD:base, :] = jnp.zeros((CONV_PAD, LRU_WIDTH), jnp.float32)
    pad_s[pad_s.shape[0] - CONV_PAD:, :] = jnp.zeros((CONV_PAD, LRU_WIDTH), jnp.float32)

    for start, length in segments:
        base = _pad_base(start)
        pad_s[base:base + length, :] = rest_ref[start:start + length, col_rx:col_rx + LRU_WIDTH]
    for start, length in segments:
        xc_s[start:start + length, :] = _dwconv(pad_s, start, length, lcw_ref, lcb_ref, 2)

    def coeffs(d, start, length):
        chunk = 256
        sp = jax.nn.softplus(-lam_ref[d:d + 1, :])
        for c0 in range(0, length, chunk):
            xc = xc_s[start + c0:start + c0 + chunk, :]
            pre = jnp.dot(xc.astype(jnp.bfloat16), wbd_ref[:, 2 * d * LRU_WIDTH:(2 * d + 2) * LRU_WIDTH],
                          preferred_element_type=jnp.float32) + rgb_ref[:, 2 * d * LRU_WIDTH:(2 * d + 2) * LRU_WIDTH]
            gate_r = jax.nn.sigmoid(pre[:, :LRU_WIDTH])
            gate_i = jax.nn.sigmoid(pre[:, LRU_WIDTH:])
            log_a = -RG_C * gate_r * sp
            a = jnp.exp(log_a)
            bb = jnp.sqrt(1.0 - a * a) * (gate_i * xc)
            a_s[start + c0:start + c0 + chunk, :] = a
            b_s[start + c0:start + c0 + chunk, :] = bb

    n_ctx_blocks = CTX_LEN // SCAN_ROWS
    n_blocks = ROWS_PER_B // SCAN_ROWS
    sub = lax.broadcasted_iota(jnp.int32, (SCAN_ROWS, LRU_WIDTH), 0)

    def scan(reverse, accumulate):
        def block(i, h_in):
            if reverse:
                blk = jnp.where(i < n_ctx_blocks, n_ctx_blocks - 1 - i, n_blocks + n_ctx_blocks - 1 - i)
            else:
                blk = i
            rows = pl.ds(pl.multiple_of(blk * SCAN_ROWS, SCAN_ROWS), SCAN_ROWS)
            a = a_s[rows, :]
            b = b_s[rows, :]
            for sh in (1, 2, 4):
                if reverse:
                    keep = sub < SCAN_ROWS - sh
                    a_n = pltpu.roll(a, SCAN_ROWS - sh, axis=0)
                    b_n = pltpu.roll(b, SCAN_ROWS - sh, axis=0)
                else:
                    keep = sub >= sh
                    a_n = pltpu.roll(a, sh, axis=0)
                    b_n = pltpu.roll(b, sh, axis=0)
                b = jnp.where(keep, a * b_n + b, b)
                a = jnp.where(keep, a * a_n, a)
            h = a * h_in + b
            y_s[rows, :] = y_s[rows, :] + h if accumulate else h
            return h[0:1, :] if reverse else h[SCAN_ROWS - 1:SCAN_ROWS, :]

        lax.fori_loop(0, n_blocks, block, jnp.zeros((1, LRU_WIDTH), jnp.float32), unroll=4)

    for d, reverse in enumerate((False, True)):
        for start, length in segments:
            coeffs(d, start, length)
        scan(reverse, accumulate=d > 0)

    out_segments = segments if need_ctx else segments[1:]
    for start, length in out_segments:
        rows = slice(start, start + length)
        y_lru = y_s[rows, :] * jax.nn.gelu(rest_ref[rows, col_rg:col_rg + LRU_WIDTH])
        o_ref[rows, 0:LRU_WIDTH] = y_lru.astype(o_ref.dtype)
        base = _pad_base(start)
        pad_s[base:base + length, :] = (rest_ref[rows, col_sc:col_sc + SC_WIDTH]
                                        * rest_ref[rows, col_sx:col_sx + SC_WIDTH])
        y_sc = rest_ref[rows, col_sb:col_sb + SC_WIDTH] * _dwconv(pad_s, start, length, scw_ref, scb_ref, 1)
        o_ref[rows, LRU_WIDTH:LRU_WIDTH + SC_WIDTH] = y_sc.astype(o_ref.dtype)
    if not need_ctx:
        o_ref[0:CTX_LEN, :] = jnp.zeros((CTX_LEN, LRU_WIDTH + SC_WIDTH), o_ref.dtype)


def _block_diag_gates(rg_w):
    eye = jnp.eye(LRU_HEADS, dtype=rg_w.dtype)
    full = jnp.einsum('dgncm,nk->dgnckm', rg_w, eye)
    full = full.reshape(2, 2, LRU_WIDTH, LRU_WIDTH)
    return full.transpose(2, 0, 1, 3).reshape(LRU_WIDTH, 4 * LRU_WIDTH)


def _lru_sconv(rest, lcw, lcb, rg_w, rg_b, rg_lam, scw, scb, bsz, need_ctx):
    wbd = _block_diag_gates(rg_w).astype(jnp.bfloat16)
    const2 = lambda b: (0, 0)
    return pl.pallas_call(
        functools.partial(_lru_kernel, need_ctx=need_ctx),
        out_shape=jax.ShapeDtypeStruct((bsz, ROWS_PER_B, LRU_WIDTH + SC_WIDTH), jnp.bfloat16),
        grid=(bsz,),
        in_specs=[
            pl.BlockSpec((None, ROWS_PER_B, REST_WIDTH), lambda b: (b, 0, 0)),
            pl.BlockSpec(lcw.shape, const2),
            pl.BlockSpec((1, LRU_WIDTH), const2),
            pl.BlockSpec((LRU_WIDTH, 4 * LRU_WIDTH), const2),
            pl.BlockSpec((1, 4 * LRU_WIDTH), const2),
            pl.BlockSpec((2, LRU_WIDTH), const2),
            pl.BlockSpec(scw.shape, const2),
            pl.BlockSpec((1, SC_WIDTH), const2),
        ],
        out_specs=pl.BlockSpec((None, ROWS_PER_B, LRU_WIDTH + SC_WIDTH), lambda b: (b, 0, 0)),
        scratch_shapes=[
            pltpu.VMEM((ROWS_PER_B, LRU_WIDTH), jnp.float32),
            pltpu.VMEM((ROWS_PER_B, LRU_WIDTH), jnp.float32),
            pltpu.VMEM((ROWS_PER_B, LRU_WIDTH), jnp.float32),
            pltpu.VMEM((ROWS_PER_B, LRU_WIDTH), jnp.float32),
            pltpu.VMEM((ROWS_PER_B + 3 * CONV_PAD, LRU_WIDTH), jnp.float32),
        ],
        compiler_params=_cparams(("arbitrary",)),
        name="lru_sconv",
    )(rest, lcw, lcb.reshape(1, LRU_WIDTH), wbd, rg_b.reshape(1, 4 * LRU_WIDTH), rg_lam, scw,
      scb.reshape(1, SC_WIDTH))


def _route(et):
    pe = [et[e:e + 1, :] for e in range(N_EXPERTS)]

    def top2_sum(v):
        best = v[0] + v[1]
        for a in range(E_PER_GROUP):
            for b in range(a + 1, E_PER_GROUP):
                if (a, b) != (0, 1):
                    best = jnp.maximum(best, v[a] + v[b])
        return best

    score = [top2_sum(pe[g * E_PER_GROUP:(g + 1) * E_PER_GROUP]) for g in range(N_GROUPS)]
    g_best, g_sel = score[0], jnp.zeros((1, TM), jnp.int32)
    for g in range(1, N_GROUPS):
        upd = score[g] > g_best
        g_sel = jnp.where(upd, g, g_sel)
        g_best = jnp.where(upd, score[g], g_best)
    p_in = []
    for k in range(E_PER_GROUP):
        v = pe[k]
        for g in range(1, N_GROUPS):
            v = jnp.where(g_sel == g, pe[g * E_PER_GROUP + k], v)
        p_in.append(v)
    m1, i1 = p_in[0], jnp.zeros((1, TM), jnp.int32)
    for k in range(1, E_PER_GROUP):
        upd = p_in[k] > m1
        i1 = jnp.where(upd, k, i1)
        m1 = jnp.where(upd, p_in[k], m1)
    m2, i2 = jnp.full((1, TM), -1.0, jnp.float32), jnp.zeros((1, TM), jnp.int32)
    for k in range(E_PER_GROUP):
        cand = jnp.where(i1 == k, -2.0, p_in[k])
        upd = cand > m2
        i2 = jnp.where(upd, k, i2)
        m2 = jnp.where(upd, cand, m2)
    lo, hi = jnp.minimum(i1, i2), jnp.maximum(i1, i2)
    pair = jnp.where(lo == 0, hi - 1, jnp.where(lo == 1, hi + 1, N_PAIRS - 1))
    bucket = g_sel * N_PAIRS + pair
    w1 = m1 / (m1 + m2)
    w2 = m2 / (m1 + m2)
    return bucket, jnp.where(i1 < i2, w1, w2), jnp.where(i1 < i2, w2, w1)


ROW_BUFS = 3


def _out_kernel(*refs, two_src, n_tiles):
    if two_src:
        ctx_ref, x_ref, *rest = refs
    else:
        x_ref, *rest = refs
    (yna_ref, yls_ref, mod_ref, og_ref, w_ref, fg_ref, wrt_ref, brt_ref, _, xo_ref, pos_ref, meta_ref, xs_ref,
     cnt_s, cur_s, alloc_s, tb_s, row_s, pos_v, pos_sm, sem_p, sem_r) = rest
    step = pl.program_id(0)

    def row_copy(buf, k, dst_row):
        return pltpu.make_async_copy(row_s.at[buf, pl.ds(k, 1), :], xs_ref.at[pl.ds(dst_row, 1), :], sem_r.at[buf])

    def pos_copy(buf):
        return pltpu.make_async_copy(pos_v.at[buf], pos_sm.at[buf], sem_p.at[buf])

    def copy_out(tile):
        buf, pbuf = tile % ROW_BUFS, tile % 2
        pos_copy(pbuf).wait()
        for k in range(TM):
            row_copy(buf, k, pos_sm[pbuf, 0, k]).start()

    def drain(buf):
        def body(k, c):
            row_copy(buf, 0, 0).wait()
            return c
        lax.fori_loop(0, TM, body, 0, unroll=8)

    @pl.when(step == 0)
    def _():
        cnt_s[...] = jnp.zeros_like(cnt_s)
        cur_s[...] = jnp.zeros_like(cur_s)
        alloc_s[...] = jnp.zeros_like(alloc_s)
        tb_s[...] = jnp.zeros_like(tb_s)

    def project(tile):
        if two_src:
            x = jnp.where(tile % TILES_PER_B == 0, ctx_ref[...], x_ref[...])
        else:
            x = x_ref[...]
        yna = yna_ref[...].astype(jnp.float32)
        yls = yls_ref[...].astype(jnp.float32)
        merged = jnp.concatenate([
            _rms(yna, og_ref[:, :NA_WIDTH]),
            _rms(yls[:, :LRU_WIDTH], og_ref[:, NA_WIDTH:NA_WIDTH + LRU_WIDTH]),
            _rms(yls[:, LRU_WIDTH:], og_ref[:, NA_WIDTH + LRU_WIDTH:]),
        ], axis=-1).astype(jnp.bfloat16)
        y = jnp.dot(merged, w_ref[...], preferred_element_type=jnp.float32)
        x_new = x + mod_ref[2:3, :] * y
        xo_ref[...] = x_new
        h2 = _rms(x_new, fg_ref[...]) * (1.0 + mod_ref[4:5, :]) + mod_ref[3:4, :]
        row_s[tile % ROW_BUFS, :, :D_MODEL] = h2

    def route(tile):
        buf = tile % ROW_BUFS
        h2 = row_s[buf, :, :D_MODEL]
        h_hi = h2.astype(jnp.bfloat16)
        h_lo = (h2 - h_hi.astype(jnp.float32)).astype(jnp.bfloat16)
        part = lax.dot_general(wrt_ref[...], h_hi, _NT, preferred_element_type=jnp.float32)
        lt = (part[:N_EXPERTS] + part[N_EXPERTS:] + brt_ref[...]
              + lax.dot_general(wrt_ref[:N_EXPERTS, :], h_lo, _NT, preferred_element_type=jnp.float32))
        bucket, w_lo, w_hi = _route(jnp.exp(lt - jnp.max(lt, axis=0, keepdims=True)))

        b_iota = lax.broadcasted_iota(jnp.int32, (BUCKET_ROWS, TM), 0)
        onehot = b_iota == bucket
        tri = (lax.broadcasted_iota(jnp.int32, (TM, TM), 0) <= lax.broadcasted_iota(jnp.int32, (TM, TM), 1))
        as_bf16 = lambda mask: jnp.where(mask, 1.0, 0.0).astype(jnp.bfloat16)
        cum = jnp.dot(as_bf16(onehot), as_bf16(tri), preferred_element_type=jnp.float32)
        cnt_new = cum[:, TM - 1:TM].astype(jnp.int32)
        cnt_old = cnt_s[:, 0:1]
        open_id = cur_s[:, 0:1]
        alloc = alloc_s[0:1, 0:1]
        shift = TM.bit_length() - 1
        q_last = (cnt_old + cnt_new - 1) >> shift
        q_prev = (cnt_old - 1) >> shift
        opens = jnp.where(cnt_new > 0, q_last - q_prev, 0)
        lower = (lax.broadcasted_iota(jnp.int32, (BUCKET_ROWS, BUCKET_ROWS), 1)
                 < lax.broadcasted_iota(jnp.int32, (BUCKET_ROWS, BUCKET_ROWS), 0))
        opens_b = jnp.broadcast_to(opens, (BUCKET_ROWS, LANES)).astype(jnp.float32).astype(jnp.bfloat16)
        before = jnp.dot(as_bf16(lower), opens_b, preferred_element_type=jnp.float32)[:, 0:1].astype(jnp.int32)
        new_id = alloc + before
        rank = cnt_old + cum.astype(jnp.int32) - 1
        tile_id = jnp.where((opens > 0) & ((rank >> shift) == q_last), new_id, open_id)
        slot = tile_id * TM + (rank & (TM - 1))
        pos = jnp.sum(jnp.where(onehot, slot, 0).astype(jnp.float32), axis=0, keepdims=True).astype(jnp.int32)
        pos_ref[...] = pos

        lane_id = lax.broadcasted_iota(jnp.int32, (BUCKET_ROWS, TM), 1)
        opened_here = (opens > 0) & (new_id == lane_id)
        opened_bucket = jnp.max(jnp.where(opened_here, b_iota, -1).astype(jnp.float32), axis=0,
                                keepdims=True).astype(jnp.int32)
        tb = jnp.where(opened_bucket >= 0, opened_bucket, tb_s[0:1, :])
        alloc_new = alloc + jnp.sum(opens.astype(jnp.float32), axis=0, keepdims=True).astype(jnp.int32)
        tb_s[...] = jnp.broadcast_to(tb, tb_s.shape)
        cnt_s[...] = jnp.broadcast_to(cnt_old + cnt_new, cnt_s.shape)
        cur_s[...] = jnp.broadcast_to(jnp.where(opens > 0, new_id, open_id), cur_s.shape)
        alloc_s[...] = jnp.broadcast_to(alloc_new, alloc_s.shape)
        meta_ref[0:1, :] = tb
        meta_ref[1:2, :] = jnp.broadcast_to(alloc_new, (1, TM))
        meta_ref[2:8, :] = jnp.zeros((6, TM), jnp.int32)

        payload = jnp.concatenate([w_lo, w_hi, jnp.zeros((LANES - 2, TM), jnp.float32)], axis=0)
        row_s[buf, :, D_MODEL:] = payload.T
        pos_v[tile % 2] = jnp.broadcast_to(pos, (8, TM))

    @pl.when((step >= ROW_BUFS) & (step < n_tiles))
    def _():
        drain(step % ROW_BUFS)
        route(step - 1)
        project(step)
        pos_copy((step - 1) % 2).start()
        copy_out(step - 2)

    @pl.when(step < ROW_BUFS)
    def _():
        project(step)

    @pl.when((step >= 1) & (step < ROW_BUFS))
    def _():
        route(step - 1)
        pos_copy((step - 1) % 2).start()

    @pl.when(step == 2)
    def _():
        copy_out(step - 2)

    @pl.when(step == n_tiles)
    def _():
        route(step - 1)
        pos_copy((step - 1) % 2).start()
        copy_out(step - 2)

    @pl.when(step == n_tiles + 1)
    def _():
        copy_out(step - 2)
        for buf in range(ROW_BUFS):
            drain(buf)


def _out_proj(srcs, yna, yls, mod, out_g, w_bf16, ffn_g, w_router, b_router, bsz, with_ctx):
    two_src = len(srcs) == 2
    n_tiles, bj, mod_row = _tile_maps(bsz, with_ctx)
    assert n_tiles >= ROW_BUFS
    n_sorted = n_tiles + N_BUCKETS
    proj = lambda i: bj(jnp.minimum(i, n_tiles - 1))
    tile = lambda width: pl.BlockSpec((None, TM, width), lambda i: (proj(i)[0], proj(i)[1], 0))
    const2 = lambda i: (0, 0)
    if two_src:
        src_specs = [
            pl.BlockSpec((None, TM, D_MODEL), lambda i: (proj(i)[0], 0, 0)),
            pl.BlockSpec((None, TM, D_MODEL), lambda i: (proj(i)[0], jnp.maximum(proj(i)[1] - 1, 0), 0)),
        ]
    else:
        src_specs = [tile(D_MODEL)]
    xs_init = jnp.zeros((n_sorted * TM, ROW_WIDTH), jnp.float32)
    wr_hi = w_router.T.astype(jnp.bfloat16)
    wr_lo = (w_router.T - wr_hi.astype(jnp.float32)).astype(jnp.bfloat16)
    wr_split = jnp.concatenate([wr_hi, wr_lo], axis=0)
    return pl.pallas_call(
        functools.partial(_out_kernel, two_src=two_src, n_tiles=n_tiles),
        out_shape=[
            jax.ShapeDtypeStruct((bsz, ROWS_PER_B, D_MODEL), jnp.float32),
            jax.ShapeDtypeStruct((n_tiles, 1, TM), jnp.int32),
            jax.ShapeDtypeStruct((8, TM), jnp.int32),
            jax.ShapeDtypeStruct(xs_init.shape, jnp.float32),
        ],
        grid=(n_tiles + 2,),
        in_specs=src_specs + [
            tile(NA_WIDTH),
            tile(LRU_WIDTH + SC_WIDTH),
            pl.BlockSpec((None, N_MOD, D_MODEL), lambda i: (mod_row(jnp.minimum(i, n_tiles - 1)), 0, 0)),
            pl.BlockSpec((1, D_MODEL), const2),
            pl.BlockSpec((D_MODEL, D_MODEL), const2),
            pl.BlockSpec((1, D_MODEL), const2),
            pl.BlockSpec((2 * N_EXPERTS, D_MODEL), const2),
            pl.BlockSpec((N_EXPERTS, 1), const2),
            pl.BlockSpec(memory_space=pl.ANY),
        ],
        out_specs=[
            tile(D_MODEL),
            pl.BlockSpec((None, 1, TM), lambda i: (jnp.clip(i - 1, 0, n_tiles - 1), 0, 0)),
            pl.BlockSpec((8, TM), const2),
            pl.BlockSpec(memory_space=pl.ANY),
        ],
        scratch_shapes=[
            pltpu.VMEM((BUCKET_ROWS, LANES), jnp.int32),
            pltpu.VMEM((BUCKET_ROWS, LANES), jnp.int32),
            pltpu.VMEM((8, LANES), jnp.int32),
            pltpu.VMEM((8, TM), jnp.int32),
            pltpu.VMEM((ROW_BUFS, TM, ROW_WIDTH), jnp.float32),
            pltpu.VMEM((2, 8, TM), jnp.int32),
            pltpu.SMEM((2, 8, TM), jnp.int32),
            pltpu.SemaphoreType.DMA((2,)),
            pltpu.SemaphoreType.DMA((ROW_BUFS,)),
        ],
        input_output_aliases={len(srcs) + 8: 3},
        compiler_params=_cparams(("arbitrary",)),
        name="out_proj_route",
    )(*srcs, yna, yls, mod, out_g.reshape(1, D_MODEL), w_bf16, ffn_g.reshape(1, D_MODEL), wr_split,
      b_router.reshape(N_EXPERTS, 1), xs_init)


def _moe_kernel(src_ref, e0_ref, e1_ref, used_ref, xs_ref, wg0, wu0, wd0, wg1, wu1, wd1, o_ref, wg_s, wu_s, wd_s):
    n = pl.program_id(0)
    prev = jnp.maximum(n - 1, 0)
    e0, e1 = e0_ref[n], e1_ref[n]
    fresh = (n == 0) | (e0 != e0_ref[prev]) | (e1 != e1_ref[prev])

    @pl.when(fresh)
    def _():
        for k, (g, u, d) in enumerate(((wg0, wu0, wd0), (wg1, wu1, wd1))):
            wg_s[k] = g[...].astype(jnp.bfloat16)
            wu_s[k] = u[...].astype(jnp.bfloat16)
            wd_s[k] = d[...].astype(jnp.bfloat16)

    @pl.when(n < used_ref[0])
    def _():
        xb = xs_ref[:, :D_MODEL].astype(jnp.bfloat16)
        weights = (xs_ref[:, D_MODEL:D_MODEL + 1], xs_ref[:, D_MODEL + 1:D_MODEL + 2])
        out = jnp.zeros((TM, D_MODEL), jnp.float32)
        for k, wk in enumerate(weights):
            gate = jnp.dot(xb, wg_s[k], preferred_element_type=jnp.float32)
            up = jnp.dot(xb, wu_s[k], preferred_element_type=jnp.float32)
            hid = (gate * jax.nn.sigmoid(gate)) * up
            out = out + wk * jnp.dot(hid.astype(jnp.bfloat16), wd_s[k], preferred_element_type=jnp.float32)
        o_ref[...] = out


def _moe(sched, xs, w_gate, w_up, w_down, layer):
    n_tiles = xs.shape[0] // TM
    src, e0, e1, used = sched
    first = lambda n, s, a, b, u: (layer, a[n], 0, 0)
    second = lambda n, s, a, b, u: (layer, b[n], 0, 0)
    gate_spec = lambda m: pl.BlockSpec((None, None, D_MODEL, D_EXPERT), m)
    down_spec = lambda m: pl.BlockSpec((None, None, D_EXPERT, D_MODEL), m)
    return pl.pallas_call(
        _moe_kernel,
        out_shape=jax.ShapeDtypeStruct((n_tiles * TM, D_MODEL), jnp.float32),
        grid_spec=pltpu.PrefetchScalarGridSpec(
            num_scalar_prefetch=4,
            grid=(n_tiles,),
            in_specs=[
                pl.BlockSpec((TM, ROW_WIDTH), lambda n, s, a, b, u: (s[n], 0)),
                gate_spec(first), gate_spec(first), down_spec(first),
                gate_spec(second), gate_spec(second), down_spec(second),
            ],
            out_specs=pl.BlockSpec((TM, D_MODEL), lambda n, s, a, b, u: (s[n], 0)),
            scratch_shapes=[
                pltpu.VMEM((2, D_MODEL, D_EXPERT), jnp.bfloat16),
                pltpu.VMEM((2, D_MODEL, D_EXPERT), jnp.bfloat16),
                pltpu.VMEM((2, D_EXPERT, D_MODEL), jnp.bfloat16),
            ],
        ),
        compiler_params=_cparams(("arbitrary",)),
        name="moe_experts",
    )(src, e0, e1, used, xs, w_gate, w_up, w_down, w_gate, w_up, w_down)


def _schedule(meta, n_tiles):
    tile_bucket = meta[0, :n_tiles]
    used = meta[1, 0]
    ids = jnp.arange(n_tiles, dtype=jnp.int32)
    order = jnp.argsort(jnp.where(ids < used, tile_bucket, N_BUCKETS), stable=True).astype(jnp.int32)
    src = jnp.where(ids < used, order, order[jnp.maximum(used - 1, 0)])
    bucket = tile_bucket[src]
    group, pair = bucket // N_PAIRS, bucket % N_PAIRS
    lo = jnp.array([0, 0, 0, 1, 1, 2], jnp.int32)[pair]
    hi = jnp.array([1, 2, 3, 2, 3, 3], jnp.int32)[pair]
    return src, group * E_PER_GROUP + lo, group * E_PER_GROUP + hi, used.reshape(1)


def _combine_kernel(pos_ref, x_ref, mod_ref, fg_ref, ys_ref, o_ref, buf, sem, *, final):
    i = pl.program_id(0)
    n = pl.num_programs(0)

    def row_copy(src_row, slot, k):
        return pltpu.make_async_copy(ys_ref.at[pl.ds(src_row, 1), :], buf.at[slot, pl.ds(k, 1), :], sem.at[slot])

    def issue(tile, slot):
        for k in range(TM):
            row_copy(pos_ref[tile * TM + k], slot, k).start()

    @pl.when(i == 0)
    def _():
        issue(0, 0)

    @pl.when(i + 1 < n)
    def _():
        issue(i + 1, (i + 1) % 2)

    slot = i % 2

    def drain(k, c):
        row_copy(0, slot, 0).wait()
        return c
    lax.fori_loop(0, TM, drain, 0, unroll=8)

    x_new = x_ref[...] + mod_ref[5:6, :] * buf[slot]
    if final:
        x_new = _rms(x_new, fg_ref[...])
    o_ref[...] = x_new


def _combine(pos, x_all, mod, final_g, ys, bsz, with_ctx, final):
    n_tiles, bj, mod_row = _tile_maps(bsz, with_ctx)
    if final:
        out_shape = jax.ShapeDtypeStruct((bsz, SEQ, D_MODEL), jnp.float32)
        out_spec = pl.BlockSpec((None, TM, D_MODEL), lambda i, p: (bj(i)[0], bj(i)[1] - 1, 0))
    else:
        out_shape = jax.ShapeDtypeStruct((bsz, ROWS_PER_B, D_MODEL), jnp.float32)
        out_spec = pl.BlockSpec((None, TM, D_MODEL), lambda i, p: (bj(i)[0], bj(i)[1], 0))
    return pl.pallas_call(
        functools.partial(_combine_kernel, final=final),
        out_shape=out_shape,
        grid_spec=pltpu.PrefetchScalarGridSpec(
            num_scalar_prefetch=1,
            grid=(n_tiles,),
            in_specs=[
                pl.BlockSpec((None, TM, D_MODEL), lambda i, p: (bj(i)[0], bj(i)[1], 0)),
                pl.BlockSpec((None, N_MOD, D_MODEL), lambda i, p: (mod_row(i), 0, 0)),
                pl.BlockSpec((1, D_MODEL), lambda i, p: (0, 0)),
                pl.BlockSpec(memory_space=pl.ANY),
            ],
            out_specs=out_spec,
            scratch_shapes=[pltpu.VMEM((2, TM, D_MODEL), jnp.float32), pltpu.SemaphoreType.DMA((2,))],
        ),
        compiler_params=_cparams(("arbitrary",)),
        name="combine",
    )(pos.reshape(-1), x_all, mod, final_g.reshape(1, D_MODEL), ys)


def kernel(x, c, ctx, c_ctx, w_ada, b_ada, norm_mix_g, w_in, lru_conv_w, lru_conv_b, rg_w, rg_b, rg_lam, na_rpb,
           sc_conv_w, sc_conv_b, mix_out_g, w_out, norm_ffn_g, w_router, b_router, w_gate, w_up, w_down, final_g):
    bsz = x.shape[0]
    mod_rows = -(-(bsz + 1) // 8) * 8
    c_all = jnp.zeros((mod_rows, D_MODEL), jnp.float32).at[:bsz].set(c).at[bsz].set(c_ctx)
    mods = _ada(c_all, w_ada, b_ada).reshape(DEPTH, mod_rows, N_MOD, D_MODEL)

    x_all = None
    out = None
    for l in range(DEPTH):
        need_ctx = l < DEPTH - 1
        mod = mods[l]
        srcs = (ctx, x) if l == 0 else (x_all,)
        qkv, rest = _in_proj(srcs, norm_mix_g[l], mod, w_in[l].astype(jnp.bfloat16), bsz)
        yna = _attention(qkv, _attn_bias_table(na_rpb[l]), bsz, need_ctx)
        yls = _lru_sconv(rest, lru_conv_w[l], lru_conv_b[l], rg_w[l], rg_b[l], rg_lam[l], sc_conv_w[l],
                         sc_conv_b[l], bsz, need_ctx)
        x_mid, pos, meta, xs = _out_proj(srcs, yna, yls, mod, mix_out_g[l], w_out[l].astype(jnp.bfloat16),
                                         norm_ffn_g[l], w_router, b_router, bsz, need_ctx)
        ys = _moe(_schedule(meta, xs.shape[0] // TM), xs, w_gate, w_up, w_down, l)
        res = _combine(pos, x_mid, mod, final_g, ys, bsz, need_ctx, final=not need_ctx)
        if need_ctx:
            x_all = res
        else:
            out = res
    return out
```

```python
import functools

import jax
import jax.numpy as jnp
from jax import lax
from jax.experimental import pallas as pl
from jax.experimental.pallas import tpu as pltpu

D_MODEL = 1024
SEQ = 2048
CTX_LEN = 256
ROWS_PER_B = CTX_LEN + SEQ
DEPTH = 2
N_MOD = 6
EPS = 1e-6
NEG_INF = -1e30

GRID_W = 64
GRID_ROWS = SEQ // GRID_W
HEAD_DIM = 64
NA_WIDTH = 512
NA_HEADS = 8
LRU_WIDTH = 256
LRU_HEADS = 4
LRU_BLOCK = 64
SC_WIDTH = 256
QKV_WIDTH = 3 * NA_WIDTH
REST_WIDTH = 2 * LRU_WIDTH + 3 * SC_WIDTH
IN_WIDTH = QKV_WIDTH + REST_WIDTH
RG_C = 8.0
WIN_R = 8
WIN_C = 16
N_EXPERTS = 16
N_GROUPS = 4
E_PER_GROUP = 4
N_PAIRS = 6
N_BUCKETS = N_GROUPS * N_PAIRS
D_EXPERT = 512

TM = 256
TILES_PER_B = ROWS_PER_B // TM
LAT_TILES_PER_B = SEQ // TM
LANES = 128
BUCKET_ROWS = 32
ROW_WIDTH = D_MODEL + LANES
HEADS_PER_STACK = 4
STACK_W = HEADS_PER_STACK * HEAD_DIM
SCAN_ROWS = 8
VMEM_LIMIT = 56 * 1024 * 1024

_HI = lax.Precision.HIGHEST
_NT = (((1,), (1,)), ((), ()))


def _cparams(sem):
    return pltpu.CompilerParams(dimension_semantics=sem, vmem_limit_bytes=VMEM_LIMIT)


def _rms(v, g):
    return v * lax.rsqrt(jnp.mean(v * v, axis=-1, keepdims=True) + EPS) * g


def _ada_kernel(c_ref, w_ref, b_ref, o_ref):
    cond = c_ref[...]
    cond = cond * jax.nn.sigmoid(cond)
    o_ref[0] = jnp.dot(cond, w_ref[0], precision=_HI, preferred_element_type=jnp.float32) + b_ref[0]


def _ada(c_all, w_ada, b_ada):
    depth, _, width = w_ada.shape
    rows = c_all.shape[0]
    tn = 1536
    return pl.pallas_call(
        _ada_kernel,
        out_shape=jax.ShapeDtypeStruct((depth, rows, width), jnp.float32),
        grid=(depth, width // tn),
        in_specs=[
            pl.BlockSpec((rows, D_MODEL), lambda l, n: (0, 0)),
            pl.BlockSpec((1, D_MODEL, tn), lambda l, n: (l, 0, n)),
            pl.BlockSpec((1, 1, tn), lambda l, n: (l, 0, n)),
        ],
        out_specs=pl.BlockSpec((1, rows, tn), lambda l, n: (l, 0, n)),
        compiler_params=_cparams(("arbitrary", "arbitrary")),
        name="ada",
    )(c_all, w_ada, b_ada.reshape(depth, 1, width))


def _tile_maps(bsz, with_ctx):
    per_b = TILES_PER_B if with_ctx else LAT_TILES_PER_B
    off = 0 if with_ctx else 1

    def bj(i):
        return i // per_b, i % per_b + off

    def mod_row(i):
        b, j = bj(i)
        return jnp.where(j == 0, bsz, b)

    return per_b * bsz, bj, mod_row


IN_SUB = 3
IN_STEPS_PER_B = TILES_PER_B // IN_SUB


def _in_kernel(*refs, two_src):
    if two_src:
        ctx_ref, *x_refs = refs[:1 + IN_SUB]
        g_ref, modc_ref, modb_ref, w_ref, qkv_ref, rest_ref = refs[1 + IN_SUB:]
    else:
        x_ref, g_ref, modc_ref, modb_ref, w_ref, qkv_ref, rest_ref = refs
    first = pl.program_id(0) % IN_STEPS_PER_B == 0
    for s in range(IN_SUB):
        rows = slice(s * TM, (s + 1) * TM)
        if two_src:
            x = jnp.where(first, ctx_ref[...], x_refs[0][...]) if s == 0 else x_refs[s][...]
        else:
            x = x_ref[rows, :]
        if s == 0:
            shift = jnp.where(first, modc_ref[0:1, :], modb_ref[0:1, :])
            scale = jnp.where(first, modc_ref[1:2, :], modb_ref[1:2, :])
        else:
            shift, scale = modb_ref[0:1, :], modb_ref[1:2, :]
        hb = (_rms(x, g_ref[...]) * (1.0 + scale) + shift).astype(jnp.bfloat16)
        qkv_ref[rows, :] = jnp.dot(hb, w_ref[:, :QKV_WIDTH], preferred_element_type=jnp.float32).astype(jnp.bfloat16)
        rest_ref[rows, :] = jnp.dot(hb, w_ref[:, QKV_WIDTH:], preferred_element_type=jnp.float32)


def _in_proj(srcs, g, mod, w_bf16, bsz):
    two_src = len(srcs) == 2
    bj = lambda i: (i // IN_STEPS_PER_B, i % IN_STEPS_PER_B)
    rows = IN_SUB * TM
    if two_src:
        lat = lambda s: pl.BlockSpec(
            (None, TM, D_MODEL), lambda i: (bj(i)[0], jnp.maximum(IN_SUB * bj(i)[1] + s - 1, 0), 0))
        src_specs = [pl.BlockSpec((None, TM, D_MODEL), lambda i: (bj(i)[0], 0, 0))] + [lat(s) for s in range(IN_SUB)]
        srcs = (srcs[0],) + (srcs[1],) * IN_SUB
    else:
        src_specs = [pl.BlockSpec((None, rows, D_MODEL), lambda i: (bj(i)[0], bj(i)[1], 0))]
    return pl.pallas_call(
        functools.partial(_in_kernel, two_src=two_src),
        out_shape=[
            jax.ShapeDtypeStruct((bsz, ROWS_PER_B, QKV_WIDTH), jnp.bfloat16),
            jax.ShapeDtypeStruct((bsz, ROWS_PER_B, REST_WIDTH), jnp.float32),
        ],
        grid=(bsz * IN_STEPS_PER_B,),
        in_specs=src_specs + [
            pl.BlockSpec((1, D_MODEL), lambda i: (0, 0)),
            pl.BlockSpec((None, N_MOD, D_MODEL), lambda i: (bsz, 0, 0)),
            pl.BlockSpec((None, N_MOD, D_MODEL), lambda i: (bj(i)[0], 0, 0)),
            pl.BlockSpec((D_MODEL, IN_WIDTH), lambda i: (0, 0)),
        ],
        out_specs=[
            pl.BlockSpec((None, rows, QKV_WIDTH), lambda i: (bj(i)[0], bj(i)[1], 0)),
            pl.BlockSpec((None, rows, REST_WIDTH), lambda i: (bj(i)[0], bj(i)[1], 0)),
        ],
        compiler_params=_cparams(("arbitrary",)),
        name="in_proj",
    )(*srcs, g.reshape(1, D_MODEL), mod, mod, w_bf16)


def _attn_kernel(q_ref, k_ref, v_ref, bias_ref, o_ref, *, need_ctx):
    lane_head = lax.broadcasted_iota(jnp.int32, (1, STACK_W), 1) // HEAD_DIM
    n_stacks = NA_WIDTH // STACK_W

    def stack_q(qg):
        zero = jnp.zeros_like(qg)
        return jnp.concatenate([jnp.where(lane_head == h, qg, zero) for h in range(HEADS_PER_STACK)], axis=0)

    def unstack(o):
        out = jnp.zeros((GRID_W, STACK_W), jnp.float32)
        for h in range(HEADS_PER_STACK):
            out = out + jnp.where(lane_head == h, o[h * GRID_W:(h + 1) * GRID_W], 0.0)
        return out

    def attend(q_rows, s, local):
        cols = slice(s * STACK_W, (s + 1) * STACK_W)
        qg = q_ref[pl.ds(q_rows, GRID_W), cols] * jnp.bfloat16(HEAD_DIM ** -0.5)
        qs = stack_q(qg)
        kc = k_ref[0:CTX_LEN, cols]
        vc = v_ref[0:CTX_LEN, cols]
        s_ctx = lax.dot_general(qs, kc, _NT, preferred_element_type=jnp.float32)
        m = jnp.max(s_ctx, axis=-1, keepdims=True)
        if local is not None:
            k_rows, delta = local
            kw = k_ref[pl.ds(k_rows, WIN_R * GRID_W), cols]
            vw = v_ref[pl.ds(k_rows, WIN_R * GRID_W), cols]
            bias = bias_ref[delta, s * HEADS_PER_STACK:(s + 1) * HEADS_PER_STACK]
            s_loc = lax.dot_general(qs, kw, _NT, preferred_element_type=jnp.float32)
            s_loc = s_loc + bias.reshape(HEADS_PER_STACK * GRID_W, WIN_R * GRID_W)
            m = jnp.maximum(m, jnp.max(s_loc, axis=-1, keepdims=True))
            p_loc = jnp.exp(s_loc - m)
        p_ctx = jnp.exp(s_ctx - m)
        denom = jnp.sum(p_ctx, axis=-1, keepdims=True)
        o = jnp.dot(p_ctx.astype(jnp.bfloat16), vc, preferred_element_type=jnp.float32)
        if local is not None:
            denom = denom + jnp.sum(p_loc, axis=-1, keepdims=True)
            o = o + jnp.dot(p_loc.astype(jnp.bfloat16), vw, preferred_element_type=jnp.float32)
        o_ref[pl.ds(q_rows, GRID_W), cols] = unstack(o / denom).astype(o_ref.dtype)

    def lat_row(r, carry):
        r0 = jnp.clip(r - WIN_R // 2, 0, GRID_ROWS - WIN_R)
        q_rows = pl.multiple_of(CTX_LEN + r * GRID_W, GRID_W)
        k_rows = pl.multiple_of(CTX_LEN + r0 * GRID_W, GRID_W)
        for s in range(n_stacks):
            attend(q_rows, s, (k_rows, r - r0))
        return carry

    lax.fori_loop(0, GRID_ROWS, lat_row, 0, unroll=8)

    if need_ctx:
        def ctx_chunk(cq, carry):
            q_rows = pl.multiple_of(cq * GRID_W, GRID_W)
            for s in range(n_stacks):
                attend(q_rows, s, None)
            return carry

        lax.fori_loop(0, CTX_LEN // GRID_W, ctx_chunk, 0, unroll=2)
    else:
        o_ref[0:CTX_LEN, :] = jnp.zeros((CTX_LEN, NA_WIDTH), o_ref.dtype)


def _attn_bias_table(rpb):
    n_rel_c = 2 * WIN_C - 1
    lead = GRID_W - WIN_C
    padded = jnp.pad(rpb, ((0, 0), (0, 0), (lead, 2 * GRID_W - lead - n_rel_c)))
    skew = jnp.tile(padded, (1, 1, GRID_W))[..., :GRID_W * (2 * GRID_W - 1)]
    skew = skew.reshape(NA_HEADS, 2 * WIN_R - 1, GRID_W, 2 * GRID_W - 1)
    toeplitz = skew[..., GRID_W - 1:]
    q_col = jnp.arange(GRID_W)[:, None]
    k_col = jnp.arange(GRID_W)[None, :]
    c_start = jnp.clip(q_col - WIN_C // 2, 0, GRID_W - WIN_C)
    ok = (k_col >= c_start) & (k_col < c_start + WIN_C)
    toeplitz = jnp.where(ok, toeplitz, NEG_INF)
    per_delta = [toeplitz[:, WIN_R - 1 - d:2 * WIN_R - 1 - d] for d in range(WIN_R)]
    tab = jnp.stack(per_delta, axis=0)
    return tab.transpose(0, 1, 3, 2, 4).reshape(WIN_R, NA_HEADS, GRID_W, WIN_R * GRID_W)


def _attention(qkv, bias, bsz, need_ctx):
    return pl.pallas_call(
        functools.partial(_attn_kernel, need_ctx=need_ctx),
        out_shape=jax.ShapeDtypeStruct((bsz, ROWS_PER_B, NA_WIDTH), jnp.bfloat16),
        grid=(bsz,),
        in_specs=[
            pl.BlockSpec((None, ROWS_PER_B, NA_WIDTH), lambda b: (b, 0, 0)),
            pl.BlockSpec((None, ROWS_PER_B, NA_WIDTH), lambda b: (b, 0, 1)),
            pl.BlockSpec((None, ROWS_PER_B, NA_WIDTH), lambda b: (b, 0, 2)),
            pl.BlockSpec((WIN_R, NA_HEADS, GRID_W, WIN_R * GRID_W), lambda b: (0, 0, 0, 0)),
        ],
        out_specs=pl.BlockSpec((None, ROWS_PER_B, NA_WIDTH), lambda b: (b, 0, 0)),
        compiler_params=_cparams(("arbitrary",)),
        name="attention",
    )(qkv, qkv, qkv, bias)


CONV_PAD = 8


def _pad_base(start):
    return start + CONV_PAD * (1 if start == 0 else 2)


def _dwconv(pad_s, start, length, w_ref, b_ref, left):
    base = _pad_base(start)
    width = w_ref.shape[0]
    y = pad_s[base - left:base - left + length, :] * w_ref[0:1, :] + b_ref[...]
    for k in range(1, width):
        y = y + pad_s[base + k - left:base + k - left + length, :] * w_ref[k:k + 1, :]
    return y


def _lru_kernel(rest_ref, lcw_ref, lcb_ref, wbd_ref, rgb_ref, lam_ref, scw_ref, scb_ref, o_ref,
                xc_s, a_s, b_s, y_s, pad_s, *, need_ctx):
    col_rx, col_rg, col_sb, col_sc, col_sx = (k * LRU_WIDTH for k in range(5))
    segments = ((0, CTX_LEN), (CTX_LEN, SEQ))

    for start, length in segments:
        base = _pad_base(start)
        pad_s[base - CONV_PAD:base, :] = jnp.zeros((CONV_PAD, LRU_WIDTH), jnp.float32)
    pad_s[pad_s.shape[0] - CONV_PAD:, :] = jnp.zeros((CONV_PAD, LRU_WIDTH), jnp.float32)

    for start, length in segments:
        base = _pad_base(start)
        pad_s[base:base + length, :] = rest_ref[start:start + length, col_rx:col_rx + LRU_WIDTH]
    for start, length in segments:
        xc_s[start:start + length, :] = _dwconv(pad_s, start, length, lcw_ref, lcb_ref, 2)

    def coeffs(d, start, length):
        chunk = 256
        sp = jax.nn.softplus(-lam_ref[d:d + 1, :])
        for c0 in range(0, length, chunk):
            xc = xc_s[start + c0:start + c0 + chunk, :]
            pre = jnp.dot(xc.astype(jnp.bfloat16), wbd_ref[:, 2 * d * LRU_WIDTH:(2 * d + 2) * LRU_WIDTH],
                          preferred_element_type=jnp.float32) + rgb_ref[:, 2 * d * LRU_WIDTH:(2 * d + 2) * LRU_WIDTH]
            gate_r = jax.nn.sigmoid(pre[:, :LRU_WIDTH])
            gate_i = jax.nn.sigmoid(pre[:, LRU_WIDTH:])
            log_a = -RG_C * gate_r * sp
            a = jnp.exp(log_a)
            bb = jnp.sqrt(1.0 - a * a) * (gate_i * xc)
            a_s[start + c0:start + c0 + chunk, :] = a
            b_s[start + c0:start + c0 + chunk, :] = bb

    n_ctx_blocks = CTX_LEN // SCAN_ROWS
    n_blocks = ROWS_PER_B // SCAN_ROWS
    sub = lax.broadcasted_iota(jnp.int32, (SCAN_ROWS, LRU_WIDTH), 0)

    def scan(reverse, accumulate):
        def block(i, h_in):
            if reverse:
                blk = jnp.where(i < n_ctx_blocks, n_ctx_blocks - 1 - i, n_blocks + n_ctx_blocks - 1 - i)
            else:
                blk = i
            rows = pl.ds(pl.multiple_of(blk * SCAN_ROWS, SCAN_ROWS), SCAN_ROWS)
            a = a_s[rows, :]
            b = b_s[rows, :]
            for sh in (1, 2, 4):
                if reverse:
                    keep = sub < SCAN_ROWS - sh
                    a_n = pltpu.roll(a, SCAN_ROWS - sh, axis=0)
                    b_n = pltpu.roll(b, SCAN_ROWS - sh, axis=0)
                else:
                    keep = sub >= sh
                    a_n = pltpu.roll(a, sh, axis=0)
                    b_n = pltpu.roll(b, sh, axis=0)
                b = jnp.where(keep, a * b_n + b, b)
                a = jnp.where(keep, a * a_n, a)
            h = a * h_in + b
            y_s[rows, :] = y_s[rows, :] + h if accumulate else h
            return h[0:1, :] if reverse else h[SCAN_ROWS - 1:SCAN_ROWS, :]

        lax.fori_loop(0, n_blocks, block, jnp.zeros((1, LRU_WIDTH), jnp.float32), unroll=4)

    for d, reverse in enumerate((False, True)):
        for start, length in segments:
            coeffs(d, start, length)
        scan(reverse, accumulate=d > 0)

    out_segments = segments if need_ctx else segments[1:]
    for start, length in out_segments:
        rows = slice(start, start + length)
        y_lru = y_s[rows, :] * jax.nn.gelu(rest_ref[rows, col_rg:col_rg + LRU_WIDTH])
        o_ref[rows, 0:LRU_WIDTH] = y_lru.astype(o_ref.dtype)
        base = _pad_base(start)
        pad_s[base:base + length, :] = (rest_ref[rows, col_sc:col_sc + SC_WIDTH]
                                        * rest_ref[rows, col_sx:col_sx + SC_WIDTH])
        y_sc = rest_ref[rows, col_sb:col_sb + SC_WIDTH] * _dwconv(pad_s, start, length, scw_ref, scb_ref, 1)
        o_ref[rows, LRU_WIDTH:LRU_WIDTH + SC_WIDTH] = y_sc.astype(o_ref.dtype)
    if not need_ctx:
        o_ref[0:CTX_LEN, :] = jnp.zeros((CTX_LEN, LRU_WIDTH + SC_WIDTH), o_ref.dtype)


def _block_diag_gates(rg_w):
    eye = jnp.eye(LRU_HEADS, dtype=rg_w.dtype)
    full = jnp.einsum('dgncm,nk->dgnckm', rg_w, eye)
    full = full.reshape(2, 2, LRU_WIDTH, LRU_WIDTH)
    return full.transpose(2, 0, 1, 3).reshape(LRU_WIDTH, 4 * LRU_WIDTH)


def _lru_sconv(rest, lcw, lcb, rg_w, rg_b, rg_lam, scw, scb, bsz, need_ctx):
    wbd = _block_diag_gates(rg_w).astype(jnp.bfloat16)
    const2 = lambda b: (0, 0)
    return pl.pallas_call(
        functools.partial(_lru_kernel, need_ctx=need_ctx),
        out_shape=jax.ShapeDtypeStruct((bsz, ROWS_PER_B, LRU_WIDTH + SC_WIDTH), jnp.bfloat16),
        grid=(bsz,),
        in_specs=[
            pl.BlockSpec((None, ROWS_PER_B, REST_WIDTH), lambda b: (b, 0, 0)),
            pl.BlockSpec(lcw.shape, const2),
            pl.BlockSpec((1, LRU_WIDTH), const2),
            pl.BlockSpec((LRU_WIDTH, 4 * LRU_WIDTH), const2),
            pl.BlockSpec((1, 4 * LRU_WIDTH), const2),
            pl.BlockSpec((2, LRU_WIDTH), const2),
            pl.BlockSpec(scw.shape, const2),
            pl.BlockSpec((1, SC_WIDTH), const2),
        ],
        out_specs=pl.BlockSpec((None, ROWS_PER_B, LRU_WIDTH + SC_WIDTH), lambda b: (b, 0, 0)),
        scratch_shapes=[
            pltpu.VMEM((ROWS_PER_B, LRU_WIDTH), jnp.float32),
            pltpu.VMEM((ROWS_PER_B, LRU_WIDTH), jnp.float32),
            pltpu.VMEM((ROWS_PER_B, LRU_WIDTH), jnp.float32),
            pltpu.VMEM((ROWS_PER_B, LRU_WIDTH), jnp.float32),
            pltpu.VMEM((ROWS_PER_B + 3 * CONV_PAD, LRU_WIDTH), jnp.float32),
        ],
        compiler_params=_cparams(("arbitrary",)),
        name="lru_sconv",
    )(rest, lcw, lcb.reshape(1, LRU_WIDTH), wbd, rg_b.reshape(1, 4 * LRU_WIDTH), rg_lam, scw,
      scb.reshape(1, SC_WIDTH))


def _route(et):
    pe = [et[e:e + 1, :] for e in range(N_EXPERTS)]

    def top2_sum(v):
        best = v[0] + v[1]
        for a in range(E_PER_GROUP):
            for b in range(a + 1, E_PER_GROUP):
                if (a, b) != (0, 1):
                    best = jnp.maximum(best, v[a] + v[b])
        return best

    score = [top2_sum(pe[g * E_PER_GROUP:(g + 1) * E_PER_GROUP]) for g in range(N_GROUPS)]
    g_best, g_sel = score[0], jnp.zeros((1, TM), jnp.int32)
    for g in range(1, N_GROUPS):
        upd = score[g] > g_best
        g_sel = jnp.where(upd, g, g_sel)
        g_best = jnp.where(upd, score[g], g_best)
    p_in = []
    for k in range(E_PER_GROUP):
        v = pe[k]
        for g in range(1, N_GROUPS):
            v = jnp.where(g_sel == g, pe[g * E_PER_GROUP + k], v)
        p_in.append(v)
    m1, i1 = p_in[0], jnp.zeros((1, TM), jnp.int32)
    for k in range(1, E_PER_GROUP):
        upd = p_in[k] > m1
        i1 = jnp.where(upd, k, i1)
        m1 = jnp.where(upd, p_in[k], m1)
    m2, i2 = jnp.full((1, TM), -1.0, jnp.float32), jnp.zeros((1, TM), jnp.int32)
    for k in range(E_PER_GROUP):
        cand = jnp.where(i1 == k, -2.0, p_in[k])
        upd = cand > m2
        i2 = jnp.where(upd, k, i2)
        m2 = jnp.where(upd, cand, m2)
    lo, hi = jnp.minimum(i1, i2), jnp.maximum(i1, i2)
    pair = jnp.where(lo == 0, hi - 1, jnp.where(lo == 1, hi + 1, N_PAIRS - 1))
    bucket = g_sel * N_PAIRS + pair
    w1 = m1 / (m1 + m2)
    w2 = m2 / (m1 + m2)
    return bucket, jnp.where(i1 < i2, w1, w2), jnp.where(i1 < i2, w2, w1)


ROW_BUFS = 3


def _out_kernel(*refs, two_src, n_tiles):
    if two_src:
        ctx_ref, x_ref, *rest = refs
    else:
        x_ref, *rest = refs
    (yna_ref, yls_ref, mod_ref, og_ref, w_ref, fg_ref, wrt_ref, brt_ref, _, xo_ref, pos_ref, meta_ref, xs_ref,
     cnt_s, cur_s, alloc_s, tb_s, row_s, pos_v, pos_sm, sem_p, sem_r) = rest
    step = pl.program_id(0)

    def row_copy(buf, k, dst_row):
        return pltpu.make_async_copy(row_s.at[buf, pl.ds(k, 1), :], xs_ref.at[pl.ds(dst_row, 1), :], sem_r.at[buf])

    def pos_copy(buf):
        return pltpu.make_async_copy(pos_v.at[buf], pos_sm.at[buf], sem_p.at[buf])

    def copy_out(tile):
        buf, pbuf = tile % ROW_BUFS, tile % 2
        pos_copy(pbuf).wait()
        for k in range(TM):
            row_copy(buf, k, pos_sm[pbuf, 0, k]).start()

    def drain(buf):
        def body(k, c):
            row_copy(buf, 0, 0).wait()
            return c
        lax.fori_loop(0, TM, body, 0, unroll=8)

    @pl.when(step == 0)
    def _():
        cnt_s[...] = jnp.zeros_like(cnt_s)
        cur_s[...] = jnp.zeros_like(cur_s)
        alloc_s[...] = jnp.zeros_like(alloc_s)
        tb_s[...] = jnp.zeros_like(tb_s)

    def project(tile):
        if two_src:
            x = jnp.where(tile % TILES_PER_B == 0, ctx_ref[...], x_ref[...])
        else:
            x = x_ref[...]
        yna = yna_ref[...].astype(jnp.float32)
        yls = yls_ref[...].astype(jnp.float32)
        merged = jnp.concatenate([
            _rms(yna, og_ref[:, :NA_WIDTH]),
            _rms(yls[:, :LRU_WIDTH], og_ref[:, NA_WIDTH:NA_WIDTH + LRU_WIDTH]),
            _rms(yls[:, LRU_WIDTH:], og_ref[:, NA_WIDTH + LRU_WIDTH:]),
        ], axis=-1).astype(jnp.bfloat16)
        y = jnp.dot(merged, w_ref[...], preferred_element_type=jnp.float32)
        x_new = x + mod_ref[2:3, :] * y
        xo_ref[...] = x_new
        h2 = _rms(x_new, fg_ref[...]) * (1.0 + mod_ref[4:5, :]) + mod_ref[3:4, :]
        row_s[tile % ROW_BUFS, :, :D_MODEL] = h2

    def route(tile):
        buf = tile % ROW_BUFS
        h2 = row_s[buf, :, :D_MODEL]
        h_hi = h2.astype(jnp.bfloat16)
        h_lo = (h2 - h_hi.astype(jnp.float32)).astype(jnp.bfloat16)
        part = lax.dot_general(wrt_ref[...], h_hi, _NT, preferred_element_type=jnp.float32)
        lt = (part[:N_EXPERTS] + part[N_EXPERTS:] + brt_ref[...]
              + lax.dot_general(wrt_ref[:N_EXPERTS, :], h_lo, _NT, preferred_element_type=jnp.float32))
        bucket, w_lo, w_hi = _route(jnp.exp(lt - jnp.max(lt, axis=0, keepdims=True)))

        b_iota = lax.broadcasted_iota(jnp.int32, (BUCKET_ROWS, TM), 0)
        onehot = b_iota == bucket
        tri = (lax.broadcasted_iota(jnp.int32, (TM, TM), 0) <= lax.broadcasted_iota(jnp.int32, (TM, TM), 1))
        as_bf16 = lambda mask: jnp.where(mask, 1.0, 0.0).astype(jnp.bfloat16)
        cum = jnp.dot(as_bf16(onehot), as_bf16(tri), preferred_element_type=jnp.float32)
        cnt_new = cum[:, TM - 1:TM].astype(jnp.int32)
        cnt_old = cnt_s[:, 0:1]
        open_id = cur_s[:, 0:1]
        alloc = alloc_s[0:1, 0:1]
        shift = TM.bit_length() - 1
        q_last = (cnt_old + cnt_new - 1) >> shift
        q_prev = (cnt_old - 1) >> shift
        opens = jnp.where(cnt_new > 0, q_last - q_prev, 0)
        lower = (lax.broadcasted_iota(jnp.int32, (BUCKET_ROWS, BUCKET_ROWS), 1)
                 < lax.broadcasted_iota(jnp.int32, (BUCKET_ROWS, BUCKET_ROWS), 0))
        opens_b = jnp.broadcast_to(opens, (BUCKET_ROWS, LANES)).astype(jnp.float32).astype(jnp.bfloat16)
        before = jnp.dot(as_bf16(lower), opens_b, preferred_element_type=jnp.float32)[:, 0:1].astype(jnp.int32)
        new_id = alloc + before
        rank = cnt_old + cum.astype(jnp.int32) - 1
        tile_id = jnp.where((opens > 0) & ((rank >> shift) == q_last), new_id, open_id)
        slot = tile_id * TM + (rank & (TM - 1))
        pos = jnp.sum(jnp.where(onehot, slot, 0).astype(jnp.float32), axis=0, keepdims=True).astype(jnp.int32)
        pos_ref[...] = pos

        lane_id = lax.broadcasted_iota(jnp.int32, (BUCKET_ROWS, TM), 1)
        opened_here = (opens > 0) & (new_id == lane_id)
        opened_bucket = jnp.max(jnp.where(opened_here, b_iota, -1).astype(jnp.float32), axis=0,
                                keepdims=True).astype(jnp.int32)
        tb = jnp.where(opened_bucket >= 0, opened_bucket, tb_s[0:1, :])
        alloc_new = alloc + jnp.sum(opens.astype(jnp.float32), axis=0, keepdims=True).astype(jnp.int32)
        tb_s[...] = jnp.broadcast_to(tb, tb_s.shape)
        cnt_s[...] = jnp.broadcast_to(cnt_old + cnt_new, cnt_s.shape)
        cur_s[...] = jnp.broadcast_to(jnp.where(opens > 0, new_id, open_id), cur_s.shape)
        alloc_s[...] = jnp.broadcast_to(alloc_new, alloc_s.shape)
        meta_ref[0:1, :] = tb
        meta_ref[1:2, :] = jnp.broadcast_to(alloc_new, (1, TM))
        meta_ref[2:8, :] = jnp.zeros((6, TM), jnp.int32)

        payload = jnp.concatenate([w_lo, w_hi, jnp.zeros((LANES - 2, TM), jnp.float32)], axis=0)
        row_s[buf, :, D_MODEL:] = payload.T
        pos_v[tile % 2] = jnp.broadcast_to(pos, (8, TM))

    @pl.when((step >= ROW_BUFS) & (step < n_tiles))
    def _():
        drain(step % ROW_BUFS)
        route(step - 1)
        project(step)
        pos_copy((step - 1) % 2).start()
        copy_out(step - 2)

    @pl.when(step < ROW_BUFS)
    def _():
        project(step)

    @pl.when((step >= 1) & (step < ROW_BUFS))
    def _():
        route(step - 1)
        pos_copy((step - 1) % 2).start()

    @pl.when(step == 2)
    def _():
        copy_out(step - 2)

    @pl.when(step == n_tiles)
    def _():
        route(step - 1)
        pos_copy((step - 1) % 2).start()
        copy_out(step - 2)

    @pl.when(step == n_tiles + 1)
    def _():
        copy_out(step - 2)
        for buf in range(ROW_BUFS):
            drain(buf)


def _out_proj(srcs, yna, yls, mod, out_g, w_bf16, ffn_g, w_router, b_router, bsz, with_ctx):
    two_src = len(srcs) == 2
    n_tiles, bj, mod_row = _tile_maps(bsz, with_ctx)
    assert n_tiles >= ROW_BUFS
    n_sorted = n_tiles + N_BUCKETS
    proj = lambda i: bj(jnp.minimum(i, n_tiles - 1))
    tile = lambda width: pl.BlockSpec((None, TM, width), lambda i: (proj(i)[0], proj(i)[1], 0))
    const2 = lambda i: (0, 0)
    if two_src:
        src_specs = [
            pl.BlockSpec((None, TM, D_MODEL), lambda i: (proj(i)[0], 0, 0)),
            pl.BlockSpec((None, TM, D_MODEL), lambda i: (proj(i)[0], jnp.maximum(proj(i)[1] - 1, 0), 0)),
        ]
    else:
        src_specs = [tile(D_MODEL)]
    xs_init = jnp.zeros((n_sorted * TM, ROW_WIDTH), jnp.float32)
    wr_hi = w_router.T.astype(jnp.bfloat16)
    wr_lo = (w_router.T - wr_hi.astype(jnp.float32)).astype(jnp.bfloat16)
    wr_split = jnp.concatenate([wr_hi, wr_lo], axis=0)
    return pl.pallas_call(
        functools.partial(_out_kernel, two_src=two_src, n_tiles=n_tiles),
        out_shape=[
            jax.ShapeDtypeStruct((bsz, ROWS_PER_B, D_MODEL), jnp.float32),
            jax.ShapeDtypeStruct((n_tiles, 1, TM), jnp.int32),
            jax.ShapeDtypeStruct((8, TM), jnp.int32),
            jax.ShapeDtypeStruct(xs_init.shape, jnp.float32),
        ],
        grid=(n_tiles + 2,),
        in_specs=src_specs + [
            tile(NA_WIDTH),
            tile(LRU_WIDTH + SC_WIDTH),
            pl.BlockSpec((None, N_MOD, D_MODEL), lambda i: (mod_row(jnp.minimum(i, n_tiles - 1)), 0, 0)),
            pl.BlockSpec((1, D_MODEL), const2),
            pl.BlockSpec((D_MODEL, D_MODEL), const2),
            pl.BlockSpec((1, D_MODEL), const2),
            pl.BlockSpec((2 * N_EXPERTS, D_MODEL), const2),
            pl.BlockSpec((N_EXPERTS, 1), const2),
            pl.BlockSpec(memory_space=pl.ANY),
        ],
        out_specs=[
            tile(D_MODEL),
            pl.BlockSpec((None, 1, TM), lambda i: (jnp.clip(i - 1, 0, n_tiles - 1), 0, 0)),
            pl.BlockSpec((8, TM), const2),
            pl.BlockSpec(memory_space=pl.ANY),
        ],
        scratch_shapes=[
            pltpu.VMEM((BUCKET_ROWS, LANES), jnp.int32),
            pltpu.VMEM((BUCKET_ROWS, LANES), jnp.int32),
            pltpu.VMEM((8, LANES), jnp.int32),
            pltpu.VMEM((8, TM), jnp.int32),
            pltpu.VMEM((ROW_BUFS, TM, ROW_WIDTH), jnp.float32),
            pltpu.VMEM((2, 8, TM), jnp.int32),
            pltpu.SMEM((2, 8, TM), jnp.int32),
            pltpu.SemaphoreType.DMA((2,)),
            pltpu.SemaphoreType.DMA((ROW_BUFS,)),
        ],
        input_output_aliases={len(srcs) + 8: 3},
        compiler_params=_cparams(("arbitrary",)),
        name="out_proj_route",
    )(*srcs, yna, yls, mod, out_g.reshape(1, D_MODEL), w_bf16, ffn_g.reshape(1, D_MODEL), wr_split,
      b_router.reshape(N_EXPERTS, 1), xs_init)


def _moe_kernel(src_ref, e0_ref, e1_ref, used_ref, xs_ref, wg0, wu0, wd0, wg1, wu1, wd1, o_ref, wg_s, wu_s, wd_s):
    n = pl.program_id(0)
    prev = jnp.maximum(n - 1, 0)
    for k, (e_ref, g, u, d) in enumerate(((e0_ref, wg0, wu0, wd0), (e1_ref, wg1, wu1, wd1))):
        @pl.when((n == 0) | (e_ref[n] != e_ref[prev]))
        def _():
            wg_s[k] = g[...].astype(jnp.bfloat16)
            wu_s[k] = u[...].astype(jnp.bfloat16)
            wd_s[k] = d[...].astype(jnp.bfloat16)

    @pl.when(n < used_ref[0])
    def _():
        xb = xs_ref[:, :D_MODEL].astype(jnp.bfloat16)
        w_lo, w_hi = xs_ref[:, D_MODEL:D_MODEL + 1], xs_ref[:, D_MODEL + 1:D_MODEL + 2]
        slot0_is_lo = e0_ref[n] < e1_ref[n]
        weights = (jnp.where(slot0_is_lo, w_lo, w_hi), jnp.where(slot0_is_lo, w_hi, w_lo))
        out = jnp.zeros((TM, D_MODEL), jnp.float32)
        for k, wk in enumerate(weights):
            gate = jnp.dot(xb, wg_s[k], preferred_element_type=jnp.float32)
            up = jnp.dot(xb, wu_s[k], preferred_element_type=jnp.float32)
            hid = (gate * jax.nn.sigmoid(gate)) * up
            out = out + wk * jnp.dot(hid.astype(jnp.bfloat16), wd_s[k], preferred_element_type=jnp.float32)
        o_ref[...] = out


def _moe(sched, xs, w_gate, w_up, w_down, layer):
    n_tiles = xs.shape[0] // TM
    src, e0, e1, used = sched
    first = lambda n, s, a, b, u: (layer, a[n], 0, 0)
    second = lambda n, s, a, b, u: (layer, b[n], 0, 0)
    gate_spec = lambda m: pl.BlockSpec((None, None, D_MODEL, D_EXPERT), m)
    down_spec = lambda m: pl.BlockSpec((None, None, D_EXPERT, D_MODEL), m)
    return pl.pallas_call(
        _moe_kernel,
        out_shape=jax.ShapeDtypeStruct((n_tiles * TM, D_MODEL), jnp.float32),
        grid_spec=pltpu.PrefetchScalarGridSpec(
            num_scalar_prefetch=4,
            grid=(n_tiles,),
            in_specs=[
                pl.BlockSpec((TM, ROW_WIDTH), lambda n, s, a, b, u: (s[n], 0)),
                gate_spec(first), gate_spec(first), down_spec(first),
                gate_spec(second), gate_spec(second), down_spec(second),
            ],
            out_specs=pl.BlockSpec((TM, D_MODEL), lambda n, s, a, b, u: (s[n], 0)),
            scratch_shapes=[
                pltpu.VMEM((2, D_MODEL, D_EXPERT), jnp.bfloat16),
                pltpu.VMEM((2, D_MODEL, D_EXPERT), jnp.bfloat16),
                pltpu.VMEM((2, D_EXPERT, D_MODEL), jnp.bfloat16),
            ],
        ),
        compiler_params=_cparams(("arbitrary",)),
        name="moe_experts",
    )(src, e0, e1, used, xs, w_gate, w_up, w_down, w_gate, w_up, w_down)


def _schedule(meta, n_tiles):
    visit = jnp.array([0, 2, 3, 1, 4, 5], jnp.int32)
    slot0 = jnp.array([0, 2, 3, 2, 3, 3], jnp.int32)
    slot1 = jnp.array([1, 0, 0, 1, 1, 2], jnp.int32)
    tile_bucket = meta[0, :n_tiles]
    used = meta[1, 0]
    ids = jnp.arange(n_tiles, dtype=jnp.int32)
    key = (tile_bucket // N_PAIRS) * N_PAIRS + visit[tile_bucket % N_PAIRS]
    order = jnp.argsort(jnp.where(ids < used, key, N_BUCKETS), stable=True).astype(jnp.int32)
    src = jnp.where(ids < used, order, order[jnp.maximum(used - 1, 0)])
    bucket = tile_bucket[src]
    group, pair = bucket // N_PAIRS, bucket % N_PAIRS
    return src, group * E_PER_GROUP + slot0[pair], group * E_PER_GROUP + slot1[pair], used.reshape(1)


def _combine_kernel(pos_ref, x_ref, mod_ref, fg_ref, ys_ref, o_ref, buf, sem, *, final):
    i = pl.program_id(0)
    n = pl.num_programs(0)

    def row_copy(src_row, slot, k):
        return pltpu.make_async_copy(ys_ref.at[pl.ds(src_row, 1), :], buf.at[slot, pl.ds(k, 1), :], sem.at[slot])

    def issue(tile, slot):
        for k in range(TM):
            row_copy(pos_ref[tile * TM + k], slot, k).start()

    @pl.when(i == 0)
    def _():
        issue(0, 0)

    @pl.when(i + 1 < n)
    def _():
        issue(i + 1, (i + 1) % 2)

    slot = i % 2

    def drain(k, c):
        row_copy(0, slot, 0).wait()
        return c
    lax.fori_loop(0, TM, drain, 0, unroll=8)

    x_new = x_ref[...] + mod_ref[5:6, :] * buf[slot]
    if final:
        x_new = _rms(x_new, fg_ref[...])
    o_ref[...] = x_new


def _combine(pos, x_all, mod, final_g, ys, bsz, with_ctx, final):
    n_tiles, bj, mod_row = _tile_maps(bsz, with_ctx)
    if final:
        out_shape = jax.ShapeDtypeStruct((bsz, SEQ, D_MODEL), jnp.float32)
        out_spec = pl.BlockSpec((None, TM, D_MODEL), lambda i, p: (bj(i)[0], bj(i)[1] - 1, 0))
    else:
        out_shape = jax.ShapeDtypeStruct((bsz, ROWS_PER_B, D_MODEL), jnp.float32)
        out_spec = pl.BlockSpec((None, TM, D_MODEL), lambda i, p: (bj(i)[0], bj(i)[1], 0))
    return pl.pallas_call(
        functools.partial(_combine_kernel, final=final),
        out_shape=out_shape,
        grid_spec=pltpu.PrefetchScalarGridSpec(
            num_scalar_prefetch=1,
            grid=(n_tiles,),
            in_specs=[
                pl.BlockSpec((None, TM, D_MODEL), lambda i, p: (bj(i)[0], bj(i)[1], 0)),
                pl.BlockSpec((None, N_MOD, D_MODEL), lambda i, p: (mod_row(i), 0, 0)),
                pl.BlockSpec((1, D_MODEL), lambda i, p: (0, 0)),
                pl.BlockSpec(memory_space=pl.ANY),
            ],
            out_specs=out_spec,
            scratch_shapes=[pltpu.VMEM((2, TM, D_MODEL), jnp.float32), pltpu.SemaphoreType.DMA((2,))],
        ),
        compiler_params=_cparams(("arbitrary",)),
        name="combine",
    )(pos.reshape(-1), x_all, mod, final_g.reshape(1, D_MODEL), ys)


def kernel(x, c, ctx, c_ctx, w_ada, b_ada, norm_mix_g, w_in, lru_conv_w, lru_conv_b, rg_w, rg_b, rg_lam, na_rpb,
           sc_conv_w, sc_conv_b, mix_out_g, w_out, norm_ffn_g, w_router, b_router, w_gate, w_up, w_down, final_g):
    bsz = x.shape[0]
    mod_rows = -(-(bsz + 1) // 8) * 8
    c_all = jnp.zeros((mod_rows, D_MODEL), jnp.float32).at[:bsz].set(c).at[bsz].set(c_ctx)
    mods = _ada(c_all, w_ada, b_ada).reshape(DEPTH, mod_rows, N_MOD, D_MODEL)

    x_all = None
    out = None
    for l in range(DEPTH):
        need_ctx = l < DEPTH - 1
        mod = mods[l]
        srcs = (ctx, x) if l == 0 else (x_all,)
        qkv, rest = _in_proj(srcs, norm_mix_g[l], mod, w_in[l].astype(jnp.bfloat16), bsz)
        yna = _attention(qkv, _attn_bias_table(na_rpb[l]), bsz, need_ctx)
        yls = _lru_sconv(rest, lru_conv_w[l], lru_conv_b[l], rg_w[l], rg_b[l], rg_lam[l], sc_conv_w[l],
                         sc_conv_b[l], bsz, need_ctx)
        x_mid, pos, meta, xs = _out_proj(srcs, yna, yls, mod, mix_out_g[l], w_out[l].astype(jnp.bfloat16),
                                         norm_ffn_g[l], w_router, b_router, bsz, need_ctx)
        ys = _moe(_schedule(meta, xs.shape[0] // TM), xs, w_gate, w_up, w_down, l)
        res = _combine(pos, x_mid, mod, final_g, ys, bsz, need_ctx, final=not need_ctx)
        if need_ctx:
            x_all = res
        else:
            out = res
    return out
```

```python
import functools

import jax
import jax.numpy as jnp
from jax import lax
from jax.experimental import pallas as pl
from jax.experimental.pallas import tpu as pltpu

D_MODEL = 1024
SEQ = 2048
CTX_LEN = 256
ROWS_PER_B = CTX_LEN + SEQ
DEPTH = 2
N_MOD = 6
EPS = 1e-6
NEG_INF = -1e30

GRID_W = 64
GRID_ROWS = SEQ // GRID_W
HEAD_DIM = 64
NA_WIDTH = 512
NA_HEADS = 8
LRU_WIDTH = 256
LRU_HEADS = 4
LRU_BLOCK = 64
SC_WIDTH = 256
QKV_WIDTH = 3 * NA_WIDTH
REST_WIDTH = 2 * LRU_WIDTH + 3 * SC_WIDTH
IN_WIDTH = QKV_WIDTH + REST_WIDTH
RG_C = 8.0
WIN_R = 8
WIN_C = 16
N_EXPERTS = 16
N_GROUPS = 4
E_PER_GROUP = 4
N_PAIRS = 6
N_BUCKETS = N_GROUPS * N_PAIRS
D_EXPERT = 512

TM = 256
TILES_PER_B = ROWS_PER_B // TM
LAT_TILES_PER_B = SEQ // TM
LANES = 128
BUCKET_ROWS = 32
ROW_WIDTH = D_MODEL + LANES
HEADS_PER_STACK = 4
STACK_W = HEADS_PER_STACK * HEAD_DIM
SCAN_ROWS = 8
VMEM_LIMIT = 56 * 1024 * 1024

_HI = lax.Precision.HIGHEST
_NT = (((1,), (1,)), ((), ()))


def _cparams(sem):
    return pltpu.CompilerParams(dimension_semantics=sem, vmem_limit_bytes=VMEM_LIMIT)


def _rms(v, g):
    return v * lax.rsqrt(jnp.mean(v * v, axis=-1, keepdims=True) + EPS) * g


def _ada_kernel(c_ref, w_ref, b_ref, o_ref):
    cond = c_ref[...]
    cond = cond * jax.nn.sigmoid(cond)
    o_ref[0] = jnp.dot(cond, w_ref[0], precision=_HI, preferred_element_type=jnp.float32) + b_ref[0]


def _ada(c_all, w_ada, b_ada):
    depth, _, width = w_ada.shape
    rows = c_all.shape[0]
    tn = 1536
    return pl.pallas_call(
        _ada_kernel,
        out_shape=jax.ShapeDtypeStruct((depth, rows, width), jnp.float32),
        grid=(depth, width // tn),
        in_specs=[
            pl.BlockSpec((rows, D_MODEL), lambda l, n: (0, 0)),
            pl.BlockSpec((1, D_MODEL, tn), lambda l, n: (l, 0, n)),
            pl.BlockSpec((1, 1, tn), lambda l, n: (l, 0, n)),
        ],
        out_specs=pl.BlockSpec((1, rows, tn), lambda l, n: (l, 0, n)),
        compiler_params=_cparams(("arbitrary", "arbitrary")),
        name="ada",
    )(c_all, w_ada, b_ada.reshape(depth, 1, width))


def _tile_maps(bsz, with_ctx):
    per_b = TILES_PER_B if with_ctx else LAT_TILES_PER_B
    off = 0 if with_ctx else 1

    def bj(i):
        return i // per_b, i % per_b + off

    def mod_row(i):
        b, j = bj(i)
        return jnp.where(j == 0, bsz, b)

    return per_b * bsz, bj, mod_row


IN_SUB = 3
IN_STEPS_PER_B = TILES_PER_B // IN_SUB


def _in_kernel(*refs, two_src):
    if two_src:
        ctx_ref, *x_refs = refs[:1 + IN_SUB]
        g_ref, modc_ref, modb_ref, w_ref, qkv_ref, rest_ref = refs[1 + IN_SUB:]
    else:
        x_ref, g_ref, modc_ref, modb_ref, w_ref, qkv_ref, rest_ref = refs
    first = pl.program_id(0) % IN_STEPS_PER_B == 0
    for s in range(IN_SUB):
        rows = slice(s * TM, (s + 1) * TM)
        if two_src:
            x = jnp.where(first, ctx_ref[...], x_refs[0][...]) if s == 0 else x_refs[s][...]
        else:
            x = x_ref[rows, :]
        if s == 0:
            shift = jnp.where(first, modc_ref[0:1, :], modb_ref[0:1, :])
            scale = jnp.where(first, modc_ref[1:2, :], modb_ref[1:2, :])
        else:
            shift, scale = modb_ref[0:1, :], modb_ref[1:2, :]
        hb = (_rms(x, g_ref[...]) * (1.0 + scale) + shift).astype(jnp.bfloat16)
        qkv_ref[rows, :] = jnp.dot(hb, w_ref[:, :QKV_WIDTH], preferred_element_type=jnp.float32).astype(jnp.bfloat16)
        rest_ref[rows, :] = jnp.dot(hb, w_ref[:, QKV_WIDTH:], preferred_element_type=jnp.float32)


def _in_proj(srcs, g, mod, w_bf16, bsz):
    two_src = len(srcs) == 2
    bj = lambda i: (i // IN_STEPS_PER_B, i % IN_STEPS_PER_B)
    rows = IN_SUB * TM
    if two_src:
        lat = lambda s: pl.BlockSpec(
            (None, TM, D_MODEL), lambda i: (bj(i)[0], jnp.maximum(IN_SUB * bj(i)[1] + s - 1, 0), 0))
        src_specs = [pl.BlockSpec((None, TM, D_MODEL), lambda i: (bj(i)[0], 0, 0))] + [lat(s) for s in range(IN_SUB)]
        srcs = (srcs[0],) + (srcs[1],) * IN_SUB
    else:
        src_specs = [pl.BlockSpec((None, rows, D_MODEL), lambda i: (bj(i)[0], bj(i)[1], 0))]
    return pl.pallas_call(
        functools.partial(_in_kernel, two_src=two_src),
        out_shape=[
            jax.ShapeDtypeStruct((bsz, ROWS_PER_B, QKV_WIDTH), jnp.bfloat16),
            jax.ShapeDtypeStruct((bsz, ROWS_PER_B, REST_WIDTH), jnp.float32),
        ],
        grid=(bsz * IN_STEPS_PER_B,),
        in_specs=src_specs + [
            pl.BlockSpec((1, D_MODEL), lambda i: (0, 0)),
            pl.BlockSpec((None, N_MOD, D_MODEL), lambda i: (bsz, 0, 0)),
            pl.BlockSpec((None, N_MOD, D_MODEL), lambda i: (bj(i)[0], 0, 0)),
            pl.BlockSpec((D_MODEL, IN_WIDTH), lambda i: (0, 0)),
        ],
        out_specs=[
            pl.BlockSpec((None, rows, QKV_WIDTH), lambda i: (bj(i)[0], bj(i)[1], 0)),
            pl.BlockSpec((None, rows, REST_WIDTH), lambda i: (bj(i)[0], bj(i)[1], 0)),
        ],
        compiler_params=_cparams(("arbitrary",)),
        name="in_proj",
    )(*srcs, g.reshape(1, D_MODEL), mod, mod, w_bf16)


def _attn_kernel(q_ref, k_ref, v_ref, bias_ref, o_ref, *, need_ctx):
    lane_head = lax.broadcasted_iota(jnp.int32, (1, STACK_W), 1) // HEAD_DIM
    n_stacks = NA_WIDTH // STACK_W

    def stack_q(qg):
        zero = jnp.zeros_like(qg)
        return jnp.concatenate([jnp.where(lane_head == h, qg, zero) for h in range(HEADS_PER_STACK)], axis=0)

    def unstack(o):
        out = jnp.zeros((GRID_W, STACK_W), jnp.float32)
        for h in range(HEADS_PER_STACK):
            out = out + jnp.where(lane_head == h, o[h * GRID_W:(h + 1) * GRID_W], 0.0)
        return out

    def attend(q_rows, s, local):
        cols = slice(s * STACK_W, (s + 1) * STACK_W)
        qg = q_ref[pl.ds(q_rows, GRID_W), cols] * jnp.bfloat16(HEAD_DIM ** -0.5)
        qs = stack_q(qg)
        kc = k_ref[0:CTX_LEN, cols]
        vc = v_ref[0:CTX_LEN, cols]
        s_ctx = lax.dot_general(qs, kc, _NT, preferred_element_type=jnp.float32)
        m = jnp.max(s_ctx, axis=-1, keepdims=True)
        if local is not None:
            k_rows, delta = local
            kw = k_ref[pl.ds(k_rows, WIN_R * GRID_W), cols]
            vw = v_ref[pl.ds(k_rows, WIN_R * GRID_W), cols]
            bias = bias_ref[delta, s * HEADS_PER_STACK:(s + 1) * HEADS_PER_STACK]
            s_loc = lax.dot_general(qs, kw, _NT, preferred_element_type=jnp.float32)
            s_loc = s_loc + bias.reshape(HEADS_PER_STACK * GRID_W, WIN_R * GRID_W)
            m = jnp.maximum(m, jnp.max(s_loc, axis=-1, keepdims=True))
            p_loc = jnp.exp(s_loc - m)
        p_ctx = jnp.exp(s_ctx - m)
        denom = jnp.sum(p_ctx, axis=-1, keepdims=True)
        o = jnp.dot(p_ctx.astype(jnp.bfloat16), vc, preferred_element_type=jnp.float32)
        if local is not None:
            denom = denom + jnp.sum(p_loc, axis=-1, keepdims=True)
            o = o + jnp.dot(p_loc.astype(jnp.bfloat16), vw, preferred_element_type=jnp.float32)
        o_ref[pl.ds(q_rows, GRID_W), cols] = unstack(o / denom).astype(o_ref.dtype)

    def lat_row(r, carry):
        r0 = jnp.clip(r - WIN_R // 2, 0, GRID_ROWS - WIN_R)
        q_rows = pl.multiple_of(CTX_LEN + r * GRID_W, GRID_W)
        k_rows = pl.multiple_of(CTX_LEN + r0 * GRID_W, GRID_W)
        for s in range(n_stacks):
            attend(q_rows, s, (k_rows, r - r0))
        return carry

    lax.fori_loop(0, GRID_ROWS, lat_row, 0, unroll=8)

    if need_ctx:
        def ctx_chunk(cq, carry):
            q_rows = pl.multiple_of(cq * GRID_W, GRID_W)
            for s in range(n_stacks):
                attend(q_rows, s, None)
            return carry

        lax.fori_loop(0, CTX_LEN // GRID_W, ctx_chunk, 0, unroll=2)
    else:
        o_ref[0:CTX_LEN, :] = jnp.zeros((CTX_LEN, NA_WIDTH), o_ref.dtype)


def _attn_bias_table(rpb):
    n_rel_c = 2 * WIN_C - 1
    lead = GRID_W - WIN_C
    padded = jnp.pad(rpb, ((0, 0), (0, 0), (lead, 2 * GRID_W - lead - n_rel_c)))
    skew = jnp.tile(padded, (1, 1, GRID_W))[..., :GRID_W * (2 * GRID_W - 1)]
    skew = skew.reshape(NA_HEADS, 2 * WIN_R - 1, GRID_W, 2 * GRID_W - 1)
    toeplitz = skew[..., GRID_W - 1:]
    q_col = jnp.arange(GRID_W)[:, None]
    k_col = jnp.arange(GRID_W)[None, :]
    c_start = jnp.clip(q_col - WIN_C // 2, 0, GRID_W - WIN_C)
    ok = (k_col >= c_start) & (k_col < c_start + WIN_C)
    toeplitz = jnp.where(ok, toeplitz, NEG_INF)
    per_delta = [toeplitz[:, WIN_R - 1 - d:2 * WIN_R - 1 - d] for d in range(WIN_R)]
    tab = jnp.stack(per_delta, axis=0)
    return tab.transpose(0, 1, 3, 2, 4).reshape(WIN_R, NA_HEADS, GRID_W, WIN_R * GRID_W)


def _attention(qkv, bias, bsz, need_ctx):
    return pl.pallas_call(
        functools.partial(_attn_kernel, need_ctx=need_ctx),
        out_shape=jax.ShapeDtypeStruct((bsz, ROWS_PER_B, NA_WIDTH), jnp.bfloat16),
        grid=(bsz,),
        in_specs=[
            pl.BlockSpec((None, ROWS_PER_B, NA_WIDTH), lambda b: (b, 0, 0)),
            pl.BlockSpec((None, ROWS_PER_B, NA_WIDTH), lambda b: (b, 0, 1)),
            pl.BlockSpec((None, ROWS_PER_B, NA_WIDTH), lambda b: (b, 0, 2)),
            pl.BlockSpec((WIN_R, NA_HEADS, GRID_W, WIN_R * GRID_W), lambda b: (0, 0, 0, 0)),
        ],
        out_specs=pl.BlockSpec((None, ROWS_PER_B, NA_WIDTH), lambda b: (b, 0, 0)),
        compiler_params=_cparams(("arbitrary",)),
        name="attention",
    )(qkv, qkv, qkv, bias)


CONV_PAD = 8


def _pad_base(start):
    return start + CONV_PAD * (1 if start == 0 else 2)


def _dwconv(pad_s, start, length, w_ref, b_ref, left):
    base = _pad_base(start)
    width = w_ref.shape[0]
    y = pad_s[base - left:base - left + length, :] * w_ref[0:1, :] + b_ref[...]
    for k in range(1, width):
        y = y + pad_s[base + k - left:base + k - left + length, :] * w_ref[k:k + 1, :]
    return y


def _lru_kernel(rest_ref, lcw_ref, lcb_ref, wbd_ref, rgb_ref, lam_ref, scw_ref, scb_ref, o_ref,
                xc_s, a_s, b_s, y_s, pad_s, *, need_ctx):
    col_rx, col_rg, col_sb, col_sc, col_sx = (k * LRU_WIDTH for k in range(5))
    segments = ((0, CTX_LEN), (CTX_LEN, SEQ))

    for start, length in segments:
        base = _pad_base(start)
        pad_s[base - CONV_PAD:base, :] = jnp.zeros((CONV_PAD, LRU_WIDTH), jnp.float32)
    pad_s[pad_s.shape[0] - CONV_PAD:, :] = jnp.zeros((CONV_PAD, LRU_WIDTH), jnp.float32)

    for start, length in segments:
        base = _pad_base(start)
        pad_s[base:base + length, :] = rest_ref[start:start + length, col_rx:col_rx + LRU_WIDTH]
    for start, length in segments:
        xc_s[start:start + length, :] = _dwconv(pad_s, start, length, lcw_ref, lcb_ref, 2)

    def coeffs(d, start, length):
        chunk = 256
        sp = jax.nn.softplus(-lam_ref[d:d + 1, :])
        for c0 in range(0, length, chunk):
            xc = xc_s[start + c0:start + c0 + chunk, :]
            pre = jnp.dot(xc.astype(jnp.bfloat16), wbd_ref[:, 2 * d * LRU_WIDTH:(2 * d + 2) * LRU_WIDTH],
                          preferred_element_type=jnp.float32) + rgb_ref[:, 2 * d * LRU_WIDTH:(2 * d + 2) * LRU_WIDTH]
            gate_r = jax.nn.sigmoid(pre[:, :LRU_WIDTH])
            gate_i = jax.nn.sigmoid(pre[:, LRU_WIDTH:])
            log_a = -RG_C * gate_r * sp
            a = jnp.exp(log_a)
            bb = jnp.sqrt(1.0 - a * a) * (gate_i * xc)
            a_s[start + c0:start + c0 + chunk, :] = a
            b_s[start + c0:start + c0 + chunk, :] = bb

    n_ctx_blocks = CTX_LEN // SCAN_ROWS
    n_blocks = ROWS_PER_B // SCAN_ROWS
    sub = lax.broadcasted_iota(jnp.int32, (SCAN_ROWS, LRU_WIDTH), 0)

    def scan(reverse, accumulate):
        def block(i, h_in):
            if reverse:
                blk = jnp.where(i < n_ctx_blocks, n_ctx_blocks - 1 - i, n_blocks + n_ctx_blocks - 1 - i)
            else:
                blk = i
            rows = pl.ds(pl.multiple_of(blk * SCAN_ROWS, SCAN_ROWS), SCAN_ROWS)
            a = a_s[rows, :]
            b = b_s[rows, :]
            for sh in (1, 2, 4):
                if reverse:
                    keep = sub < SCAN_ROWS - sh
                    a_n = pltpu.roll(a, SCAN_ROWS - sh, axis=0)
                    b_n = pltpu.roll(b, SCAN_ROWS - sh, axis=0)
                else:
                    keep = sub >= sh
                    a_n = pltpu.roll(a, sh, axis=0)
                    b_n = pltpu.roll(b, sh, axis=0)
                b = jnp.where(keep, a * b_n + b, b)
                a = jnp.where(keep, a * a_n, a)
            h = a * h_in + b
            y_s[rows, :] = y_s[rows, :] + h if accumulate else h
            return h[0:1, :] if reverse else h[SCAN_ROWS - 1:SCAN_ROWS, :]

        lax.fori_loop(0, n_blocks, block, jnp.zeros((1, LRU_WIDTH), jnp.float32), unroll=4)

    for d, reverse in enumerate((False, True)):
        for start, length in segments:
            coeffs(d, start, length)
        scan(reverse, accumulate=d > 0)

    out_segments = segments if need_ctx else segments[1:]
    for start, length in out_segments:
        rows = slice(start, start + length)
        y_lru = y_s[rows, :] * jax.nn.gelu(rest_ref[rows, col_rg:col_rg + LRU_WIDTH])
        o_ref[rows, 0:LRU_WIDTH] = y_lru.astype(o_ref.dtype)
        base = _pad_base(start)
        pad_s[base:base + length, :] = (rest_ref[rows, col_sc:col_sc + SC_WIDTH]
                                        * rest_ref[rows, col_sx:col_sx + SC_WIDTH])
        y_sc = rest_ref[rows, col_sb:col_sb + SC_WIDTH] * _dwconv(pad_s, start, length, scw_ref, scb_ref, 1)
        o_ref[rows, LRU_WIDTH:LRU_WIDTH + SC_WIDTH] = y_sc.astype(o_ref.dtype)
    if not need_ctx:
        o_ref[0:CTX_LEN, :] = jnp.zeros((CTX_LEN, LRU_WIDTH + SC_WIDTH), o_ref.dtype)


def _block_diag_gates(rg_w):
    eye = jnp.eye(LRU_HEADS, dtype=rg_w.dtype)
    full = jnp.einsum('dgncm,nk->dgnckm', rg_w, eye)
    full = full.reshape(2, 2, LRU_WIDTH, LRU_WIDTH)
    return full.transpose(2, 0, 1, 3).reshape(LRU_WIDTH, 4 * LRU_WIDTH)


def _lru_sconv(rest, lcw, lcb, rg_w, rg_b, rg_lam, scw, scb, bsz, need_ctx):
    wbd = _block_diag_gates(rg_w).astype(jnp.bfloat16)
    const2 = lambda b: (0, 0)
    return pl.pallas_call(
        functools.partial(_lru_kernel, need_ctx=need_ctx),
        out_shape=jax.ShapeDtypeStruct((bsz, ROWS_PER_B, LRU_WIDTH + SC_WIDTH), jnp.bfloat16),
        grid=(bsz,),
        in_specs=[
            pl.BlockSpec((None, ROWS_PER_B, REST_WIDTH), lambda b: (b, 0, 0)),
            pl.BlockSpec(lcw.shape, const2),
            pl.BlockSpec((1, LRU_WIDTH), const2),
            pl.BlockSpec((LRU_WIDTH, 4 * LRU_WIDTH), const2),
            pl.BlockSpec((1, 4 * LRU_WIDTH), const2),
            pl.BlockSpec((2, LRU_WIDTH), const2),
            pl.BlockSpec(scw.shape, const2),
            pl.BlockSpec((1, SC_WIDTH), const2),
        ],
        out_specs=pl.BlockSpec((None, ROWS_PER_B, LRU_WIDTH + SC_WIDTH), lambda b: (b, 0, 0)),
        scratch_shapes=[
            pltpu.VMEM((ROWS_PER_B, LRU_WIDTH), jnp.float32),
            pltpu.VMEM((ROWS_PER_B, LRU_WIDTH), jnp.float32),
            pltpu.VMEM((ROWS_PER_B, LRU_WIDTH), jnp.float32),
            pltpu.VMEM((ROWS_PER_B, LRU_WIDTH), jnp.float32),
            pltpu.VMEM((ROWS_PER_B + 3 * CONV_PAD, LRU_WIDTH), jnp.float32),
        ],
        compiler_params=_cparams(("arbitrary",)),
        name="lru_sconv",
    )(rest, lcw, lcb.reshape(1, LRU_WIDTH), wbd, rg_b.reshape(1, 4 * LRU_WIDTH), rg_lam, scw,
      scb.reshape(1, SC_WIDTH))


def _route(et):
    pe = [et[e:e + 1, :] for e in range(N_EXPERTS)]

    def top2_sum(v):
        best = v[0] + v[1]
        for a in range(E_PER_GROUP):
            for b in range(a + 1, E_PER_GROUP):
                if (a, b) != (0, 1):
                    best = jnp.maximum(best, v[a] + v[b])
        return best

    score = [top2_sum(pe[g * E_PER_GROUP:(g + 1) * E_PER_GROUP]) for g in range(N_GROUPS)]
    g_best, g_sel = score[0], jnp.zeros((1, TM), jnp.int32)
    for g in range(1, N_GROUPS):
        upd = score[g] > g_best
        g_sel = jnp.where(upd, g, g_sel)
        g_best = jnp.where(upd, score[g], g_best)
    p_in = []
    for k in range(E_PER_GROUP):
        v = pe[k]
        for g in range(1, N_GROUPS):
            v = jnp.where(g_sel == g, pe[g * E_PER_GROUP + k], v)
        p_in.append(v)
    m1, i1 = p_in[0], jnp.zeros((1, TM), jnp.int32)
    for k in range(1, E_PER_GROUP):
        upd = p_in[k] > m1
        i1 = jnp.where(upd, k, i1)
        m1 = jnp.where(upd, p_in[k], m1)
    m2, i2 = jnp.full((1, TM), -1.0, jnp.float32), jnp.zeros((1, TM), jnp.int32)
    for k in range(E_PER_GROUP):
        cand = jnp.where(i1 == k, -2.0, p_in[k])
        upd = cand > m2
        i2 = jnp.where(upd, k, i2)
        m2 = jnp.where(upd, cand, m2)
    lo, hi = jnp.minimum(i1, i2), jnp.maximum(i1, i2)
    pair = jnp.where(lo == 0, hi - 1, jnp.where(lo == 1, hi + 1, N_PAIRS - 1))
    bucket = g_sel * N_PAIRS + pair
    w1 = m1 / (m1 + m2)
    w2 = m2 / (m1 + m2)
    return bucket, jnp.where(i1 < i2, w1, w2), jnp.where(i1 < i2, w2, w1)


ROW_BUFS = 4


def _out_kernel(*refs, two_src, n_tiles):
    if two_src:
        ctx_ref, x_ref, *rest = refs
    else:
        x_ref, *rest = refs
    (yna_ref, yls_ref, mod_ref, og_ref, w_ref, fg_ref, wrt_ref, brt_ref, _, xo_ref, pos_ref, meta_ref, xs_ref,
     cnt_s, cur_s, alloc_s, tb_s, row_s, pos_v, pos_sm, sem_p, sem_r) = rest
    step = pl.program_id(0)

    def row_copy(buf, k, dst_row):
        return pltpu.make_async_copy(row_s.at[buf, pl.ds(k, 1), :], xs_ref.at[pl.ds(dst_row, 1), :], sem_r.at[buf])

    def pos_copy(buf):
        return pltpu.make_async_copy(pos_v.at[buf], pos_sm.at[buf], sem_p.at[buf])

    def copy_out(buf):
        pbuf = buf % 2
        pos_copy(pbuf).wait()
        for k in range(TM):
            row_copy(buf, k, pos_sm[pbuf, 0, k]).start()

    def drain(buf):
        def body(k, c):
            row_copy(buf, 0, 0).wait()
            return c
        lax.fori_loop(0, TM, body, 0, unroll=8)

    @pl.when(step == 0)
    def _():
        cnt_s[...] = jnp.zeros_like(cnt_s)
        cur_s[...] = jnp.zeros_like(cur_s)
        alloc_s[...] = jnp.zeros_like(alloc_s)
        tb_s[...] = jnp.zeros_like(tb_s)

    def project(tile):
        if two_src:
            x = jnp.where(tile % TILES_PER_B == 0, ctx_ref[...], x_ref[...])
        else:
            x = x_ref[...]
        yna = yna_ref[...].astype(jnp.float32)
        yls = yls_ref[...].astype(jnp.float32)
        merged = jnp.concatenate([
            _rms(yna, og_ref[:, :NA_WIDTH]),
            _rms(yls[:, :LRU_WIDTH], og_ref[:, NA_WIDTH:NA_WIDTH + LRU_WIDTH]),
            _rms(yls[:, LRU_WIDTH:], og_ref[:, NA_WIDTH + LRU_WIDTH:]),
        ], axis=-1).astype(jnp.bfloat16)
        y = jnp.dot(merged, w_ref[...], preferred_element_type=jnp.float32)
        x_new = x + mod_ref[2:3, :] * y
        xo_ref[...] = x_new
        h2 = _rms(x_new, fg_ref[...]) * (1.0 + mod_ref[4:5, :]) + mod_ref[3:4, :]
        row_s[tile % ROW_BUFS, :, :D_MODEL] = h2

    def route(tile):
        buf = tile % ROW_BUFS
        h2 = row_s[buf, :, :D_MODEL]
        h_hi = h2.astype(jnp.bfloat16)
        h_lo = (h2 - h_hi.astype(jnp.float32)).astype(jnp.bfloat16)
        part = lax.dot_general(wrt_ref[...], h_hi, _NT, preferred_element_type=jnp.float32)
        lt = (part[:N_EXPERTS] + part[N_EXPERTS:] + brt_ref[...]
              + lax.dot_general(wrt_ref[:N_EXPERTS, :], h_lo, _NT, preferred_element_type=jnp.float32))
        bucket, w_lo, w_hi = _route(jnp.exp(lt - jnp.max(lt, axis=0, keepdims=True)))

        b_iota = lax.broadcasted_iota(jnp.int32, (BUCKET_ROWS, TM), 0)
        onehot = b_iota == bucket
        tri = (lax.broadcasted_iota(jnp.int32, (TM, TM), 0) <= lax.broadcasted_iota(jnp.int32, (TM, TM), 1))
        as_bf16 = lambda mask: jnp.where(mask, 1.0, 0.0).astype(jnp.bfloat16)
        cum = jnp.dot(as_bf16(onehot), as_bf16(tri), preferred_element_type=jnp.float32)
        cnt_new = cum[:, TM - 1:TM].astype(jnp.int32)
        cnt_old = cnt_s[:, 0:1]
        open_id = cur_s[:, 0:1]
        alloc = alloc_s[0:1, 0:1]
        shift = TM.bit_length() - 1
        q_last = (cnt_old + cnt_new - 1) >> shift
        q_prev = (cnt_old - 1) >> shift
        opens = jnp.where(cnt_new > 0, q_last - q_prev, 0)
        lower = (lax.broadcasted_iota(jnp.int32, (BUCKET_ROWS, BUCKET_ROWS), 1)
                 < lax.broadcasted_iota(jnp.int32, (BUCKET_ROWS, BUCKET_ROWS), 0))
        opens_b = jnp.broadcast_to(opens, (BUCKET_ROWS, LANES)).astype(jnp.float32).astype(jnp.bfloat16)
        before = jnp.dot(as_bf16(lower), opens_b, preferred_element_type=jnp.float32)[:, 0:1].astype(jnp.int32)
        new_id = alloc + before
        rank = cnt_old + cum.astype(jnp.int32) - 1
        tile_id = jnp.where((opens > 0) & ((rank >> shift) == q_last), new_id, open_id)
        slot = tile_id * TM + (rank & (TM - 1))
        pos = jnp.sum(jnp.where(onehot, slot, 0).astype(jnp.float32), axis=0, keepdims=True).astype(jnp.int32)
        pos_ref[...] = pos

        lane_id = lax.broadcasted_iota(jnp.int32, (BUCKET_ROWS, TM), 1)
        opened_here = (opens > 0) & (new_id == lane_id)
        opened_bucket = jnp.max(jnp.where(opened_here, b_iota, -1).astype(jnp.float32), axis=0,
                                keepdims=True).astype(jnp.int32)
        tb = jnp.where(opened_bucket >= 0, opened_bucket, tb_s[0:1, :])
        alloc_new = alloc + jnp.sum(opens.astype(jnp.float32), axis=0, keepdims=True).astype(jnp.int32)
        tb_s[...] = jnp.broadcast_to(tb, tb_s.shape)
        cnt_s[...] = jnp.broadcast_to(cnt_old + cnt_new, cnt_s.shape)
        cur_s[...] = jnp.broadcast_to(jnp.where(opens > 0, new_id, open_id), cur_s.shape)
        alloc_s[...] = jnp.broadcast_to(alloc_new, alloc_s.shape)
        meta_ref[0:1, :] = tb
        meta_ref[1:2, :] = jnp.broadcast_to(alloc_new, (1, TM))
        meta_ref[2:8, :] = jnp.zeros((6, TM), jnp.int32)

        payload = jnp.concatenate([w_lo, w_hi, jnp.zeros((LANES - 2, TM), jnp.float32)], axis=0)
        row_s[buf, :, D_MODEL:] = payload.T
        pos_v[tile % 2] = jnp.broadcast_to(pos, (8, TM))

    @pl.when((step >= ROW_BUFS) & (step < n_tiles))
    def _():
        drain(step % ROW_BUFS)
        route(step - 1)
        project(step)
        pos_copy((step - 1) % 2).start()

    @pl.when(step < ROW_BUFS)
    def _():
        project(step)

    @pl.when(((step >= 1) & (step < ROW_BUFS)) | (step == n_tiles))
    def _():
        route(step - 1)
        pos_copy((step - 1) % 2).start()

    for buf in range(ROW_BUFS):
        @pl.when((step >= 2) & ((step - 2) % ROW_BUFS == buf))
        def _():
            copy_out(buf)

    @pl.when(step == n_tiles + 1)
    def _():
        for buf in range(ROW_BUFS):
            drain(buf)


def _out_proj(srcs, yna, yls, mod, out_g, w_bf16, ffn_g, w_router, b_router, bsz, with_ctx):
    two_src = len(srcs) == 2
    n_tiles, bj, mod_row = _tile_maps(bsz, with_ctx)
    assert n_tiles >= ROW_BUFS
    n_sorted = n_tiles + N_BUCKETS
    proj = lambda i: bj(jnp.minimum(i, n_tiles - 1))
    tile = lambda width: pl.BlockSpec((None, TM, width), lambda i: (proj(i)[0], proj(i)[1], 0))
    const2 = lambda i: (0, 0)
    if two_src:
        src_specs = [
            pl.BlockSpec((None, TM, D_MODEL), lambda i: (proj(i)[0], 0, 0)),
            pl.BlockSpec((None, TM, D_MODEL), lambda i: (proj(i)[0], jnp.maximum(proj(i)[1] - 1, 0), 0)),
        ]
    else:
        src_specs = [tile(D_MODEL)]
    xs_init = jnp.zeros((n_sorted * TM, ROW_WIDTH), jnp.float32)
    wr_hi = w_router.T.astype(jnp.bfloat16)
    wr_lo = (w_router.T - wr_hi.astype(jnp.float32)).astype(jnp.bfloat16)
    wr_split = jnp.concatenate([wr_hi, wr_lo], axis=0)
    return pl.pallas_call(
        functools.partial(_out_kernel, two_src=two_src, n_tiles=n_tiles),
        out_shape=[
            jax.ShapeDtypeStruct((bsz, ROWS_PER_B, D_MODEL), jnp.float32),
            jax.ShapeDtypeStruct((n_tiles, 1, TM), jnp.int32),
            jax.ShapeDtypeStruct((8, TM), jnp.int32),
            jax.ShapeDtypeStruct(xs_init.shape, jnp.float32),
        ],
        grid=(n_tiles + 2,),
        in_specs=src_specs + [
            tile(NA_WIDTH),
            tile(LRU_WIDTH + SC_WIDTH),
            pl.BlockSpec((None, N_MOD, D_MODEL), lambda i: (mod_row(jnp.minimum(i, n_tiles - 1)), 0, 0)),
            pl.BlockSpec((1, D_MODEL), const2),
            pl.BlockSpec((D_MODEL, D_MODEL), const2),
            pl.BlockSpec((1, D_MODEL), const2),
            pl.BlockSpec((2 * N_EXPERTS, D_MODEL), const2),
            pl.BlockSpec((N_EXPERTS, 1), const2),
            pl.BlockSpec(memory_space=pl.ANY),
        ],
        out_specs=[
            tile(D_MODEL),
            pl.BlockSpec((None, 1, TM), lambda i: (jnp.clip(i - 1, 0, n_tiles - 1), 0, 0)),
            pl.BlockSpec((8, TM), const2),
            pl.BlockSpec(memory_space=pl.ANY),
        ],
        scratch_shapes=[
            pltpu.VMEM((BUCKET_ROWS, LANES), jnp.int32),
            pltpu.VMEM((BUCKET_ROWS, LANES), jnp.int32),
            pltpu.VMEM((8, LANES), jnp.int32),
            pltpu.VMEM((8, TM), jnp.int32),
            pltpu.VMEM((ROW_BUFS, TM, ROW_WIDTH), jnp.float32),
            pltpu.VMEM((2, 8, TM), jnp.int32),
            pltpu.SMEM((2, 8, TM), jnp.int32),
            pltpu.SemaphoreType.DMA((2,)),
            pltpu.SemaphoreType.DMA((ROW_BUFS,)),
        ],
        input_output_aliases={len(srcs) + 8: 3},
        compiler_params=_cparams(("arbitrary",)),
        name="out_proj_route",
    )(*srcs, yna, yls, mod, out_g.reshape(1, D_MODEL), w_bf16, ffn_g.reshape(1, D_MODEL), wr_split,
      b_router.reshape(N_EXPERTS, 1), xs_init)


def _moe_kernel(src_ref, e0_ref, e1_ref, used_ref, xs_ref, wg0, wu0, wd0, wg1, wu1, wd1, o_ref, wg_s, wu_s, wd_s):
    n = pl.program_id(0)
    prev = jnp.maximum(n - 1, 0)
    for k, (e_ref, g, u, d) in enumerate(((e0_ref, wg0, wu0, wd0), (e1_ref, wg1, wu1, wd1))):
        @pl.when((n == 0) | (e_ref[n] != e_ref[prev]))
        def _():
            wg_s[k] = g[...].astype(jnp.bfloat16)
            wu_s[k] = u[...].astype(jnp.bfloat16)
            wd_s[k] = d[...].astype(jnp.bfloat16)

    @pl.when(n < used_ref[0])
    def _():
        xb = xs_ref[:, :D_MODEL].astype(jnp.bfloat16)
        w_lo, w_hi = xs_ref[:, D_MODEL:D_MODEL + 1], xs_ref[:, D_MODEL + 1:D_MODEL + 2]
        slot0_is_lo = e0_ref[n] < e1_ref[n]
        weights = (jnp.where(slot0_is_lo, w_lo, w_hi), jnp.where(slot0_is_lo, w_hi, w_lo))
        out = jnp.zeros((TM, D_MODEL), jnp.float32)
        for k, wk in enumerate(weights):
            gate = jnp.dot(xb, wg_s[k], preferred_element_type=jnp.float32)
            up = jnp.dot(xb, wu_s[k], preferred_element_type=jnp.float32)
            hid = (gate * jax.nn.sigmoid(gate)) * up
            out = out + wk * jnp.dot(hid.astype(jnp.bfloat16), wd_s[k], preferred_element_type=jnp.float32)
        o_ref[...] = out


def _moe(sched, xs, w_gate, w_up, w_down, layer):
    n_tiles = xs.shape[0] // TM
    src, e0, e1, used = sched
    first = lambda n, s, a, b, u: (layer, a[n], 0, 0)
    second = lambda n, s, a, b, u: (layer, b[n], 0, 0)
    gate_spec = lambda m: pl.BlockSpec((None, None, D_MODEL, D_EXPERT), m)
    down_spec = lambda m: pl.BlockSpec((None, None, D_EXPERT, D_MODEL), m)
    return pl.pallas_call(
        _moe_kernel,
        out_shape=jax.ShapeDtypeStruct((n_tiles * TM, D_MODEL), jnp.float32),
        grid_spec=pltpu.PrefetchScalarGridSpec(
            num_scalar_prefetch=4,
            grid=(n_tiles,),
            in_specs=[
                pl.BlockSpec((TM, ROW_WIDTH), lambda n, s, a, b, u: (s[n], 0)),
                gate_spec(first), gate_spec(first), down_spec(first),
                gate_spec(second), gate_spec(second), down_spec(second),
            ],
            out_specs=pl.BlockSpec((TM, D_MODEL), lambda n, s, a, b, u: (s[n], 0)),
            scratch_shapes=[
                pltpu.VMEM((2, D_MODEL, D_EXPERT), jnp.bfloat16),
                pltpu.VMEM((2, D_MODEL, D_EXPERT), jnp.bfloat16),
                pltpu.VMEM((2, D_EXPERT, D_MODEL), jnp.bfloat16),
            ],
        ),
        compiler_params=_cparams(("arbitrary",)),
        name="moe_experts",
    )(src, e0, e1, used, xs, w_gate, w_up, w_down, w_gate, w_up, w_down)


def _schedule(meta, n_tiles):
    visit = jnp.array([0, 2, 3, 1, 4, 5], jnp.int32)
    slot0 = jnp.array([0, 2, 3, 2, 3, 3], jnp.int32)
    slot1 = jnp.array([1, 0, 0, 1, 1, 2], jnp.int32)
    tile_bucket = meta[0, :n_tiles]
    used = meta[1, 0]
    ids = jnp.arange(n_tiles, dtype=jnp.int32)
    key = (tile_bucket // N_PAIRS) * N_PAIRS + visit[tile_bucket % N_PAIRS]
    order = jnp.argsort(jnp.where(ids < used, key, N_BUCKETS), stable=True).astype(jnp.int32)
    src = jnp.where(ids < used, order, order[jnp.maximum(used - 1, 0)])
    bucket = tile_bucket[src]
    group, pair = bucket // N_PAIRS, bucket % N_PAIRS
    return src, group * E_PER_GROUP + slot0[pair], group * E_PER_GROUP + slot1[pair], used.reshape(1)


def _combine_kernel(pos_ref, x_ref, mod_ref, fg_ref, ys_ref, o_ref, buf, sem, *, final):
    i = pl.program_id(0)
    n = pl.num_programs(0)

    def row_copy(src_row, slot, k):
        return pltpu.make_async_copy(ys_ref.at[pl.ds(src_row, 1), :], buf.at[slot, pl.ds(k, 1), :], sem.at[slot])

    def issue(tile, slot):
        for k in range(TM):
            row_copy(pos_ref[tile * TM + k], slot, k).start()

    @pl.when(i == 0)
    def _():
        issue(0, 0)

    for nxt in range(2):
        @pl.when((i + 1 < n) & ((i + 1) % 2 == nxt))
        def _():
            issue(i + 1, nxt)

    slot = i % 2

    def drain(k, c):
        row_copy(0, slot, 0).wait()
        return c
    lax.fori_loop(0, TM, drain, 0, unroll=8)

    x_new = x_ref[...] + mod_ref[5:6, :] * buf[slot]
    if final:
        x_new = _rms(x_new, fg_ref[...])
    o_ref[...] = x_new


def _combine(pos, x_all, mod, final_g, ys, bsz, with_ctx, final):
    n_tiles, bj, mod_row = _tile_maps(bsz, with_ctx)
    if final:
        out_shape = jax.ShapeDtypeStruct((bsz, SEQ, D_MODEL), jnp.float32)
        out_spec = pl.BlockSpec((None, TM, D_MODEL), lambda i, p: (bj(i)[0], bj(i)[1] - 1, 0))
    else:
        out_shape = jax.ShapeDtypeStruct((bsz, ROWS_PER_B, D_MODEL), jnp.float32)
        out_spec = pl.BlockSpec((None, TM, D_MODEL), lambda i, p: (bj(i)[0], bj(i)[1], 0))
    return pl.pallas_call(
        functools.partial(_combine_kernel, final=final),
        out_shape=out_shape,
        grid_spec=pltpu.PrefetchScalarGridSpec(
            num_scalar_prefetch=1,
            grid=(n_tiles,),
            in_specs=[
                pl.BlockSpec((None, TM, D_MODEL), lambda i, p: (bj(i)[0], bj(i)[1], 0)),
                pl.BlockSpec((None, N_MOD, D_MODEL), lambda i, p: (mod_row(i), 0, 0)),
                pl.BlockSpec((1, D_MODEL), lambda i, p: (0, 0)),
                pl.BlockSpec(memory_space=pl.ANY),
            ],
            out_specs=out_spec,
            scratch_shapes=[pltpu.VMEM((2, TM, D_MODEL), jnp.float32), pltpu.SemaphoreType.DMA((2,))],
        ),
        compiler_params=_cparams(("arbitrary",)),
        name="combine",
    )(pos.reshape(-1), x_all, mod, final_g.reshape(1, D_MODEL), ys)


def kernel(x, c, ctx, c_ctx, w_ada, b_ada, norm_mix_g, w_in, lru_conv_w, lru_conv_b, rg_w, rg_b, rg_lam, na_rpb,
           sc_conv_w, sc_conv_b, mix_out_g, w_out, norm_ffn_g, w_router, b_router, w_gate, w_up, w_down, final_g):
    bsz = x.shape[0]
    mod_rows = -(-(bsz + 1) // 8) * 8
    c_all = jnp.zeros((mod_rows, D_MODEL), jnp.float32).at[:bsz].set(c).at[bsz].set(c_ctx)
    mods = _ada(c_all, w_ada, b_ada).reshape(DEPTH, mod_rows, N_MOD, D_MODEL)

    x_all = None
    out = None
    for l in range(DEPTH):
        need_ctx = l < DEPTH - 1
        mod = mods[l]
        srcs = (ctx, x) if l == 0 else (x_all,)
        qkv, rest = _in_proj(srcs, norm_mix_g[l], mod, w_in[l].astype(jnp.bfloat16), bsz)
        yna = _attention(qkv, _attn_bias_table(na_rpb[l]), bsz, need_ctx)
        yls = _lru_sconv(rest, lru_conv_w[l], lru_conv_b[l], rg_w[l], rg_b[l], rg_lam[l], sc_conv_w[l],
                         sc_conv_b[l], bsz, need_ctx)
        x_mid, pos, meta, xs = _out_proj(srcs, yna, yls, mod, mix_out_g[l], w_out[l].astype(jnp.bfloat16),
                                         norm_ffn_g[l], w_router, b_router, bsz, need_ctx)
        ys = _moe(_schedule(meta, xs.shape[0] // TM), xs, w_gate, w_up, w_down, l)
        res = _combine(pos, x_mid, mod, final_g, ys, bsz, need_ctx, final=not need_ctx)
        if need_ctx:
            x_all = res
        else:
            out = res
    return out
```

```python
import functools

import jax
import jax.numpy as jnp
from jax import lax
from jax.experimental import pallas as pl
from jax.experimental.pallas import tpu as pltpu

D_MODEL = 1024
SEQ = 2048
CTX_LEN = 256
ROWS_PER_B = CTX_LEN + SEQ
DEPTH = 2
N_MOD = 6
EPS = 1e-6
NEG_INF = -1e30

GRID_W = 64
GRID_ROWS = SEQ // GRID_W
HEAD_DIM = 64
NA_WIDTH = 512
NA_HEADS = 8
LRU_WIDTH = 256
LRU_HEADS = 4
LRU_BLOCK = 64
SC_WIDTH = 256
QKV_WIDTH = 3 * NA_WIDTH
REST_WIDTH = 2 * LRU_WIDTH + 3 * SC_WIDTH
IN_WIDTH = QKV_WIDTH + REST_WIDTH
RG_C = 8.0
WIN_R = 8
WIN_C = 16
N_EXPERTS = 16
N_GROUPS = 4
E_PER_GROUP = 4
N_PAIRS = 6
N_BUCKETS = N_GROUPS * N_PAIRS
D_EXPERT = 512

TM = 256
TILES_PER_B = ROWS_PER_B // TM
LAT_TILES_PER_B = SEQ // TM
LANES = 128
BUCKET_ROWS = 32
ROW_WIDTH = D_MODEL + LANES
HEADS_PER_STACK = 4
STACK_W = HEADS_PER_STACK * HEAD_DIM
SCAN_ROWS = 8
VMEM_LIMIT = 56 * 1024 * 1024

_HI = lax.Precision.HIGHEST
_NT = (((1,), (1,)), ((), ()))


def _cparams(sem):
    return pltpu.CompilerParams(dimension_semantics=sem, vmem_limit_bytes=VMEM_LIMIT)


def _rms(v, g):
    return v * lax.rsqrt(jnp.mean(v * v, axis=-1, keepdims=True) + EPS) * g


def _ada_kernel(c_ref, w_ref, b_ref, o_ref):
    cond = c_ref[...]
    cond = cond * jax.nn.sigmoid(cond)
    o_ref[0] = jnp.dot(cond, w_ref[0], precision=_HI, preferred_element_type=jnp.float32) + b_ref[0]


def _ada(c_all, w_ada, b_ada):
    depth, _, width = w_ada.shape
    rows = c_all.shape[0]
    tn = 1536
    return pl.pallas_call(
        _ada_kernel,
        out_shape=jax.ShapeDtypeStruct((depth, rows, width), jnp.float32),
        grid=(depth, width // tn),
        in_specs=[
            pl.BlockSpec((rows, D_MODEL), lambda l, n: (0, 0)),
            pl.BlockSpec((1, D_MODEL, tn), lambda l, n: (l, 0, n)),
            pl.BlockSpec((1, 1, tn), lambda l, n: (l, 0, n)),
        ],
        out_specs=pl.BlockSpec((1, rows, tn), lambda l, n: (l, 0, n)),
        compiler_params=_cparams(("arbitrary", "arbitrary")),
        name="ada",
    )(c_all, w_ada, b_ada.reshape(depth, 1, width))


def _tile_maps(bsz, with_ctx):
    per_b = TILES_PER_B if with_ctx else LAT_TILES_PER_B
    off = 0 if with_ctx else 1

    def bj(i):
        return i // per_b, i % per_b + off

    def mod_row(i):
        b, j = bj(i)
        return jnp.where(j == 0, bsz, b)

    return per_b * bsz, bj, mod_row


IN_SUB = 3
IN_STEPS_PER_B = TILES_PER_B // IN_SUB


def _in_kernel(*refs, two_src):
    if two_src:
        ctx_ref, *x_refs = refs[:1 + IN_SUB]
        g_ref, modc_ref, modb_ref, w_ref, qkv_ref, rest_ref = refs[1 + IN_SUB:]
    else:
        x_ref, g_ref, modc_ref, modb_ref, w_ref, qkv_ref, rest_ref = refs
    first = pl.program_id(0) % IN_STEPS_PER_B == 0
    for s in range(IN_SUB):
        rows = slice(s * TM, (s + 1) * TM)
        if two_src:
            x = jnp.where(first, ctx_ref[...], x_refs[0][...]) if s == 0 else x_refs[s][...]
        else:
            x = x_ref[rows, :]
        if s == 0:
            shift = jnp.where(first, modc_ref[0:1, :], modb_ref[0:1, :])
            scale = jnp.where(first, modc_ref[1:2, :], modb_ref[1:2, :])
        else:
            shift, scale = modb_ref[0:1, :], modb_ref[1:2, :]
        hb = (_rms(x, g_ref[...]) * (1.0 + scale) + shift).astype(jnp.bfloat16)
        qkv_ref[rows, :] = jnp.dot(hb, w_ref[:, :QKV_WIDTH], preferred_element_type=jnp.float32).astype(jnp.bfloat16)
        rest_ref[rows, :] = jnp.dot(hb, w_ref[:, QKV_WIDTH:], preferred_element_type=jnp.float32)


def _in_proj(srcs, g, mod, w_bf16, bsz):
    two_src = len(srcs) == 2
    bj = lambda i: (i // IN_STEPS_PER_B, i % IN_STEPS_PER_B)
    rows = IN_SUB * TM
    if two_src:
        lat = lambda s: pl.BlockSpec(
            (None, TM, D_MODEL), lambda i: (bj(i)[0], jnp.maximum(IN_SUB * bj(i)[1] + s - 1, 0), 0))
        src_specs = [pl.BlockSpec((None, TM, D_MODEL), lambda i: (bj(i)[0], 0, 0))] + [lat(s) for s in range(IN_SUB)]
        srcs = (srcs[0],) + (srcs[1],) * IN_SUB
    else:
        src_specs = [pl.BlockSpec((None, rows, D_MODEL), lambda i: (bj(i)[0], bj(i)[1], 0))]
    return pl.pallas_call(
        functools.partial(_in_kernel, two_src=two_src),
        out_shape=[
            jax.ShapeDtypeStruct((bsz, ROWS_PER_B, QKV_WIDTH), jnp.bfloat16),
            jax.ShapeDtypeStruct((bsz, ROWS_PER_B, REST_WIDTH), jnp.float32),
        ],
        grid=(bsz * IN_STEPS_PER_B,),
        in_specs=src_specs + [
            pl.BlockSpec((1, D_MODEL), lambda i: (0, 0)),
            pl.BlockSpec((None, N_MOD, D_MODEL), lambda i: (bsz, 0, 0)),
            pl.BlockSpec((None, N_MOD, D_MODEL), lambda i: (bj(i)[0], 0, 0)),
            pl.BlockSpec((D_MODEL, IN_WIDTH), lambda i: (0, 0)),
        ],
        out_specs=[
            pl.BlockSpec((None, rows, QKV_WIDTH), lambda i: (bj(i)[0], bj(i)[1], 0)),
            pl.BlockSpec((None, rows, REST_WIDTH), lambda i: (bj(i)[0], bj(i)[1], 0)),
        ],
        compiler_params=_cparams(("arbitrary",)),
        name="in_proj",
    )(*srcs, g.reshape(1, D_MODEL), mod, mod, w_bf16)


def _attn_kernel(q_ref, k_ref, v_ref, bias_ref, o_ref, *, need_ctx):
    lane_head = lax.broadcasted_iota(jnp.int32, (1, STACK_W), 1) // HEAD_DIM
    n_stacks = NA_WIDTH // STACK_W

    def stack_q(qg):
        zero = jnp.zeros_like(qg)
        return jnp.concatenate([jnp.where(lane_head == h, qg, zero) for h in range(HEADS_PER_STACK)], axis=0)

    def unstack(o):
        out = jnp.zeros((GRID_W, STACK_W), jnp.float32)
        for h in range(HEADS_PER_STACK):
            out = out + jnp.where(lane_head == h, o[h * GRID_W:(h + 1) * GRID_W], 0.0)
        return out

    def attend(q_rows, s, local):
        cols = slice(s * STACK_W, (s + 1) * STACK_W)
        qg = q_ref[pl.ds(q_rows, GRID_W), cols] * jnp.bfloat16(HEAD_DIM ** -0.5)
        qs = stack_q(qg)
        kc = k_ref[0:CTX_LEN, cols]
        vc = v_ref[0:CTX_LEN, cols]
        s_ctx = lax.dot_general(qs, kc, _NT, preferred_element_type=jnp.float32)
        m = jnp.max(s_ctx, axis=-1, keepdims=True)
        if local is not None:
            k_rows, delta = local
            kw = k_ref[pl.ds(k_rows, WIN_R * GRID_W), cols]
            vw = v_ref[pl.ds(k_rows, WIN_R * GRID_W), cols]
            bias = bias_ref[delta, s * HEADS_PER_STACK:(s + 1) * HEADS_PER_STACK]
            s_loc = lax.dot_general(qs, kw, _NT, preferred_element_type=jnp.float32)
            s_loc = s_loc + bias.reshape(HEADS_PER_STACK * GRID_W, WIN_R * GRID_W)
            m = jnp.maximum(m, jnp.max(s_loc, axis=-1, keepdims=True))
            p_loc = jnp.exp(s_loc - m)
        p_ctx = jnp.exp(s_ctx - m)
        denom = jnp.sum(p_ctx, axis=-1, keepdims=True)
        o = jnp.dot(p_ctx.astype(jnp.bfloat16), vc, preferred_element_type=jnp.float32)
        if local is not None:
            denom = denom + jnp.sum(p_loc, axis=-1, keepdims=True)
            o = o + jnp.dot(p_loc.astype(jnp.bfloat16), vw, preferred_element_type=jnp.float32)
        o_ref[pl.ds(q_rows, GRID_W), cols] = unstack(o / denom).astype(o_ref.dtype)

    def lat_row(r, carry):
        r0 = jnp.clip(r - WIN_R // 2, 0, GRID_ROWS - WIN_R)
        q_rows = pl.multiple_of(CTX_LEN + r * GRID_W, GRID_W)
        k_rows = pl.multiple_of(CTX_LEN + r0 * GRID_W, GRID_W)
        for s in range(n_stacks):
            attend(q_rows, s, (k_rows, r - r0))
        return carry

    lax.fori_loop(0, GRID_ROWS, lat_row, 0, unroll=8)

    if need_ctx:
        def ctx_chunk(cq, carry):
            q_rows = pl.multiple_of(cq * GRID_W, GRID_W)
            for s in range(n_stacks):
                attend(q_rows, s, None)
            return carry

        lax.fori_loop(0, CTX_LEN // GRID_W, ctx_chunk, 0, unroll=2)
    else:
        o_ref[0:CTX_LEN, :] = jnp.zeros((CTX_LEN, NA_WIDTH), o_ref.dtype)


def _attn_bias_table(rpb):
    n_rel_c = 2 * WIN_C - 1
    lead = GRID_W - WIN_C
    padded = jnp.pad(rpb, ((0, 0), (0, 0), (lead, 2 * GRID_W - lead - n_rel_c)))
    skew = jnp.tile(padded, (1, 1, GRID_W))[..., :GRID_W * (2 * GRID_W - 1)]
    skew = skew.reshape(NA_HEADS, 2 * WIN_R - 1, GRID_W, 2 * GRID_W - 1)
    toeplitz = skew[..., GRID_W - 1:]
    q_col = jnp.arange(GRID_W)[:, None]
    k_col = jnp.arange(GRID_W)[None, :]
    c_start = jnp.clip(q_col - WIN_C // 2, 0, GRID_W - WIN_C)
    ok = (k_col >= c_start) & (k_col < c_start + WIN_C)
    toeplitz = jnp.where(ok, toeplitz, NEG_INF)
    per_delta = [toeplitz[:, WIN_R - 1 - d:2 * WIN_R - 1 - d] for d in range(WIN_R)]
    tab = jnp.stack(per_delta, axis=0)
    return tab.transpose(0, 1, 3, 2, 4).reshape(WIN_R, NA_HEADS, GRID_W, WIN_R * GRID_W)


def _attention(qkv, bias, bsz, need_ctx):
    return pl.pallas_call(
        functools.partial(_attn_kernel, need_ctx=need_ctx),
        out_shape=jax.ShapeDtypeStruct((bsz, ROWS_PER_B, NA_WIDTH), jnp.bfloat16),
        grid=(bsz,),
        in_specs=[
            pl.BlockSpec((None, ROWS_PER_B, NA_WIDTH), lambda b: (b, 0, 0)),
            pl.BlockSpec((None, ROWS_PER_B, NA_WIDTH), lambda b: (b, 0, 1)),
            pl.BlockSpec((None, ROWS_PER_B, NA_WIDTH), lambda b: (b, 0, 2)),
            pl.BlockSpec((WIN_R, NA_HEADS, GRID_W, WIN_R * GRID_W), lambda b: (0, 0, 0, 0)),
        ],
        out_specs=pl.BlockSpec((None, ROWS_PER_B, NA_WIDTH), lambda b: (b, 0, 0)),
        compiler_params=_cparams(("arbitrary",)),
        name="attention",
    )(qkv, qkv, qkv, bias)


CONV_PAD = 8


def _pad_base(start):
    return start + CONV_PAD * (1 if start == 0 else 2)


def _dwconv(pad_s, start, length, w_ref, b_ref, left):
    base = _pad_base(start)
    width = w_ref.shape[0]
    y = pad_s[base - left:base - left + length, :] * w_ref[0:1, :] + b_ref[...]
    for k in range(1, width):
        y = y + pad_s[base + k - left:base + k - left + length, :] * w_ref[k:k + 1, :]
    return y


def _lru_kernel(rest_ref, lcw_ref, lcb_ref, wbd_ref, rgb_ref, lam_ref, scw_ref, scb_ref, o_ref,
                xc_s, a_s, b_s, y_s, pad_s, *, need_ctx):
    col_rx, col_rg, col_sb, col_sc, col_sx = (k * LRU_WIDTH for k in range(5))
    segments = ((0, CTX_LEN), (CTX_LEN, SEQ))

    for start, length in segments:
        base = _pad_base(start)
        pad_s[base - CONV_PAD:base, :] = jnp.zeros((CONV_PAD, LRU_WIDTH), jnp.float32)
    pad_s[pad_s.shape[0] - CONV_PAD:, :] = jnp.zeros((CONV_PAD, LRU_WIDTH), jnp.float32)

    for start, length in segments:
        base = _pad_base(start)
        pad_s[base:base + length, :] = rest_ref[start:start + length, col_rx:col_rx + LRU_WIDTH]
    for start, length in segments:
        xc_s[start:start + length, :] = _dwconv(pad_s, start, length, lcw_ref, lcb_ref, 2)

    def coeffs(d, start, length):
        chunk = 256
        sp = jax.nn.softplus(-lam_ref[d:d + 1, :])
        for c0 in range(0, length, chunk):
            xc = xc_s[start + c0:start + c0 + chunk, :]
            pre = jnp.dot(xc.astype(jnp.bfloat16), wbd_ref[:, 2 * d * LRU_WIDTH:(2 * d + 2) * LRU_WIDTH],
                          preferred_element_type=jnp.float32) + rgb_ref[:, 2 * d * LRU_WIDTH:(2 * d + 2) * LRU_WIDTH]
            gate_r = jax.nn.sigmoid(pre[:, :LRU_WIDTH])
            gate_i = jax.nn.sigmoid(pre[:, LRU_WIDTH:])
            log_a = -RG_C * gate_r * sp
            a = jnp.exp(log_a)
            bb = jnp.sqrt(1.0 - a * a) * (gate_i * xc)
            a_s[start + c0:start + c0 + chunk, :] = a
            b_s[start + c0:start + c0 + chunk, :] = bb

    n_ctx_blocks = CTX_LEN // SCAN_ROWS
    n_blocks = ROWS_PER_B // SCAN_ROWS
    sub = lax.broadcasted_iota(jnp.int32, (SCAN_ROWS, LRU_WIDTH), 0)

    def scan(reverse, accumulate):
        def block(i, h_in):
            if reverse:
                blk = jnp.where(i < n_ctx_blocks, n_ctx_blocks - 1 - i, n_blocks + n_ctx_blocks - 1 - i)
            else:
                blk = i
            rows = pl.ds(pl.multiple_of(blk * SCAN_ROWS, SCAN_ROWS), SCAN_ROWS)
            a = a_s[rows, :]
            b = b_s[rows, :]
            for sh in (1, 2, 4):
                if reverse:
                    keep = sub < SCAN_ROWS - sh
                    a_n = pltpu.roll(a, SCAN_ROWS - sh, axis=0)
                    b_n = pltpu.roll(b, SCAN_ROWS - sh, axis=0)
                else:
                    keep = sub >= sh
                    a_n = pltpu.roll(a, sh, axis=0)
                    b_n = pltpu.roll(b, sh, axis=0)
                b = jnp.where(keep, a * b_n + b, b)
                a = jnp.where(keep, a * a_n, a)
            h = a * h_in + b
            y_s[rows, :] = y_s[rows, :] + h if accumulate else h
            return h[0:1, :] if reverse else h[SCAN_ROWS - 1:SCAN_ROWS, :]

        lax.fori_loop(0, n_blocks, block, jnp.zeros((1, LRU_WIDTH), jnp.float32), unroll=4)

    for d, reverse in enumerate((False, True)):
        for start, length in segments:
            coeffs(d, start, length)
        scan(reverse, accumulate=d > 0)

    out_segments = segments if need_ctx else segments[1:]
    for start, length in out_segments:
        rows = slice(start, start + length)
        y_lru = y_s[rows, :] * jax.nn.gelu(rest_ref[rows, col_rg:col_rg + LRU_WIDTH])
        o_ref[rows, 0:LRU_WIDTH] = y_lru.astype(o_ref.dtype)
        base = _pad_base(start)
        pad_s[base:base + length, :] = (rest_ref[rows, col_sc:col_sc + SC_WIDTH]
                                        * rest_ref[rows, col_sx:col_sx + SC_WIDTH])
        y_sc = rest_ref[rows, col_sb:col_sb + SC_WIDTH] * _dwconv(pad_s, start, length, scw_ref, scb_ref, 1)
        o_ref[rows, LRU_WIDTH:LRU_WIDTH + SC_WIDTH] = y_sc.astype(o_ref.dtype)
    if not need_ctx:
        o_ref[0:CTX_LEN, :] = jnp.zeros((CTX_LEN, LRU_WIDTH + SC_WIDTH), o_ref.dtype)


def _block_diag_gates(rg_w):
    eye = jnp.eye(LRU_HEADS, dtype=rg_w.dtype)
    full = jnp.einsum('dgncm,nk->dgnckm', rg_w, eye)
    full = full.reshape(2, 2, LRU_WIDTH, LRU_WIDTH)
    return full.transpose(2, 0, 1, 3).reshape(LRU_WIDTH, 4 * LRU_WIDTH)


def _lru_sconv(rest, lcw, lcb, rg_w, rg_b, rg_lam, scw, scb, bsz, need_ctx):
    wbd = _block_diag_gates(rg_w).astype(jnp.bfloat16)
    const2 = lambda b: (0, 0)
    return pl.pallas_call(
        functools.partial(_lru_kernel, need_ctx=need_ctx),
        out_shape=jax.ShapeDtypeStruct((bsz, ROWS_PER_B, LRU_WIDTH + SC_WIDTH), jnp.bfloat16),
        grid=(bsz,),
        in_specs=[
            pl.BlockSpec((None, ROWS_PER_B, REST_WIDTH), lambda b: (b, 0, 0)),
            pl.BlockSpec(lcw.shape, const2),
            pl.BlockSpec((1, LRU_WIDTH), const2),
            pl.BlockSpec((LRU_WIDTH, 4 * LRU_WIDTH), const2),
            pl.BlockSpec((1, 4 * LRU_WIDTH), const2),
            pl.BlockSpec((2, LRU_WIDTH), const2),
            pl.BlockSpec(scw.shape, const2),
            pl.BlockSpec((1, SC_WIDTH), const2),
        ],
        out_specs=pl.BlockSpec((None, ROWS_PER_B, LRU_WIDTH + SC_WIDTH), lambda b: (b, 0, 0)),
        scratch_shapes=[
            pltpu.VMEM((ROWS_PER_B, LRU_WIDTH), jnp.float32),
            pltpu.VMEM((ROWS_PER_B, LRU_WIDTH), jnp.float32),
            pltpu.VMEM((ROWS_PER_B, LRU_WIDTH), jnp.float32),
            pltpu.VMEM((ROWS_PER_B, LRU_WIDTH), jnp.float32),
            pltpu.VMEM((ROWS_PER_B + 3 * CONV_PAD, LRU_WIDTH), jnp.float32),
        ],
        compiler_params=_cparams(("arbitrary",)),
        name="lru_sconv",
    )(rest, lcw, lcb.reshape(1, LRU_WIDTH), wbd, rg_b.reshape(1, 4 * LRU_WIDTH), rg_lam, scw,
      scb.reshape(1, SC_WIDTH))


def _route(et):
    pe = [et[e:e + 1, :] for e in range(N_EXPERTS)]

    def top2_sum(v):
        best = v[0] + v[1]
        for a in range(E_PER_GROUP):
            for b in range(a + 1, E_PER_GROUP):
                if (a, b) != (0, 1):
                    best = jnp.maximum(best, v[a] + v[b])
        return best

    score = [top2_sum(pe[g * E_PER_GROUP:(g + 1) * E_PER_GROUP]) for g in range(N_GROUPS)]
    g_best, g_sel = score[0], jnp.zeros((1, TM), jnp.int32)
    for g in range(1, N_GROUPS):
        upd = score[g] > g_best
        g_sel = jnp.where(upd, g, g_sel)
        g_best = jnp.where(upd, score[g], g_best)
    p_in = []
    for k in range(E_PER_GROUP):
        v = pe[k]
        for g in range(1, N_GROUPS):
            v = jnp.where(g_sel == g, pe[g * E_PER_GROUP + k], v)
        p_in.append(v)
    m1, i1 = p_in[0], jnp.zeros((1, TM), jnp.int32)
    for k in range(1, E_PER_GROUP):
        upd = p_in[k] > m1
        i1 = jnp.where(upd, k, i1)
        m1 = jnp.where(upd, p_in[k], m1)
    m2, i2 = jnp.full((1, TM), -1.0, jnp.float32), jnp.zeros((1, TM), jnp.int32)
    for k in range(E_PER_GROUP):
        cand = jnp.where(i1 == k, -2.0, p_in[k])
        upd = cand > m2
        i2 = jnp.where(upd, k, i2)
        m2 = jnp.where(upd, cand, m2)
    lo, hi = jnp.minimum(i1, i2), jnp.maximum(i1, i2)
    pair = jnp.where(lo == 0, hi - 1, jnp.where(lo == 1, hi + 1, N_PAIRS - 1))
    bucket = g_sel * N_PAIRS + pair
    w1 = m1 / (m1 + m2)
    w2 = m2 / (m1 + m2)
    return bucket, jnp.where(i1 < i2, w1, w2), jnp.where(i1 < i2, w2, w1)


ROW_BUFS = 4


def _out_kernel(*refs, two_src, n_tiles):
    if two_src:
        ctx_ref, x_ref, *rest = refs
    else:
        x_ref, *rest = refs
    (yna_ref, yls_ref, mod_ref, og_ref, w_ref, fg_ref, wrt_ref, brt_ref, _, xo_ref, pos_ref, meta_ref, xs_ref,
     cnt_s, cur_s, alloc_s, tb_s, row_s, pos_v, pos_sm, sem_p, sem_r) = rest
    step = pl.program_id(0)

    def row_copy(buf, k, dst_row):
        return pltpu.make_async_copy(row_s.at[buf, pl.ds(k, 1), :], xs_ref.at[pl.ds(dst_row, 1), :], sem_r.at[buf])

    def pos_copy(buf):
        return pltpu.make_async_copy(pos_v.at[buf], pos_sm.at[buf], sem_p.at[buf])

    def copy_out(buf):
        pbuf = buf % 2
        pos_copy(pbuf).wait()
        for k in range(TM):
            row_copy(buf, k, pos_sm[pbuf, 0, k]).start(priority=k % 2)

    def drain(buf):
        def body(k, c):
            row_copy(buf, 0, 0).wait()
            return c
        lax.fori_loop(0, TM, body, 0, unroll=8)

    @pl.when(step == 0)
    def _():
        cnt_s[...] = jnp.zeros_like(cnt_s)
        cur_s[...] = jnp.zeros_like(cur_s)
        alloc_s[...] = jnp.zeros_like(alloc_s)
        tb_s[...] = jnp.zeros_like(tb_s)

    def project(tile):
        if two_src:
            x = jnp.where(tile % TILES_PER_B == 0, ctx_ref[...], x_ref[...])
        else:
            x = x_ref[...]
        yna = yna_ref[...].astype(jnp.float32)
        yls = yls_ref[...].astype(jnp.float32)
        merged = jnp.concatenate([
            _rms(yna, og_ref[:, :NA_WIDTH]),
            _rms(yls[:, :LRU_WIDTH], og_ref[:, NA_WIDTH:NA_WIDTH + LRU_WIDTH]),
            _rms(yls[:, LRU_WIDTH:], og_ref[:, NA_WIDTH + LRU_WIDTH:]),
        ], axis=-1).astype(jnp.bfloat16)
        y = jnp.dot(merged, w_ref[...], preferred_element_type=jnp.float32)
        x_new = x + mod_ref[2:3, :] * y
        xo_ref[...] = x_new
        h2 = _rms(x_new, fg_ref[...]) * (1.0 + mod_ref[4:5, :]) + mod_ref[3:4, :]
        row_s[tile % ROW_BUFS, :, :D_MODEL] = h2

    def route(tile):
        buf = tile % ROW_BUFS
        h2 = row_s[buf, :, :D_MODEL]
        h_hi = h2.astype(jnp.bfloat16)
        h_lo = (h2 - h_hi.astype(jnp.float32)).astype(jnp.bfloat16)
        part = lax.dot_general(wrt_ref[...], h_hi, _NT, preferred_element_type=jnp.float32)
        lt = (part[:N_EXPERTS] + part[N_EXPERTS:] + brt_ref[...]
              + lax.dot_general(wrt_ref[:N_EXPERTS, :], h_lo, _NT, preferred_element_type=jnp.float32))
        bucket, w_lo, w_hi = _route(jnp.exp(lt - jnp.max(lt, axis=0, keepdims=True)))

        b_iota = lax.broadcasted_iota(jnp.int32, (BUCKET_ROWS, TM), 0)
        onehot = b_iota == bucket
        tri = (lax.broadcasted_iota(jnp.int32, (TM, TM), 0) <= lax.broadcasted_iota(jnp.int32, (TM, TM), 1))
        as_bf16 = lambda mask: jnp.where(mask, 1.0, 0.0).astype(jnp.bfloat16)
        cum = jnp.dot(as_bf16(onehot), as_bf16(tri), preferred_element_type=jnp.float32)
        cnt_new = cum[:, TM - 1:TM].astype(jnp.int32)
        cnt_old = cnt_s[:, 0:1]
        open_id = cur_s[:, 0:1]
        alloc = alloc_s[0:1, 0:1]
        shift = TM.bit_length() - 1
        q_last = (cnt_old + cnt_new - 1) >> shift
        q_prev = (cnt_old - 1) >> shift
        opens = jnp.where(cnt_new > 0, q_last - q_prev, 0)
        lower = (lax.broadcasted_iota(jnp.int32, (BUCKET_ROWS, BUCKET_ROWS), 1)
                 < lax.broadcasted_iota(jnp.int32, (BUCKET_ROWS, BUCKET_ROWS), 0))
        opens_b = jnp.broadcast_to(opens, (BUCKET_ROWS, LANES)).astype(jnp.float32).astype(jnp.bfloat16)
        before = jnp.dot(as_bf16(lower), opens_b, preferred_element_type=jnp.float32)[:, 0:1].astype(jnp.int32)
        new_id = alloc + before
        rank = cnt_old + cum.astype(jnp.int32) - 1
        tile_id = jnp.where((opens > 0) & ((rank >> shift) == q_last), new_id, open_id)
        slot = tile_id * TM + (rank & (TM - 1))
        pos = jnp.sum(jnp.where(onehot, slot, 0).astype(jnp.float32), axis=0, keepdims=True).astype(jnp.int32)
        pos_ref[...] = pos

        lane_id = lax.broadcasted_iota(jnp.int32, (BUCKET_ROWS, TM), 1)
        opened_here = (opens > 0) & (new_id == lane_id)
        opened_bucket = jnp.max(jnp.where(opened_here, b_iota, -1).astype(jnp.float32), axis=0,
                                keepdims=True).astype(jnp.int32)
        tb = jnp.where(opened_bucket >= 0, opened_bucket, tb_s[0:1, :])
        alloc_new = alloc + jnp.sum(opens.astype(jnp.float32), axis=0, keepdims=True).astype(jnp.int32)
        tb_s[...] = jnp.broadcast_to(tb, tb_s.shape)
        cnt_s[...] = jnp.broadcast_to(cnt_old + cnt_new, cnt_s.shape)
        cur_s[...] = jnp.broadcast_to(jnp.where(opens > 0, new_id, open_id), cur_s.shape)
        alloc_s[...] = jnp.broadcast_to(alloc_new, alloc_s.shape)
        meta_ref[0:1, :] = tb
        meta_ref[1:2, :] = jnp.broadcast_to(alloc_new, (1, TM))
        meta_ref[2:8, :] = jnp.zeros((6, TM), jnp.int32)

        payload = jnp.concatenate([w_lo, w_hi, jnp.zeros((LANES - 2, TM), jnp.float32)], axis=0)
        row_s[buf, :, D_MODEL:] = payload.T
        pos_v[tile % 2] = jnp.broadcast_to(pos, (8, TM))

    @pl.when((step >= ROW_BUFS) & (step < n_tiles))
    def _():
        drain(step % ROW_BUFS)
        route(step - 1)
        project(step)
        pos_copy((step - 1) % 2).start()

    @pl.when(((step >= 1) & (step < ROW_BUFS)) | (step == n_tiles))
    def _():
        route(step - 1)
        pos_copy((step - 1) % 2).start()

    @pl.when(step < ROW_BUFS)
    def _():
        project(step)

    for buf in range(ROW_BUFS):
        @pl.when((step >= 2) & ((step - 2) % ROW_BUFS == buf))
        def _():
            copy_out(buf)

    @pl.when(step == n_tiles + 1)
    def _():
        for buf in range(ROW_BUFS):
            drain(buf)


def _out_proj(srcs, yna, yls, mod, out_g, w_bf16, ffn_g, w_router, b_router, bsz, with_ctx):
    two_src = len(srcs) == 2
    n_tiles, bj, mod_row = _tile_maps(bsz, with_ctx)
    assert n_tiles >= ROW_BUFS
    n_sorted = n_tiles + N_BUCKETS
    proj = lambda i: bj(jnp.minimum(i, n_tiles - 1))
    tile = lambda width: pl.BlockSpec((None, TM, width), lambda i: (proj(i)[0], proj(i)[1], 0))
    const2 = lambda i: (0, 0)
    if two_src:
        src_specs = [
            pl.BlockSpec((None, TM, D_MODEL), lambda i: (proj(i)[0], 0, 0)),
            pl.BlockSpec((None, TM, D_MODEL), lambda i: (proj(i)[0], jnp.maximum(proj(i)[1] - 1, 0), 0)),
        ]
    else:
        src_specs = [tile(D_MODEL)]
    xs_init = jnp.zeros((n_sorted * TM, ROW_WIDTH), jnp.float32)
    wr_hi = w_router.T.astype(jnp.bfloat16)
    wr_lo = (w_router.T - wr_hi.astype(jnp.float32)).astype(jnp.bfloat16)
    wr_split = jnp.concatenate([wr_hi, wr_lo], axis=0)
    return pl.pallas_call(
        functools.partial(_out_kernel, two_src=two_src, n_tiles=n_tiles),
        out_shape=[
            jax.ShapeDtypeStruct((bsz, ROWS_PER_B, D_MODEL), jnp.float32),
            jax.ShapeDtypeStruct((n_tiles, 1, TM), jnp.int32),
            jax.ShapeDtypeStruct((8, TM), jnp.int32),
            jax.ShapeDtypeStruct(xs_init.shape, jnp.float32),
        ],
        grid=(n_tiles + 2,),
        in_specs=src_specs + [
            tile(NA_WIDTH),
            tile(LRU_WIDTH + SC_WIDTH),
            pl.BlockSpec((None, N_MOD, D_MODEL), lambda i: (mod_row(jnp.minimum(i, n_tiles - 1)), 0, 0)),
            pl.BlockSpec((1, D_MODEL), const2),
            pl.BlockSpec((D_MODEL, D_MODEL), const2),
            pl.BlockSpec((1, D_MODEL), const2),
            pl.BlockSpec((2 * N_EXPERTS, D_MODEL), const2),
            pl.BlockSpec((N_EXPERTS, 1), const2),
            pl.BlockSpec(memory_space=pl.ANY),
        ],
        out_specs=[
            tile(D_MODEL),
            pl.BlockSpec((None, 1, TM), lambda i: (jnp.clip(i - 1, 0, n_tiles - 1), 0, 0)),
            pl.BlockSpec((8, TM), const2),
            pl.BlockSpec(memory_space=pl.ANY),
        ],
        scratch_shapes=[
            pltpu.VMEM((BUCKET_ROWS, LANES), jnp.int32),
            pltpu.VMEM((BUCKET_ROWS, LANES), jnp.int32),
            pltpu.VMEM((8, LANES), jnp.int32),
            pltpu.VMEM((8, TM), jnp.int32),
            pltpu.VMEM((ROW_BUFS, TM, ROW_WIDTH), jnp.float32),
            pltpu.VMEM((2, 8, TM), jnp.int32),
            pltpu.SMEM((2, 8, TM), jnp.int32),
            pltpu.SemaphoreType.DMA((2,)),
            pltpu.SemaphoreType.DMA((ROW_BUFS,)),
        ],
        input_output_aliases={len(srcs) + 8: 3},
        compiler_params=_cparams(("arbitrary",)),
        name="out_proj_route",
    )(*srcs, yna, yls, mod, out_g.reshape(1, D_MODEL), w_bf16, ffn_g.reshape(1, D_MODEL), wr_split,
      b_router.reshape(N_EXPERTS, 1), xs_init)


def _moe_kernel(src_ref, e0_ref, e1_ref, used_ref, xs_ref, wg0, wu0, wd0, wg1, wu1, wd1, o_ref, wg_s, wu_s, wd_s):
    n = pl.program_id(0)
    prev = jnp.maximum(n - 1, 0)
    for k, (e_ref, g, u, d) in enumerate(((e0_ref, wg0, wu0, wd0), (e1_ref, wg1, wu1, wd1))):
        @pl.when((n == 0) | (e_ref[n] != e_ref[prev]))
        def _():
            wg_s[k] = g[...].astype(jnp.bfloat16)
            wu_s[k] = u[...].astype(jnp.bfloat16)
            wd_s[k] = d[...].astype(jnp.bfloat16)

    @pl.when(n < used_ref[0])
    def _():
        xb = xs_ref[:, :D_MODEL].astype(jnp.bfloat16)
        w_lo, w_hi = xs_ref[:, D_MODEL:D_MODEL + 1], xs_ref[:, D_MODEL + 1:D_MODEL + 2]
        slot0_is_lo = e0_ref[n] < e1_ref[n]
        weights = (jnp.where(slot0_is_lo, w_lo, w_hi), jnp.where(slot0_is_lo, w_hi, w_lo))
        out = jnp.zeros((TM, D_MODEL), jnp.float32)
        for k, wk in enumerate(weights):
            gate = jnp.dot(xb, wg_s[k], preferred_element_type=jnp.float32)
            up = jnp.dot(xb, wu_s[k], preferred_element_type=jnp.float32)
            hid = (gate * jax.nn.sigmoid(gate)) * up
            out = out + wk * jnp.dot(hid.astype(jnp.bfloat16), wd_s[k], preferred_element_type=jnp.float32)
        o_ref[...] = out


def _moe(sched, xs, w_gate, w_up, w_down, layer):
    n_tiles = xs.shape[0] // TM
    src, e0, e1, used = sched
    first = lambda n, s, a, b, u: (layer, a[n], 0, 0)
    second = lambda n, s, a, b, u: (layer, b[n], 0, 0)
    gate_spec = lambda m: pl.BlockSpec((None, None, D_MODEL, D_EXPERT), m)
    down_spec = lambda m: pl.BlockSpec((None, None, D_EXPERT, D_MODEL), m)
    return pl.pallas_call(
        _moe_kernel,
        out_shape=jax.ShapeDtypeStruct((n_tiles * TM, D_MODEL), jnp.float32),
        grid_spec=pltpu.PrefetchScalarGridSpec(
            num_scalar_prefetch=4,
            grid=(n_tiles,),
            in_specs=[
                pl.BlockSpec((TM, ROW_WIDTH), lambda n, s, a, b, u: (s[n], 0)),
                gate_spec(first), gate_spec(first), down_spec(first),
                gate_spec(second), gate_spec(second), down_spec(second),
            ],
            out_specs=pl.BlockSpec((TM, D_MODEL), lambda n, s, a, b, u: (s[n], 0)),
            scratch_shapes=[
                pltpu.VMEM((2, D_MODEL, D_EXPERT), jnp.bfloat16),
                pltpu.VMEM((2, D_MODEL, D_EXPERT), jnp.bfloat16),
                pltpu.VMEM((2, D_EXPERT, D_MODEL), jnp.bfloat16),
            ],
        ),
        compiler_params=_cparams(("arbitrary",)),
        name="moe_experts",
    )(src, e0, e1, used, xs, w_gate, w_up, w_down, w_gate, w_up, w_down)


def _schedule(meta, n_tiles):
    visit = jnp.array([0, 2, 3, 1, 4, 5], jnp.int32)
    slot0 = jnp.array([0, 2, 3, 2, 3, 3], jnp.int32)
    slot1 = jnp.array([1, 0, 0, 1, 1, 2], jnp.int32)
    tile_bucket = meta[0, :n_tiles]
    used = meta[1, 0]
    ids = jnp.arange(n_tiles, dtype=jnp.int32)
    key = (tile_bucket // N_PAIRS) * N_PAIRS + visit[tile_bucket % N_PAIRS]
    order = jnp.argsort(jnp.where(ids < used, key, N_BUCKETS), stable=True).astype(jnp.int32)
    src = jnp.where(ids < used, order, order[jnp.maximum(used - 1, 0)])
    bucket = tile_bucket[src]
    group, pair = bucket // N_PAIRS, bucket % N_PAIRS
    return src, group * E_PER_GROUP + slot0[pair], group * E_PER_GROUP + slot1[pair], used.reshape(1)


def _combine_kernel(pos_ref, x_ref, mod_ref, fg_ref, ys_ref, o_ref, buf, sem, *, final):
    i = pl.program_id(0)
    n = pl.num_programs(0)

    def row_copy(src_row, slot, k):
        return pltpu.make_async_copy(ys_ref.at[pl.ds(src_row, 1), :], buf.at[slot, pl.ds(k, 1), :], sem.at[slot])

    def issue(tile, slot):
        for k in range(TM):
            row_copy(pos_ref[tile * TM + k], slot, k).start(priority=k % 2)

    @pl.when(i == 0)
    def _():
        issue(0, 0)

    for nxt in range(2):
        @pl.when((i + 1 < n) & ((i + 1) % 2 == nxt))
        def _():
            issue(i + 1, nxt)

    slot = i % 2

    def drain(k, c):
        row_copy(0, slot, 0).wait()
        return c
    lax.fori_loop(0, TM, drain, 0, unroll=8)

    x_new = x_ref[...] + mod_ref[5:6, :] * buf[slot]
    if final:
        x_new = _rms(x_new, fg_ref[...])
    o_ref[...] = x_new


def _combine(pos, x_all, mod, final_g, ys, bsz, with_ctx, final):
    n_tiles, bj, mod_row = _tile_maps(bsz, with_ctx)
    if final:
        out_shape = jax.ShapeDtypeStruct((bsz, SEQ, D_MODEL), jnp.float32)
        out_spec = pl.BlockSpec((None, TM, D_MODEL), lambda i, p: (bj(i)[0], bj(i)[1] - 1, 0))
    else:
        out_shape = jax.ShapeDtypeStruct((bsz, ROWS_PER_B, D_MODEL), jnp.float32)
        out_spec = pl.BlockSpec((None, TM, D_MODEL), lambda i, p: (bj(i)[0], bj(i)[1], 0))
    return pl.pallas_call(
        functools.partial(_combine_kernel, final=final),
        out_shape=out_shape,
        grid_spec=pltpu.PrefetchScalarGridSpec(
            num_scalar_prefetch=1,
            grid=(n_tiles,),
            in_specs=[
                pl.BlockSpec((None, TM, D_MODEL), lambda i, p: (bj(i)[0], bj(i)[1], 0)),
                pl.BlockSpec((None, N_MOD, D_MODEL), lambda i, p: (mod_row(i), 0, 0)),
                pl.BlockSpec((1, D_MODEL), lambda i, p: (0, 0)),
                pl.BlockSpec(memory_space=pl.ANY),
            ],
            out_specs=out_spec,
            scratch_shapes=[pltpu.VMEM((2, TM, D_MODEL), jnp.float32), pltpu.SemaphoreType.DMA((2,))],
        ),
        compiler_params=_cparams(("arbitrary",)),
        name="combine",
    )(pos.reshape(-1), x_all, mod, final_g.reshape(1, D_MODEL), ys)


def kernel(x, c, ctx, c_ctx, w_ada, b_ada, norm_mix_g, w_in, lru_conv_w, lru_conv_b, rg_w, rg_b, rg_lam, na_rpb,
           sc_conv_w, sc_conv_b, mix_out_g, w_out, norm_ffn_g, w_router, b_router, w_gate, w_up, w_down, final_g):
    bsz = x.shape[0]
    mod_rows = -(-(bsz + 1) // 8) * 8
    c_all = jnp.zeros((mod_rows, D_MODEL), jnp.float32).at[:bsz].set(c).at[bsz].set(c_ctx)
    mods = _ada(c_all, w_ada, b_ada).reshape(DEPTH, mod_rows, N_MOD, D_MODEL)

    x_all = None
    out = None
    for l in range(DEPTH):
        need_ctx = l < DEPTH - 1
        mod = mods[l]
        srcs = (ctx, x) if l == 0 else (x_all,)
        qkv, rest = _in_proj(srcs, norm_mix_g[l], mod, w_in[l].astype(jnp.bfloat16), bsz)
        yna = _attention(qkv, _attn_bias_table(na_rpb[l]), bsz, need_ctx)
        yls = _lru_sconv(rest, lru_conv_w[l], lru_conv_b[l], rg_w[l], rg_b[l], rg_lam[l], sc_conv_w[l],
                         sc_conv_b[l], bsz, need_ctx)
        x_mid, pos, meta, xs = _out_proj(srcs, yna, yls, mod, mix_out_g[l], w_out[l].astype(jnp.bfloat16),
                                         norm_ffn_g[l], w_router, b_router, bsz, need_ctx)
        ys = _moe(_schedule(meta, xs.shape[0] // TM), xs, w_gate, w_up, w_down, l)
        res = _combine(pos, x_mid, mod, final_g, ys, bsz, need_ctx, final=not need_ctx)
        if need_ctx:
            x_all = res
        else:
            out = res
    return out
```

```python
import functools

import jax
import jax.numpy as jnp
from jax import lax
from jax.experimental import pallas as pl
from jax.experimental.pallas import tpu as pltpu

D_MODEL = 1024
SEQ = 2048
CTX_LEN = 256
ROWS_PER_B = CTX_LEN + SEQ
DEPTH = 2
N_MOD = 6
EPS = 1e-6
NEG_INF = -1e30

GRID_W = 64
GRID_ROWS = SEQ // GRID_W
HEAD_DIM = 64
NA_WIDTH = 512
NA_HEADS = 8
LRU_WIDTH = 256
LRU_HEADS = 4
LRU_BLOCK = 64
SC_WIDTH = 256
QKV_WIDTH = 3 * NA_WIDTH
REST_WIDTH = 2 * LRU_WIDTH + 3 * SC_WIDTH
IN_WIDTH = QKV_WIDTH + REST_WIDTH
RG_C = 8.0
WIN_R = 8
WIN_C = 16
N_EXPERTS = 16
N_GROUPS = 4
E_PER_GROUP = 4
N_PAIRS = 6
N_BUCKETS = N_GROUPS * N_PAIRS
D_EXPERT = 512

TM = 256
TILES_PER_B = ROWS_PER_B // TM
LAT_TILES_PER_B = SEQ // TM
LANES = 128
BUCKET_ROWS = 32
ROW_WIDTH = D_MODEL + LANES
HEADS_PER_STACK = 4
STACK_W = HEADS_PER_STACK * HEAD_DIM
SCAN_ROWS = 8
VMEM_LIMIT = 56 * 1024 * 1024

_HI = lax.Precision.HIGHEST
_NT = (((1,), (1,)), ((), ()))


def _cparams(sem):
    return pltpu.CompilerParams(dimension_semantics=sem, vmem_limit_bytes=VMEM_LIMIT)


def _rms(v, g):
    return v * lax.rsqrt(jnp.mean(v * v, axis=-1, keepdims=True) + EPS) * g


def _ada_kernel(c_ref, w_ref, b_ref, o_ref):
    cond = c_ref[...]
    cond = cond * jax.nn.sigmoid(cond)
    o_ref[0] = jnp.dot(cond, w_ref[0], precision=_HI, preferred_element_type=jnp.float32) + b_ref[0]


def _ada(c_all, w_ada, b_ada):
    depth, _, width = w_ada.shape
    rows = c_all.shape[0]
    tn = 1536
    return pl.pallas_call(
        _ada_kernel,
        out_shape=jax.ShapeDtypeStruct((depth, rows, width), jnp.float32),
        grid=(depth, width // tn),
        in_specs=[
            pl.BlockSpec((rows, D_MODEL), lambda l, n: (0, 0)),
            pl.BlockSpec((1, D_MODEL, tn), lambda l, n: (l, 0, n)),
            pl.BlockSpec((1, 1, tn), lambda l, n: (l, 0, n)),
        ],
        out_specs=pl.BlockSpec((1, rows, tn), lambda l, n: (l, 0, n)),
        compiler_params=_cparams(("arbitrary", "arbitrary")),
        name="ada",
    )(c_all, w_ada, b_ada.reshape(depth, 1, width))


def _tile_maps(bsz, with_ctx):
    per_b = TILES_PER_B if with_ctx else LAT_TILES_PER_B
    off = 0 if with_ctx else 1

    def bj(i):
        return i // per_b, i % per_b + off

    def mod_row(i):
        b, j = bj(i)
        return jnp.where(j == 0, bsz, b)

    return per_b * bsz, bj, mod_row


IN_SUB = 3
IN_STEPS_PER_B = TILES_PER_B // IN_SUB


def _in_kernel(*refs, two_src):
    if two_src:
        ctx_ref, *x_refs = refs[:1 + IN_SUB]
        g_ref, modc_ref, modb_ref, w_ref, qkv_ref, rest_ref = refs[1 + IN_SUB:]
    else:
        x_ref, g_ref, modc_ref, modb_ref, w_ref, qkv_ref, rest_ref = refs
    first = pl.program_id(0) % IN_STEPS_PER_B == 0
    for s in range(IN_SUB):
        rows = slice(s * TM, (s + 1) * TM)
        if two_src:
            x = jnp.where(first, ctx_ref[...], x_refs[0][...]) if s == 0 else x_refs[s][...]
        else:
            x = x_ref[rows, :]
        if s == 0:
            shift = jnp.where(first, modc_ref[0:1, :], modb_ref[0:1, :])
            scale = jnp.where(first, modc_ref[1:2, :], modb_ref[1:2, :])
        else:
            shift, scale = modb_ref[0:1, :], modb_ref[1:2, :]
        hb = (_rms(x, g_ref[...]) * (1.0 + scale) + shift).astype(jnp.bfloat16)
        qkv_ref[rows, :] = jnp.dot(hb, w_ref[:, :QKV_WIDTH], preferred_element_type=jnp.float32).astype(jnp.bfloat16)
        rest_ref[rows, :] = jnp.dot(hb, w_ref[:, QKV_WIDTH:], preferred_element_type=jnp.float32)


def _in_proj(srcs, g, mod, w_bf16, bsz):
    two_src = len(srcs) == 2
    bj = lambda i: (i // IN_STEPS_PER_B, i % IN_STEPS_PER_B)
    rows = IN_SUB * TM
    if two_src:
        lat = lambda s: pl.BlockSpec(
            (None, TM, D_MODEL), lambda i: (bj(i)[0], jnp.maximum(IN_SUB * bj(i)[1] + s - 1, 0), 0))
        src_specs = [pl.BlockSpec((None, TM, D_MODEL), lambda i: (bj(i)[0], 0, 0))] + [lat(s) for s in range(IN_SUB)]
        srcs = (srcs[0],) + (srcs[1],) * IN_SUB
    else:
        src_specs = [pl.BlockSpec((None, rows, D_MODEL), lambda i: (bj(i)[0], bj(i)[1], 0))]
    return pl.pallas_call(
        functools.partial(_in_kernel, two_src=two_src),
        out_shape=[
            jax.ShapeDtypeStruct((bsz, ROWS_PER_B, QKV_WIDTH), jnp.bfloat16),
            jax.ShapeDtypeStruct((bsz, ROWS_PER_B, REST_WIDTH), jnp.float32),
        ],
        grid=(bsz * IN_STEPS_PER_B,),
        in_specs=src_specs + [
            pl.BlockSpec((1, D_MODEL), lambda i: (0, 0)),
            pl.BlockSpec((None, N_MOD, D_MODEL), lambda i: (bsz, 0, 0)),
            pl.BlockSpec((None, N_MOD, D_MODEL), lambda i: (bj(i)[0], 0, 0)),
            pl.BlockSpec((D_MODEL, IN_WIDTH), lambda i: (0, 0)),
        ],
        out_specs=[
            pl.BlockSpec((None, rows, QKV_WIDTH), lambda i: (bj(i)[0], bj(i)[1], 0)),
            pl.BlockSpec((None, rows, REST_WIDTH), lambda i: (bj(i)[0], bj(i)[1], 0)),
        ],
        compiler_params=_cparams(("arbitrary",)),
        name="in_proj",
    )(*srcs, g.reshape(1, D_MODEL), mod, mod, w_bf16)


def _attn_kernel(q_ref, k_ref, v_ref, bias_ref, o_ref, *, need_ctx):
    lane_head = lax.broadcasted_iota(jnp.int32, (1, STACK_W), 1) // HEAD_DIM
    n_stacks = NA_WIDTH // STACK_W

    def stack_q(qg):
        zero = jnp.zeros_like(qg)
        return jnp.concatenate([jnp.where(lane_head == h, qg, zero) for h in range(HEADS_PER_STACK)], axis=0)

    def unstack(o):
        out = jnp.zeros((GRID_W, STACK_W), jnp.float32)
        for h in range(HEADS_PER_STACK):
            out = out + jnp.where(lane_head == h, o[h * GRID_W:(h + 1) * GRID_W], 0.0)
        return out

    def attend(q_rows, s, local):
        cols = slice(s * STACK_W, (s + 1) * STACK_W)
        qg = q_ref[pl.ds(q_rows, GRID_W), cols] * jnp.bfloat16(HEAD_DIM ** -0.5)
        qs = stack_q(qg)
        kc = k_ref[0:CTX_LEN, cols]
        vc = v_ref[0:CTX_LEN, cols]
        s_ctx = lax.dot_general(qs, kc, _NT, preferred_element_type=jnp.float32)
        m = jnp.max(s_ctx, axis=-1, keepdims=True)
        if local is not None:
            k_rows, delta = local
            kw = k_ref[pl.ds(k_rows, WIN_R * GRID_W), cols]
            vw = v_ref[pl.ds(k_rows, WIN_R * GRID_W), cols]
            bias = bias_ref[delta, s * HEADS_PER_STACK:(s + 1) * HEADS_PER_STACK]
            s_loc = lax.dot_general(qs, kw, _NT, preferred_element_type=jnp.float32)
            s_loc = s_loc + bias.reshape(HEADS_PER_STACK * GRID_W, WIN_R * GRID_W)
            m = jnp.maximum(m, jnp.max(s_loc, axis=-1, keepdims=True))
            p_loc = jnp.exp(s_loc - m)
        p_ctx = jnp.exp(s_ctx - m)
        denom = jnp.sum(p_ctx, axis=-1, keepdims=True)
        o = jnp.dot(p_ctx.astype(jnp.bfloat16), vc, preferred_element_type=jnp.float32)
        if local is not None:
            denom = denom + jnp.sum(p_loc, axis=-1, keepdims=True)
            o = o + jnp.dot(p_loc.astype(jnp.bfloat16), vw, preferred_element_type=jnp.float32)
        o_ref[pl.ds(q_rows, GRID_W), cols] = unstack(o / denom).astype(o_ref.dtype)

    def lat_row(r, carry):
        r0 = jnp.clip(r - WIN_R // 2, 0, GRID_ROWS - WIN_R)
        q_rows = pl.multiple_of(CTX_LEN + r * GRID_W, GRID_W)
        k_rows = pl.multiple_of(CTX_LEN + r0 * GRID_W, GRID_W)
        for s in range(n_stacks):
            attend(q_rows, s, (k_rows, r - r0))
        return carry

    lax.fori_loop(0, GRID_ROWS, lat_row, 0, unroll=8)

    if need_ctx:
        def ctx_chunk(cq, carry):
            q_rows = pl.multiple_of(cq * GRID_W, GRID_W)
            for s in range(n_stacks):
                attend(q_rows, s, None)
            return carry

        lax.fori_loop(0, CTX_LEN // GRID_W, ctx_chunk, 0, unroll=2)
    else:
        o_ref[0:CTX_LEN, :] = jnp.zeros((CTX_LEN, NA_WIDTH), o_ref.dtype)


def _attn_bias_table(rpb):
    n_rel_c = 2 * WIN_C - 1
    lead = GRID_W - WIN_C
    padded = jnp.pad(rpb, ((0, 0), (0, 0), (lead, 2 * GRID_W - lead - n_rel_c)))
    skew = jnp.tile(padded, (1, 1, GRID_W))[..., :GRID_W * (2 * GRID_W - 1)]
    skew = skew.reshape(NA_HEADS, 2 * WIN_R - 1, GRID_W, 2 * GRID_W - 1)
    toeplitz = skew[..., GRID_W - 1:]
    q_col = jnp.arange(GRID_W)[:, None]
    k_col = jnp.arange(GRID_W)[None, :]
    c_start = jnp.clip(q_col - WIN_C // 2, 0, GRID_W - WIN_C)
    ok = (k_col >= c_start) & (k_col < c_start + WIN_C)
    toeplitz = jnp.where(ok, toeplitz, NEG_INF)
    per_delta = [toeplitz[:, WIN_R - 1 - d:2 * WIN_R - 1 - d] for d in range(WIN_R)]
    tab = jnp.stack(per_delta, axis=0)
    return tab.transpose(0, 1, 3, 2, 4).reshape(WIN_R, NA_HEADS, GRID_W, WIN_R * GRID_W)


def _attention(qkv, bias, bsz, need_ctx):
    return pl.pallas_call(
        functools.partial(_attn_kernel, need_ctx=need_ctx),
        out_shape=jax.ShapeDtypeStruct((bsz, ROWS_PER_B, NA_WIDTH), jnp.bfloat16),
        grid=(bsz,),
        in_specs=[
            pl.BlockSpec((None, ROWS_PER_B, NA_WIDTH), lambda b: (b, 0, 0)),
            pl.BlockSpec((None, ROWS_PER_B, NA_WIDTH), lambda b: (b, 0, 1)),
            pl.BlockSpec((None, ROWS_PER_B, NA_WIDTH), lambda b: (b, 0, 2)),
            pl.BlockSpec((WIN_R, NA_HEADS, GRID_W, WIN_R * GRID_W), lambda b: (0, 0, 0, 0)),
        ],
        out_specs=pl.BlockSpec((None, ROWS_PER_B, NA_WIDTH), lambda b: (b, 0, 0)),
        compiler_params=_cparams(("arbitrary",)),
        name="attention",
    )(qkv, qkv, qkv, bias)


CONV_PAD = 8


def _pad_base(start):
    return start + CONV_PAD * (1 if start == 0 else 2)


def _dwconv(pad_s, start, length, w_ref, b_ref, left):
    base = _pad_base(start)
    width = w_ref.shape[0]
    y = pad_s[base - left:base - left + length, :] * w_ref[0:1, :] + b_ref[...]
    for k in range(1, width):
        y = y + pad_s[base + k - left:base + k - left + length, :] * w_ref[k:k + 1, :]
    return y


def _lru_kernel(rest_ref, lcw_ref, lcb_ref, wbd_ref, rgb_ref, lam_ref, scw_ref, scb_ref, o_ref,
                xc_s, a_s, b_s, y_s, pad_s, *, need_ctx):
    col_rx, col_rg, col_sb, col_sc, col_sx = (k * LRU_WIDTH for k in range(5))
    segments = ((0, CTX_LEN), (CTX_LEN, SEQ))

    for start, length in segments:
        base = _pad_base(start)
        pad_s[base - CONV_PAD:base, :] = jnp.zeros((CONV_PAD, LRU_WIDTH), jnp.float32)
    pad_s[pad_s.shape[0] - CONV_PAD:, :] = jnp.zeros((CONV_PAD, LRU_WIDTH), jnp.float32)

    for start, length in segments:
        base = _pad_base(start)
        pad_s[base:base + length, :] = rest_ref[start:start + length, col_rx:col_rx + LRU_WIDTH]
    for start, length in segments:
        xc_s[start:start + length, :] = _dwconv(pad_s, start, length, lcw_ref, lcb_ref, 2)

    def coeffs(d, start, length):
        chunk = 256
        sp = jax.nn.softplus(-lam_ref[d:d + 1, :])
        for c0 in range(0, length, chunk):
            xc = xc_s[start + c0:start + c0 + chunk, :]
            pre = jnp.dot(xc.astype(jnp.bfloat16), wbd_ref[:, 2 * d * LRU_WIDTH:(2 * d + 2) * LRU_WIDTH],
                          preferred_element_type=jnp.float32) + rgb_ref[:, 2 * d * LRU_WIDTH:(2 * d + 2) * LRU_WIDTH]
            gate_r = jax.nn.sigmoid(pre[:, :LRU_WIDTH])
            gate_i = jax.nn.sigmoid(pre[:, LRU_WIDTH:])
            log_a = -RG_C * gate_r * sp
            a = jnp.exp(log_a)
            bb = jnp.sqrt(1.0 - a * a) * (gate_i * xc)
            a_s[start + c0:start + c0 + chunk, :] = a
            b_s[start + c0:start + c0 + chunk, :] = bb

    n_ctx_blocks = CTX_LEN // SCAN_ROWS
    n_blocks = ROWS_PER_B // SCAN_ROWS
    sub = lax.broadcasted_iota(jnp.int32, (SCAN_ROWS, LRU_WIDTH), 0)

    def scan(reverse, accumulate):
        def block(i, h_in):
            if reverse:
                blk = jnp.where(i < n_ctx_blocks, n_ctx_blocks - 1 - i, n_blocks + n_ctx_blocks - 1 - i)
            else:
                blk = i
            rows = pl.ds(pl.multiple_of(blk * SCAN_ROWS, SCAN_ROWS), SCAN_ROWS)
            a = a_s[rows, :]
            b = b_s[rows, :]
            for sh in (1, 2, 4):
                if reverse:
                    keep = sub < SCAN_ROWS - sh
                    a_n = pltpu.roll(a, SCAN_ROWS - sh, axis=0)
                    b_n = pltpu.roll(b, SCAN_ROWS - sh, axis=0)
                else:
                    keep = sub >= sh
                    a_n = pltpu.roll(a, sh, axis=0)
                    b_n = pltpu.roll(b, sh, axis=0)
                b = jnp.where(keep, a * b_n + b, b)
                a = jnp.where(keep, a * a_n, a)
            h = a * h_in + b
            y_s[rows, :] = y_s[rows, :] + h if accumulate else h
            return h[0:1, :] if reverse else h[SCAN_ROWS - 1:SCAN_ROWS, :]

        lax.fori_loop(0, n_blocks, block, jnp.zeros((1, LRU_WIDTH), jnp.float32), unroll=4)

    for d, reverse in enumerate((False, True)):
        for start, length in segments:
            coeffs(d, start, length)
        scan(reverse, accumulate=d > 0)

    out_segments = segments if need_ctx else segments[1:]
    for start, length in out_segments:
        rows = slice(start, start + length)
        y_lru = y_s[rows, :] * jax.nn.gelu(rest_ref[rows, col_rg:col_rg + LRU_WIDTH])
        o_ref[rows, 0:LRU_WIDTH] = y_lru.astype(o_ref.dtype)
        base = _pad_base(start)
        pad_s[base:base + length, :] = (rest_ref[rows, col_sc:col_sc + SC_WIDTH]
                                        * rest_ref[rows, col_sx:col_sx + SC_WIDTH])
        y_sc = rest_ref[rows, col_sb:col_sb + SC_WIDTH] * _dwconv(pad_s, start, length, scw_ref, scb_ref, 1)
        o_ref[rows, LRU_WIDTH:LRU_WIDTH + SC_WIDTH] = y_sc.astype(o_ref.dtype)
    if not need_ctx:
        o_ref[0:CTX_LEN, :] = jnp.zeros((CTX_LEN, LRU_WIDTH + SC_WIDTH), o_ref.dtype)


def _block_diag_gates(rg_w):
    eye = jnp.eye(LRU_HEADS, dtype=rg_w.dtype)
    full = jnp.einsum('dgncm,nk->dgnckm', rg_w, eye)
    full = full.reshape(2, 2, LRU_WIDTH, LRU_WIDTH)
    return full.transpose(2, 0, 1, 3).reshape(LRU_WIDTH, 4 * LRU_WIDTH)


def _lru_sconv(rest, lcw, lcb, rg_w, rg_b, rg_lam, scw, scb, bsz, need_ctx):
    wbd = _block_diag_gates(rg_w).astype(jnp.bfloat16)
    const2 = lambda b: (0, 0)
    return pl.pallas_call(
        functools.partial(_lru_kernel, need_ctx=need_ctx),
        out_shape=jax.ShapeDtypeStruct((bsz, ROWS_PER_B, LRU_WIDTH + SC_WIDTH), jnp.bfloat16),
        grid=(bsz,),
        in_specs=[
            pl.BlockSpec((None, ROWS_PER_B, REST_WIDTH), lambda b: (b, 0, 0)),
            pl.BlockSpec(lcw.shape, const2),
            pl.BlockSpec((1, LRU_WIDTH), const2),
            pl.BlockSpec((LRU_WIDTH, 4 * LRU_WIDTH), const2),
            pl.BlockSpec((1, 4 * LRU_WIDTH), const2),
            pl.BlockSpec((2, LRU_WIDTH), const2),
            pl.BlockSpec(scw.shape, const2),
            pl.BlockSpec((1, SC_WIDTH), const2),
        ],
        out_specs=pl.BlockSpec((None, ROWS_PER_B, LRU_WIDTH + SC_WIDTH), lambda b: (b, 0, 0)),
        scratch_shapes=[
            pltpu.VMEM((ROWS_PER_B, LRU_WIDTH), jnp.float32),
            pltpu.VMEM((ROWS_PER_B, LRU_WIDTH), jnp.float32),
            pltpu.VMEM((ROWS_PER_B, LRU_WIDTH), jnp.float32),
            pltpu.VMEM((ROWS_PER_B, LRU_WIDTH), jnp.float32),
            pltpu.VMEM((ROWS_PER_B + 3 * CONV_PAD, LRU_WIDTH), jnp.float32),
        ],
        compiler_params=_cparams(("arbitrary",)),
        name="lru_sconv",
    )(rest, lcw, lcb.reshape(1, LRU_WIDTH), wbd, rg_b.reshape(1, 4 * LRU_WIDTH), rg_lam, scw,
      scb.reshape(1, SC_WIDTH))


def _route(et):
    pe = [et[e:e + 1, :] for e in range(N_EXPERTS)]

    def top2_sum(v):
        best = v[0] + v[1]
        for a in range(E_PER_GROUP):
            for b in range(a + 1, E_PER_GROUP):
                if (a, b) != (0, 1):
                    best = jnp.maximum(best, v[a] + v[b])
        return best

    score = [top2_sum(pe[g * E_PER_GROUP:(g + 1) * E_PER_GROUP]) for g in range(N_GROUPS)]
    g_best, g_sel = score[0], jnp.zeros((1, TM), jnp.int32)
    for g in range(1, N_GROUPS):
        upd = score[g] > g_best
        g_sel = jnp.where(upd, g, g_sel)
        g_best = jnp.where(upd, score[g], g_best)
    p_in = []
    for k in range(E_PER_GROUP):
        v = pe[k]
        for g in range(1, N_GROUPS):
            v = jnp.where(g_sel == g, pe[g * E_PER_GROUP + k], v)
        p_in.append(v)
    m1, i1 = p_in[0], jnp.zeros((1, TM), jnp.int32)
    for k in range(1, E_PER_GROUP):
        upd = p_in[k] > m1
        i1 = jnp.where(upd, k, i1)
        m1 = jnp.where(upd, p_in[k], m1)
    m2, i2 = jnp.full((1, TM), -1.0, jnp.float32), jnp.zeros((1, TM), jnp.int32)
    for k in range(E_PER_GROUP):
        cand = jnp.where(i1 == k, -2.0, p_in[k])
        upd = cand > m2
        i2 = jnp.where(upd, k, i2)
        m2 = jnp.where(upd, cand, m2)
    lo, hi = jnp.minimum(i1, i2), jnp.maximum(i1, i2)
    pair = jnp.where(lo == 0, hi - 1, jnp.where(lo == 1, hi + 1, N_PAIRS - 1))
    bucket = g_sel * N_PAIRS + pair
    w1 = m1 / (m1 + m2)
    w2 = m2 / (m1 + m2)
    return bucket, jnp.where(i1 < i2, w1, w2), jnp.where(i1 < i2, w2, w1)


ROW_BUFS = 4


def _out_kernel(*refs, two_src, n_tiles):
    if two_src:
        ctx_ref, x_ref, *rest = refs
    else:
        x_ref, *rest = refs
    (yna_ref, yls_ref, mod_ref, og_ref, w_ref, fg_ref, wrt_ref, brt_ref, xo_ref, pos_ref, meta_ref, xs_ref,
     cnt_s, cur_s, alloc_s, tb_s, row_s, zero_s, pos_v, pos_sm, sem_p, sem_r, sem_z) = rest
    step = pl.program_id(0)
    shift_tm = TM.bit_length() - 1

    def row_copy(buf, k, dst_row):
        return pltpu.make_async_copy(row_s.at[buf, pl.ds(k, 1), :], xs_ref.at[pl.ds(dst_row, 1), :], sem_r.at[buf])

    def pos_copy(buf):
        return pltpu.make_async_copy(pos_v.at[buf], pos_sm.at[buf], sem_p.at[buf])

    def copy_out(buf):
        pbuf = buf % 2
        pos_copy(pbuf).wait()
        for k in range(TM):
            row_copy(buf, k, pos_sm[pbuf, 0, k]).start(priority=k % 2)

    def drain(buf):
        def body(k, c):
            row_copy(buf, 0, 0).wait()
            return c
        lax.fori_loop(0, TM, body, 0, unroll=8)

    @pl.when(step == 0)
    def _():
        cnt_s[...] = jnp.zeros_like(cnt_s)
        cur_s[...] = jnp.zeros_like(cur_s)
        alloc_s[...] = jnp.zeros_like(alloc_s)
        tb_s[...] = jnp.zeros_like(tb_s)

    def project(tile):
        if two_src:
            x = jnp.where(tile % TILES_PER_B == 0, ctx_ref[...], x_ref[...])
        else:
            x = x_ref[...]
        yna = yna_ref[...].astype(jnp.float32)
        yls = yls_ref[...].astype(jnp.float32)
        merged = jnp.concatenate([
            _rms(yna, og_ref[:, :NA_WIDTH]),
            _rms(yls[:, :LRU_WIDTH], og_ref[:, NA_WIDTH:NA_WIDTH + LRU_WIDTH]),
            _rms(yls[:, LRU_WIDTH:], og_ref[:, NA_WIDTH + LRU_WIDTH:]),
        ], axis=-1).astype(jnp.bfloat16)
        y = jnp.dot(merged, w_ref[...], preferred_element_type=jnp.float32)
        x_new = x + mod_ref[2:3, :] * y
        xo_ref[...] = x_new
        h2 = _rms(x_new, fg_ref[...]) * (1.0 + mod_ref[4:5, :]) + mod_ref[3:4, :]
        row_s[tile % ROW_BUFS, :, :D_MODEL] = h2

    def route(tile):
        buf = tile % ROW_BUFS
        h2 = row_s[buf, :, :D_MODEL]
        h_hi = h2.astype(jnp.bfloat16)
        h_lo = (h2 - h_hi.astype(jnp.float32)).astype(jnp.bfloat16)
        part = lax.dot_general(wrt_ref[...], h_hi, _NT, preferred_element_type=jnp.float32)
        lt = (part[:N_EXPERTS] + part[N_EXPERTS:] + brt_ref[...]
              + lax.dot_general(wrt_ref[:N_EXPERTS, :], h_lo, _NT, preferred_element_type=jnp.float32))
        bucket, w_lo, w_hi = _route(jnp.exp(lt - jnp.max(lt, axis=0, keepdims=True)))

        b_iota = lax.broadcasted_iota(jnp.int32, (BUCKET_ROWS, TM), 0)
        onehot = b_iota == bucket
        tri = (lax.broadcasted_iota(jnp.int32, (TM, TM), 0) <= lax.broadcasted_iota(jnp.int32, (TM, TM), 1))
        as_bf16 = lambda mask: jnp.where(mask, 1.0, 0.0).astype(jnp.bfloat16)
        cum = jnp.dot(as_bf16(onehot), as_bf16(tri), preferred_element_type=jnp.float32)
        cnt_new = cum[:, TM - 1:TM].astype(jnp.int32)
        cnt_old = cnt_s[:, 0:1]
        open_id = cur_s[:, 0:1]
        alloc = alloc_s[0:1, 0:1]
        shift = TM.bit_length() - 1
        q_last = (cnt_old + cnt_new - 1) >> shift
        q_prev = (cnt_old - 1) >> shift
        opens = jnp.where(cnt_new > 0, q_last - q_prev, 0)
        lower = (lax.broadcasted_iota(jnp.int32, (BUCKET_ROWS, BUCKET_ROWS), 1)
                 < lax.broadcasted_iota(jnp.int32, (BUCKET_ROWS, BUCKET_ROWS), 0))
        opens_b = jnp.broadcast_to(opens, (BUCKET_ROWS, LANES)).astype(jnp.float32).astype(jnp.bfloat16)
        before = jnp.dot(as_bf16(lower), opens_b, preferred_element_type=jnp.float32)[:, 0:1].astype(jnp.int32)
        new_id = alloc + before
        rank = cnt_old + cum.astype(jnp.int32) - 1
        tile_id = jnp.where((opens > 0) & ((rank >> shift) == q_last), new_id, open_id)
        slot = tile_id * TM + (rank & (TM - 1))
        pos = jnp.sum(jnp.where(onehot, slot, 0).astype(jnp.float32), axis=0, keepdims=True).astype(jnp.int32)
        pos_ref[...] = pos

        lane_id = lax.broadcasted_iota(jnp.int32, (BUCKET_ROWS, TM), 1)
        opened_here = (opens > 0) & (new_id == lane_id)
        opened_bucket = jnp.max(jnp.where(opened_here, b_iota, -1).astype(jnp.float32), axis=0,
                                keepdims=True).astype(jnp.int32)
        tb = jnp.where(opened_bucket >= 0, opened_bucket, tb_s[0:1, :])
        alloc_new = alloc + jnp.sum(opens.astype(jnp.float32), axis=0, keepdims=True).astype(jnp.int32)
        tb_s[...] = jnp.broadcast_to(tb, tb_s.shape)
        cnt_s[...] = jnp.broadcast_to(cnt_old + cnt_new, cnt_s.shape)
        cur_s[...] = jnp.broadcast_to(jnp.where(opens > 0, new_id, open_id), cur_s.shape)
        alloc_s[...] = jnp.broadcast_to(alloc_new, alloc_s.shape)
        meta_ref[0:1, :] = tb
        meta_ref[1:2, :] = jnp.broadcast_to(alloc_new, (1, TM))
        meta_ref[2:8, :] = jnp.zeros((6, TM), jnp.int32)

        payload = jnp.concatenate([w_lo, w_hi, jnp.zeros((LANES - 2, TM), jnp.float32)], axis=0)
        row_s[buf, :, D_MODEL:] = payload.T
        pos_v[tile % 2] = jnp.broadcast_to(pos, (8, TM))

    @pl.when((step >= ROW_BUFS) & (step < n_tiles))
    def _():
        drain(step % ROW_BUFS)
        route(step - 1)
        project(step)
        pos_copy((step - 1) % 2).start()

    @pl.when(((step >= 1) & (step < ROW_BUFS)) | (step == n_tiles))
    def _():
        route(step - 1)
        pos_copy((step - 1) % 2).start()

    @pl.when(step < ROW_BUFS)
    def _():
        project(step)

    for buf in range(ROW_BUFS):
        @pl.when((step >= 2) & ((step - 2) % ROW_BUFS == buf))
        def _():
            copy_out(buf)

    @pl.when(step == n_tiles + 1)
    def _():
        for buf in range(ROW_BUFS):
            drain(buf)

        b_iota = lax.broadcasted_iota(jnp.int32, (BUCKET_ROWS, TM), 0)
        lane_id = lax.broadcasted_iota(jnp.int32, (BUCKET_ROWS, TM), 1)
        on_lanes = lambda col: jnp.sum(jnp.where(b_iota == lane_id, col, 0).astype(jnp.float32), axis=0,
                                       keepdims=True).astype(jnp.int32)
        pos_v[0, 0:1, :] = on_lanes(cnt_s[:, 0:1])
        pos_v[0, 1:2, :] = on_lanes(cur_s[:, 0:1])
        pos_copy(0).start()
        zero_s[...] = jnp.zeros_like(zero_s)
        pos_copy(0).wait()
        sub = SCAN_ROWS
        tail_bits = [1 << k for k in range(TM.bit_length() - 2, sub.bit_length() - 2, -1)]

        def tail_copies(act):
            def zero_rows(row, n_rows):
                return pltpu.make_async_copy(zero_s.at[pl.ds(0, n_rows), :], xs_ref.at[pl.ds(row, n_rows), :], sem_z)

            def per_bucket(b, c):
                cnt, tile = pos_sm[0, 0, b], pos_sm[0, 1, b]
                fill = cnt - (((cnt - 1) >> shift_tm) << shift_tm)
                aligned = ((fill + sub - 1) >> (sub.bit_length() - 1)) << (sub.bit_length() - 1)
                for j in range(sub - 1):
                    @pl.when((cnt > 0) & (fill + j < aligned))
                    def _():
                        act(zero_rows(tile * TM + fill + j, 1))
                tail = TM - aligned
                off = aligned
                for bit in tail_bits:
                    @pl.when((cnt > 0) & ((tail & bit) != 0))
                    def _():
                        act(zero_rows(pl.multiple_of(tile * TM + off, sub), bit))
                    off = off + (tail & bit)
                return c
            lax.fori_loop(0, N_BUCKETS, per_bucket, 0)

        tail_copies(lambda cp: cp.start())
        tail_copies(lambda cp: cp.wait())


def _out_proj(srcs, yna, yls, mod, out_g, w_bf16, ffn_g, w_router, b_router, bsz, with_ctx):
    two_src = len(srcs) == 2
    n_tiles, bj, mod_row = _tile_maps(bsz, with_ctx)
    assert n_tiles >= ROW_BUFS
    n_sorted = n_tiles + N_BUCKETS
    proj = lambda i: bj(jnp.minimum(i, n_tiles - 1))
    tile = lambda width: pl.BlockSpec((None, TM, width), lambda i: (proj(i)[0], proj(i)[1], 0))
    const2 = lambda i: (0, 0)
    if two_src:
        src_specs = [
            pl.BlockSpec((None, TM, D_MODEL), lambda i: (proj(i)[0], 0, 0)),
            pl.BlockSpec((None, TM, D_MODEL), lambda i: (proj(i)[0], jnp.maximum(proj(i)[1] - 1, 0), 0)),
        ]
    else:
        src_specs = [tile(D_MODEL)]
    wr_hi = w_router.T.astype(jnp.bfloat16)
    wr_lo = (w_router.T - wr_hi.astype(jnp.float32)).astype(jnp.bfloat16)
    wr_split = jnp.concatenate([wr_hi, wr_lo], axis=0)
    return pl.pallas_call(
        functools.partial(_out_kernel, two_src=two_src, n_tiles=n_tiles),
        out_shape=[
            jax.ShapeDtypeStruct((bsz, ROWS_PER_B, D_MODEL), jnp.float32),
            jax.ShapeDtypeStruct((n_tiles, 1, TM), jnp.int32),
            jax.ShapeDtypeStruct((8, TM), jnp.int32),
            jax.ShapeDtypeStruct((n_sorted * TM, ROW_WIDTH), jnp.float32),
        ],
        grid=(n_tiles + 2,),
        in_specs=src_specs + [
            tile(NA_WIDTH),
            tile(LRU_WIDTH + SC_WIDTH),
            pl.BlockSpec((None, N_MOD, D_MODEL), lambda i: (mod_row(jnp.minimum(i, n_tiles - 1)), 0, 0)),
            pl.BlockSpec((1, D_MODEL), const2),
            pl.BlockSpec((D_MODEL, D_MODEL), const2),
            pl.BlockSpec((1, D_MODEL), const2),
            pl.BlockSpec((2 * N_EXPERTS, D_MODEL), const2),
            pl.BlockSpec((N_EXPERTS, 1), const2),
        ],
        out_specs=[
            tile(D_MODEL),
            pl.BlockSpec((None, 1, TM), lambda i: (jnp.clip(i - 1, 0, n_tiles - 1), 0, 0)),
            pl.BlockSpec((8, TM), const2),
            pl.BlockSpec(memory_space=pl.ANY),
        ],
        scratch_shapes=[
            pltpu.VMEM((BUCKET_ROWS, LANES), jnp.int32),
            pltpu.VMEM((BUCKET_ROWS, LANES), jnp.int32),
            pltpu.VMEM((8, LANES), jnp.int32),
            pltpu.VMEM((8, TM), jnp.int32),
            pltpu.VMEM((ROW_BUFS, TM, ROW_WIDTH), jnp.float32),
            pltpu.VMEM((TM // 2, ROW_WIDTH), jnp.float32),
            pltpu.VMEM((2, 8, TM), jnp.int32),
            pltpu.SMEM((2, 8, TM), jnp.int32),
            pltpu.SemaphoreType.DMA((2,)),
            pltpu.SemaphoreType.DMA((ROW_BUFS,)),
            pltpu.SemaphoreType.DMA(()),
        ],
        compiler_params=_cparams(("arbitrary",)),
        name="out_proj_route",
    )(*srcs, yna, yls, mod, out_g.reshape(1, D_MODEL), w_bf16, ffn_g.reshape(1, D_MODEL), wr_split,
      b_router.reshape(N_EXPERTS, 1))


def _moe_kernel(src_ref, e0_ref, e1_ref, used_ref, xs_ref, wg0, wu0, wd0, wg1, wu1, wd1, o_ref, wg_s, wu_s, wd_s):
    n = pl.program_id(0)
    prev = jnp.maximum(n - 1, 0)
    for k, (e_ref, g, u, d) in enumerate(((e0_ref, wg0, wu0, wd0), (e1_ref, wg1, wu1, wd1))):
        @pl.when((n == 0) | (e_ref[n] != e_ref[prev]))
        def _():
            wg_s[k] = g[...].astype(jnp.bfloat16)
            wu_s[k] = u[...].astype(jnp.bfloat16)
            wd_s[k] = d[...].astype(jnp.bfloat16)

    @pl.when(n < used_ref[0])
    def _():
        xb = xs_ref[:, :D_MODEL].astype(jnp.bfloat16)
        w_lo, w_hi = xs_ref[:, D_MODEL:D_MODEL + 1], xs_ref[:, D_MODEL + 1:D_MODEL + 2]
        slot0_is_lo = e0_ref[n] < e1_ref[n]
        weights = (jnp.where(slot0_is_lo, w_lo, w_hi), jnp.where(slot0_is_lo, w_hi, w_lo))
        out = jnp.zeros((TM, D_MODEL), jnp.float32)
        for k, wk in enumerate(weights):
            gate = jnp.dot(xb, wg_s[k], preferred_element_type=jnp.float32)
            up = jnp.dot(xb, wu_s[k], preferred_element_type=jnp.float32)
            hid = (gate * jax.nn.sigmoid(gate)) * up
            out = out + wk * jnp.dot(hid.astype(jnp.bfloat16), wd_s[k], preferred_element_type=jnp.float32)
        o_ref[...] = out


def _moe(sched, xs, w_gate, w_up, w_down, layer):
    n_tiles = xs.shape[0] // TM
    src, e0, e1, used = sched
    first = lambda n, s, a, b, u: (layer, a[n], 0, 0)
    second = lambda n, s, a, b, u: (layer, b[n], 0, 0)
    gate_spec = lambda m: pl.BlockSpec((None, None, D_MODEL, D_EXPERT), m)
    down_spec = lambda m: pl.BlockSpec((None, None, D_EXPERT, D_MODEL), m)
    return pl.pallas_call(
        _moe_kernel,
        out_shape=jax.ShapeDtypeStruct((n_tiles * TM, D_MODEL), jnp.float32),
        grid_spec=pltpu.PrefetchScalarGridSpec(
            num_scalar_prefetch=4,
            grid=(n_tiles,),
            in_specs=[
                pl.BlockSpec((TM, ROW_WIDTH), lambda n, s, a, b, u: (s[n], 0)),
                gate_spec(first), gate_spec(first), down_spec(first),
                gate_spec(second), gate_spec(second), down_spec(second),
            ],
            out_specs=pl.BlockSpec((TM, D_MODEL), lambda n, s, a, b, u: (s[n], 0)),
            scratch_shapes=[
                pltpu.VMEM((2, D_MODEL, D_EXPERT), jnp.bfloat16),
                pltpu.VMEM((2, D_MODEL, D_EXPERT), jnp.bfloat16),
                pltpu.VMEM((2, D_EXPERT, D_MODEL), jnp.bfloat16),
            ],
        ),
        compiler_params=_cparams(("arbitrary",)),
        name="moe_experts",
    )(src, e0, e1, used, xs, w_gate, w_up, w_down, w_gate, w_up, w_down)


def _schedule(meta, n_tiles):
    visit = jnp.array([0, 2, 3, 1, 4, 5], jnp.int32)
    slot0 = jnp.array([0, 2, 3, 2, 3, 3], jnp.int32)
    slot1 = jnp.array([1, 0, 0, 1, 1, 2], jnp.int32)
    tile_bucket = meta[0, :n_tiles]
    used = meta[1, 0]
    ids = jnp.arange(n_tiles, dtype=jnp.int32)
    key = (tile_bucket // N_PAIRS) * N_PAIRS + visit[tile_bucket % N_PAIRS]
    order = jnp.argsort(jnp.where(ids < used, key, N_BUCKETS), stable=True).astype(jnp.int32)
    src = jnp.where(ids < used, order, order[jnp.maximum(used - 1, 0)])
    bucket = tile_bucket[src]
    group, pair = bucket // N_PAIRS, bucket % N_PAIRS
    return src, group * E_PER_GROUP + slot0[pair], group * E_PER_GROUP + slot1[pair], used.reshape(1)


def _combine_kernel(pos_ref, x_ref, mod_ref, fg_ref, ys_ref, o_ref, buf, sem, *, final):
    i = pl.program_id(0)
    n = pl.num_programs(0)

    def row_copy(src_row, slot, k):
        return pltpu.make_async_copy(ys_ref.at[pl.ds(src_row, 1), :], buf.at[slot, pl.ds(k, 1), :], sem.at[slot])

    def issue(tile, slot):
        for k in range(TM):
            row_copy(pos_ref[tile * TM + k], slot, k).start(priority=k % 2)

    @pl.when(i == 0)
    def _():
        issue(0, 0)

    for nxt in range(2):
        @pl.when((i + 1 < n) & ((i + 1) % 2 == nxt))
        def _():
            issue(i + 1, nxt)

    slot = i % 2

    def drain(k, c):
        row_copy(0, slot, 0).wait()
        return c
    lax.fori_loop(0, TM, drain, 0, unroll=8)

    x_new = x_ref[...] + mod_ref[5:6, :] * buf[slot]
    if final:
        x_new = _rms(x_new, fg_ref[...])
    o_ref[...] = x_new


def _combine(pos, x_all, mod, final_g, ys, bsz, with_ctx, final):
    n_tiles, bj, mod_row = _tile_maps(bsz, with_ctx)
    if final:
        out_shape = jax.ShapeDtypeStruct((bsz, SEQ, D_MODEL), jnp.float32)
        out_spec = pl.BlockSpec((None, TM, D_MODEL), lambda i, p: (bj(i)[0], bj(i)[1] - 1, 0))
    else:
        out_shape = jax.ShapeDtypeStruct((bsz, ROWS_PER_B, D_MODEL), jnp.float32)
        out_spec = pl.BlockSpec((None, TM, D_MODEL), lambda i, p: (bj(i)[0], bj(i)[1], 0))
    return pl.pallas_call(
        functools.partial(_combine_kernel, final=final),
        out_shape=out_shape,
        grid_spec=pltpu.PrefetchScalarGridSpec(
            num_scalar_prefetch=1,
            grid=(n_tiles,),
            in_specs=[
                pl.BlockSpec((None, TM, D_MODEL), lambda i, p: (bj(i)[0], bj(i)[1], 0)),
                pl.BlockSpec((None, N_MOD, D_MODEL), lambda i, p: (mod_row(i), 0, 0)),
                pl.BlockSpec((1, D_MODEL), lambda i, p: (0, 0)),
                pl.BlockSpec(memory_space=pl.ANY),
            ],
            out_specs=out_spec,
            scratch_shapes=[pltpu.VMEM((2, TM, D_MODEL), jnp.float32), pltpu.SemaphoreType.DMA((2,))],
        ),
        compiler_params=_cparams(("arbitrary",)),
        name="combine",
    )(pos.reshape(-1), x_all, mod, final_g.reshape(1, D_MODEL), ys)


def kernel(x, c, ctx, c_ctx, w_ada, b_ada, norm_mix_g, w_in, lru_conv_w, lru_conv_b, rg_w, rg_b, rg_lam, na_rpb,
           sc_conv_w, sc_conv_b, mix_out_g, w_out, norm_ffn_g, w_router, b_router, w_gate, w_up, w_down, final_g):
    bsz = x.shape[0]
    mod_rows = -(-(bsz + 1) // 8) * 8
    c_all = jnp.zeros((mod_rows, D_MODEL), jnp.float32).at[:bsz].set(c).at[bsz].set(c_ctx)
    mods = _ada(c_all, w_ada, b_ada).reshape(DEPTH, mod_rows, N_MOD, D_MODEL)

    x_all = None
    out = None
    for l in range(DEPTH):
        need_ctx = l < DEPTH - 1
        mod = mods[l]
        srcs = (ctx, x) if l == 0 else (x_all,)
        qkv, rest = _in_proj(srcs, norm_mix_g[l], mod, w_in[l].astype(jnp.bfloat16), bsz)
        yna = _attention(qkv, _attn_bias_table(na_rpb[l]), bsz, need_ctx)
        yls = _lru_sconv(rest, lru_conv_w[l], lru_conv_b[l], rg_w[l], rg_b[l], rg_lam[l], sc_conv_w[l],
                         sc_conv_b[l], bsz, need_ctx)
        x_mid, pos, meta, xs = _out_proj(srcs, yna, yls, mod, mix_out_g[l], w_out[l].astype(jnp.bfloat16),
                                         norm_ffn_g[l], w_router, b_router, bsz, need_ctx)
        ys = _moe(_schedule(meta, xs.shape[0] // TM), xs, w_gate, w_up, w_down, l)
        res = _combine(pos, x_mid, mod, final_g, ys, bsz, need_ctx, final=not need_ctx)
        if need_ctx:
            x_all = res
        else:
            out = res
    return out
```

```python
import functools

import jax
import jax.numpy as jnp
from jax import lax
from jax.experimental import pallas as pl
from jax.experimental.pallas import tpu as pltpu

D_MODEL = 1024
SEQ = 2048
CTX_LEN = 256
ROWS_PER_B = CTX_LEN + SEQ
DEPTH = 2
N_MOD = 6
EPS = 1e-6
NEG_INF = -1e30

GRID_W = 64
GRID_ROWS = SEQ // GRID_W
HEAD_DIM = 64
NA_WIDTH = 512
NA_HEADS = 8
LRU_WIDTH = 256
LRU_HEADS = 4
LRU_BLOCK = 64
SC_WIDTH = 256
QKV_WIDTH = 3 * NA_WIDTH
REST_WIDTH = 2 * LRU_WIDTH + 3 * SC_WIDTH
IN_WIDTH = QKV_WIDTH + REST_WIDTH
RG_C = 8.0
WIN_R = 8
WIN_C = 16
N_EXPERTS = 16
N_GROUPS = 4
E_PER_GROUP = 4
N_PAIRS = 6
N_BUCKETS = N_GROUPS * N_PAIRS
D_EXPERT = 512

TM = 256
TILES_PER_B = ROWS_PER_B // TM
LAT_TILES_PER_B = SEQ // TM
LANES = 128
BUCKET_ROWS = 32
ROW_WIDTH = D_MODEL + LANES
HEADS_PER_STACK = 4
STACK_W = HEADS_PER_STACK * HEAD_DIM
SCAN_ROWS = 8
VMEM_LIMIT = 56 * 1024 * 1024

_HI = lax.Precision.HIGHEST
_NT = (((1,), (1,)), ((), ()))


def _cparams(sem):
    return pltpu.CompilerParams(dimension_semantics=sem, vmem_limit_bytes=VMEM_LIMIT)


def _rms(v, g):
    return v * lax.rsqrt(jnp.mean(v * v, axis=-1, keepdims=True) + EPS) * g


def _ada_kernel(c_ref, w_ref, b_ref, o_ref):
    cond = c_ref[...]
    cond = cond * jax.nn.sigmoid(cond)
    w = w_ref[0]
    c_hi = cond.astype(jnp.bfloat16)
    c_lo = (cond - c_hi.astype(jnp.float32)).astype(jnp.bfloat16)
    w_hi = w.astype(jnp.bfloat16)
    w_lo = (w - w_hi.astype(jnp.float32)).astype(jnp.bfloat16)
    rows = cond.shape[0]
    part = jnp.dot(jnp.concatenate([c_hi, c_lo], axis=0), w_hi, preferred_element_type=jnp.float32)
    o_ref[0] = (part[:rows] + part[rows:] + jnp.dot(c_hi, w_lo, preferred_element_type=jnp.float32)) + b_ref[0]


def _ada(c_all, w_ada, b_ada):
    depth, _, width = w_ada.shape
    rows = c_all.shape[0]
    tn = 1536
    return pl.pallas_call(
        _ada_kernel,
        out_shape=jax.ShapeDtypeStruct((depth, rows, width), jnp.float32),
        grid=(depth, width // tn),
        in_specs=[
            pl.BlockSpec((rows, D_MODEL), lambda l, n: (0, 0)),
            pl.BlockSpec((1, D_MODEL, tn), lambda l, n: (l, 0, n)),
            pl.BlockSpec((1, 1, tn), lambda l, n: (l, 0, n)),
        ],
        out_specs=pl.BlockSpec((1, rows, tn), lambda l, n: (l, 0, n)),
        compiler_params=_cparams(("arbitrary", "arbitrary")),
        name="ada",
    )(c_all, w_ada, b_ada.reshape(depth, 1, width))


def _tile_maps(bsz, with_ctx):
    per_b = TILES_PER_B if with_ctx else LAT_TILES_PER_B
    off = 0 if with_ctx else 1

    def bj(i):
        return i // per_b, i % per_b + off

    def mod_row(i):
        b, j = bj(i)
        return jnp.where(j == 0, bsz, b)

    return per_b * bsz, bj, mod_row


IN_SUB = 3
IN_STEPS_PER_B = TILES_PER_B // IN_SUB


def _in_kernel(*refs, two_src):
    if two_src:
        ctx_ref, *x_refs = refs[:1 + IN_SUB]
        g_ref, modc_ref, modb_ref, w_ref, qkv_ref, rest_ref = refs[1 + IN_SUB:]
    else:
        x_ref, g_ref, modc_ref, modb_ref, w_ref, qkv_ref, rest_ref = refs
    first = pl.program_id(0) % IN_STEPS_PER_B == 0
    for s in range(IN_SUB):
        rows = slice(s * TM, (s + 1) * TM)
        if two_src:
            x = jnp.where(first, ctx_ref[...], x_refs[0][...]) if s == 0 else x_refs[s][...]
        else:
            x = x_ref[rows, :]
        if s == 0:
            shift = jnp.where(first, modc_ref[0:1, :], modb_ref[0:1, :])
            scale = jnp.where(first, modc_ref[1:2, :], modb_ref[1:2, :])
        else:
            shift, scale = modb_ref[0:1, :], modb_ref[1:2, :]
        hb = (_rms(x, g_ref[...]) * (1.0 + scale) + shift).astype(jnp.bfloat16)
        qkv_ref[rows, :] = jnp.dot(hb, w_ref[:, :QKV_WIDTH], preferred_element_type=jnp.float32).astype(jnp.bfloat16)
        rest_ref[rows, :] = jnp.dot(hb, w_ref[:, QKV_WIDTH:], preferred_element_type=jnp.float32)


def _in_proj(srcs, g, mod, w_bf16, bsz):
    two_src = len(srcs) == 2
    bj = lambda i: (i // IN_STEPS_PER_B, i % IN_STEPS_PER_B)
    rows = IN_SUB * TM
    if two_src:
        lat = lambda s: pl.BlockSpec(
            (None, TM, D_MODEL), lambda i: (bj(i)[0], jnp.maximum(IN_SUB * bj(i)[1] + s - 1, 0), 0))
        src_specs = [pl.BlockSpec((None, TM, D_MODEL), lambda i: (bj(i)[0], 0, 0))] + [lat(s) for s in range(IN_SUB)]
        srcs = (srcs[0],) + (srcs[1],) * IN_SUB
    else:
        src_specs = [pl.BlockSpec((None, rows, D_MODEL), lambda i: (bj(i)[0], bj(i)[1], 0))]
    return pl.pallas_call(
        functools.partial(_in_kernel, two_src=two_src),
        out_shape=[
            jax.ShapeDtypeStruct((bsz, ROWS_PER_B, QKV_WIDTH), jnp.bfloat16),
            jax.ShapeDtypeStruct((bsz, ROWS_PER_B, REST_WIDTH), jnp.float32),
        ],
        grid=(bsz * IN_STEPS_PER_B,),
        in_specs=src_specs + [
            pl.BlockSpec((1, D_MODEL), lambda i: (0, 0)),
            pl.BlockSpec((None, N_MOD, D_MODEL), lambda i: (bsz, 0, 0)),
            pl.BlockSpec((None, N_MOD, D_MODEL), lambda i: (bj(i)[0], 0, 0)),
            pl.BlockSpec((D_MODEL, IN_WIDTH), lambda i: (0, 0)),
        ],
        out_specs=[
            pl.BlockSpec((None, rows, QKV_WIDTH), lambda i: (bj(i)[0], bj(i)[1], 0)),
            pl.BlockSpec((None, rows, REST_WIDTH), lambda i: (bj(i)[0], bj(i)[1], 0)),
        ],
        compiler_params=_cparams(("arbitrary",)),
        name="in_proj",
    )(*srcs, g.reshape(1, D_MODEL), mod, mod, w_bf16)


def _attn_kernel(q_ref, k_ref, v_ref, bias_ref, o_ref, *, need_ctx):
    lane_head = lax.broadcasted_iota(jnp.int32, (1, STACK_W), 1) // HEAD_DIM
    n_stacks = NA_WIDTH // STACK_W

    def stack_q(qg):
        zero = jnp.zeros_like(qg)
        return jnp.concatenate([jnp.where(lane_head == h, qg, zero) for h in range(HEADS_PER_STACK)], axis=0)

    def unstack(o):
        out = jnp.zeros((GRID_W, STACK_W), jnp.float32)
        for h in range(HEADS_PER_STACK):
            out = out + jnp.where(lane_head == h, o[h * GRID_W:(h + 1) * GRID_W], 0.0)
        return out

    def attend(q_rows, s, local):
        cols = slice(s * STACK_W, (s + 1) * STACK_W)
        qg = q_ref[pl.ds(q_rows, GRID_W), cols] * jnp.bfloat16(HEAD_DIM ** -0.5)
        qs = stack_q(qg)
        kc = k_ref[0:CTX_LEN, cols]
        vc = v_ref[0:CTX_LEN, cols]
        s_ctx = lax.dot_general(qs, kc, _NT, preferred_element_type=jnp.float32)
        m = jnp.max(s_ctx, axis=-1, keepdims=True)
        if local is not None:
            k_rows, delta = local
            kw = k_ref[pl.ds(k_rows, WIN_R * GRID_W), cols]
            vw = v_ref[pl.ds(k_rows, WIN_R * GRID_W), cols]
            heads = slice(s * HEADS_PER_STACK, (s + 1) * HEADS_PER_STACK)
            bias = jnp.concatenate(
                [bias_ref[heads, pl.ds(2 * m - delta + WIN_R - 1, 1)].reshape(HEADS_PER_STACK * GRID_W, 2 * GRID_W)
                 for m in range(WIN_R // 2)], axis=-1)
            s_loc = lax.dot_general(qs, kw, _NT, preferred_element_type=jnp.float32) + bias
            m = jnp.maximum(m, jnp.max(s_loc, axis=-1, keepdims=True))
            p_loc = jnp.exp(s_loc - m)
        p_ctx = jnp.exp(s_ctx - m)
        denom = jnp.sum(p_ctx, axis=-1, keepdims=True)
        o = jnp.dot(p_ctx.astype(jnp.bfloat16), vc, preferred_element_type=jnp.float32)
        if local is not None:
            denom = denom + jnp.sum(p_loc, axis=-1, keepdims=True)
            o = o + jnp.dot(p_loc.astype(jnp.bfloat16), vw, preferred_element_type=jnp.float32)
        o_ref[pl.ds(q_rows, GRID_W), cols] = unstack(o / denom).astype(o_ref.dtype)

    def lat_row(r, carry):
        r0 = jnp.clip(r - WIN_R // 2, 0, GRID_ROWS - WIN_R)
        q_rows = pl.multiple_of(CTX_LEN + r * GRID_W, GRID_W)
        k_rows = pl.multiple_of(CTX_LEN + r0 * GRID_W, GRID_W)
        for s in range(n_stacks):
            attend(q_rows, s, (k_rows, r - r0))
        return carry

    lax.fori_loop(0, GRID_ROWS, lat_row, 0, unroll=8)

    if need_ctx:
        def ctx_chunk(cq, carry):
            q_rows = pl.multiple_of(cq * GRID_W, GRID_W)
            for s in range(n_stacks):
                attend(q_rows, s, None)
            return carry

        lax.fori_loop(0, CTX_LEN // GRID_W, ctx_chunk, 0, unroll=2)
    else:
        o_ref[0:CTX_LEN, :] = jnp.zeros((CTX_LEN, NA_WIDTH), o_ref.dtype)


def _attn_bias_table(rpb):
    n_rel_c = 2 * WIN_C - 1
    lead = GRID_W - WIN_C
    padded = jnp.pad(rpb, ((0, 0), (0, 0), (lead, 2 * GRID_W - lead - n_rel_c)))
    skew = jnp.tile(padded, (1, 1, GRID_W))[..., :GRID_W * (2 * GRID_W - 1)]
    skew = skew.reshape(NA_HEADS, 2 * WIN_R - 1, GRID_W, 2 * GRID_W - 1)
    toeplitz = skew[..., GRID_W - 1:]
    q_col = jnp.arange(GRID_W)[:, None]
    k_col = jnp.arange(GRID_W)[None, :]
    c_start = jnp.clip(q_col - WIN_C // 2, 0, GRID_W - WIN_C)
    ok = (k_col >= c_start) & (k_col < c_start + WIN_C)
    toeplitz = jnp.where(ok, toeplitz, NEG_INF)
    return jnp.concatenate([toeplitz[:, :-1], toeplitz[:, 1:]], axis=-1)


def _attention(qkv, bias, bsz, need_ctx):
    return pl.pallas_call(
        functools.partial(_attn_kernel, need_ctx=need_ctx),
        out_shape=jax.ShapeDtypeStruct((bsz, ROWS_PER_B, NA_WIDTH), jnp.bfloat16),
        grid=(bsz,),
        in_specs=[
            pl.BlockSpec((None, ROWS_PER_B, NA_WIDTH), lambda b: (b, 0, 0)),
            pl.BlockSpec((None, ROWS_PER_B, NA_WIDTH), lambda b: (b, 0, 1)),
            pl.BlockSpec((None, ROWS_PER_B, NA_WIDTH), lambda b: (b, 0, 2)),
            pl.BlockSpec((NA_HEADS, 2 * WIN_R - 2, GRID_W, 2 * GRID_W), lambda b: (0, 0, 0, 0)),
        ],
        out_specs=pl.BlockSpec((None, ROWS_PER_B, NA_WIDTH), lambda b: (b, 0, 0)),
        compiler_params=_cparams(("arbitrary",)),
        name="attention",
    )(qkv, qkv, qkv, bias)


CONV_PAD = 8


def _pad_base(start):
    return start + CONV_PAD * (1 if start == 0 else 2)


def _dwconv(pad_s, start, length, w_ref, b_ref, left):
    base = _pad_base(start)
    width = w_ref.shape[0]
    y = pad_s[base - left:base - left + length, :] * w_ref[0:1, :] + b_ref[...]
    for k in range(1, width):
        y = y + pad_s[base + k - left:base + k - left + length, :] * w_ref[k:k + 1, :]
    return y


def _lru_kernel(rest_ref, lcw_ref, lcb_ref, wbd_ref, rgb_ref, lam_ref, scw_ref, scb_ref, o_ref,
                xc_s, a_s, b_s, y_s, pad_s, *, need_ctx):
    col_rx, col_rg, col_sb, col_sc, col_sx = (k * LRU_WIDTH for k in range(5))
    segments = ((0, CTX_LEN), (CTX_LEN, SEQ))

    for start, length in segments:
        base = _pad_base(start)
        pad_s[base - CONV_PAD:base, :] = jnp.zeros((CONV_PAD, LRU_WIDTH), jnp.float32)
    pad_s[pad_s.shape[0] - CONV_PAD:, :] = jnp.zeros((CONV_PAD, LRU_WIDTH), jnp.float32)

    for start, length in segments:
        base = _pad_base(start)
        pad_s[base:base + length, :] = rest_ref[start:start + length, col_rx:col_rx + LRU_WIDTH]
    for start, length in segments:
        xc_s[start:start + length, :] = _dwconv(pad_s, start, length, lcw_ref, lcb_ref, 2)

    def coeffs(d, start, length):
        chunk = 256
        sp = jax.nn.softplus(-lam_ref[d:d + 1, :])
        for c0 in range(0, length, chunk):
            xc = xc_s[start + c0:start + c0 + chunk, :]
            pre = jnp.dot(xc.astype(jnp.bfloat16), wbd_ref[:, 2 * d * LRU_WIDTH:(2 * d + 2) * LRU_WIDTH],
                          preferred_element_type=jnp.float32) + rgb_ref[:, 2 * d * LRU_WIDTH:(2 * d + 2) * LRU_WIDTH]
            gate_r = jax.nn.sigmoid(pre[:, :LRU_WIDTH])
            gate_i = jax.nn.sigmoid(pre[:, LRU_WIDTH:])
            log_a = -RG_C * gate_r * sp
            a = jnp.exp(log_a)
            bb = jnp.sqrt(1.0 - a * a) * (gate_i * xc)
            a_s[start + c0:start + c0 + chunk, :] = a
            b_s[start + c0:start + c0 + chunk, :] = bb

    n_ctx_blocks = CTX_LEN // SCAN_ROWS
    n_blocks = ROWS_PER_B // SCAN_ROWS
    sub = lax.broadcasted_iota(jnp.int32, (SCAN_ROWS, LRU_WIDTH), 0)

    def scan(reverse, accumulate):
        def block(i, h_in):
            if reverse:
                blk = jnp.where(i < n_ctx_blocks, n_ctx_blocks - 1 - i, n_blocks + n_ctx_blocks - 1 - i)
            else:
                blk = i
            rows = pl.ds(pl.multiple_of(blk * SCAN_ROWS, SCAN_ROWS), SCAN_ROWS)
            a = a_s[rows, :]
            b = b_s[rows, :]
            for sh in (1, 2, 4):
                if reverse:
                    keep = sub < SCAN_ROWS - sh
                    a_n = pltpu.roll(a, SCAN_ROWS - sh, axis=0)
                    b_n = pltpu.roll(b, SCAN_ROWS - sh, axis=0)
                else:
                    keep = sub >= sh
                    a_n = pltpu.roll(a, sh, axis=0)
                    b_n = pltpu.roll(b, sh, axis=0)
                b = jnp.where(keep, a * b_n + b, b)
                a = jnp.where(keep, a * a_n, a)
            h = a * h_in + b
            y_s[rows, :] = y_s[rows, :] + h if accumulate else h
            return h[0:1, :] if reverse else h[SCAN_ROWS - 1:SCAN_ROWS, :]

        lax.fori_loop(0, n_blocks, block, jnp.zeros((1, LRU_WIDTH), jnp.float32), unroll=4)

    for d, reverse in enumerate((False, True)):
        for start, length in segments:
            coeffs(d, start, length)
        scan(reverse, accumulate=d > 0)

    out_segments = segments if need_ctx else segments[1:]
    for start, length in out_segments:
        rows = slice(start, start + length)
        y_lru = y_s[rows, :] * jax.nn.gelu(rest_ref[rows, col_rg:col_rg + LRU_WIDTH])
        o_ref[rows, 0:LRU_WIDTH] = y_lru.astype(o_ref.dtype)
        base = _pad_base(start)
        pad_s[base:base + length, :] = (rest_ref[rows, col_sc:col_sc + SC_WIDTH]
                                        * rest_ref[rows, col_sx:col_sx + SC_WIDTH])
        y_sc = rest_ref[rows, col_sb:col_sb + SC_WIDTH] * _dwconv(pad_s, start, length, scw_ref, scb_ref, 1)
        o_ref[rows, LRU_WIDTH:LRU_WIDTH + SC_WIDTH] = y_sc.astype(o_ref.dtype)
    if not need_ctx:
        o_ref[0:CTX_LEN, :] = jnp.zeros((CTX_LEN, LRU_WIDTH + SC_WIDTH), o_ref.dtype)


def _block_diag_gates(rg_w):
    eye = jnp.eye(LRU_HEADS, dtype=rg_w.dtype)
    full = jnp.einsum('dgncm,nk->dgnckm', rg_w, eye)
    full = full.reshape(2, 2, LRU_WIDTH, LRU_WIDTH)
    return full.transpose(2, 0, 1, 3).reshape(LRU_WIDTH, 4 * LRU_WIDTH)


def _lru_sconv(rest, lcw, lcb, rg_w, rg_b, rg_lam, scw, scb, bsz, need_ctx):
    wbd = _block_diag_gates(rg_w).astype(jnp.bfloat16)
    const2 = lambda b: (0, 0)
    return pl.pallas_call(
        functools.partial(_lru_kernel, need_ctx=need_ctx),
        out_shape=jax.ShapeDtypeStruct((bsz, ROWS_PER_B, LRU_WIDTH + SC_WIDTH), jnp.bfloat16),
        grid=(bsz,),
        in_specs=[
            pl.BlockSpec((None, ROWS_PER_B, REST_WIDTH), lambda b: (b, 0, 0)),
            pl.BlockSpec(lcw.shape, const2),
            pl.BlockSpec((1, LRU_WIDTH), const2),
            pl.BlockSpec((LRU_WIDTH, 4 * LRU_WIDTH), const2),
            pl.BlockSpec((1, 4 * LRU_WIDTH), const2),
            pl.BlockSpec((2, LRU_WIDTH), const2),
            pl.BlockSpec(scw.shape, const2),
            pl.BlockSpec((1, SC_WIDTH), const2),
        ],
        out_specs=pl.BlockSpec((None, ROWS_PER_B, LRU_WIDTH + SC_WIDTH), lambda b: (b, 0, 0)),
        scratch_shapes=[
            pltpu.VMEM((ROWS_PER_B, LRU_WIDTH), jnp.float32),
            pltpu.VMEM((ROWS_PER_B, LRU_WIDTH), jnp.float32),
            pltpu.VMEM((ROWS_PER_B, LRU_WIDTH), jnp.float32),
            pltpu.VMEM((ROWS_PER_B, LRU_WIDTH), jnp.float32),
            pltpu.VMEM((ROWS_PER_B + 3 * CONV_PAD, LRU_WIDTH), jnp.float32),
        ],
        compiler_params=_cparams(("arbitrary",)),
        name="lru_sconv",
    )(rest, lcw, lcb.reshape(1, LRU_WIDTH), wbd, rg_b.reshape(1, 4 * LRU_WIDTH), rg_lam, scw,
      scb.reshape(1, SC_WIDTH))


def _route(et):
    pe = [et[e:e + 1, :] for e in range(N_EXPERTS)]

    def top2_sum(v):
        best = v[0] + v[1]
        for a in range(E_PER_GROUP):
            for b in range(a + 1, E_PER_GROUP):
                if (a, b) != (0, 1):
                    best = jnp.maximum(best, v[a] + v[b])
        return best

    score = [top2_sum(pe[g * E_PER_GROUP:(g + 1) * E_PER_GROUP]) for g in range(N_GROUPS)]
    g_best, g_sel = score[0], jnp.zeros((1, TM), jnp.int32)
    for g in range(1, N_GROUPS):
        upd = score[g] > g_best
        g_sel = jnp.where(upd, g, g_sel)
        g_best = jnp.where(upd, score[g], g_best)
    p_in = []
    for k in range(E_PER_GROUP):
        v = pe[k]
        for g in range(1, N_GROUPS):
            v = jnp.where(g_sel == g, pe[g * E_PER_GROUP + k], v)
        p_in.append(v)
    m1, i1 = p_in[0], jnp.zeros((1, TM), jnp.int32)
    for k in range(1, E_PER_GROUP):
        upd = p_in[k] > m1
        i1 = jnp.where(upd, k, i1)
        m1 = jnp.where(upd, p_in[k], m1)
    m2, i2 = jnp.full((1, TM), -1.0, jnp.float32), jnp.zeros((1, TM), jnp.int32)
    for k in range(E_PER_GROUP):
        cand = jnp.where(i1 == k, -2.0, p_in[k])
        upd = cand > m2
        i2 = jnp.where(upd, k, i2)
        m2 = jnp.where(upd, cand, m2)
    lo, hi = jnp.minimum(i1, i2), jnp.maximum(i1, i2)
    pair = jnp.where(lo == 0, hi - 1, jnp.where(lo == 1, hi + 1, N_PAIRS - 1))
    bucket = g_sel * N_PAIRS + pair
    w1 = m1 / (m1 + m2)
    w2 = m2 / (m1 + m2)
    return bucket, jnp.where(i1 < i2, w1, w2), jnp.where(i1 < i2, w2, w1)


ROW_BUFS = 4


def _out_kernel(*refs, two_src, n_tiles):
    if two_src:
        ctx_ref, x_ref, *rest = refs
    else:
        x_ref, *rest = refs
    (yna_ref, yls_ref, mod_ref, og_ref, w_ref, fg_ref, wrt_ref, brt_ref, xo_ref, pos_ref, meta_ref, xs_ref,
     cnt_s, cur_s, alloc_s, tb_s, row_s, zero_s, pos_v, pos_sm, sem_p, sem_r, sem_z) = rest
    step = pl.program_id(0)
    shift_tm = TM.bit_length() - 1

    def row_copy(buf, k, dst_row):
        return pltpu.make_async_copy(row_s.at[buf, pl.ds(k, 1), :], xs_ref.at[pl.ds(dst_row, 1), :], sem_r.at[buf])

    def pos_copy(buf):
        return pltpu.make_async_copy(pos_v.at[buf], pos_sm.at[buf], sem_p.at[buf])

    def copy_out(buf):
        pbuf = buf % 2
        pos_copy(pbuf).wait()
        for k in range(TM):
            row_copy(buf, k, pos_sm[pbuf, 0, k]).start(priority=k % 2)

    def drain(buf):
        def body(k, c):
            row_copy(buf, 0, 0).wait()
            return c
        lax.fori_loop(0, TM, body, 0, unroll=8)

    @pl.when(step == 0)
    def _():
        cnt_s[...] = jnp.zeros_like(cnt_s)
        cur_s[...] = jnp.zeros_like(cur_s)
        alloc_s[...] = jnp.zeros_like(alloc_s)
        tb_s[...] = jnp.zeros_like(tb_s)

    def project(tile):
        if two_src:
            x = jnp.where(tile % TILES_PER_B == 0, ctx_ref[...], x_ref[...])
        else:
            x = x_ref[...]
        yna = yna_ref[...].astype(jnp.float32)
        yls = yls_ref[...].astype(jnp.float32)
        merged = jnp.concatenate([
            _rms(yna, og_ref[:, :NA_WIDTH]),
            _rms(yls[:, :LRU_WIDTH], og_ref[:, NA_WIDTH:NA_WIDTH + LRU_WIDTH]),
            _rms(yls[:, LRU_WIDTH:], og_ref[:, NA_WIDTH + LRU_WIDTH:]),
        ], axis=-1).astype(jnp.bfloat16)
        y = jnp.dot(merged, w_ref[...], preferred_element_type=jnp.float32)
        x_new = x + mod_ref[2:3, :] * y
        xo_ref[...] = x_new
        h2 = _rms(x_new, fg_ref[...]) * (1.0 + mod_ref[4:5, :]) + mod_ref[3:4, :]
        row_s[tile % ROW_BUFS, :, :D_MODEL] = h2

    def route(tile):
        buf = tile % ROW_BUFS
        h2 = row_s[buf, :, :D_MODEL]
        h_hi = h2.astype(jnp.bfloat16)
        h_lo = (h2 - h_hi.astype(jnp.float32)).astype(jnp.bfloat16)
        part = lax.dot_general(wrt_ref[...], h_hi, _NT, preferred_element_type=jnp.float32)
        lt = (part[:N_EXPERTS] + part[N_EXPERTS:] + brt_ref[...]
              + lax.dot_general(wrt_ref[:N_EXPERTS, :], h_lo, _NT, preferred_element_type=jnp.float32))
        bucket, w_lo, w_hi = _route(jnp.exp(lt - jnp.max(lt, axis=0, keepdims=True)))

        b_iota = lax.broadcasted_iota(jnp.int32, (BUCKET_ROWS, TM), 0)
        onehot = b_iota == bucket
        tri = (lax.broadcasted_iota(jnp.int32, (TM, TM), 0) <= lax.broadcasted_iota(jnp.int32, (TM, TM), 1))
        as_bf16 = lambda mask: jnp.where(mask, 1.0, 0.0).astype(jnp.bfloat16)
        cum = jnp.dot(as_bf16(onehot), as_bf16(tri), preferred_element_type=jnp.float32)
        cnt_new = cum[:, TM - 1:TM].astype(jnp.int32)
        cnt_old = cnt_s[:, 0:1]
        open_id = cur_s[:, 0:1]
        alloc = alloc_s[0:1, 0:1]
        shift = TM.bit_length() - 1
        q_last = (cnt_old + cnt_new - 1) >> shift
        q_prev = (cnt_old - 1) >> shift
        opens = jnp.where(cnt_new > 0, q_last - q_prev, 0)
        lower = (lax.broadcasted_iota(jnp.int32, (BUCKET_ROWS, BUCKET_ROWS), 1)
                 < lax.broadcasted_iota(jnp.int32, (BUCKET_ROWS, BUCKET_ROWS), 0))
        opens_b = jnp.broadcast_to(opens, (BUCKET_ROWS, LANES)).astype(jnp.float32).astype(jnp.bfloat16)
        before = jnp.dot(as_bf16(lower), opens_b, preferred_element_type=jnp.float32)[:, 0:1].astype(jnp.int32)
        new_id = alloc + before
        rank = cnt_old + cum.astype(jnp.int32) - 1
        tile_id = jnp.where((opens > 0) & ((rank >> shift) == q_last), new_id, open_id)
        slot = tile_id * TM + (rank & (TM - 1))
        pos = jnp.sum(jnp.where(onehot, slot, 0).astype(jnp.float32), axis=0, keepdims=True).astype(jnp.int32)
        pos_ref[...] = pos

        lane_id = lax.broadcasted_iota(jnp.int32, (BUCKET_ROWS, TM), 1)
        opened_here = (opens > 0) & (new_id == lane_id)
        opened_bucket = jnp.max(jnp.where(opened_here, b_iota, -1).astype(jnp.float32), axis=0,
                                keepdims=True).astype(jnp.int32)
        tb = jnp.where(opened_bucket >= 0, opened_bucket, tb_s[0:1, :])
        alloc_new = alloc + jnp.sum(opens.astype(jnp.float32), axis=0, keepdims=True).astype(jnp.int32)
        tb_s[...] = jnp.broadcast_to(tb, tb_s.shape)
        cnt_s[...] = jnp.broadcast_to(cnt_old + cnt_new, cnt_s.shape)
        cur_s[...] = jnp.broadcast_to(jnp.where(opens > 0, new_id, open_id), cur_s.shape)
        alloc_s[...] = jnp.broadcast_to(alloc_new, alloc_s.shape)
        meta_ref[0:1, :] = tb
        meta_ref[1:2, :] = jnp.broadcast_to(alloc_new, (1, TM))
        meta_ref[2:8, :] = jnp.zeros((6, TM), jnp.int32)

        payload = jnp.concatenate([w_lo, w_hi, jnp.zeros((LANES - 2, TM), jnp.float32)], axis=0)
        row_s[buf, :, D_MODEL:] = payload.T
        pos_v[tile % 2] = jnp.broadcast_to(pos, (8, TM))

    @pl.when((step >= ROW_BUFS) & (step < n_tiles))
    def _():
        drain(step % ROW_BUFS)
        route(step - 1)
        project(step)
        pos_copy((step - 1) % 2).start()

    @pl.when(((step >= 1) & (step < ROW_BUFS)) | (step == n_tiles))
    def _():
        route(step - 1)
        pos_copy((step - 1) % 2).start()

    @pl.when(step < ROW_BUFS)
    def _():
        project(step)

    for buf in range(ROW_BUFS):
        @pl.when((step >= 2) & ((step - 2) % ROW_BUFS == buf))
        def _():
            copy_out(buf)

    @pl.when(step == n_tiles + 1)
    def _():
        for buf in range(ROW_BUFS):
            drain(buf)

        b_iota = lax.broadcasted_iota(jnp.int32, (BUCKET_ROWS, TM), 0)
        lane_id = lax.broadcasted_iota(jnp.int32, (BUCKET_ROWS, TM), 1)
        on_lanes = lambda col: jnp.sum(jnp.where(b_iota == lane_id, col, 0).astype(jnp.float32), axis=0,
                                       keepdims=True).astype(jnp.int32)
        pos_v[0, 0:1, :] = on_lanes(cnt_s[:, 0:1])
        pos_v[0, 1:2, :] = on_lanes(cur_s[:, 0:1])
        pos_copy(0).start()
        zero_s[...] = jnp.zeros_like(zero_s)
        pos_copy(0).wait()
        sub = SCAN_ROWS
        tail_bits = [1 << k for k in range(TM.bit_length() - 2, sub.bit_length() - 2, -1)]

        def tail_copies(act):
            def zero_rows(row, n_rows):
                return pltpu.make_async_copy(zero_s.at[pl.ds(0, n_rows), :], xs_ref.at[pl.ds(row, n_rows), :], sem_z)

            def per_bucket(b, c):
                cnt, tile = pos_sm[0, 0, b], pos_sm[0, 1, b]
                fill = cnt - (((cnt - 1) >> shift_tm) << shift_tm)
                aligned = ((fill + sub - 1) >> (sub.bit_length() - 1)) << (sub.bit_length() - 1)
                for j in range(sub - 1):
                    @pl.when((cnt > 0) & (fill + j < aligned))
                    def _():
                        act(zero_rows(tile * TM + fill + j, 1))
                tail = TM - aligned
                off = aligned
                for bit in tail_bits:
                    @pl.when((cnt > 0) & ((tail & bit) != 0))
                    def _():
                        act(zero_rows(pl.multiple_of(tile * TM + off, sub), bit))
                    off = off + (tail & bit)
                return c
            lax.fori_loop(0, N_BUCKETS, per_bucket, 0)

        tail_copies(lambda cp: cp.start())
        tail_copies(lambda cp: cp.wait())


def _out_proj(srcs, yna, yls, mod, out_g, w_bf16, ffn_g, w_router, b_router, bsz, with_ctx):
    two_src = len(srcs) == 2
    n_tiles, bj, mod_row = _tile_maps(bsz, with_ctx)
    assert n_tiles >= ROW_BUFS
    n_sorted = n_tiles + N_BUCKETS
    proj = lambda i: bj(jnp.minimum(i, n_tiles - 1))
    tile = lambda width: pl.BlockSpec((None, TM, width), lambda i: (proj(i)[0], proj(i)[1], 0))
    const2 = lambda i: (0, 0)
    if two_src:
        src_specs = [
            pl.BlockSpec((None, TM, D_MODEL), lambda i: (proj(i)[0], 0, 0)),
            pl.BlockSpec((None, TM, D_MODEL), lambda i: (proj(i)[0], jnp.maximum(proj(i)[1] - 1, 0), 0)),
        ]
    else:
        src_specs = [tile(D_MODEL)]
    wr_hi = w_router.T.astype(jnp.bfloat16)
    wr_lo = (w_router.T - wr_hi.astype(jnp.float32)).astype(jnp.bfloat16)
    wr_split = jnp.concatenate([wr_hi, wr_lo], axis=0)
    return pl.pallas_call(
        functools.partial(_out_kernel, two_src=two_src, n_tiles=n_tiles),
        out_shape=[
            jax.ShapeDtypeStruct((bsz, ROWS_PER_B, D_MODEL), jnp.float32),
            jax.ShapeDtypeStruct((n_tiles, 1, TM), jnp.int32),
            jax.ShapeDtypeStruct((8, TM), jnp.int32),
            jax.ShapeDtypeStruct((n_sorted * TM, ROW_WIDTH), jnp.float32),
        ],
        grid=(n_tiles + 2,),
        in_specs=src_specs + [
            tile(NA_WIDTH),
            tile(LRU_WIDTH + SC_WIDTH),
            pl.BlockSpec((None, N_MOD, D_MODEL), lambda i: (mod_row(jnp.minimum(i, n_tiles - 1)), 0, 0)),
            pl.BlockSpec((1, D_MODEL), const2),
            pl.BlockSpec((D_MODEL, D_MODEL), const2),
            pl.BlockSpec((1, D_MODEL), const2),
            pl.BlockSpec((2 * N_EXPERTS, D_MODEL), const2),
            pl.BlockSpec((N_EXPERTS, 1), const2),
        ],
        out_specs=[
            tile(D_MODEL),
            pl.BlockSpec((None, 1, TM), lambda i: (jnp.clip(i - 1, 0, n_tiles - 1), 0, 0)),
            pl.BlockSpec((8, TM), const2),
            pl.BlockSpec(memory_space=pl.ANY),
        ],
        scratch_shapes=[
            pltpu.VMEM((BUCKET_ROWS, LANES), jnp.int32),
            pltpu.VMEM((BUCKET_ROWS, LANES), jnp.int32),
            pltpu.VMEM((8, LANES), jnp.int32),
            pltpu.VMEM((8, TM), jnp.int32),
            pltpu.VMEM((ROW_BUFS, TM, ROW_WIDTH), jnp.float32),
            pltpu.VMEM((TM // 2, ROW_WIDTH), jnp.float32),
            pltpu.VMEM((2, 8, TM), jnp.int32),
            pltpu.SMEM((2, 8, TM), jnp.int32),
            pltpu.SemaphoreType.DMA((2,)),
            pltpu.SemaphoreType.DMA((ROW_BUFS,)),
            pltpu.SemaphoreType.DMA(()),
        ],
        compiler_params=_cparams(("arbitrary",)),
        name="out_proj_route",
    )(*srcs, yna, yls, mod, out_g.reshape(1, D_MODEL), w_bf16, ffn_g.reshape(1, D_MODEL), wr_split,
      b_router.reshape(N_EXPERTS, 1))


def _moe_kernel(src_ref, e0_ref, e1_ref, used_ref, xs_ref, wg0, wu0, wd0, wg1, wu1, wd1, o_ref, wg_s, wu_s, wd_s):
    n = pl.program_id(0)
    prev = jnp.maximum(n - 1, 0)
    for k, (e_ref, g, u, d) in enumerate(((e0_ref, wg0, wu0, wd0), (e1_ref, wg1, wu1, wd1))):
        @pl.when((n == 0) | (e_ref[n] != e_ref[prev]))
        def _():
            wg_s[k] = g[...].astype(jnp.bfloat16)
            wu_s[k] = u[...].astype(jnp.bfloat16)
            wd_s[k] = d[...].astype(jnp.bfloat16)

    @pl.when(n < used_ref[0])
    def _():
        xb = xs_ref[:, :D_MODEL].astype(jnp.bfloat16)
        w_lo, w_hi = xs_ref[:, D_MODEL:D_MODEL + 1], xs_ref[:, D_MODEL + 1:D_MODEL + 2]
        slot0_is_lo = e0_ref[n] < e1_ref[n]
        weights = (jnp.where(slot0_is_lo, w_lo, w_hi), jnp.where(slot0_is_lo, w_hi, w_lo))
        out = jnp.zeros((TM, D_MODEL), jnp.float32)
        for k, wk in enumerate(weights):
            gate = jnp.dot(xb, wg_s[k], preferred_element_type=jnp.float32)
            up = jnp.dot(xb, wu_s[k], preferred_element_type=jnp.float32)
            hid = (gate * jax.nn.sigmoid(gate)) * up
            out = out + wk * jnp.dot(hid.astype(jnp.bfloat16), wd_s[k], preferred_element_type=jnp.float32)
        o_ref[...] = out


def _moe(sched, xs, w_gate, w_up, w_down, layer):
    n_tiles = xs.shape[0] // TM
    src, e0, e1, used = sched
    first = lambda n, s, a, b, u: (layer, a[n], 0, 0)
    second = lambda n, s, a, b, u: (layer, b[n], 0, 0)
    gate_spec = lambda m: pl.BlockSpec((None, None, D_MODEL, D_EXPERT), m)
    down_spec = lambda m: pl.BlockSpec((None, None, D_EXPERT, D_MODEL), m)
    return pl.pallas_call(
        _moe_kernel,
        out_shape=jax.ShapeDtypeStruct((n_tiles * TM, D_MODEL), jnp.float32),
        grid_spec=pltpu.PrefetchScalarGridSpec(
            num_scalar_prefetch=4,
            grid=(n_tiles,),
            in_specs=[
                pl.BlockSpec((TM, ROW_WIDTH), lambda n, s, a, b, u: (s[n], 0)),
                gate_spec(first), gate_spec(first), down_spec(first),
                gate_spec(second), gate_spec(second), down_spec(second),
            ],
            out_specs=pl.BlockSpec((TM, D_MODEL), lambda n, s, a, b, u: (s[n], 0)),
            scratch_shapes=[
                pltpu.VMEM((2, D_MODEL, D_EXPERT), jnp.bfloat16),
                pltpu.VMEM((2, D_MODEL, D_EXPERT), jnp.bfloat16),
                pltpu.VMEM((2, D_EXPERT, D_MODEL), jnp.bfloat16),
            ],
        ),
        compiler_params=_cparams(("arbitrary",)),
        name="moe_experts",
    )(src, e0, e1, used, xs, w_gate, w_up, w_down, w_gate, w_up, w_down)


def _schedule(meta, n_tiles):
    visit = jnp.array([0, 2, 3, 1, 4, 5], jnp.int32)
    slot0 = jnp.array([0, 2, 3, 2, 3, 3], jnp.int32)
    slot1 = jnp.array([1, 0, 0, 1, 1, 2], jnp.int32)
    tile_bucket = meta[0, :n_tiles]
    used = meta[1, 0]
    ids = jnp.arange(n_tiles, dtype=jnp.int32)
    key = (tile_bucket // N_PAIRS) * N_PAIRS + visit[tile_bucket % N_PAIRS]
    order = jnp.argsort(jnp.where(ids < used, key, N_BUCKETS), stable=True).astype(jnp.int32)
    src = jnp.where(ids < used, order, order[jnp.maximum(used - 1, 0)])
    bucket = tile_bucket[src]
    group, pair = bucket // N_PAIRS, bucket % N_PAIRS
    return src, group * E_PER_GROUP + slot0[pair], group * E_PER_GROUP + slot1[pair], used.reshape(1)


def _combine_kernel(pos_ref, x_ref, mod_ref, fg_ref, ys_ref, o_ref, buf, sem, *, final):
    i = pl.program_id(0)
    n = pl.num_programs(0)

    def row_copy(src_row, slot, k):
        return pltpu.make_async_copy(ys_ref.at[pl.ds(src_row, 1), :], buf.at[slot, pl.ds(k, 1), :], sem.at[slot])

    def issue(tile, slot):
        for k in range(TM):
            row_copy(pos_ref[tile * TM + k], slot, k).start(priority=k % 2)

    @pl.when(i == 0)
    def _():
        issue(0, 0)

    for nxt in range(2):
        @pl.when((i + 1 < n) & ((i + 1) % 2 == nxt))
        def _():
            issue(i + 1, nxt)

    slot = i % 2

    def drain(k, c):
        row_copy(0, slot, 0).wait()
        return c
    lax.fori_loop(0, TM, drain, 0, unroll=8)

    x_new = x_ref[...] + mod_ref[5:6, :] * buf[slot]
    if final:
        x_new = _rms(x_new, fg_ref[...])
    o_ref[...] = x_new


def _combine(pos, x_all, mod, final_g, ys, bsz, with_ctx, final):
    n_tiles, bj, mod_row = _tile_maps(bsz, with_ctx)
    if final:
        out_shape = jax.ShapeDtypeStruct((bsz, SEQ, D_MODEL), jnp.float32)
        out_spec = pl.BlockSpec((None, TM, D_MODEL), lambda i, p: (bj(i)[0], bj(i)[1] - 1, 0))
    else:
        out_shape = jax.ShapeDtypeStruct((bsz, ROWS_PER_B, D_MODEL), jnp.float32)
        out_spec = pl.BlockSpec((None, TM, D_MODEL), lambda i, p: (bj(i)[0], bj(i)[1], 0))
    return pl.pallas_call(
        functools.partial(_combine_kernel, final=final),
        out_shape=out_shape,
        grid_spec=pltpu.PrefetchScalarGridSpec(
            num_scalar_prefetch=1,
            grid=(n_tiles,),
            in_specs=[
                pl.BlockSpec((None, TM, D_MODEL), lambda i, p: (bj(i)[0], bj(i)[1], 0)),
                pl.BlockSpec((None, N_MOD, D_MODEL), lambda i, p: (mod_row(i), 0, 0)),
                pl.BlockSpec((1, D_MODEL), lambda i, p: (0, 0)),
                pl.BlockSpec(memory_space=pl.ANY),
            ],
            out_specs=out_spec,
            scratch_shapes=[pltpu.VMEM((2, TM, D_MODEL), jnp.float32), pltpu.SemaphoreType.DMA((2,))],
        ),
        compiler_params=_cparams(("arbitrary",)),
        name="combine",
    )(pos.reshape(-1), x_all, mod, final_g.reshape(1, D_MODEL), ys)


def kernel(x, c, ctx, c_ctx, w_ada, b_ada, norm_mix_g, w_in, lru_conv_w, lru_conv_b, rg_w, rg_b, rg_lam, na_rpb,
           sc_conv_w, sc_conv_b, mix_out_g, w_out, norm_ffn_g, w_router, b_router, w_gate, w_up, w_down, final_g):
    bsz = x.shape[0]
    mod_rows = -(-(bsz + 1) // 8) * 8
    c_all = jnp.zeros((mod_rows, D_MODEL), jnp.float32).at[:bsz].set(c).at[bsz].set(c_ctx)
    mods = _ada(c_all, w_ada, b_ada).reshape(DEPTH, mod_rows, N_MOD, D_MODEL)

    x_all = None
    out = None
    for l in range(DEPTH):
        need_ctx = l < DEPTH - 1
        mod = mods[l]
        srcs = (ctx, x) if l == 0 else (x_all,)
        qkv, rest = _in_proj(srcs, norm_mix_g[l], mod, w_in[l].astype(jnp.bfloat16), bsz)
        yna = _attention(qkv, _attn_bias_table(na_rpb[l]), bsz, need_ctx)
        yls = _lru_sconv(rest, lru_conv_w[l], lru_conv_b[l], rg_w[l], rg_b[l], rg_lam[l], sc_conv_w[l],
                         sc_conv_b[l], bsz, need_ctx)
        x_mid, pos, meta, xs = _out_proj(srcs, yna, yls, mod, mix_out_g[l], w_out[l].astype(jnp.bfloat16),
                                         norm_ffn_g[l], w_router, b_router, bsz, need_ctx)
        ys = _moe(_schedule(meta, xs.shape[0] // TM), xs, w_gate, w_up, w_down, l)
        res = _combine(pos, x_mid, mod, final_g, ys, bsz, need_ctx, final=not need_ctx)
        if need_ctx:
            x_all = res
        else:
            out = res
    return out
```

```python
import functools

import jax
import jax.numpy as jnp
from jax import lax
from jax.experimental import pallas as pl
from jax.experimental.pallas import tpu as pltpu

D_MODEL = 1024
SEQ = 2048
CTX_LEN = 256
ROWS_PER_B = CTX_LEN + SEQ
DEPTH = 2
N_MOD = 6
EPS = 1e-6
NEG_INF = -1e30

GRID_W = 64
GRID_ROWS = SEQ // GRID_W
HEAD_DIM = 64
NA_WIDTH = 512
NA_HEADS = 8
LRU_WIDTH = 256
LRU_HEADS = 4
LRU_BLOCK = 64
SC_WIDTH = 256
QKV_WIDTH = 3 * NA_WIDTH
REST_WIDTH = 2 * LRU_WIDTH + 3 * SC_WIDTH
IN_WIDTH = QKV_WIDTH + REST_WIDTH
RG_C = 8.0
WIN_R = 8
WIN_C = 16
N_EXPERTS = 16
N_GROUPS = 4
E_PER_GROUP = 4
N_PAIRS = 6
N_BUCKETS = N_GROUPS * N_PAIRS
D_EXPERT = 512

TM = 256
TILES_PER_B = ROWS_PER_B // TM
LAT_TILES_PER_B = SEQ // TM
LANES = 128
BUCKET_ROWS = 32
ROW_WIDTH = D_MODEL + LANES
HEADS_PER_STACK = 4
STACK_W = HEADS_PER_STACK * HEAD_DIM
SCAN_ROWS = 8
VMEM_LIMIT = 56 * 1024 * 1024

_HI = lax.Precision.HIGHEST
_NT = (((1,), (1,)), ((), ()))


def _cparams(sem):
    return pltpu.CompilerParams(dimension_semantics=sem, vmem_limit_bytes=VMEM_LIMIT)


def _rms(v, g):
    return v * lax.rsqrt(jnp.mean(v * v, axis=-1, keepdims=True) + EPS) * g


def _ada_kernel(c_ref, w_ref, b_ref, o_ref):
    cond = c_ref[...]
    cond = cond * jax.nn.sigmoid(cond)
    w = w_ref[0]
    c_hi = cond.astype(jnp.bfloat16)
    c_lo = (cond - c_hi.astype(jnp.float32)).astype(jnp.bfloat16)
    w_hi = w.astype(jnp.bfloat16)
    w_lo = (w - w_hi.astype(jnp.float32)).astype(jnp.bfloat16)
    rows = cond.shape[0]
    part = jnp.dot(jnp.concatenate([c_hi, c_lo], axis=0), w_hi, preferred_element_type=jnp.float32)
    o_ref[0] = (part[:rows] + part[rows:] + jnp.dot(c_hi, w_lo, preferred_element_type=jnp.float32)) + b_ref[0]


def _ada(c_all, w_ada, b_ada):
    depth, _, width = w_ada.shape
    rows = c_all.shape[0]
    tn = 1536
    return pl.pallas_call(
        _ada_kernel,
        out_shape=jax.ShapeDtypeStruct((depth, rows, width), jnp.float32),
        grid=(depth, width // tn),
        in_specs=[
            pl.BlockSpec((rows, D_MODEL), lambda l, n: (0, 0)),
            pl.BlockSpec((1, D_MODEL, tn), lambda l, n: (l, 0, n)),
            pl.BlockSpec((1, 1, tn), lambda l, n: (l, 0, n)),
        ],
        out_specs=pl.BlockSpec((1, rows, tn), lambda l, n: (l, 0, n)),
        compiler_params=_cparams(("arbitrary", "arbitrary")),
        name="ada",
    )(c_all, w_ada, b_ada.reshape(depth, 1, width))


def _tile_maps(bsz, with_ctx):
    per_b = TILES_PER_B if with_ctx else LAT_TILES_PER_B
    off = 0 if with_ctx else 1

    def bj(i):
        return i // per_b, i % per_b + off

    def mod_row(i):
        b, j = bj(i)
        return jnp.where(j == 0, bsz, b)

    return per_b * bsz, bj, mod_row


IN_SUB = 3
IN_STEPS_PER_B = TILES_PER_B // IN_SUB


def _in_kernel(*refs, two_src):
    if two_src:
        ctx_ref, *x_refs = refs[:1 + IN_SUB]
        g_ref, modc_ref, modb_ref, w_ref, qkv_ref, rest_ref = refs[1 + IN_SUB:]
    else:
        x_ref, g_ref, modc_ref, modb_ref, w_ref, qkv_ref, rest_ref = refs
    first = pl.program_id(0) % IN_STEPS_PER_B == 0
    for s in range(IN_SUB):
        rows = slice(s * TM, (s + 1) * TM)
        if two_src:
            x = jnp.where(first, ctx_ref[...], x_refs[0][...]) if s == 0 else x_refs[s][...]
        else:
            x = x_ref[rows, :]
        if s == 0:
            shift = jnp.where(first, modc_ref[0:1, :], modb_ref[0:1, :])
            scale = jnp.where(first, modc_ref[1:2, :], modb_ref[1:2, :])
        else:
            shift, scale = modb_ref[0:1, :], modb_ref[1:2, :]
        hb = (_rms(x, g_ref[...]) * (1.0 + scale) + shift).astype(jnp.bfloat16)
        qkv_ref[rows, :] = jnp.dot(hb, w_ref[:, :QKV_WIDTH], preferred_element_type=jnp.float32).astype(jnp.bfloat16)
        rest_ref[rows, :] = jnp.dot(hb, w_ref[:, QKV_WIDTH:], preferred_element_type=jnp.float32)


def _in_proj(srcs, g, mod, w_bf16, bsz):
    two_src = len(srcs) == 2
    bj = lambda i: (i // IN_STEPS_PER_B, i % IN_STEPS_PER_B)
    rows = IN_SUB * TM
    if two_src:
        lat = lambda s: pl.BlockSpec(
            (None, TM, D_MODEL), lambda i: (bj(i)[0], jnp.maximum(IN_SUB * bj(i)[1] + s - 1, 0), 0))
        src_specs = [pl.BlockSpec((None, TM, D_MODEL), lambda i: (bj(i)[0], 0, 0))] + [lat(s) for s in range(IN_SUB)]
        srcs = (srcs[0],) + (srcs[1],) * IN_SUB
    else:
        src_specs = [pl.BlockSpec((None, rows, D_MODEL), lambda i: (bj(i)[0], bj(i)[1], 0))]
    return pl.pallas_call(
        functools.partial(_in_kernel, two_src=two_src),
        out_shape=[
            jax.ShapeDtypeStruct((bsz, ROWS_PER_B, QKV_WIDTH), jnp.bfloat16),
            jax.ShapeDtypeStruct((bsz, ROWS_PER_B, REST_WIDTH), jnp.float32),
        ],
        grid=(bsz * IN_STEPS_PER_B,),
        in_specs=src_specs + [
            pl.BlockSpec((1, D_MODEL), lambda i: (0, 0)),
            pl.BlockSpec((None, N_MOD, D_MODEL), lambda i: (bsz, 0, 0)),
            pl.BlockSpec((None, N_MOD, D_MODEL), lambda i: (bj(i)[0], 0, 0)),
            pl.BlockSpec((D_MODEL, IN_WIDTH), lambda i: (0, 0)),
        ],
        out_specs=[
            pl.BlockSpec((None, rows, QKV_WIDTH), lambda i: (bj(i)[0], bj(i)[1], 0)),
            pl.BlockSpec((None, rows, REST_WIDTH), lambda i: (bj(i)[0], bj(i)[1], 0)),
        ],
        compiler_params=_cparams(("arbitrary",)),
        name="in_proj",
    )(*srcs, g.reshape(1, D_MODEL), mod, mod, w_bf16)


def _attn_kernel(q_ref, k_ref, v_ref, bias_ref, o_ref, *, need_ctx):
    lane_head = lax.broadcasted_iota(jnp.int32, (1, STACK_W), 1) // HEAD_DIM
    n_stacks = NA_WIDTH // STACK_W

    def stack_q(qg):
        zero = jnp.zeros_like(qg)
        return jnp.concatenate([jnp.where(lane_head == h, qg, zero) for h in range(HEADS_PER_STACK)], axis=0)

    def unstack(o):
        out = jnp.zeros((GRID_W, STACK_W), jnp.float32)
        for h in range(HEADS_PER_STACK):
            out = out + jnp.where(lane_head == h, o[h * GRID_W:(h + 1) * GRID_W], 0.0)
        return out

    def attend(q_rows, s, local):
        cols = slice(s * STACK_W, (s + 1) * STACK_W)
        qg = q_ref[pl.ds(q_rows, GRID_W), cols] * jnp.bfloat16(HEAD_DIM ** -0.5)
        qs = stack_q(qg)
        kc = k_ref[0:CTX_LEN, cols]
        vc = v_ref[0:CTX_LEN, cols]
        s_ctx = lax.dot_general(qs, kc, _NT, preferred_element_type=jnp.float32)
        m = jnp.max(s_ctx, axis=-1, keepdims=True)
        if local is not None:
            k_rows, delta = local
            kw = k_ref[pl.ds(k_rows, WIN_R * GRID_W), cols]
            vw = v_ref[pl.ds(k_rows, WIN_R * GRID_W), cols]
            heads = slice(s * HEADS_PER_STACK, (s + 1) * HEADS_PER_STACK)
            bias = jnp.concatenate(
                [bias_ref[heads, pl.ds(2 * m - delta + WIN_R - 1, 1)].reshape(HEADS_PER_STACK * GRID_W, 2 * GRID_W)
                 for m in range(WIN_R // 2)], axis=-1)
            s_loc = lax.dot_general(qs, kw, _NT, preferred_element_type=jnp.float32) + bias
            m = jnp.maximum(m, jnp.max(s_loc, axis=-1, keepdims=True))
            p_loc = jnp.exp(s_loc - m)
        p_ctx = jnp.exp(s_ctx - m)
        denom = jnp.sum(p_ctx, axis=-1, keepdims=True)
        o = jnp.dot(p_ctx.astype(jnp.bfloat16), vc, preferred_element_type=jnp.float32)
        if local is not None:
            denom = denom + jnp.sum(p_loc, axis=-1, keepdims=True)
            o = o + jnp.dot(p_loc.astype(jnp.bfloat16), vw, preferred_element_type=jnp.float32)
        o_ref[pl.ds(q_rows, GRID_W), cols] = unstack(o / denom).astype(o_ref.dtype)

    def lat_row(r, carry):
        r0 = jnp.clip(r - WIN_R // 2, 0, GRID_ROWS - WIN_R)
        q_rows = pl.multiple_of(CTX_LEN + r * GRID_W, GRID_W)
        k_rows = pl.multiple_of(CTX_LEN + r0 * GRID_W, GRID_W)
        for s in range(n_stacks):
            attend(q_rows, s, (k_rows, r - r0))
        return carry

    lax.fori_loop(0, GRID_ROWS, lat_row, 0, unroll=8)

    if need_ctx:
        def ctx_chunk(cq, carry):
            q_rows = pl.multiple_of(cq * GRID_W, GRID_W)
            for s in range(n_stacks):
                attend(q_rows, s, None)
            return carry

        lax.fori_loop(0, CTX_LEN // GRID_W, ctx_chunk, 0, unroll=2)
    else:
        o_ref[0:CTX_LEN, :] = jnp.zeros((CTX_LEN, NA_WIDTH), o_ref.dtype)


def _attn_bias_table(rpb):
    n_rel_c = 2 * WIN_C - 1
    lead = GRID_W - WIN_C
    padded = jnp.pad(rpb, ((0, 0), (0, 0), (lead, 2 * GRID_W - lead - n_rel_c)))
    skew = jnp.tile(padded, (1, 1, GRID_W))[..., :GRID_W * (2 * GRID_W - 1)]
    skew = skew.reshape(NA_HEADS, 2 * WIN_R - 1, GRID_W, 2 * GRID_W - 1)
    toeplitz = skew[..., GRID_W - 1:]
    q_col = jnp.arange(GRID_W)[:, None]
    k_col = jnp.arange(GRID_W)[None, :]
    c_start = jnp.clip(q_col - WIN_C // 2, 0, GRID_W - WIN_C)
    ok = (k_col >= c_start) & (k_col < c_start + WIN_C)
    toeplitz = jnp.where(ok, toeplitz, NEG_INF)
    return jnp.concatenate([toeplitz[:, :-1], toeplitz[:, 1:]], axis=-1)


def _attention(qkv, bias, bsz, need_ctx):
    return pl.pallas_call(
        functools.partial(_attn_kernel, need_ctx=need_ctx),
        out_shape=jax.ShapeDtypeStruct((bsz, ROWS_PER_B, NA_WIDTH), jnp.bfloat16),
        grid=(bsz,),
        in_specs=[
            pl.BlockSpec((None, ROWS_PER_B, NA_WIDTH), lambda b: (b, 0, 0)),
            pl.BlockSpec((None, ROWS_PER_B, NA_WIDTH), lambda b: (b, 0, 1)),
            pl.BlockSpec((None, ROWS_PER_B, NA_WIDTH), lambda b: (b, 0, 2)),
            pl.BlockSpec((NA_HEADS, 2 * WIN_R - 2, GRID_W, 2 * GRID_W), lambda b: (0, 0, 0, 0)),
        ],
        out_specs=pl.BlockSpec((None, ROWS_PER_B, NA_WIDTH), lambda b: (b, 0, 0)),
        compiler_params=_cparams(("arbitrary",)),
        name="attention",
    )(qkv, qkv, qkv, bias)


CONV_PAD = 8


def _pad_base(start):
    return start + CONV_PAD * (1 if start == 0 else 2)


def _dwconv(pad_s, start, length, w_ref, b_ref, left):
    base = _pad_base(start)
    width = w_ref.shape[0]
    y = pad_s[base - left:base - left + length, :] * w_ref[0:1, :] + b_ref[...]
    for k in range(1, width):
        y = y + pad_s[base + k - left:base + k - left + length, :] * w_ref[k:k + 1, :]
    return y


def _lru_kernel(rest_ref, lcw_ref, lcb_ref, wbd_ref, rgb_ref, lam_ref, scw_ref, scb_ref, o_ref,
                xc_s, a_s, b_s, y_s, pad_s, *, need_ctx):
    col_rx, col_rg, col_sb, col_sc, col_sx = (k * LRU_WIDTH for k in range(5))
    segments = ((0, CTX_LEN), (CTX_LEN, SEQ))

    for start, length in segments:
        base = _pad_base(start)
        pad_s[base - CONV_PAD:base, :] = jnp.zeros((CONV_PAD, LRU_WIDTH), jnp.float32)
    pad_s[pad_s.shape[0] - CONV_PAD:, :] = jnp.zeros((CONV_PAD, LRU_WIDTH), jnp.float32)

    for start, length in segments:
        base = _pad_base(start)
        pad_s[base:base + length, :] = rest_ref[start:start + length, col_rx:col_rx + LRU_WIDTH]
    for start, length in segments:
        xc_s[start:start + length, :] = _dwconv(pad_s, start, length, lcw_ref, lcb_ref, 2)

    def coeffs(d, start, length):
        chunk = 256
        sp = jax.nn.softplus(-lam_ref[d:d + 1, :])
        for c0 in range(0, length, chunk):
            xc = xc_s[start + c0:start + c0 + chunk, :]
            pre = jnp.dot(xc.astype(jnp.bfloat16), wbd_ref[:, 2 * d * LRU_WIDTH:(2 * d + 2) * LRU_WIDTH],
                          preferred_element_type=jnp.float32) + rgb_ref[:, 2 * d * LRU_WIDTH:(2 * d + 2) * LRU_WIDTH]
            gate_r = jax.nn.sigmoid(pre[:, :LRU_WIDTH])
            gate_i = jax.nn.sigmoid(pre[:, LRU_WIDTH:])
            log_a = -RG_C * gate_r * sp
            a = jnp.exp(log_a)
            bb = jnp.sqrt(1.0 - a * a) * (gate_i * xc)
            a_s[start + c0:start + c0 + chunk, :] = a
            b_s[start + c0:start + c0 + chunk, :] = bb

    n_ctx_blocks = CTX_LEN // SCAN_ROWS
    n_blocks = ROWS_PER_B // SCAN_ROWS
    sub = lax.broadcasted_iota(jnp.int32, (SCAN_ROWS, LRU_WIDTH), 0)

    def scan(reverse, accumulate):
        def block(i, h_in):
            if reverse:
                blk = jnp.where(i < n_ctx_blocks, n_ctx_blocks - 1 - i, n_blocks + n_ctx_blocks - 1 - i)
            else:
                blk = i
            rows = pl.ds(pl.multiple_of(blk * SCAN_ROWS, SCAN_ROWS), SCAN_ROWS)
            a = a_s[rows, :]
            b = b_s[rows, :]
            for sh in (1, 2, 4):
                if reverse:
                    keep = sub < SCAN_ROWS - sh
                    a_n = pltpu.roll(a, SCAN_ROWS - sh, axis=0)
                    b_n = pltpu.roll(b, SCAN_ROWS - sh, axis=0)
                else:
                    keep = sub >= sh
                    a_n = pltpu.roll(a, sh, axis=0)
                    b_n = pltpu.roll(b, sh, axis=0)
                b = jnp.where(keep, a * b_n + b, b)
                a = jnp.where(keep, a * a_n, a)
            h = a * h_in + b
            y_s[rows, :] = y_s[rows, :] + h if accumulate else h
            return h[0:1, :] if reverse else h[SCAN_ROWS - 1:SCAN_ROWS, :]

        lax.fori_loop(0, n_blocks, block, jnp.zeros((1, LRU_WIDTH), jnp.float32), unroll=8)

    for d, reverse in enumerate((False, True)):
        for start, length in segments:
            coeffs(d, start, length)
        scan(reverse, accumulate=d > 0)

    out_segments = segments if need_ctx else segments[1:]
    for start, length in out_segments:
        rows = slice(start, start + length)
        y_lru = y_s[rows, :] * jax.nn.gelu(rest_ref[rows, col_rg:col_rg + LRU_WIDTH])
        o_ref[rows, 0:LRU_WIDTH] = y_lru.astype(o_ref.dtype)
        base = _pad_base(start)
        pad_s[base:base + length, :] = (rest_ref[rows, col_sc:col_sc + SC_WIDTH]
                                        * rest_ref[rows, col_sx:col_sx + SC_WIDTH])
        y_sc = rest_ref[rows, col_sb:col_sb + SC_WIDTH] * _dwconv(pad_s, start, length, scw_ref, scb_ref, 1)
        o_ref[rows, LRU_WIDTH:LRU_WIDTH + SC_WIDTH] = y_sc.astype(o_ref.dtype)
    if not need_ctx:
        o_ref[0:CTX_LEN, :] = jnp.zeros((CTX_LEN, LRU_WIDTH + SC_WIDTH), o_ref.dtype)


def _block_diag_gates(rg_w):
    eye = jnp.eye(LRU_HEADS, dtype=rg_w.dtype)
    full = jnp.einsum('dgncm,nk->dgnckm', rg_w, eye)
    full = full.reshape(2, 2, LRU_WIDTH, LRU_WIDTH)
    return full.transpose(2, 0, 1, 3).reshape(LRU_WIDTH, 4 * LRU_WIDTH)


def _lru_sconv(rest, lcw, lcb, rg_w, rg_b, rg_lam, scw, scb, bsz, need_ctx):
    wbd = _block_diag_gates(rg_w).astype(jnp.bfloat16)
    const2 = lambda b: (0, 0)
    return pl.pallas_call(
        functools.partial(_lru_kernel, need_ctx=need_ctx),
        out_shape=jax.ShapeDtypeStruct((bsz, ROWS_PER_B, LRU_WIDTH + SC_WIDTH), jnp.bfloat16),
        grid=(bsz,),
        in_specs=[
            pl.BlockSpec((None, ROWS_PER_B, REST_WIDTH), lambda b: (b, 0, 0)),
            pl.BlockSpec(lcw.shape, const2),
            pl.BlockSpec((1, LRU_WIDTH), const2),
            pl.BlockSpec((LRU_WIDTH, 4 * LRU_WIDTH), const2),
            pl.BlockSpec((1, 4 * LRU_WIDTH), const2),
            pl.BlockSpec((2, LRU_WIDTH), const2),
            pl.BlockSpec(scw.shape, const2),
            pl.BlockSpec((1, SC_WIDTH), const2),
        ],
        out_specs=pl.BlockSpec((None, ROWS_PER_B, LRU_WIDTH + SC_WIDTH), lambda b: (b, 0, 0)),
        scratch_shapes=[
            pltpu.VMEM((ROWS_PER_B, LRU_WIDTH), jnp.float32),
            pltpu.VMEM((ROWS_PER_B, LRU_WIDTH), jnp.float32),
            pltpu.VMEM((ROWS_PER_B, LRU_WIDTH), jnp.float32),
            pltpu.VMEM((ROWS_PER_B, LRU_WIDTH), jnp.float32),
            pltpu.VMEM((ROWS_PER_B + 3 * CONV_PAD, LRU_WIDTH), jnp.float32),
        ],
        compiler_params=_cparams(("arbitrary",)),
        name="lru_sconv",
    )(rest, lcw, lcb.reshape(1, LRU_WIDTH), wbd, rg_b.reshape(1, 4 * LRU_WIDTH), rg_lam, scw,
      scb.reshape(1, SC_WIDTH))


def _route(et):
    pe = [et[e:e + 1, :] for e in range(N_EXPERTS)]

    def top2_sum(v):
        best = v[0] + v[1]
        for a in range(E_PER_GROUP):
            for b in range(a + 1, E_PER_GROUP):
                if (a, b) != (0, 1):
                    best = jnp.maximum(best, v[a] + v[b])
        return best

    score = [top2_sum(pe[g * E_PER_GROUP:(g + 1) * E_PER_GROUP]) for g in range(N_GROUPS)]
    g_best, g_sel = score[0], jnp.zeros((1, TM), jnp.int32)
    for g in range(1, N_GROUPS):
        upd = score[g] > g_best
        g_sel = jnp.where(upd, g, g_sel)
        g_best = jnp.where(upd, score[g], g_best)
    p_in = []
    for k in range(E_PER_GROUP):
        v = pe[k]
        for g in range(1, N_GROUPS):
            v = jnp.where(g_sel == g, pe[g * E_PER_GROUP + k], v)
        p_in.append(v)
    m1, i1 = p_in[0], jnp.zeros((1, TM), jnp.int32)
    for k in range(1, E_PER_GROUP):
        upd = p_in[k] > m1
        i1 = jnp.where(upd, k, i1)
        m1 = jnp.where(upd, p_in[k], m1)
    m2, i2 = jnp.full((1, TM), -1.0, jnp.float32), jnp.zeros((1, TM), jnp.int32)
    for k in range(E_PER_GROUP):
        cand = jnp.where(i1 == k, -2.0, p_in[k])
        upd = cand > m2
        i2 = jnp.where(upd, k, i2)
        m2 = jnp.where(upd, cand, m2)
    lo, hi = jnp.minimum(i1, i2), jnp.maximum(i1, i2)
    pair = jnp.where(lo == 0, hi - 1, jnp.where(lo == 1, hi + 1, N_PAIRS - 1))
    bucket = g_sel * N_PAIRS + pair
    w1 = m1 / (m1 + m2)
    w2 = m2 / (m1 + m2)
    return bucket, jnp.where(i1 < i2, w1, w2), jnp.where(i1 < i2, w2, w1)


ROW_BUFS = 4


def _out_kernel(*refs, two_src, n_tiles):
    if two_src:
        ctx_ref, x_ref, *rest = refs
    else:
        x_ref, *rest = refs
    (yna_ref, yls_ref, mod_ref, og_ref, w_ref, fg_ref, wrt_ref, brt_ref, xo_ref, pos_ref, meta_ref, xs_ref,
     cnt_s, cur_s, alloc_s, tb_s, row_s, zero_s, pos_v, pos_sm, sem_p, sem_r, sem_z) = rest
    step = pl.program_id(0)
    shift_tm = TM.bit_length() - 1

    def row_copy(buf, k, dst_row):
        return pltpu.make_async_copy(row_s.at[buf, pl.ds(k, 1), :], xs_ref.at[pl.ds(dst_row, 1), :], sem_r.at[buf])

    def pos_copy(buf):
        return pltpu.make_async_copy(pos_v.at[buf], pos_sm.at[buf], sem_p.at[buf])

    def copy_out(buf):
        pbuf = buf % 2
        pos_copy(pbuf).wait()
        for k in range(TM):
            row_copy(buf, k, pos_sm[pbuf, 0, k]).start(priority=k % 2)

    def drain(buf):
        def body(k, c):
            row_copy(buf, 0, 0).wait()
            return c
        lax.fori_loop(0, TM, body, 0, unroll=8)

    @pl.when(step == 0)
    def _():
        cnt_s[...] = jnp.zeros_like(cnt_s)
        cur_s[...] = jnp.zeros_like(cur_s)
        alloc_s[...] = jnp.zeros_like(alloc_s)
        tb_s[...] = jnp.zeros_like(tb_s)

    def project(tile):
        if two_src:
            x = jnp.where(tile % TILES_PER_B == 0, ctx_ref[...], x_ref[...])
        else:
            x = x_ref[...]
        yna = yna_ref[...].astype(jnp.float32)
        yls = yls_ref[...].astype(jnp.float32)
        merged = jnp.concatenate([
            _rms(yna, og_ref[:, :NA_WIDTH]),
            _rms(yls[:, :LRU_WIDTH], og_ref[:, NA_WIDTH:NA_WIDTH + LRU_WIDTH]),
            _rms(yls[:, LRU_WIDTH:], og_ref[:, NA_WIDTH + LRU_WIDTH:]),
        ], axis=-1).astype(jnp.bfloat16)
        y = jnp.dot(merged, w_ref[...], preferred_element_type=jnp.float32)
        x_new = x + mod_ref[2:3, :] * y
        xo_ref[...] = x_new
        h2 = _rms(x_new, fg_ref[...]) * (1.0 + mod_ref[4:5, :]) + mod_ref[3:4, :]
        row_s[tile % ROW_BUFS, :, :D_MODEL] = h2

    def route(tile):
        buf = tile % ROW_BUFS
        h2 = row_s[buf, :, :D_MODEL]
        h_hi = h2.astype(jnp.bfloat16)
        h_lo = (h2 - h_hi.astype(jnp.float32)).astype(jnp.bfloat16)
        part = lax.dot_general(wrt_ref[...], h_hi, _NT, preferred_element_type=jnp.float32)
        lt = (part[:N_EXPERTS] + part[N_EXPERTS:] + brt_ref[...]
              + lax.dot_general(wrt_ref[:N_EXPERTS, :], h_lo, _NT, preferred_element_type=jnp.float32))
        bucket, w_lo, w_hi = _route(jnp.exp(lt - jnp.max(lt, axis=0, keepdims=True)))

        b_iota = lax.broadcasted_iota(jnp.int32, (BUCKET_ROWS, TM), 0)
        onehot = b_iota == bucket
        tri = (lax.broadcasted_iota(jnp.int32, (TM, TM), 0) <= lax.broadcasted_iota(jnp.int32, (TM, TM), 1))
        as_bf16 = lambda mask: jnp.where(mask, 1.0, 0.0).astype(jnp.bfloat16)
        cum = jnp.dot(as_bf16(onehot), as_bf16(tri), preferred_element_type=jnp.float32)
        cnt_new = cum[:, TM - 1:TM].astype(jnp.int32)
        cnt_old = cnt_s[:, 0:1]
        open_id = cur_s[:, 0:1]
        alloc = alloc_s[0:1, 0:1]
        shift = TM.bit_length() - 1
        q_last = (cnt_old + cnt_new - 1) >> shift
        q_prev = (cnt_old - 1) >> shift
        opens = jnp.where(cnt_new > 0, q_last - q_prev, 0)
        lower = (lax.broadcasted_iota(jnp.int32, (BUCKET_ROWS, BUCKET_ROWS), 1)
                 < lax.broadcasted_iota(jnp.int32, (BUCKET_ROWS, BUCKET_ROWS), 0))
        opens_b = jnp.broadcast_to(opens, (BUCKET_ROWS, LANES)).astype(jnp.float32).astype(jnp.bfloat16)
        before = jnp.dot(as_bf16(lower), opens_b, preferred_element_type=jnp.float32)[:, 0:1].astype(jnp.int32)
        new_id = alloc + before
        rank = cnt_old + cum.astype(jnp.int32) - 1
        tile_id = jnp.where((opens > 0) & ((rank >> shift) == q_last), new_id, open_id)
        slot = tile_id * TM + (rank & (TM - 1))
        pos = jnp.sum(jnp.where(onehot, slot, 0).astype(jnp.float32), axis=0, keepdims=True).astype(jnp.int32)
        pos_ref[...] = pos

        lane_id = lax.broadcasted_iota(jnp.int32, (BUCKET_ROWS, TM), 1)
        opened_here = (opens > 0) & (new_id == lane_id)
        opened_bucket = jnp.max(jnp.where(opened_here, b_iota, -1).astype(jnp.float32), axis=0,
                                keepdims=True).astype(jnp.int32)
        tb = jnp.where(opened_bucket >= 0, opened_bucket, tb_s[0:1, :])
        alloc_new = alloc + jnp.sum(opens.astype(jnp.float32), axis=0, keepdims=True).astype(jnp.int32)
        tb_s[...] = jnp.broadcast_to(tb, tb_s.shape)
        cnt_s[...] = jnp.broadcast_to(cnt_old + cnt_new, cnt_s.shape)
        cur_s[...] = jnp.broadcast_to(jnp.where(opens > 0, new_id, open_id), cur_s.shape)
        alloc_s[...] = jnp.broadcast_to(alloc_new, alloc_s.shape)
        meta_ref[0:1, :] = tb
        meta_ref[1:2, :] = jnp.broadcast_to(alloc_new, (1, TM))
        meta_ref[2:8, :] = jnp.zeros((6, TM), jnp.int32)

        payload = jnp.concatenate([w_lo, w_hi, jnp.zeros((LANES - 2, TM), jnp.float32)], axis=0)
        row_s[buf, :, D_MODEL:] = payload.T
        pos_v[tile % 2] = jnp.broadcast_to(pos, (8, TM))

    @pl.when((step >= ROW_BUFS) & (step < n_tiles))
    def _():
        drain(step % ROW_BUFS)
        route(step - 1)
        project(step)
        pos_copy((step - 1) % 2).start()

    @pl.when(((step >= 1) & (step < ROW_BUFS)) | (step == n_tiles))
    def _():
        route(step - 1)
        pos_copy((step - 1) % 2).start()

    @pl.when(step < ROW_BUFS)
    def _():
        project(step)

    for buf in range(ROW_BUFS):
        @pl.when((step >= 2) & ((step - 2) % ROW_BUFS == buf))
        def _():
            copy_out(buf)

    @pl.when(step == n_tiles + 1)
    def _():
        for buf in range(ROW_BUFS):
            drain(buf)

        b_iota = lax.broadcasted_iota(jnp.int32, (BUCKET_ROWS, TM), 0)
        lane_id = lax.broadcasted_iota(jnp.int32, (BUCKET_ROWS, TM), 1)
        on_lanes = lambda col: jnp.sum(jnp.where(b_iota == lane_id, col, 0).astype(jnp.float32), axis=0,
                                       keepdims=True).astype(jnp.int32)
        pos_v[0, 0:1, :] = on_lanes(cnt_s[:, 0:1])
        pos_v[0, 1:2, :] = on_lanes(cur_s[:, 0:1])
        pos_copy(0).start()
        zero_s[...] = jnp.zeros_like(zero_s)
        pos_copy(0).wait()
        sub = SCAN_ROWS
        tail_bits = [1 << k for k in range(TM.bit_length() - 2, sub.bit_length() - 2, -1)]

        def tail_copies(act):
            def zero_rows(row, n_rows):
                return pltpu.make_async_copy(zero_s.at[pl.ds(0, n_rows), :], xs_ref.at[pl.ds(row, n_rows), :], sem_z)

            def per_bucket(b, c):
                cnt, tile = pos_sm[0, 0, b], pos_sm[0, 1, b]
                fill = cnt - (((cnt - 1) >> shift_tm) << shift_tm)
                aligned = ((fill + sub - 1) >> (sub.bit_length() - 1)) << (sub.bit_length() - 1)
                for j in range(sub - 1):
                    @pl.when((cnt > 0) & (fill + j < aligned))
                    def _():
                        act(zero_rows(tile * TM + fill + j, 1))
                tail = TM - aligned
                off = aligned
                for bit in tail_bits:
                    @pl.when((cnt > 0) & ((tail & bit) != 0))
                    def _():
                        act(zero_rows(pl.multiple_of(tile * TM + off, sub), bit))
                    off = off + (tail & bit)
                return c
            lax.fori_loop(0, N_BUCKETS, per_bucket, 0)

        tail_copies(lambda cp: cp.start())
        tail_copies(lambda cp: cp.wait())


def _out_proj(srcs, yna, yls, mod, out_g, w_bf16, ffn_g, w_router, b_router, bsz, with_ctx):
    two_src = len(srcs) == 2
    n_tiles, bj, mod_row = _tile_maps(bsz, with_ctx)
    assert n_tiles >= ROW_BUFS
    n_sorted = n_tiles + N_BUCKETS
    proj = lambda i: bj(jnp.minimum(i, n_tiles - 1))
    tile = lambda width: pl.BlockSpec((None, TM, width), lambda i: (proj(i)[0], proj(i)[1], 0))
    const2 = lambda i: (0, 0)
    if two_src:
        src_specs = [
            pl.BlockSpec((None, TM, D_MODEL), lambda i: (proj(i)[0], 0, 0)),
            pl.BlockSpec((None, TM, D_MODEL), lambda i: (proj(i)[0], jnp.maximum(proj(i)[1] - 1, 0), 0)),
        ]
    else:
        src_specs = [tile(D_MODEL)]
    wr_hi = w_router.T.astype(jnp.bfloat16)
    wr_lo = (w_router.T - wr_hi.astype(jnp.float32)).astype(jnp.bfloat16)
    wr_split = jnp.concatenate([wr_hi, wr_lo], axis=0)
    return pl.pallas_call(
        functools.partial(_out_kernel, two_src=two_src, n_tiles=n_tiles),
        out_shape=[
            jax.ShapeDtypeStruct((bsz, ROWS_PER_B, D_MODEL), jnp.float32),
            jax.ShapeDtypeStruct((n_tiles, 1, TM), jnp.int32),
            jax.ShapeDtypeStruct((8, TM), jnp.int32),
            jax.ShapeDtypeStruct((n_sorted * TM, ROW_WIDTH), jnp.float32),
        ],
        grid=(n_tiles + 2,),
        in_specs=src_specs + [
            tile(NA_WIDTH),
            tile(LRU_WIDTH + SC_WIDTH),
            pl.BlockSpec((None, N_MOD, D_MODEL), lambda i: (mod_row(jnp.minimum(i, n_tiles - 1)), 0, 0)),
            pl.BlockSpec((1, D_MODEL), const2),
            pl.BlockSpec((D_MODEL, D_MODEL), const2),
            pl.BlockSpec((1, D_MODEL), const2),
            pl.BlockSpec((2 * N_EXPERTS, D_MODEL), const2),
            pl.BlockSpec((N_EXPERTS, 1), const2),
        ],
        out_specs=[
            tile(D_MODEL),
            pl.BlockSpec((None, 1, TM), lambda i: (jnp.clip(i - 1, 0, n_tiles - 1), 0, 0)),
            pl.BlockSpec((8, TM), const2),
            pl.BlockSpec(memory_space=pl.ANY),
        ],
        scratch_shapes=[
            pltpu.VMEM((BUCKET_ROWS, LANES), jnp.int32),
            pltpu.VMEM((BUCKET_ROWS, LANES), jnp.int32),
            pltpu.VMEM((8, LANES), jnp.int32),
            pltpu.VMEM((8, TM), jnp.int32),
            pltpu.VMEM((ROW_BUFS, TM, ROW_WIDTH), jnp.float32),
            pltpu.VMEM((TM // 2, ROW_WIDTH), jnp.float32),
            pltpu.VMEM((2, 8, TM), jnp.int32),
            pltpu.SMEM((2, 8, TM), jnp.int32),
            pltpu.SemaphoreType.DMA((2,)),
            pltpu.SemaphoreType.DMA((ROW_BUFS,)),
            pltpu.SemaphoreType.DMA(()),
        ],
        compiler_params=_cparams(("arbitrary",)),
        name="out_proj_route",
    )(*srcs, yna, yls, mod, out_g.reshape(1, D_MODEL), w_bf16, ffn_g.reshape(1, D_MODEL), wr_split,
      b_router.reshape(N_EXPERTS, 1))


def _moe_kernel(src_ref, e0_ref, e1_ref, used_ref, xs_ref, wg0, wu0, wd0, wg1, wu1, wd1, o_ref, wg_s, wu_s, wd_s):
    n = pl.program_id(0)
    prev = jnp.maximum(n - 1, 0)
    for k, (e_ref, g, u, d) in enumerate(((e0_ref, wg0, wu0, wd0), (e1_ref, wg1, wu1, wd1))):
        @pl.when((n == 0) | (e_ref[n] != e_ref[prev]))
        def _():
            wg_s[k] = g[...].astype(jnp.bfloat16)
            wu_s[k] = u[...].astype(jnp.bfloat16)
            wd_s[k] = d[...].astype(jnp.bfloat16)

    @pl.when(n < used_ref[0])
    def _():
        xb = xs_ref[:, :D_MODEL].astype(jnp.bfloat16)
        w_lo, w_hi = xs_ref[:, D_MODEL:D_MODEL + 1], xs_ref[:, D_MODEL + 1:D_MODEL + 2]
        slot0_is_lo = e0_ref[n] < e1_ref[n]
        weights = (jnp.where(slot0_is_lo, w_lo, w_hi), jnp.where(slot0_is_lo, w_hi, w_lo))
        out = jnp.zeros((TM, D_MODEL), jnp.float32)
        for k, wk in enumerate(weights):
            gate = jnp.dot(xb, wg_s[k], preferred_element_type=jnp.float32)
            up = jnp.dot(xb, wu_s[k], preferred_element_type=jnp.float32)
            hid = (gate * jax.nn.sigmoid(gate)) * up
            out = out + wk * jnp.dot(hid.astype(jnp.bfloat16), wd_s[k], preferred_element_type=jnp.float32)
        o_ref[...] = out


def _moe(sched, xs, w_gate, w_up, w_down, layer):
    n_tiles = xs.shape[0] // TM
    src, e0, e1, used = sched
    first = lambda n, s, a, b, u: (layer, a[n], 0, 0)
    second = lambda n, s, a, b, u: (layer, b[n], 0, 0)
    gate_spec = lambda m: pl.BlockSpec((None, None, D_MODEL, D_EXPERT), m)
    down_spec = lambda m: pl.BlockSpec((None, None, D_EXPERT, D_MODEL), m)
    return pl.pallas_call(
        _moe_kernel,
        out_shape=jax.ShapeDtypeStruct((n_tiles * TM, D_MODEL), jnp.float32),
        grid_spec=pltpu.PrefetchScalarGridSpec(
            num_scalar_prefetch=4,
            grid=(n_tiles,),
            in_specs=[
                pl.BlockSpec((TM, ROW_WIDTH), lambda n, s, a, b, u: (s[n], 0)),
                gate_spec(first), gate_spec(first), down_spec(first),
                gate_spec(second), gate_spec(second), down_spec(second),
            ],
            out_specs=pl.BlockSpec((TM, D_MODEL), lambda n, s, a, b, u: (s[n], 0)),
            scratch_shapes=[
                pltpu.VMEM((2, D_MODEL, D_EXPERT), jnp.bfloat16),
                pltpu.VMEM((2, D_MODEL, D_EXPERT), jnp.bfloat16),
                pltpu.VMEM((2, D_EXPERT, D_MODEL), jnp.bfloat16),
            ],
        ),
        compiler_params=_cparams(("arbitrary",)),
        name="moe_experts",
    )(src, e0, e1, used, xs, w_gate, w_up, w_down, w_gate, w_up, w_down)


def _schedule(meta, n_tiles):
    visit = jnp.array([0, 2, 3, 1, 4, 5], jnp.int32)
    slot0 = jnp.array([0, 2, 3, 2, 3, 3], jnp.int32)
    slot1 = jnp.array([1, 0, 0, 1, 1, 2], jnp.int32)
    tile_bucket = meta[0, :n_tiles]
    used = meta[1, 0]
    ids = jnp.arange(n_tiles, dtype=jnp.int32)
    key = (tile_bucket // N_PAIRS) * N_PAIRS + visit[tile_bucket % N_PAIRS]
    order = jnp.argsort(jnp.where(ids < used, key, N_BUCKETS), stable=True).astype(jnp.int32)
    src = jnp.where(ids < used, order, order[jnp.maximum(used - 1, 0)])
    bucket = tile_bucket[src]
    group, pair = bucket // N_PAIRS, bucket % N_PAIRS
    return src, group * E_PER_GROUP + slot0[pair], group * E_PER_GROUP + slot1[pair], used.reshape(1)


def _combine_kernel(pos_ref, x_ref, mod_ref, fg_ref, ys_ref, o_ref, buf, sem, *, final):
    i = pl.program_id(0)
    n = pl.num_programs(0)

    def row_copy(src_row, slot, k):
        return pltpu.make_async_copy(ys_ref.at[pl.ds(src_row, 1), :], buf.at[slot, pl.ds(k, 1), :], sem.at[slot])

    def issue(tile, slot):
        for k in range(TM):
            row_copy(pos_ref[tile * TM + k], slot, k).start(priority=k % 2)

    @pl.when(i == 0)
    def _():
        issue(0, 0)

    for nxt in range(2):
        @pl.when((i + 1 < n) & ((i + 1) % 2 == nxt))
        def _():
            issue(i + 1, nxt)

    slot = i % 2

    def drain(k, c):
        row_copy(0, slot, 0).wait()
        return c
    lax.fori_loop(0, TM, drain, 0, unroll=8)

    x_new = x_ref[...] + mod_ref[5:6, :] * buf[slot]
    if final:
        x_new = _rms(x_new, fg_ref[...])
    o_ref[...] = x_new


def _combine(pos, x_all, mod, final_g, ys, bsz, with_ctx, final):
    n_tiles, bj, mod_row = _tile_maps(bsz, with_ctx)
    if final:
        out_shape = jax.ShapeDtypeStruct((bsz, SEQ, D_MODEL), jnp.float32)
        out_spec = pl.BlockSpec((None, TM, D_MODEL), lambda i, p: (bj(i)[0], bj(i)[1] - 1, 0))
    else:
        out_shape = jax.ShapeDtypeStruct((bsz, ROWS_PER_B, D_MODEL), jnp.float32)
        out_spec = pl.BlockSpec((None, TM, D_MODEL), lambda i, p: (bj(i)[0], bj(i)[1], 0))
    return pl.pallas_call(
        functools.partial(_combine_kernel, final=final),
        out_shape=out_shape,
        grid_spec=pltpu.PrefetchScalarGridSpec(
            num_scalar_prefetch=1,
            grid=(n_tiles,),
            in_specs=[
                pl.BlockSpec((None, TM, D_MODEL), lambda i, p: (bj(i)[0], bj(i)[1], 0)),
                pl.BlockSpec((None, N_MOD, D_MODEL), lambda i, p: (mod_row(i), 0, 0)),
                pl.BlockSpec((1, D_MODEL), lambda i, p: (0, 0)),
                pl.BlockSpec(memory_space=pl.ANY),
            ],
            out_specs=out_spec,
            scratch_shapes=[pltpu.VMEM((2, TM, D_MODEL), jnp.float32), pltpu.SemaphoreType.DMA((2,))],
        ),
        compiler_params=_cparams(("arbitrary",)),
        name="combine",
    )(pos.reshape(-1), x_all, mod, final_g.reshape(1, D_MODEL), ys)


def kernel(x, c, ctx, c_ctx, w_ada, b_ada, norm_mix_g, w_in, lru_conv_w, lru_conv_b, rg_w, rg_b, rg_lam, na_rpb,
           sc_conv_w, sc_conv_b, mix_out_g, w_out, norm_ffn_g, w_router, b_router, w_gate, w_up, w_down, final_g):
    bsz = x.shape[0]
    mod_rows = -(-(bsz + 1) // 8) * 8
    c_all = jnp.zeros((mod_rows, D_MODEL), jnp.float32).at[:bsz].set(c).at[bsz].set(c_ctx)
    mods = _ada(c_all, w_ada, b_ada).reshape(DEPTH, mod_rows, N_MOD, D_MODEL)

    x_all = None
    out = None
    for l in range(DEPTH):
        need_ctx = l < DEPTH - 1
        mod = mods[l]
        srcs = (ctx, x) if l == 0 else (x_all,)
        qkv, rest = _in_proj(srcs, norm_mix_g[l], mod, w_in[l].astype(jnp.bfloat16), bsz)
        yna = _attention(qkv, _attn_bias_table(na_rpb[l]), bsz, need_ctx)
        yls = _lru_sconv(rest, lru_conv_w[l], lru_conv_b[l], rg_w[l], rg_b[l], rg_lam[l], sc_conv_w[l],
                         sc_conv_b[l], bsz, need_ctx)
        x_mid, pos, meta, xs = _out_proj(srcs, yna, yls, mod, mix_out_g[l], w_out[l].astype(jnp.bfloat16),
                                         norm_ffn_g[l], w_router, b_router, bsz, need_ctx)
        ys = _moe(_schedule(meta, xs.shape[0] // TM), xs, w_gate, w_up, w_down, l)
        res = _combine(pos, x_mid, mod, final_g, ys, bsz, need_ctx, final=not need_ctx)
        if need_ctx:
            x_all = res
        else:
            out = res
    return out
```

```python
import functools

import jax
import jax.numpy as jnp
from jax import lax
from jax.experimental import pallas as pl
from jax.experimental.pallas import tpu as pltpu

D_MODEL = 1024
SEQ = 2048
CTX_LEN = 256
ROWS_PER_B = CTX_LEN + SEQ
DEPTH = 2
N_MOD = 6
EPS = 1e-6
NEG_INF = -1e30

GRID_W = 64
GRID_ROWS = SEQ // GRID_W
HEAD_DIM = 64
NA_WIDTH = 512
NA_HEADS = 8
LRU_WIDTH = 256
LRU_HEADS = 4
LRU_BLOCK = 64
SC_WIDTH = 256
QKV_WIDTH = 3 * NA_WIDTH
REST_WIDTH = 2 * LRU_WIDTH + 3 * SC_WIDTH
IN_WIDTH = QKV_WIDTH + REST_WIDTH
RG_C = 8.0
WIN_R = 8
WIN_C = 16
N_EXPERTS = 16
N_GROUPS = 4
E_PER_GROUP = 4
N_PAIRS = 6
N_BUCKETS = N_GROUPS * N_PAIRS
D_EXPERT = 512

TM = 256
TILES_PER_B = ROWS_PER_B // TM
LAT_TILES_PER_B = SEQ // TM
LANES = 128
BUCKET_ROWS = 32
ROW_WIDTH = D_MODEL + LANES
HEADS_PER_STACK = 4
STACK_W = HEADS_PER_STACK * HEAD_DIM
SCAN_ROWS = 8
VMEM_LIMIT = 56 * 1024 * 1024

_HI = lax.Precision.HIGHEST
_NT = (((1,), (1,)), ((), ()))


def _cparams(sem):
    return pltpu.CompilerParams(dimension_semantics=sem, vmem_limit_bytes=VMEM_LIMIT)


def _rms(v, g):
    return v * lax.rsqrt(jnp.mean(v * v, axis=-1, keepdims=True) + EPS) * g


def _ada_kernel(c_ref, w_ref, b_ref, o_ref):
    cond = c_ref[...]
    cond = cond * jax.nn.sigmoid(cond)
    w = w_ref[0]
    c_hi = cond.astype(jnp.bfloat16)
    c_lo = (cond - c_hi.astype(jnp.float32)).astype(jnp.bfloat16)
    w_hi = w.astype(jnp.bfloat16)
    w_lo = (w - w_hi.astype(jnp.float32)).astype(jnp.bfloat16)
    rows = cond.shape[0]
    part = jnp.dot(jnp.concatenate([c_hi, c_lo], axis=0), w_hi, preferred_element_type=jnp.float32)
    o_ref[0] = (part[:rows] + part[rows:] + jnp.dot(c_hi, w_lo, preferred_element_type=jnp.float32)) + b_ref[0]


def _ada(c_all, w_ada, b_ada):
    depth, _, width = w_ada.shape
    rows = c_all.shape[0]
    tn = 1536
    return pl.pallas_call(
        _ada_kernel,
        out_shape=jax.ShapeDtypeStruct((depth, rows, width), jnp.float32),
        grid=(depth, width // tn),
        in_specs=[
            pl.BlockSpec((rows, D_MODEL), lambda l, n: (0, 0)),
            pl.BlockSpec((1, D_MODEL, tn), lambda l, n: (l, 0, n)),
            pl.BlockSpec((1, 1, tn), lambda l, n: (l, 0, n)),
        ],
        out_specs=pl.BlockSpec((1, rows, tn), lambda l, n: (l, 0, n)),
        compiler_params=_cparams(("arbitrary", "arbitrary")),
        name="ada",
    )(c_all, w_ada, b_ada.reshape(depth, 1, width))


def _tile_maps(bsz, with_ctx):
    per_b = TILES_PER_B if with_ctx else LAT_TILES_PER_B
    off = 0 if with_ctx else 1

    def bj(i):
        return i // per_b, i % per_b + off

    def mod_row(i):
        b, j = bj(i)
        return jnp.where(j == 0, bsz, b)

    return per_b * bsz, bj, mod_row


IN_SUB = 3
IN_STEPS_PER_B = TILES_PER_B // IN_SUB


def _in_kernel(*refs, two_src):
    if two_src:
        ctx_ref, *x_refs = refs[:1 + IN_SUB]
        g_ref, modc_ref, modb_ref, w_ref, qkv_ref, rest_ref = refs[1 + IN_SUB:]
    else:
        x_ref, g_ref, modc_ref, modb_ref, w_ref, qkv_ref, rest_ref = refs
    first = pl.program_id(0) % IN_STEPS_PER_B == 0
    for s in range(IN_SUB):
        rows = slice(s * TM, (s + 1) * TM)
        if two_src:
            x = jnp.where(first, ctx_ref[...], x_refs[0][...]) if s == 0 else x_refs[s][...]
        else:
            x = x_ref[rows, :]
        if s == 0:
            shift = jnp.where(first, modc_ref[0:1, :], modb_ref[0:1, :])
            scale = jnp.where(first, modc_ref[1:2, :], modb_ref[1:2, :])
        else:
            shift, scale = modb_ref[0:1, :], modb_ref[1:2, :]
        hb = (_rms(x, g_ref[...]) * (1.0 + scale) + shift).astype(jnp.bfloat16)
        qkv_ref[rows, :] = jnp.dot(hb, w_ref[:, :QKV_WIDTH], preferred_element_type=jnp.float32).astype(jnp.bfloat16)
        rest_ref[rows, :] = jnp.dot(hb, w_ref[:, QKV_WIDTH:], preferred_element_type=jnp.float32)


def _in_proj(srcs, g, mod, w_bf16, bsz):
    two_src = len(srcs) == 2
    bj = lambda i: (i // IN_STEPS_PER_B, i % IN_STEPS_PER_B)
    rows = IN_SUB * TM
    if two_src:
        lat = lambda s: pl.BlockSpec(
            (None, TM, D_MODEL), lambda i: (bj(i)[0], jnp.maximum(IN_SUB * bj(i)[1] + s - 1, 0), 0))
        src_specs = [pl.BlockSpec((None, TM, D_MODEL), lambda i: (bj(i)[0], 0, 0))] + [lat(s) for s in range(IN_SUB)]
        srcs = (srcs[0],) + (srcs[1],) * IN_SUB
    else:
        src_specs = [pl.BlockSpec((None, rows, D_MODEL), lambda i: (bj(i)[0], bj(i)[1], 0))]
    return pl.pallas_call(
        functools.partial(_in_kernel, two_src=two_src),
        out_shape=[
            jax.ShapeDtypeStruct((bsz, ROWS_PER_B, QKV_WIDTH), jnp.bfloat16),
            jax.ShapeDtypeStruct((bsz, ROWS_PER_B, REST_WIDTH), jnp.float32),
        ],
        grid=(bsz * IN_STEPS_PER_B,),
        in_specs=src_specs + [
            pl.BlockSpec((1, D_MODEL), lambda i: (0, 0)),
            pl.BlockSpec((None, N_MOD, D_MODEL), lambda i: (bsz, 0, 0)),
            pl.BlockSpec((None, N_MOD, D_MODEL), lambda i: (bj(i)[0], 0, 0)),
            pl.BlockSpec((D_MODEL, IN_WIDTH), lambda i: (0, 0)),
        ],
        out_specs=[
            pl.BlockSpec((None, rows, QKV_WIDTH), lambda i: (bj(i)[0], bj(i)[1], 0)),
            pl.BlockSpec((None, rows, REST_WIDTH), lambda i: (bj(i)[0], bj(i)[1], 0)),
        ],
        compiler_params=_cparams(("arbitrary",)),
        name="in_proj",
    )(*srcs, g.reshape(1, D_MODEL), mod, mod, w_bf16)


def _attn_kernel(q_ref, k_ref, v_ref, bias_ref, o_ref, *, need_ctx):
    lane_head = lax.broadcasted_iota(jnp.int32, (1, STACK_W), 1) // HEAD_DIM
    n_stacks = NA_WIDTH // STACK_W

    def stack_q(qg):
        zero = jnp.zeros_like(qg)
        return jnp.concatenate([jnp.where(lane_head == h, qg, zero) for h in range(HEADS_PER_STACK)], axis=0)

    def unstack(o):
        out = jnp.zeros((GRID_W, STACK_W), jnp.float32)
        for h in range(HEADS_PER_STACK):
            out = out + jnp.where(lane_head == h, o[h * GRID_W:(h + 1) * GRID_W], 0.0)
        return out

    def attend(q_rows, s, local):
        cols = slice(s * STACK_W, (s + 1) * STACK_W)
        qg = q_ref[pl.ds(q_rows, GRID_W), cols] * jnp.bfloat16(HEAD_DIM ** -0.5)
        qs = stack_q(qg)
        kc = k_ref[0:CTX_LEN, cols]
        vc = v_ref[0:CTX_LEN, cols]
        s_ctx = lax.dot_general(qs, kc, _NT, preferred_element_type=jnp.float32)
        m = jnp.max(s_ctx, axis=-1, keepdims=True)
        if local is not None:
            k_rows, delta = local
            kw = k_ref[pl.ds(k_rows, WIN_R * GRID_W), cols]
            vw = v_ref[pl.ds(k_rows, WIN_R * GRID_W), cols]
            heads = slice(s * HEADS_PER_STACK, (s + 1) * HEADS_PER_STACK)
            bias = jnp.concatenate(
                [bias_ref[heads, pl.ds(2 * m - delta + WIN_R - 1, 1)].reshape(HEADS_PER_STACK * GRID_W, 2 * GRID_W)
                 for m in range(WIN_R // 2)], axis=-1)
            s_loc = lax.dot_general(qs, kw, _NT, preferred_element_type=jnp.float32) + bias
            m = jnp.maximum(m, jnp.max(s_loc, axis=-1, keepdims=True))
            p_loc = jnp.exp(s_loc - m)
        p_ctx = jnp.exp(s_ctx - m)
        denom = jnp.sum(p_ctx, axis=-1, keepdims=True)
        o = jnp.dot(p_ctx.astype(jnp.bfloat16), vc, preferred_element_type=jnp.float32)
        if local is not None:
            denom = denom + jnp.sum(p_loc, axis=-1, keepdims=True)
            o = o + jnp.dot(p_loc.astype(jnp.bfloat16), vw, preferred_element_type=jnp.float32)
        o_ref[pl.ds(q_rows, GRID_W), cols] = unstack(o / denom).astype(o_ref.dtype)

    def lat_row(r, carry):
        r0 = jnp.clip(r - WIN_R // 2, 0, GRID_ROWS - WIN_R)
        q_rows = pl.multiple_of(CTX_LEN + r * GRID_W, GRID_W)
        k_rows = pl.multiple_of(CTX_LEN + r0 * GRID_W, GRID_W)
        for s in range(n_stacks):
            attend(q_rows, s, (k_rows, r - r0))
        return carry

    lax.fori_loop(0, GRID_ROWS, lat_row, 0, unroll=16)

    if need_ctx:
        def ctx_chunk(cq, carry):
            q_rows = pl.multiple_of(cq * GRID_W, GRID_W)
            for s in range(n_stacks):
                attend(q_rows, s, None)
            return carry

        lax.fori_loop(0, CTX_LEN // GRID_W, ctx_chunk, 0, unroll=2)
    else:
        o_ref[0:CTX_LEN, :] = jnp.zeros((CTX_LEN, NA_WIDTH), o_ref.dtype)


def _attn_bias_table(rpb):
    n_rel_c = 2 * WIN_C - 1
    lead = GRID_W - WIN_C
    padded = jnp.pad(rpb, ((0, 0), (0, 0), (lead, 2 * GRID_W - lead - n_rel_c)))
    skew = jnp.tile(padded, (1, 1, GRID_W))[..., :GRID_W * (2 * GRID_W - 1)]
    skew = skew.reshape(NA_HEADS, 2 * WIN_R - 1, GRID_W, 2 * GRID_W - 1)
    toeplitz = skew[..., GRID_W - 1:]
    q_col = jnp.arange(GRID_W)[:, None]
    k_col = jnp.arange(GRID_W)[None, :]
    c_start = jnp.clip(q_col - WIN_C // 2, 0, GRID_W - WIN_C)
    ok = (k_col >= c_start) & (k_col < c_start + WIN_C)
    toeplitz = jnp.where(ok, toeplitz, NEG_INF)
    return jnp.concatenate([toeplitz[:, :-1], toeplitz[:, 1:]], axis=-1)


def _attention(qkv, bias, bsz, need_ctx):
    return pl.pallas_call(
        functools.partial(_attn_kernel, need_ctx=need_ctx),
        out_shape=jax.ShapeDtypeStruct((bsz, ROWS_PER_B, NA_WIDTH), jnp.bfloat16),
        grid=(bsz,),
        in_specs=[
            pl.BlockSpec((None, ROWS_PER_B, NA_WIDTH), lambda b: (b, 0, 0)),
            pl.BlockSpec((None, ROWS_PER_B, NA_WIDTH), lambda b: (b, 0, 1)),
            pl.BlockSpec((None, ROWS_PER_B, NA_WIDTH), lambda b: (b, 0, 2)),
            pl.BlockSpec((NA_HEADS, 2 * WIN_R - 2, GRID_W, 2 * GRID_W), lambda b: (0, 0, 0, 0)),
        ],
        out_specs=pl.BlockSpec((None, ROWS_PER_B, NA_WIDTH), lambda b: (b, 0, 0)),
        compiler_params=_cparams(("arbitrary",)),
        name="attention",
    )(qkv, qkv, qkv, bias)


CONV_PAD = 8


def _pad_base(start):
    return start + CONV_PAD * (1 if start == 0 else 2)


def _dwconv(pad_s, start, length, w_ref, b_ref, left):
    base = _pad_base(start)
    width = w_ref.shape[0]
    y = pad_s[base - left:base - left + length, :] * w_ref[0:1, :] + b_ref[...]
    for k in range(1, width):
        y = y + pad_s[base + k - left:base + k - left + length, :] * w_ref[k:k + 1, :]
    return y


def _lru_kernel(rest_ref, lcw_ref, lcb_ref, wbd_ref, rgb_ref, lam_ref, scw_ref, scb_ref, o_ref,
                xc_s, a_s, b_s, y_s, pad_s, *, need_ctx):
    col_rx, col_rg, col_sb, col_sc, col_sx = (k * LRU_WIDTH for k in range(5))
    segments = ((0, CTX_LEN), (CTX_LEN, SEQ))

    for start, length in segments:
        base = _pad_base(start)
        pad_s[base - CONV_PAD:base, :] = jnp.zeros((CONV_PAD, LRU_WIDTH), jnp.float32)
    pad_s[pad_s.shape[0] - CONV_PAD:, :] = jnp.zeros((CONV_PAD, LRU_WIDTH), jnp.float32)

    for start, length in segments:
        base = _pad_base(start)
        pad_s[base:base + length, :] = rest_ref[start:start + length, col_rx:col_rx + LRU_WIDTH]
    for start, length in segments:
        xc_s[start:start + length, :] = _dwconv(pad_s, start, length, lcw_ref, lcb_ref, 2)

    def coeffs(d, start, length):
        chunk = 256
        sp = jax.nn.softplus(-lam_ref[d:d + 1, :])
        for c0 in range(0, length, chunk):
            xc = xc_s[start + c0:start + c0 + chunk, :]
            pre = jnp.dot(xc.astype(jnp.bfloat16), wbd_ref[:, 2 * d * LRU_WIDTH:(2 * d + 2) * LRU_WIDTH],
                          preferred_element_type=jnp.float32) + rgb_ref[:, 2 * d * LRU_WIDTH:(2 * d + 2) * LRU_WIDTH]
            gate_r = jax.nn.sigmoid(pre[:, :LRU_WIDTH])
            gate_i = jax.nn.sigmoid(pre[:, LRU_WIDTH:])
            log_a = -RG_C * gate_r * sp
            a = jnp.exp(log_a)
            bb = jnp.sqrt(1.0 - a * a) * (gate_i * xc)
            a_s[start + c0:start + c0 + chunk, :] = a
            b_s[start + c0:start + c0 + chunk, :] = bb

    n_ctx_blocks = CTX_LEN // SCAN_ROWS
    n_blocks = ROWS_PER_B // SCAN_ROWS
    sub = lax.broadcasted_iota(jnp.int32, (SCAN_ROWS, LRU_WIDTH), 0)

    def scan(reverse, accumulate):
        def block(i, h_in):
            if reverse:
                blk = jnp.where(i < n_ctx_blocks, n_ctx_blocks - 1 - i, n_blocks + n_ctx_blocks - 1 - i)
            else:
                blk = i
            rows = pl.ds(pl.multiple_of(blk * SCAN_ROWS, SCAN_ROWS), SCAN_ROWS)
            a = a_s[rows, :]
            b = b_s[rows, :]
            for sh in (1, 2, 4):
                if reverse:
                    keep = sub < SCAN_ROWS - sh
                    a_n = pltpu.roll(a, SCAN_ROWS - sh, axis=0)
                    b_n = pltpu.roll(b, SCAN_ROWS - sh, axis=0)
                else:
                    keep = sub >= sh
                    a_n = pltpu.roll(a, sh, axis=0)
                    b_n = pltpu.roll(b, sh, axis=0)
                b = jnp.where(keep, a * b_n + b, b)
                a = jnp.where(keep, a * a_n, a)
            h = a * h_in + b
            y_s[rows, :] = y_s[rows, :] + h if accumulate else h
            return h[0:1, :] if reverse else h[SCAN_ROWS - 1:SCAN_ROWS, :]

        lax.fori_loop(0, n_blocks, block, jnp.zeros((1, LRU_WIDTH), jnp.float32), unroll=8)

    for d, reverse in enumerate((False, True)):
        for start, length in segments:
            coeffs(d, start, length)
        scan(reverse, accumulate=d > 0)

    out_segments = segments if need_ctx else segments[1:]
    for start, length in out_segments:
        rows = slice(start, start + length)
        y_lru = y_s[rows, :] * jax.nn.gelu(rest_ref[rows, col_rg:col_rg + LRU_WIDTH])
        o_ref[rows, 0:LRU_WIDTH] = y_lru.astype(o_ref.dtype)
        base = _pad_base(start)
        pad_s[base:base + length, :] = (rest_ref[rows, col_sc:col_sc + SC_WIDTH]
                                        * rest_ref[rows, col_sx:col_sx + SC_WIDTH])
        y_sc = rest_ref[rows, col_sb:col_sb + SC_WIDTH] * _dwconv(pad_s, start, length, scw_ref, scb_ref, 1)
        o_ref[rows, LRU_WIDTH:LRU_WIDTH + SC_WIDTH] = y_sc.astype(o_ref.dtype)
    if not need_ctx:
        o_ref[0:CTX_LEN, :] = jnp.zeros((CTX_LEN, LRU_WIDTH + SC_WIDTH), o_ref.dtype)


def _block_diag_gates(rg_w):
    eye = jnp.eye(LRU_HEADS, dtype=rg_w.dtype)
    full = jnp.einsum('dgncm,nk->dgnckm', rg_w, eye)
    full = full.reshape(2, 2, LRU_WIDTH, LRU_WIDTH)
    return full.transpose(2, 0, 1, 3).reshape(LRU_WIDTH, 4 * LRU_WIDTH)


def _lru_sconv(rest, lcw, lcb, rg_w, rg_b, rg_lam, scw, scb, bsz, need_ctx):
    wbd = _block_diag_gates(rg_w).astype(jnp.bfloat16)
    const2 = lambda b: (0, 0)
    return pl.pallas_call(
        functools.partial(_lru_kernel, need_ctx=need_ctx),
        out_shape=jax.ShapeDtypeStruct((bsz, ROWS_PER_B, LRU_WIDTH + SC_WIDTH), jnp.bfloat16),
        grid=(bsz,),
        in_specs=[
            pl.BlockSpec((None, ROWS_PER_B, REST_WIDTH), lambda b: (b, 0, 0)),
            pl.BlockSpec(lcw.shape, const2),
            pl.BlockSpec((1, LRU_WIDTH), const2),
            pl.BlockSpec((LRU_WIDTH, 4 * LRU_WIDTH), const2),
            pl.BlockSpec((1, 4 * LRU_WIDTH), const2),
            pl.BlockSpec((2, LRU_WIDTH), const2),
            pl.BlockSpec(scw.shape, const2),
            pl.BlockSpec((1, SC_WIDTH), const2),
        ],
        out_specs=pl.BlockSpec((None, ROWS_PER_B, LRU_WIDTH + SC_WIDTH), lambda b: (b, 0, 0)),
        scratch_shapes=[
            pltpu.VMEM((ROWS_PER_B, LRU_WIDTH), jnp.float32),
            pltpu.VMEM((ROWS_PER_B, LRU_WIDTH), jnp.float32),
            pltpu.VMEM((ROWS_PER_B, LRU_WIDTH), jnp.float32),
            pltpu.VMEM((ROWS_PER_B, LRU_WIDTH), jnp.float32),
            pltpu.VMEM((ROWS_PER_B + 3 * CONV_PAD, LRU_WIDTH), jnp.float32),
        ],
        compiler_params=_cparams(("arbitrary",)),
        name="lru_sconv",
    )(rest, lcw, lcb.reshape(1, LRU_WIDTH), wbd, rg_b.reshape(1, 4 * LRU_WIDTH), rg_lam, scw,
      scb.reshape(1, SC_WIDTH))


def _route(et):
    pe = [et[e:e + 1, :] for e in range(N_EXPERTS)]

    def top2_sum(v):
        best = v[0] + v[1]
        for a in range(E_PER_GROUP):
            for b in range(a + 1, E_PER_GROUP):
                if (a, b) != (0, 1):
                    best = jnp.maximum(best, v[a] + v[b])
        return best

    score = [top2_sum(pe[g * E_PER_GROUP:(g + 1) * E_PER_GROUP]) for g in range(N_GROUPS)]
    g_best, g_sel = score[0], jnp.zeros((1, TM), jnp.int32)
    for g in range(1, N_GROUPS):
        upd = score[g] > g_best
        g_sel = jnp.where(upd, g, g_sel)
        g_best = jnp.where(upd, score[g], g_best)
    p_in = []
    for k in range(E_PER_GROUP):
        v = pe[k]
        for g in range(1, N_GROUPS):
            v = jnp.where(g_sel == g, pe[g * E_PER_GROUP + k], v)
        p_in.append(v)
    m1, i1 = p_in[0], jnp.zeros((1, TM), jnp.int32)
    for k in range(1, E_PER_GROUP):
        upd = p_in[k] > m1
        i1 = jnp.where(upd, k, i1)
        m1 = jnp.where(upd, p_in[k], m1)
    m2, i2 = jnp.full((1, TM), -1.0, jnp.float32), jnp.zeros((1, TM), jnp.int32)
    for k in range(E_PER_GROUP):
        cand = jnp.where(i1 == k, -2.0, p_in[k])
        upd = cand > m2
        i2 = jnp.where(upd, k, i2)
        m2 = jnp.where(upd, cand, m2)
    lo, hi = jnp.minimum(i1, i2), jnp.maximum(i1, i2)
    pair = jnp.where(lo == 0, hi - 1, jnp.where(lo == 1, hi + 1, N_PAIRS - 1))
    bucket = g_sel * N_PAIRS + pair
    w1 = m1 / (m1 + m2)
    w2 = m2 / (m1 + m2)
    return bucket, jnp.where(i1 < i2, w1, w2), jnp.where(i1 < i2, w2, w1)


ROW_BUFS = 4


def _out_kernel(*refs, two_src, n_tiles):
    if two_src:
        ctx_ref, x_ref, *rest = refs
    else:
        x_ref, *rest = refs
    (yna_ref, yls_ref, mod_ref, og_ref, w_ref, fg_ref, wrt_ref, brt_ref, xo_ref, pos_ref, meta_ref, xs_ref,
     cnt_s, cur_s, alloc_s, tb_s, row_s, zero_s, pos_v, pos_sm, sem_p, sem_r, sem_z) = rest
    step = pl.program_id(0)
    shift_tm = TM.bit_length() - 1

    def row_copy(buf, k, dst_row):
        return pltpu.make_async_copy(row_s.at[buf, pl.ds(k, 1), :], xs_ref.at[pl.ds(dst_row, 1), :], sem_r.at[buf])

    def pos_copy(buf):
        return pltpu.make_async_copy(pos_v.at[buf], pos_sm.at[buf], sem_p.at[buf])

    def copy_out(buf):
        pbuf = buf % 2
        pos_copy(pbuf).wait()
        for k in range(TM):
            row_copy(buf, k, pos_sm[pbuf, 0, k]).start(priority=k % 2)

    def drain(buf):
        pltpu.make_async_copy(row_s.at[buf], xs_ref.at[pl.ds(0, TM), :], sem_r.at[buf]).wait()

    @pl.when(step == 0)
    def _():
        cnt_s[...] = jnp.zeros_like(cnt_s)
        cur_s[...] = jnp.zeros_like(cur_s)
        alloc_s[...] = jnp.zeros_like(alloc_s)
        tb_s[...] = jnp.zeros_like(tb_s)

    def project(tile):
        if two_src:
            x = jnp.where(tile % TILES_PER_B == 0, ctx_ref[...], x_ref[...])
        else:
            x = x_ref[...]
        yna = yna_ref[...].astype(jnp.float32)
        yls = yls_ref[...].astype(jnp.float32)
        merged = jnp.concatenate([
            _rms(yna, og_ref[:, :NA_WIDTH]),
            _rms(yls[:, :LRU_WIDTH], og_ref[:, NA_WIDTH:NA_WIDTH + LRU_WIDTH]),
            _rms(yls[:, LRU_WIDTH:], og_ref[:, NA_WIDTH + LRU_WIDTH:]),
        ], axis=-1).astype(jnp.bfloat16)
        y = jnp.dot(merged, w_ref[...], preferred_element_type=jnp.float32)
        x_new = x + mod_ref[2:3, :] * y
        xo_ref[...] = x_new
        h2 = _rms(x_new, fg_ref[...]) * (1.0 + mod_ref[4:5, :]) + mod_ref[3:4, :]
        row_s[tile % ROW_BUFS, :, :D_MODEL] = h2

    def route(tile):
        buf = tile % ROW_BUFS
        h2 = row_s[buf, :, :D_MODEL]
        h_hi = h2.astype(jnp.bfloat16)
        h_lo = (h2 - h_hi.astype(jnp.float32)).astype(jnp.bfloat16)
        part = lax.dot_general(wrt_ref[...], h_hi, _NT, preferred_element_type=jnp.float32)
        lt = (part[:N_EXPERTS] + part[N_EXPERTS:] + brt_ref[...]
              + lax.dot_general(wrt_ref[:N_EXPERTS, :], h_lo, _NT, preferred_element_type=jnp.float32))
        bucket, w_lo, w_hi = _route(jnp.exp(lt - jnp.max(lt, axis=0, keepdims=True)))

        b_iota = lax.broadcasted_iota(jnp.int32, (BUCKET_ROWS, TM), 0)
        onehot = b_iota == bucket
        tri = (lax.broadcasted_iota(jnp.int32, (TM, TM), 0) <= lax.broadcasted_iota(jnp.int32, (TM, TM), 1))
        as_bf16 = lambda mask: jnp.where(mask, 1.0, 0.0).astype(jnp.bfloat16)
        cum = jnp.dot(as_bf16(onehot), as_bf16(tri), preferred_element_type=jnp.float32)
        cnt_new = cum[:, TM - 1:TM].astype(jnp.int32)
        cnt_old = cnt_s[:, 0:1]
        open_id = cur_s[:, 0:1]
        alloc = alloc_s[0:1, 0:1]
        shift = TM.bit_length() - 1
        q_last = (cnt_old + cnt_new - 1) >> shift
        q_prev = (cnt_old - 1) >> shift
        opens = jnp.where(cnt_new > 0, q_last - q_prev, 0)
        lower = (lax.broadcasted_iota(jnp.int32, (BUCKET_ROWS, BUCKET_ROWS), 1)
                 < lax.broadcasted_iota(jnp.int32, (BUCKET_ROWS, BUCKET_ROWS), 0))
        opens_b = jnp.broadcast_to(opens, (BUCKET_ROWS, LANES)).astype(jnp.float32).astype(jnp.bfloat16)
        before = jnp.dot(as_bf16(lower), opens_b, preferred_element_type=jnp.float32)[:, 0:1].astype(jnp.int32)
        new_id = alloc + before
        rank = cnt_old + cum.astype(jnp.int32) - 1
        tile_id = jnp.where((opens > 0) & ((rank >> shift) == q_last), new_id, open_id)
        slot = tile_id * TM + (rank & (TM - 1))
        pos = jnp.sum(jnp.where(onehot, slot, 0).astype(jnp.float32), axis=0, keepdims=True).astype(jnp.int32)
        pos_ref[...] = pos

        lane_id = lax.broadcasted_iota(jnp.int32, (BUCKET_ROWS, TM), 1)
        opened_here = (opens > 0) & (new_id == lane_id)
        opened_bucket = jnp.max(jnp.where(opened_here, b_iota, -1).astype(jnp.float32), axis=0,
                                keepdims=True).astype(jnp.int32)
        tb = jnp.where(opened_bucket >= 0, opened_bucket, tb_s[0:1, :])
        alloc_new = alloc + jnp.sum(opens.astype(jnp.float32), axis=0, keepdims=True).astype(jnp.int32)
        tb_s[...] = jnp.broadcast_to(tb, tb_s.shape)
        cnt_s[...] = jnp.broadcast_to(cnt_old + cnt_new, cnt_s.shape)
        cur_s[...] = jnp.broadcast_to(jnp.where(opens > 0, new_id, open_id), cur_s.shape)
        alloc_s[...] = jnp.broadcast_to(alloc_new, alloc_s.shape)
        meta_ref[0:1, :] = tb
        meta_ref[1:2, :] = jnp.broadcast_to(alloc_new, (1, TM))
        meta_ref[2:8, :] = jnp.zeros((6, TM), jnp.int32)

        payload = jnp.concatenate([w_lo, w_hi, jnp.zeros((LANES - 2, TM), jnp.float32)], axis=0)
        row_s[buf, :, D_MODEL:] = payload.T
        pos_v[tile % 2] = jnp.broadcast_to(pos, (8, TM))

    @pl.when((step >= ROW_BUFS) & (step < n_tiles))
    def _():
        drain(step % ROW_BUFS)
        route(step - 1)
        project(step)
        pos_copy((step - 1) % 2).start()

    @pl.when(((step >= 1) & (step < ROW_BUFS)) | (step == n_tiles))
    def _():
        route(step - 1)
        pos_copy((step - 1) % 2).start()

    @pl.when(step < ROW_BUFS)
    def _():
        project(step)

    for buf in range(ROW_BUFS):
        @pl.when((step >= 2) & ((step - 2) % ROW_BUFS == buf))
        def _():
            copy_out(buf)

    @pl.when(step == n_tiles + 1)
    def _():
        for buf in range(ROW_BUFS):
            drain(buf)

        b_iota = lax.broadcasted_iota(jnp.int32, (BUCKET_ROWS, TM), 0)
        lane_id = lax.broadcasted_iota(jnp.int32, (BUCKET_ROWS, TM), 1)
        on_lanes = lambda col: jnp.sum(jnp.where(b_iota == lane_id, col, 0).astype(jnp.float32), axis=0,
                                       keepdims=True).astype(jnp.int32)
        pos_v[0, 0:1, :] = on_lanes(cnt_s[:, 0:1])
        pos_v[0, 1:2, :] = on_lanes(cur_s[:, 0:1])
        pos_copy(0).start()
        zero_s[...] = jnp.zeros_like(zero_s)
        pos_copy(0).wait()
        sub = SCAN_ROWS
        tail_bits = [1 << k for k in range(TM.bit_length() - 2, sub.bit_length() - 2, -1)]

        def tail_copies(act):
            def zero_rows(row, n_rows):
                return pltpu.make_async_copy(zero_s.at[pl.ds(0, n_rows), :], xs_ref.at[pl.ds(row, n_rows), :], sem_z)

            def per_bucket(b, c):
                cnt, tile = pos_sm[0, 0, b], pos_sm[0, 1, b]
                fill = cnt - (((cnt - 1) >> shift_tm) << shift_tm)
                aligned = ((fill + sub - 1) >> (sub.bit_length() - 1)) << (sub.bit_length() - 1)
                for j in range(sub - 1):
                    @pl.when((cnt > 0) & (fill + j < aligned))
                    def _():
                        act(zero_rows(tile * TM + fill + j, 1))
                tail = TM - aligned
                off = aligned
                for bit in tail_bits:
                    @pl.when((cnt > 0) & ((tail & bit) != 0))
                    def _():
                        act(zero_rows(pl.multiple_of(tile * TM + off, sub), bit))
                    off = off + (tail & bit)
                return c
            lax.fori_loop(0, N_BUCKETS, per_bucket, 0)

        tail_copies(lambda cp: cp.start())
        tail_copies(lambda cp: cp.wait())


def _out_proj(srcs, yna, yls, mod, out_g, w_bf16, ffn_g, w_router, b_router, bsz, with_ctx):
    two_src = len(srcs) == 2
    n_tiles, bj, mod_row = _tile_maps(bsz, with_ctx)
    assert n_tiles >= ROW_BUFS
    n_sorted = n_tiles + N_BUCKETS
    proj = lambda i: bj(jnp.minimum(i, n_tiles - 1))
    tile = lambda width: pl.BlockSpec((None, TM, width), lambda i: (proj(i)[0], proj(i)[1], 0))
    const2 = lambda i: (0, 0)
    if two_src:
        src_specs = [
            pl.BlockSpec((None, TM, D_MODEL), lambda i: (proj(i)[0], 0, 0)),
            pl.BlockSpec((None, TM, D_MODEL), lambda i: (proj(i)[0], jnp.maximum(proj(i)[1] - 1, 0), 0)),
        ]
    else:
        src_specs = [tile(D_MODEL)]
    wr_hi = w_router.T.astype(jnp.bfloat16)
    wr_lo = (w_router.T - wr_hi.astype(jnp.float32)).astype(jnp.bfloat16)
    wr_split = jnp.concatenate([wr_hi, wr_lo], axis=0)
    return pl.pallas_call(
        functools.partial(_out_kernel, two_src=two_src, n_tiles=n_tiles),
        out_shape=[
            jax.ShapeDtypeStruct((bsz, ROWS_PER_B, D_MODEL), jnp.float32),
            jax.ShapeDtypeStruct((n_tiles, 1, TM), jnp.int32),
            jax.ShapeDtypeStruct((8, TM), jnp.int32),
            jax.ShapeDtypeStruct((n_sorted * TM, ROW_WIDTH), jnp.float32),
        ],
        grid=(n_tiles + 2,),
        in_specs=src_specs + [
            tile(NA_WIDTH),
            tile(LRU_WIDTH + SC_WIDTH),
            pl.BlockSpec((None, N_MOD, D_MODEL), lambda i: (mod_row(jnp.minimum(i, n_tiles - 1)), 0, 0)),
            pl.BlockSpec((1, D_MODEL), const2),
            pl.BlockSpec((D_MODEL, D_MODEL), const2),
            pl.BlockSpec((1, D_MODEL), const2),
            pl.BlockSpec((2 * N_EXPERTS, D_MODEL), const2),
            pl.BlockSpec((N_EXPERTS, 1), const2),
        ],
        out_specs=[
            tile(D_MODEL),
            pl.BlockSpec((None, 1, TM), lambda i: (jnp.clip(i - 1, 0, n_tiles - 1), 0, 0)),
            pl.BlockSpec((8, TM), const2),
            pl.BlockSpec(memory_space=pl.ANY),
        ],
        scratch_shapes=[
            pltpu.VMEM((BUCKET_ROWS, LANES), jnp.int32),
            pltpu.VMEM((BUCKET_ROWS, LANES), jnp.int32),
            pltpu.VMEM((8, LANES), jnp.int32),
            pltpu.VMEM((8, TM), jnp.int32),
            pltpu.VMEM((ROW_BUFS, TM, ROW_WIDTH), jnp.float32),
            pltpu.VMEM((TM // 2, ROW_WIDTH), jnp.float32),
            pltpu.VMEM((2, 8, TM), jnp.int32),
            pltpu.SMEM((2, 8, TM), jnp.int32),
            pltpu.SemaphoreType.DMA((2,)),
            pltpu.SemaphoreType.DMA((ROW_BUFS,)),
            pltpu.SemaphoreType.DMA(()),
        ],
        compiler_params=_cparams(("arbitrary",)),
        name="out_proj_route",
    )(*srcs, yna, yls, mod, out_g.reshape(1, D_MODEL), w_bf16, ffn_g.reshape(1, D_MODEL), wr_split,
      b_router.reshape(N_EXPERTS, 1))


def _moe_kernel(src_ref, e0_ref, e1_ref, used_ref, xs_ref, wg0, wu0, wd0, wg1, wu1, wd1, o_ref, wg_s, wu_s, wd_s):
    n = pl.program_id(0)
    prev = jnp.maximum(n - 1, 0)
    for k, (e_ref, g, u, d) in enumerate(((e0_ref, wg0, wu0, wd0), (e1_ref, wg1, wu1, wd1))):
        @pl.when((n == 0) | (e_ref[n] != e_ref[prev]))
        def _():
            wg_s[k] = g[...].astype(jnp.bfloat16)
            wu_s[k] = u[...].astype(jnp.bfloat16)
            wd_s[k] = d[...].astype(jnp.bfloat16)

    @pl.when(n < used_ref[0])
    def _():
        xb = xs_ref[:, :D_MODEL].astype(jnp.bfloat16)
        w_lo, w_hi = xs_ref[:, D_MODEL:D_MODEL + 1], xs_ref[:, D_MODEL + 1:D_MODEL + 2]
        slot0_is_lo = e0_ref[n] < e1_ref[n]
        weights = (jnp.where(slot0_is_lo, w_lo, w_hi), jnp.where(slot0_is_lo, w_hi, w_lo))
        out = jnp.zeros((TM, D_MODEL), jnp.float32)
        for k, wk in enumerate(weights):
            gate = jnp.dot(xb, wg_s[k], preferred_element_type=jnp.float32)
            up = jnp.dot(xb, wu_s[k], preferred_element_type=jnp.float32)
            hid = (gate * jax.nn.sigmoid(gate)) * up
            out = out + wk * jnp.dot(hid.astype(jnp.bfloat16), wd_s[k], preferred_element_type=jnp.float32)
        o_ref[...] = out


def _moe(sched, xs, w_gate, w_up, w_down, layer):
    n_tiles = xs.shape[0] // TM
    src, e0, e1, used = sched
    first = lambda n, s, a, b, u: (layer, a[n], 0, 0)
    second = lambda n, s, a, b, u: (layer, b[n], 0, 0)
    gate_spec = lambda m: pl.BlockSpec((None, None, D_MODEL, D_EXPERT), m)
    down_spec = lambda m: pl.BlockSpec((None, None, D_EXPERT, D_MODEL), m)
    return pl.pallas_call(
        _moe_kernel,
        out_shape=jax.ShapeDtypeStruct((n_tiles * TM, D_MODEL), jnp.float32),
        grid_spec=pltpu.PrefetchScalarGridSpec(
            num_scalar_prefetch=4,
            grid=(n_tiles,),
            in_specs=[
                pl.BlockSpec((TM, ROW_WIDTH), lambda n, s, a, b, u: (s[n], 0)),
                gate_spec(first), gate_spec(first), down_spec(first),
                gate_spec(second), gate_spec(second), down_spec(second),
            ],
            out_specs=pl.BlockSpec((TM, D_MODEL), lambda n, s, a, b, u: (s[n], 0)),
            scratch_shapes=[
                pltpu.VMEM((2, D_MODEL, D_EXPERT), jnp.bfloat16),
                pltpu.VMEM((2, D_MODEL, D_EXPERT), jnp.bfloat16),
                pltpu.VMEM((2, D_EXPERT, D_MODEL), jnp.bfloat16),
            ],
        ),
        compiler_params=_cparams(("arbitrary",)),
        name="moe_experts",
    )(src, e0, e1, used, xs, w_gate, w_up, w_down, w_gate, w_up, w_down)


def _schedule(meta, n_tiles):
    visit = jnp.array([0, 2, 3, 1, 4, 5], jnp.int32)
    slot0 = jnp.array([0, 2, 3, 2, 3, 3], jnp.int32)
    slot1 = jnp.array([1, 0, 0, 1, 1, 2], jnp.int32)
    tile_bucket = meta[0, :n_tiles]
    used = meta[1, 0]
    ids = jnp.arange(n_tiles, dtype=jnp.int32)
    key = (tile_bucket // N_PAIRS) * N_PAIRS + visit[tile_bucket % N_PAIRS]
    order = jnp.argsort(jnp.where(ids < used, key, N_BUCKETS), stable=True).astype(jnp.int32)
    src = jnp.where(ids < used, order, order[jnp.maximum(used - 1, 0)])
    bucket = tile_bucket[src]
    group, pair = bucket // N_PAIRS, bucket % N_PAIRS
    return src, group * E_PER_GROUP + slot0[pair], group * E_PER_GROUP + slot1[pair], used.reshape(1)


def _combine_kernel(pos_ref, x_ref, mod_ref, fg_ref, ys_ref, o_ref, buf, sem, *, final):
    i = pl.program_id(0)
    n = pl.num_programs(0)

    def row_copy(src_row, slot, k):
        return pltpu.make_async_copy(ys_ref.at[pl.ds(src_row, 1), :], buf.at[slot, pl.ds(k, 1), :], sem.at[slot])

    def issue(tile, slot):
        for k in range(TM):
            row_copy(pos_ref[tile * TM + k], slot, k).start(priority=k % 2)

    @pl.when(i == 0)
    def _():
        issue(0, 0)

    for nxt in range(2):
        @pl.when((i + 1 < n) & ((i + 1) % 2 == nxt))
        def _():
            issue(i + 1, nxt)

    slot = i % 2

    pltpu.make_async_copy(ys_ref.at[pl.ds(0, TM), :], buf.at[slot], sem.at[slot]).wait()

    x_new = x_ref[...] + mod_ref[5:6, :] * buf[slot]
    if final:
        x_new = _rms(x_new, fg_ref[...])
    o_ref[...] = x_new


def _combine(pos, x_all, mod, final_g, ys, bsz, with_ctx, final):
    n_tiles, bj, mod_row = _tile_maps(bsz, with_ctx)
    if final:
        out_shape = jax.ShapeDtypeStruct((bsz, SEQ, D_MODEL), jnp.float32)
        out_spec = pl.BlockSpec((None, TM, D_MODEL), lambda i, p: (bj(i)[0], bj(i)[1] - 1, 0))
    else:
        out_shape = jax.ShapeDtypeStruct((bsz, ROWS_PER_B, D_MODEL), jnp.float32)
        out_spec = pl.BlockSpec((None, TM, D_MODEL), lambda i, p: (bj(i)[0], bj(i)[1], 0))
    return pl.pallas_call(
        functools.partial(_combine_kernel, final=final),
        out_shape=out_shape,
        grid_spec=pltpu.PrefetchScalarGridSpec(
            num_scalar_prefetch=1,
            grid=(n_tiles,),
            in_specs=[
                pl.BlockSpec((None, TM, D_MODEL), lambda i, p: (bj(i)[0], bj(i)[1], 0)),
                pl.BlockSpec((None, N_MOD, D_MODEL), lambda i, p: (mod_row(i), 0, 0)),
                pl.BlockSpec((1, D_MODEL), lambda i, p: (0, 0)),
                pl.BlockSpec(memory_space=pl.ANY),
            ],
            out_specs=out_spec,
            scratch_shapes=[pltpu.VMEM((2, TM, D_MODEL), jnp.float32), pltpu.SemaphoreType.DMA((2,))],
        ),
        compiler_params=_cparams(("arbitrary",)),
        name="combine",
    )(pos.reshape(-1), x_all, mod, final_g.reshape(1, D_MODEL), ys)


def kernel(x, c, ctx, c_ctx, w_ada, b_ada, norm_mix_g, w_in, lru_conv_w, lru_conv_b, rg_w, rg_b, rg_lam, na_rpb,
           sc_conv_w, sc_conv_b, mix_out_g, w_out, norm_ffn_g, w_router, b_router, w_gate, w_up, w_down, final_g):
    bsz = x.shape[0]
    mod_rows = -(-(bsz + 1) // 8) * 8
    c_all = jnp.zeros((mod_rows, D_MODEL), jnp.float32).at[:bsz].set(c).at[bsz].set(c_ctx)
    mods = _ada(c_all, w_ada, b_ada).reshape(DEPTH, mod_rows, N_MOD, D_MODEL)

    x_all = None
    out = None
    for l in range(DEPTH):
        need_ctx = l < DEPTH - 1
        mod = mods[l]
        srcs = (ctx, x) if l == 0 else (x_all,)
        qkv, rest = _in_proj(srcs, norm_mix_g[l], mod, w_in[l].astype(jnp.bfloat16), bsz)
        yna = _attention(qkv, _attn_bias_table(na_rpb[l]), bsz, need_ctx)
        yls = _lru_sconv(rest, lru_conv_w[l], lru_conv_b[l], rg_w[l], rg_b[l], rg_lam[l], sc_conv_w[l],
                         sc_conv_b[l], bsz, need_ctx)
        x_mid, pos, meta, xs = _out_proj(srcs, yna, yls, mod, mix_out_g[l], w_out[l].astype(jnp.bfloat16),
                                         norm_ffn_g[l], w_router, b_router, bsz, need_ctx)
        ys = _moe(_schedule(meta, xs.shape[0] // TM), xs, w_gate, w_up, w_down, l)
        res = _combine(pos, x_mid, mod, final_g, ys, bsz, need_ctx, final=not need_ctx)
        if need_ctx:
            x_all = res
        else:
            out = res
    return out
```

```python
import functools

import jax
import jax.numpy as jnp
from jax import lax
from jax.experimental import pallas as pl
from jax.experimental.pallas import tpu as pltpu

D_MODEL = 1024
SEQ = 2048
CTX_LEN = 256
ROWS_PER_B = CTX_LEN + SEQ
DEPTH = 2
N_MOD = 6
EPS = 1e-6
NEG_INF = -1e30

GRID_W = 64
GRID_ROWS = SEQ // GRID_W
HEAD_DIM = 64
NA_WIDTH = 512
NA_HEADS = 8
LRU_WIDTH = 256
LRU_HEADS = 4
LRU_BLOCK = 64
SC_WIDTH = 256
QKV_WIDTH = 3 * NA_WIDTH
REST_WIDTH = 2 * LRU_WIDTH + 3 * SC_WIDTH
IN_WIDTH = QKV_WIDTH + REST_WIDTH
RG_C = 8.0
WIN_R = 8
WIN_C = 16
N_EXPERTS = 16
N_GROUPS = 4
E_PER_GROUP = 4
N_PAIRS = 6
N_BUCKETS = N_GROUPS * N_PAIRS
D_EXPERT = 512

TM = 256
TILES_PER_B = ROWS_PER_B // TM
LAT_TILES_PER_B = SEQ // TM
LANES = 128
BUCKET_ROWS = 32
ROW_WIDTH = D_MODEL + LANES
HEADS_PER_STACK = 4
STACK_W = HEADS_PER_STACK * HEAD_DIM
SCAN_ROWS = 8
VMEM_LIMIT = 56 * 1024 * 1024

_HI = lax.Precision.HIGHEST
_NT = (((1,), (1,)), ((), ()))


def _cparams(sem):
    return pltpu.CompilerParams(dimension_semantics=sem, vmem_limit_bytes=VMEM_LIMIT)


def _rms(v, g):
    return v * lax.rsqrt(jnp.mean(v * v, axis=-1, keepdims=True) + EPS) * g


def _ada_kernel(c_ref, w_ref, b_ref, o_ref):
    cond = c_ref[...]
    cond = cond * jax.nn.sigmoid(cond)
    w = w_ref[0]
    c_hi = cond.astype(jnp.bfloat16)
    c_lo = (cond - c_hi.astype(jnp.float32)).astype(jnp.bfloat16)
    w_hi = w.astype(jnp.bfloat16)
    w_lo = (w - w_hi.astype(jnp.float32)).astype(jnp.bfloat16)
    rows = cond.shape[0]
    part = jnp.dot(jnp.concatenate([c_hi, c_lo], axis=0), w_hi, preferred_element_type=jnp.float32)
    o_ref[0] = (part[:rows] + part[rows:] + jnp.dot(c_hi, w_lo, preferred_element_type=jnp.float32)) + b_ref[0]


def _ada(c_all, w_ada, b_ada):
    depth, _, width = w_ada.shape
    rows = c_all.shape[0]
    tn = 1536
    return pl.pallas_call(
        _ada_kernel,
        out_shape=jax.ShapeDtypeStruct((depth, rows, width), jnp.float32),
        grid=(depth, width // tn),
        in_specs=[
            pl.BlockSpec((rows, D_MODEL), lambda l, n: (0, 0)),
            pl.BlockSpec((1, D_MODEL, tn), lambda l, n: (l, 0, n)),
            pl.BlockSpec((1, 1, tn), lambda l, n: (l, 0, n)),
        ],
        out_specs=pl.BlockSpec((1, rows, tn), lambda l, n: (l, 0, n)),
        compiler_params=_cparams(("arbitrary", "arbitrary")),
        name="ada",
    )(c_all, w_ada, b_ada.reshape(depth, 1, width))


def _tile_maps(bsz, with_ctx):
    per_b = TILES_PER_B if with_ctx else LAT_TILES_PER_B
    off = 0 if with_ctx else 1

    def bj(i):
        return i // per_b, i % per_b + off

    def mod_row(i):
        b, j = bj(i)
        return jnp.where(j == 0, bsz, b)

    return per_b * bsz, bj, mod_row


IN_SUB = 3
IN_STEPS_PER_B = TILES_PER_B // IN_SUB


def _in_kernel(*refs, two_src):
    if two_src:
        ctx_ref, *x_refs = refs[:1 + IN_SUB]
        g_ref, modc_ref, modb_ref, w_ref, qkv_ref, rest_ref = refs[1 + IN_SUB:]
    else:
        x_ref, g_ref, modc_ref, modb_ref, w_ref, qkv_ref, rest_ref = refs
    first = pl.program_id(0) % IN_STEPS_PER_B == 0
    for s in range(IN_SUB):
        rows = slice(s * TM, (s + 1) * TM)
        if two_src:
            x = jnp.where(first, ctx_ref[...], x_refs[0][...]) if s == 0 else x_refs[s][...]
        else:
            x = x_ref[rows, :]
        if s == 0:
            shift = jnp.where(first, modc_ref[0:1, :], modb_ref[0:1, :])
            scale = jnp.where(first, modc_ref[1:2, :], modb_ref[1:2, :])
        else:
            shift, scale = modb_ref[0:1, :], modb_ref[1:2, :]
        hb = (_rms(x, g_ref[...]) * (1.0 + scale) + shift).astype(jnp.bfloat16)
        qkv_ref[rows, :] = jnp.dot(hb, w_ref[:, :QKV_WIDTH], preferred_element_type=jnp.float32).astype(jnp.bfloat16)
        rest_ref[rows, :] = jnp.dot(hb, w_ref[:, QKV_WIDTH:], preferred_element_type=jnp.float32)


def _in_proj(srcs, g, mod, w_bf16, bsz):
    two_src = len(srcs) == 2
    bj = lambda i: (i // IN_STEPS_PER_B, i % IN_STEPS_PER_B)
    rows = IN_SUB * TM
    if two_src:
        lat = lambda s: pl.BlockSpec(
            (None, TM, D_MODEL), lambda i: (bj(i)[0], jnp.maximum(IN_SUB * bj(i)[1] + s - 1, 0), 0))
        src_specs = [pl.BlockSpec((None, TM, D_MODEL), lambda i: (bj(i)[0], 0, 0))] + [lat(s) for s in range(IN_SUB)]
        srcs = (srcs[0],) + (srcs[1],) * IN_SUB
    else:
        src_specs = [pl.BlockSpec((None, rows, D_MODEL), lambda i: (bj(i)[0], bj(i)[1], 0))]
    return pl.pallas_call(
        functools.partial(_in_kernel, two_src=two_src),
        out_shape=[
            jax.ShapeDtypeStruct((bsz, ROWS_PER_B, QKV_WIDTH), jnp.bfloat16),
            jax.ShapeDtypeStruct((bsz, ROWS_PER_B, REST_WIDTH), jnp.float32),
        ],
        grid=(bsz * IN_STEPS_PER_B,),
        in_specs=src_specs + [
            pl.BlockSpec((1, D_MODEL), lambda i: (0, 0)),
            pl.BlockSpec((None, N_MOD, D_MODEL), lambda i: (bsz, 0, 0)),
            pl.BlockSpec((None, N_MOD, D_MODEL), lambda i: (bj(i)[0], 0, 0)),
            pl.BlockSpec((D_MODEL, IN_WIDTH), lambda i: (0, 0)),
        ],
        out_specs=[
            pl.BlockSpec((None, rows, QKV_WIDTH), lambda i: (bj(i)[0], bj(i)[1], 0)),
            pl.BlockSpec((None, rows, REST_WIDTH), lambda i: (bj(i)[0], bj(i)[1], 0)),
        ],
        compiler_params=_cparams(("arbitrary",)),
        name="in_proj",
    )(*srcs, g.reshape(1, D_MODEL), mod, mod, w_bf16)


def _attn_kernel(q_ref, k_ref, v_ref, bias_ref, o_ref, *, need_ctx):
    lane_head = lax.broadcasted_iota(jnp.int32, (1, STACK_W), 1) // HEAD_DIM
    n_stacks = NA_WIDTH // STACK_W

    def stack_q(qg):
        zero = jnp.zeros_like(qg)
        return jnp.concatenate([jnp.where(lane_head == h, qg, zero) for h in range(HEADS_PER_STACK)], axis=0)

    def unstack(o):
        out = jnp.zeros((GRID_W, STACK_W), jnp.float32)
        for h in range(HEADS_PER_STACK):
            out = out + jnp.where(lane_head == h, o[h * GRID_W:(h + 1) * GRID_W], 0.0)
        return out

    def attend(q_rows, s, local):
        cols = slice(s * STACK_W, (s + 1) * STACK_W)
        qg = q_ref[pl.ds(q_rows, GRID_W), cols] * jnp.bfloat16(HEAD_DIM ** -0.5)
        qs = stack_q(qg)
        kc = k_ref[0:CTX_LEN, cols]
        vc = v_ref[0:CTX_LEN, cols]
        s_ctx = lax.dot_general(qs, kc, _NT, preferred_element_type=jnp.float32)
        m = jnp.max(s_ctx, axis=-1, keepdims=True)
        if local is not None:
            k_rows, delta = local
            kw = k_ref[pl.ds(k_rows, WIN_R * GRID_W), cols]
            vw = v_ref[pl.ds(k_rows, WIN_R * GRID_W), cols]
            heads = slice(s * HEADS_PER_STACK, (s + 1) * HEADS_PER_STACK)
            bias = jnp.concatenate(
                [bias_ref[heads, pl.ds(2 * m - delta + WIN_R - 1, 1)].reshape(HEADS_PER_STACK * GRID_W, 2 * GRID_W)
                 for m in range(WIN_R // 2)], axis=-1)
            s_loc = lax.dot_general(qs, kw, _NT, preferred_element_type=jnp.float32) + bias
            m = jnp.maximum(m, jnp.max(s_loc, axis=-1, keepdims=True))
            p_loc = jnp.exp(s_loc - m)
        p_ctx = jnp.exp(s_ctx - m)
        denom = jnp.sum(p_ctx, axis=-1, keepdims=True)
        o = jnp.dot(p_ctx.astype(jnp.bfloat16), vc, preferred_element_type=jnp.float32)
        if local is not None:
            denom = denom + jnp.sum(p_loc, axis=-1, keepdims=True)
            o = o + jnp.dot(p_loc.astype(jnp.bfloat16), vw, preferred_element_type=jnp.float32)
        o_ref[pl.ds(q_rows, GRID_W), cols] = unstack(o / denom).astype(o_ref.dtype)

    def lat_row(r, carry):
        r0 = jnp.clip(r - WIN_R // 2, 0, GRID_ROWS - WIN_R)
        q_rows = pl.multiple_of(CTX_LEN + r * GRID_W, GRID_W)
        k_rows = pl.multiple_of(CTX_LEN + r0 * GRID_W, GRID_W)
        for s in range(n_stacks):
            attend(q_rows, s, (k_rows, r - r0))
        return carry

    lax.fori_loop(0, GRID_ROWS, lat_row, 0, unroll=16)

    if need_ctx:
        def ctx_chunk(cq, carry):
            q_rows = pl.multiple_of(cq * GRID_W, GRID_W)
            for s in range(n_stacks):
                attend(q_rows, s, None)
            return carry

        lax.fori_loop(0, CTX_LEN // GRID_W, ctx_chunk, 0, unroll=2)
    else:
        o_ref[0:CTX_LEN, :] = jnp.zeros((CTX_LEN, NA_WIDTH), o_ref.dtype)


def _attn_bias_table(rpb):
    n_rel_c = 2 * WIN_C - 1
    lead = GRID_W - WIN_C
    padded = jnp.pad(rpb, ((0, 0), (0, 0), (lead, 2 * GRID_W - lead - n_rel_c)))
    skew = jnp.tile(padded, (1, 1, GRID_W))[..., :GRID_W * (2 * GRID_W - 1)]
    skew = skew.reshape(NA_HEADS, 2 * WIN_R - 1, GRID_W, 2 * GRID_W - 1)
    toeplitz = skew[..., GRID_W - 1:]
    q_col = jnp.arange(GRID_W)[:, None]
    k_col = jnp.arange(GRID_W)[None, :]
    c_start = jnp.clip(q_col - WIN_C // 2, 0, GRID_W - WIN_C)
    ok = (k_col >= c_start) & (k_col < c_start + WIN_C)
    toeplitz = jnp.where(ok, toeplitz, NEG_INF)
    return jnp.concatenate([toeplitz[:, :-1], toeplitz[:, 1:]], axis=-1)


def _attention(qkv, bias, bsz, need_ctx):
    return pl.pallas_call(
        functools.partial(_attn_kernel, need_ctx=need_ctx),
        out_shape=jax.ShapeDtypeStruct((bsz, ROWS_PER_B, NA_WIDTH), jnp.bfloat16),
        grid=(bsz,),
        in_specs=[
            pl.BlockSpec((None, ROWS_PER_B, NA_WIDTH), lambda b: (b, 0, 0)),
            pl.BlockSpec((None, ROWS_PER_B, NA_WIDTH), lambda b: (b, 0, 1)),
            pl.BlockSpec((None, ROWS_PER_B, NA_WIDTH), lambda b: (b, 0, 2)),
            pl.BlockSpec((NA_HEADS, 2 * WIN_R - 2, GRID_W, 2 * GRID_W), lambda b: (0, 0, 0, 0)),
        ],
        out_specs=pl.BlockSpec((None, ROWS_PER_B, NA_WIDTH), lambda b: (b, 0, 0)),
        compiler_params=_cparams(("arbitrary",)),
        name="attention",
    )(qkv, qkv, qkv, bias)


CONV_PAD = 8


def _pad_base(start):
    return start + CONV_PAD * (1 if start == 0 else 2)


def _dwconv(pad_s, start, length, w_ref, b_ref, left):
    base = _pad_base(start)
    width = w_ref.shape[0]
    y = pad_s[base - left:base - left + length, :] * w_ref[0:1, :] + b_ref[...]
    for k in range(1, width):
        y = y + pad_s[base + k - left:base + k - left + length, :] * w_ref[k:k + 1, :]
    return y


def _lru_kernel(rest_ref, lcw_ref, lcb_ref, wbd_ref, rgb_ref, lam_ref, scw_ref, scb_ref, o_ref,
                xc_s, a_s, b_s, y_s, pad_s, *, need_ctx):
    col_rx, col_rg, col_sb, col_sc, col_sx = (k * LRU_WIDTH for k in range(5))
    segments = ((0, CTX_LEN), (CTX_LEN, SEQ))

    for start, length in segments:
        base = _pad_base(start)
        pad_s[base - CONV_PAD:base, :] = jnp.zeros((CONV_PAD, LRU_WIDTH), jnp.float32)
    pad_s[pad_s.shape[0] - CONV_PAD:, :] = jnp.zeros((CONV_PAD, LRU_WIDTH), jnp.float32)

    for start, length in segments:
        base = _pad_base(start)
        pad_s[base:base + length, :] = rest_ref[start:start + length, col_rx:col_rx + LRU_WIDTH]
    for start, length in segments:
        xc_s[start:start + length, :] = _dwconv(pad_s, start, length, lcw_ref, lcb_ref, 2)

    def coeffs(d, start, length):
        chunk = 256
        sp = jax.nn.softplus(-lam_ref[d:d + 1, :])
        for c0 in range(0, length, chunk):
            xc = xc_s[start + c0:start + c0 + chunk, :]
            pre = jnp.dot(xc.astype(jnp.bfloat16), wbd_ref[:, 2 * d * LRU_WIDTH:(2 * d + 2) * LRU_WIDTH],
                          preferred_element_type=jnp.float32) + rgb_ref[:, 2 * d * LRU_WIDTH:(2 * d + 2) * LRU_WIDTH]
            gate_r = jax.nn.sigmoid(pre[:, :LRU_WIDTH])
            gate_i = jax.nn.sigmoid(pre[:, LRU_WIDTH:])
            log_a = -RG_C * gate_r * sp
            a = jnp.exp(log_a)
            bb = jnp.sqrt(1.0 - a * a) * (gate_i * xc)
            a_s[start + c0:start + c0 + chunk, :] = a
            b_s[start + c0:start + c0 + chunk, :] = bb

    n_ctx_blocks = CTX_LEN // SCAN_ROWS
    n_blocks = ROWS_PER_B // SCAN_ROWS
    sub = lax.broadcasted_iota(jnp.int32, (SCAN_ROWS, LRU_WIDTH), 0)

    def scan(reverse, accumulate):
        def block(i, h_in):
            if reverse:
                blk = jnp.where(i < n_ctx_blocks, n_ctx_blocks - 1 - i, n_blocks + n_ctx_blocks - 1 - i)
            else:
                blk = i
            rows = pl.ds(pl.multiple_of(blk * SCAN_ROWS, SCAN_ROWS), SCAN_ROWS)
            a = a_s[rows, :]
            b = b_s[rows, :]
            for sh in (1, 2, 4):
                if reverse:
                    keep = sub < SCAN_ROWS - sh
                    a_n = pltpu.roll(a, SCAN_ROWS - sh, axis=0)
                    b_n = pltpu.roll(b, SCAN_ROWS - sh, axis=0)
                else:
                    keep = sub >= sh
                    a_n = pltpu.roll(a, sh, axis=0)
                    b_n = pltpu.roll(b, sh, axis=0)
                b = jnp.where(keep, a * b_n + b, b)
                a = jnp.where(keep, a * a_n, a)
            h = a * h_in + b
            y_s[rows, :] = y_s[rows, :] + h if accumulate else h
            return h[0:1, :] if reverse else h[SCAN_ROWS - 1:SCAN_ROWS, :]

        lax.fori_loop(0, n_blocks, block, jnp.zeros((1, LRU_WIDTH), jnp.float32), unroll=8)

    for d, reverse in enumerate((False, True)):
        for start, length in segments:
            coeffs(d, start, length)
        scan(reverse, accumulate=d > 0)

    out_segments = segments if need_ctx else segments[1:]
    for start, length in out_segments:
        rows = slice(start, start + length)
        y_lru = y_s[rows, :] * jax.nn.gelu(rest_ref[rows, col_rg:col_rg + LRU_WIDTH])
        o_ref[rows, 0:LRU_WIDTH] = y_lru.astype(o_ref.dtype)
        base = _pad_base(start)
        pad_s[base:base + length, :] = (rest_ref[rows, col_sc:col_sc + SC_WIDTH]
                                        * rest_ref[rows, col_sx:col_sx + SC_WIDTH])
        y_sc = rest_ref[rows, col_sb:col_sb + SC_WIDTH] * _dwconv(pad_s, start, length, scw_ref, scb_ref, 1)
        o_ref[rows, LRU_WIDTH:LRU_WIDTH + SC_WIDTH] = y_sc.astype(o_ref.dtype)
    if not need_ctx:
        o_ref[0:CTX_LEN, :] = jnp.zeros((CTX_LEN, LRU_WIDTH + SC_WIDTH), o_ref.dtype)


def _block_diag_gates(rg_w):
    eye = jnp.eye(LRU_HEADS, dtype=rg_w.dtype)
    full = jnp.einsum('dgncm,nk->dgnckm', rg_w, eye)
    full = full.reshape(2, 2, LRU_WIDTH, LRU_WIDTH)
    return full.transpose(2, 0, 1, 3).reshape(LRU_WIDTH, 4 * LRU_WIDTH)


def _lru_sconv(rest, lcw, lcb, rg_w, rg_b, rg_lam, scw, scb, bsz, need_ctx):
    wbd = _block_diag_gates(rg_w).astype(jnp.bfloat16)
    const2 = lambda b: (0, 0)
    return pl.pallas_call(
        functools.partial(_lru_kernel, need_ctx=need_ctx),
        out_shape=jax.ShapeDtypeStruct((bsz, ROWS_PER_B, LRU_WIDTH + SC_WIDTH), jnp.bfloat16),
        grid=(bsz,),
        in_specs=[
            pl.BlockSpec((None, ROWS_PER_B, REST_WIDTH), lambda b: (b, 0, 0)),
            pl.BlockSpec(lcw.shape, const2),
            pl.BlockSpec((1, LRU_WIDTH), const2),
            pl.BlockSpec((LRU_WIDTH, 4 * LRU_WIDTH), const2),
            pl.BlockSpec((1, 4 * LRU_WIDTH), const2),
            pl.BlockSpec((2, LRU_WIDTH), const2),
            pl.BlockSpec(scw.shape, const2),
            pl.BlockSpec((1, SC_WIDTH), const2),
        ],
        out_specs=pl.BlockSpec((None, ROWS_PER_B, LRU_WIDTH + SC_WIDTH), lambda b: (b, 0, 0)),
        scratch_shapes=[
            pltpu.VMEM((ROWS_PER_B, LRU_WIDTH), jnp.float32),
            pltpu.VMEM((ROWS_PER_B, LRU_WIDTH), jnp.float32),
            pltpu.VMEM((ROWS_PER_B, LRU_WIDTH), jnp.float32),
            pltpu.VMEM((ROWS_PER_B, LRU_WIDTH), jnp.float32),
            pltpu.VMEM((ROWS_PER_B + 3 * CONV_PAD, LRU_WIDTH), jnp.float32),
        ],
        compiler_params=_cparams(("arbitrary",)),
        name="lru_sconv",
    )(rest, lcw, lcb.reshape(1, LRU_WIDTH), wbd, rg_b.reshape(1, 4 * LRU_WIDTH), rg_lam, scw,
      scb.reshape(1, SC_WIDTH))


def _route(et):
    pe = [et[e:e + 1, :] for e in range(N_EXPERTS)]

    def top2_sum(v):
        best = v[0] + v[1]
        for a in range(E_PER_GROUP):
            for b in range(a + 1, E_PER_GROUP):
                if (a, b) != (0, 1):
                    best = jnp.maximum(best, v[a] + v[b])
        return best

    score = [top2_sum(pe[g * E_PER_GROUP:(g + 1) * E_PER_GROUP]) for g in range(N_GROUPS)]
    g_best, g_sel = score[0], jnp.zeros((1, TM), jnp.int32)
    for g in range(1, N_GROUPS):
        upd = score[g] > g_best
        g_sel = jnp.where(upd, g, g_sel)
        g_best = jnp.where(upd, score[g], g_best)
    p_in = []
    for k in range(E_PER_GROUP):
        v = pe[k]
        for g in range(1, N_GROUPS):
            v = jnp.where(g_sel == g, pe[g * E_PER_GROUP + k], v)
        p_in.append(v)
    m1, i1 = p_in[0], jnp.zeros((1, TM), jnp.int32)
    for k in range(1, E_PER_GROUP):
        upd = p_in[k] > m1
        i1 = jnp.where(upd, k, i1)
        m1 = jnp.where(upd, p_in[k], m1)
    m2, i2 = jnp.full((1, TM), -1.0, jnp.float32), jnp.zeros((1, TM), jnp.int32)
    for k in range(E_PER_GROUP):
        cand = jnp.where(i1 == k, -2.0, p_in[k])
        upd = cand > m2
        i2 = jnp.where(upd, k, i2)
        m2 = jnp.where(upd, cand, m2)
    lo, hi = jnp.minimum(i1, i2), jnp.maximum(i1, i2)
    pair = jnp.where(lo == 0, hi - 1, jnp.where(lo == 1, hi + 1, N_PAIRS - 1))
    bucket = g_sel * N_PAIRS + pair
    w1 = m1 / (m1 + m2)
    w2 = m2 / (m1 + m2)
    return bucket, jnp.where(i1 < i2, w1, w2), jnp.where(i1 < i2, w2, w1)


ROW_BUFS = 4


def _out_kernel(*refs, two_src, n_tiles):
    if two_src:
        ctx_ref, x_ref, *rest = refs
    else:
        x_ref, *rest = refs
    (yna_ref, yls_ref, mod_ref, og_ref, w_ref, fg_ref, wrt_ref, brt_ref, xo_ref, pos_ref, meta_ref, xs_ref,
     cnt_s, cur_s, alloc_s, tb_s, row_s, zero_s, pos_v, pos_sm, sem_p, sem_r, sem_z) = rest
    step = pl.program_id(0)
    shift_tm = TM.bit_length() - 1

    def row_copy(buf, k, dst_row):
        return pltpu.make_async_copy(row_s.at[buf, pl.ds(k, 1), :], xs_ref.at[pl.ds(dst_row, 1), :], sem_r.at[buf])

    def pos_copy(buf):
        return pltpu.make_async_copy(pos_v.at[buf], pos_sm.at[buf], sem_p.at[buf])

    def copy_out(buf):
        pbuf = buf % 2
        pos_copy(pbuf).wait()
        for k in range(TM):
            row_copy(buf, k, pos_sm[pbuf, 0, k]).start(priority=k % 2)

    def drain(buf):
        pltpu.make_async_copy(row_s.at[buf], xs_ref.at[pl.ds(0, TM), :], sem_r.at[buf]).wait()

    @pl.when(step == 0)
    def _():
        cnt_s[...] = jnp.zeros_like(cnt_s)
        cur_s[...] = jnp.zeros_like(cur_s)
        alloc_s[...] = jnp.zeros_like(alloc_s)
        tb_s[...] = jnp.zeros_like(tb_s)

    def project(tile):
        if two_src:
            x = jnp.where(tile % TILES_PER_B == 0, ctx_ref[...], x_ref[...])
        else:
            x = x_ref[...]
        yna = yna_ref[...].astype(jnp.float32)
        yls = yls_ref[...].astype(jnp.float32)
        merged = jnp.concatenate([
            _rms(yna, og_ref[:, :NA_WIDTH]),
            _rms(yls[:, :LRU_WIDTH], og_ref[:, NA_WIDTH:NA_WIDTH + LRU_WIDTH]),
            _rms(yls[:, LRU_WIDTH:], og_ref[:, NA_WIDTH + LRU_WIDTH:]),
        ], axis=-1).astype(jnp.bfloat16)
        y = jnp.dot(merged, w_ref[...], preferred_element_type=jnp.float32)
        x_new = x + mod_ref[2:3, :] * y
        xo_ref[...] = x_new
        h2 = _rms(x_new, fg_ref[...]) * (1.0 + mod_ref[4:5, :]) + mod_ref[3:4, :]
        row_s[tile % ROW_BUFS, :, :D_MODEL] = h2

    def route(tile):
        buf = tile % ROW_BUFS
        h2 = row_s[buf, :, :D_MODEL]
        h_hi = h2.astype(jnp.bfloat16)
        h_lo = (h2 - h_hi.astype(jnp.float32)).astype(jnp.bfloat16)
        part = lax.dot_general(wrt_ref[...], h_hi, _NT, preferred_element_type=jnp.float32)
        lt = (part[:N_EXPERTS] + part[N_EXPERTS:] + brt_ref[...]
              + lax.dot_general(wrt_ref[:N_EXPERTS, :], h_lo, _NT, preferred_element_type=jnp.float32))
        bucket, w_lo, w_hi = _route(jnp.exp(lt - jnp.max(lt, axis=0, keepdims=True)))

        b_iota = lax.broadcasted_iota(jnp.int32, (BUCKET_ROWS, TM), 0)
        onehot = b_iota == bucket
        tri = (lax.broadcasted_iota(jnp.int32, (TM, TM), 0) <= lax.broadcasted_iota(jnp.int32, (TM, TM), 1))
        as_bf16 = lambda mask: jnp.where(mask, 1.0, 0.0).astype(jnp.bfloat16)
        cum = jnp.dot(as_bf16(onehot), as_bf16(tri), preferred_element_type=jnp.float32)
        cnt_new = cum[:, TM - 1:TM].astype(jnp.int32)
        cnt_old = cnt_s[:, 0:1]
        open_id = cur_s[:, 0:1]
        alloc = alloc_s[0:1, 0:1]
        shift = TM.bit_length() - 1
        q_last = (cnt_old + cnt_new - 1) >> shift
        q_prev = (cnt_old - 1) >> shift
        opens = jnp.where(cnt_new > 0, q_last - q_prev, 0)
        lower = (lax.broadcasted_iota(jnp.int32, (BUCKET_ROWS, BUCKET_ROWS), 1)
                 < lax.broadcasted_iota(jnp.int32, (BUCKET_ROWS, BUCKET_ROWS), 0))
        opens_b = jnp.broadcast_to(opens, (BUCKET_ROWS, LANES)).astype(jnp.float32).astype(jnp.bfloat16)
        before = jnp.dot(as_bf16(lower), opens_b, preferred_element_type=jnp.float32)[:, 0:1].astype(jnp.int32)
        new_id = alloc + before
        rank = cnt_old + cum.astype(jnp.int32) - 1
        tile_id = jnp.where((opens > 0) & ((rank >> shift) == q_last), new_id, open_id)
        slot = tile_id * TM + (rank & (TM - 1))
        pos = jnp.sum(jnp.where(onehot, slot, 0).astype(jnp.float32), axis=0, keepdims=True).astype(jnp.int32)
        pos_ref[...] = pos

        lane_id = lax.broadcasted_iota(jnp.int32, (BUCKET_ROWS, TM), 1)
        opened_here = (opens > 0) & (new_id == lane_id)
        opened_bucket = jnp.max(jnp.where(opened_here, b_iota, -1).astype(jnp.float32), axis=0,
                                keepdims=True).astype(jnp.int32)
        tb = jnp.where(opened_bucket >= 0, opened_bucket, tb_s[0:1, :])
        alloc_new = alloc + jnp.sum(opens.astype(jnp.float32), axis=0, keepdims=True).astype(jnp.int32)
        tb_s[...] = jnp.broadcast_to(tb, tb_s.shape)
        cnt_s[...] = jnp.broadcast_to(cnt_old + cnt_new, cnt_s.shape)
        cur_s[...] = jnp.broadcast_to(jnp.where(opens > 0, new_id, open_id), cur_s.shape)
        alloc_s[...] = jnp.broadcast_to(alloc_new, alloc_s.shape)
        meta_ref[0:1, :] = tb
        meta_ref[1:2, :] = jnp.broadcast_to(alloc_new, (1, TM))
        meta_ref[2:8, :] = jnp.zeros((6, TM), jnp.int32)

        payload = jnp.concatenate([w_lo, w_hi, jnp.zeros((LANES - 2, TM), jnp.float32)], axis=0)
        row_s[buf, :, D_MODEL:] = payload.T
        pos_v[tile % 2] = jnp.broadcast_to(pos, (8, TM))

    @pl.when((step >= ROW_BUFS) & (step < n_tiles))
    def _():
        drain(step % ROW_BUFS)
        route(step - 1)
        project(step)
        pos_copy((step - 1) % 2).start()

    @pl.when(((step >= 1) & (step < ROW_BUFS)) | (step == n_tiles))
    def _():
        route(step - 1)
        pos_copy((step - 1) % 2).start()

    @pl.when(step < ROW_BUFS)
    def _():
        project(step)

    for buf in range(ROW_BUFS):
        @pl.when((step >= 2) & ((step - 2) % ROW_BUFS == buf))
        def _():
            copy_out(buf)

    @pl.when(step == n_tiles + 1)
    def _():
        for buf in range(ROW_BUFS):
            drain(buf)

        b_iota = lax.broadcasted_iota(jnp.int32, (BUCKET_ROWS, TM), 0)
        lane_id = lax.broadcasted_iota(jnp.int32, (BUCKET_ROWS, TM), 1)
        on_lanes = lambda col: jnp.sum(jnp.where(b_iota == lane_id, col, 0).astype(jnp.float32), axis=0,
                                       keepdims=True).astype(jnp.int32)
        pos_v[0, 0:1, :] = on_lanes(cnt_s[:, 0:1])
        pos_v[0, 1:2, :] = on_lanes(cur_s[:, 0:1])
        pos_v[0, 2:3, :] = jnp.broadcast_to(alloc_s[0:1, 0:1], (1, TM))
        pos_copy(0).start()
        zero_s[...] = jnp.zeros_like(zero_s)
        pos_copy(0).wait()
        sub = SCAN_ROWS
        tail_bits = [1 << k for k in range(TM.bit_length() - 2, sub.bit_length() - 2, -1)]

        def tail_copies(act):
            def zero_rows(row, n_rows):
                return pltpu.make_async_copy(zero_s.at[pl.ds(0, n_rows), :], xs_ref.at[pl.ds(row, n_rows), :], sem_z)

            def per_bucket(b, c):
                cnt, tile = pos_sm[0, 0, b], pos_sm[0, 1, b]
                fill = cnt - (((cnt - 1) >> shift_tm) << shift_tm)
                aligned = ((fill + sub - 1) >> (sub.bit_length() - 1)) << (sub.bit_length() - 1)
                for j in range(sub - 1):
                    @pl.when((cnt > 0) & (fill + j < aligned))
                    def _():
                        act(zero_rows(tile * TM + fill + j, 1))
                tail = TM - aligned
                off = aligned
                for bit in tail_bits:
                    @pl.when((cnt > 0) & ((tail & bit) != 0))
                    def _():
                        act(zero_rows(pl.multiple_of(tile * TM + off, sub), bit))
                    off = off + (tail & bit)
                return c
            lax.fori_loop(0, N_BUCKETS, per_bucket, 0)

        def unused_tiles(act):
            def per_tile(t, c):
                for part in range(TM // zero_s.shape[0]):
                    row = pl.multiple_of(t * TM + part * zero_s.shape[0], sub)
                    act(pltpu.make_async_copy(zero_s, xs_ref.at[pl.ds(row, zero_s.shape[0]), :], sem_z))
                return c
            lax.fori_loop(pos_sm[0, 2, 0], n_tiles + N_BUCKETS, per_tile, 0)

        tail_copies(lambda cp: cp.start())
        unused_tiles(lambda cp: cp.start())
        tail_copies(lambda cp: cp.wait())
        unused_tiles(lambda cp: cp.wait())


def _out_proj(srcs, yna, yls, mod, out_g, w_bf16, ffn_g, w_router, b_router, bsz, with_ctx):
    two_src = len(srcs) == 2
    n_tiles, bj, mod_row = _tile_maps(bsz, with_ctx)
    assert n_tiles >= ROW_BUFS
    n_sorted = n_tiles + N_BUCKETS
    proj = lambda i: bj(jnp.minimum(i, n_tiles - 1))
    tile = lambda width: pl.BlockSpec((None, TM, width), lambda i: (proj(i)[0], proj(i)[1], 0))
    const2 = lambda i: (0, 0)
    if two_src:
        src_specs = [
            pl.BlockSpec((None, TM, D_MODEL), lambda i: (proj(i)[0], 0, 0)),
            pl.BlockSpec((None, TM, D_MODEL), lambda i: (proj(i)[0], jnp.maximum(proj(i)[1] - 1, 0), 0)),
        ]
    else:
        src_specs = [tile(D_MODEL)]
    wr_hi = w_router.T.astype(jnp.bfloat16)
    wr_lo = (w_router.T - wr_hi.astype(jnp.float32)).astype(jnp.bfloat16)
    wr_split = jnp.concatenate([wr_hi, wr_lo], axis=0)
    return pl.pallas_call(
        functools.partial(_out_kernel, two_src=two_src, n_tiles=n_tiles),
        out_shape=[
            jax.ShapeDtypeStruct((bsz, ROWS_PER_B if with_ctx else SEQ, D_MODEL), jnp.float32),
            jax.ShapeDtypeStruct((n_tiles, 1, TM), jnp.int32),
            jax.ShapeDtypeStruct((8, TM), jnp.int32),
            jax.ShapeDtypeStruct((n_sorted * TM, ROW_WIDTH), jnp.float32),
        ],
        grid=(n_tiles + 2,),
        in_specs=src_specs + [
            tile(NA_WIDTH),
            tile(LRU_WIDTH + SC_WIDTH),
            pl.BlockSpec((None, N_MOD, D_MODEL), lambda i: (mod_row(jnp.minimum(i, n_tiles - 1)), 0, 0)),
            pl.BlockSpec((1, D_MODEL), const2),
            pl.BlockSpec((D_MODEL, D_MODEL), const2),
            pl.BlockSpec((1, D_MODEL), const2),
            pl.BlockSpec((2 * N_EXPERTS, D_MODEL), const2),
            pl.BlockSpec((N_EXPERTS, 1), const2),
        ],
        out_specs=[
            pl.BlockSpec((None, TM, D_MODEL), lambda i: (proj(i)[0], proj(i)[1] - (0 if with_ctx else 1), 0)),
            pl.BlockSpec((None, 1, TM), lambda i: (jnp.clip(i - 1, 0, n_tiles - 1), 0, 0)),
            pl.BlockSpec((8, TM), const2),
            pl.BlockSpec(memory_space=pl.ANY),
        ],
        scratch_shapes=[
            pltpu.VMEM((BUCKET_ROWS, LANES), jnp.int32),
            pltpu.VMEM((BUCKET_ROWS, LANES), jnp.int32),
            pltpu.VMEM((8, LANES), jnp.int32),
            pltpu.VMEM((8, TM), jnp.int32),
            pltpu.VMEM((ROW_BUFS, TM, ROW_WIDTH), jnp.float32),
            pltpu.VMEM((TM // 2, ROW_WIDTH), jnp.float32),
            pltpu.VMEM((2, 8, TM), jnp.int32),
            pltpu.SMEM((2, 8, TM), jnp.int32),
            pltpu.SemaphoreType.DMA((2,)),
            pltpu.SemaphoreType.DMA((ROW_BUFS,)),
            pltpu.SemaphoreType.DMA(()),
        ],
        compiler_params=_cparams(("arbitrary",)),
        name="out_proj_route",
    )(*srcs, yna, yls, mod, out_g.reshape(1, D_MODEL), w_bf16, ffn_g.reshape(1, D_MODEL), wr_split,
      b_router.reshape(N_EXPERTS, 1))


def _moe_kernel(src_ref, dst_ref, e0_ref, e1_ref, used_ref, xs_ref, wg0, wu0, wd0, wg1, wu1, wd1, o_ref,
                wg_s, wu_s, wd_s):
    n = pl.program_id(0)

    @pl.when(n >= used_ref[0])
    def _():
        o_ref[...] = jnp.zeros_like(o_ref)
    prev = jnp.maximum(n - 1, 0)
    for k, (e_ref, g, u, d) in enumerate(((e0_ref, wg0, wu0, wd0), (e1_ref, wg1, wu1, wd1))):
        @pl.when((n == 0) | (e_ref[n] != e_ref[prev]))
        def _():
            wg_s[k] = g[...].astype(jnp.bfloat16)
            wu_s[k] = u[...].astype(jnp.bfloat16)
            wd_s[k] = d[...].astype(jnp.bfloat16)

    @pl.when(n < used_ref[0])
    def _():
        xb = xs_ref[:, :D_MODEL].astype(jnp.bfloat16)
        w_lo, w_hi = xs_ref[:, D_MODEL:D_MODEL + 1], xs_ref[:, D_MODEL + 1:D_MODEL + 2]
        slot0_is_lo = e0_ref[n] < e1_ref[n]
        weights = (jnp.where(slot0_is_lo, w_lo, w_hi), jnp.where(slot0_is_lo, w_hi, w_lo))
        out = jnp.zeros((TM, D_MODEL), jnp.float32)
        for k, wk in enumerate(weights):
            gate = jnp.dot(xb, wg_s[k], preferred_element_type=jnp.float32)
            up = jnp.dot(xb, wu_s[k], preferred_element_type=jnp.float32)
            hid = (gate * jax.nn.sigmoid(gate)) * up
            out = out + wk * jnp.dot(hid.astype(jnp.bfloat16), wd_s[k], preferred_element_type=jnp.float32)
        o_ref[...] = out


def _moe(sched, xs, w_gate, w_up, w_down, layer):
    n_tiles = xs.shape[0] // TM
    src, dst, e0, e1, used = sched
    first = lambda n, s, d, a, b, u: (layer, a[n], 0, 0)
    second = lambda n, s, d, a, b, u: (layer, b[n], 0, 0)
    gate_spec = lambda m: pl.BlockSpec((None, None, D_MODEL, D_EXPERT), m)
    down_spec = lambda m: pl.BlockSpec((None, None, D_EXPERT, D_MODEL), m)
    return pl.pallas_call(
        _moe_kernel,
        out_shape=jax.ShapeDtypeStruct((n_tiles * TM, D_MODEL), jnp.float32),
        grid_spec=pltpu.PrefetchScalarGridSpec(
            num_scalar_prefetch=5,
            grid=(n_tiles,),
            in_specs=[
                pl.BlockSpec((TM, ROW_WIDTH), lambda n, s, d, a, b, u: (s[n], 0)),
                gate_spec(first), gate_spec(first), down_spec(first),
                gate_spec(second), gate_spec(second), down_spec(second),
            ],
            out_specs=pl.BlockSpec((TM, D_MODEL), lambda n, s, d, a, b, u: (d[n], 0)),
            scratch_shapes=[
                pltpu.VMEM((2, D_MODEL, D_EXPERT), jnp.bfloat16),
                pltpu.VMEM((2, D_MODEL, D_EXPERT), jnp.bfloat16),
                pltpu.VMEM((2, D_EXPERT, D_MODEL), jnp.bfloat16),
            ],
        ),
        compiler_params=_cparams(("arbitrary",)),
        name="moe_experts",
    )(src, dst, e0, e1, used, xs, w_gate, w_up, w_down, w_gate, w_up, w_down)


def _schedule(meta, n_tiles):
    visit = jnp.array([0, 2, 3, 1, 4, 5], jnp.int32)
    slot0 = jnp.array([0, 2, 3, 2, 3, 3], jnp.int32)
    slot1 = jnp.array([1, 0, 0, 1, 1, 2], jnp.int32)
    tile_bucket = meta[0, :n_tiles]
    used = meta[1, 0]
    ids = jnp.arange(n_tiles, dtype=jnp.int32)
    key = (tile_bucket // N_PAIRS) * N_PAIRS + visit[tile_bucket % N_PAIRS]
    order = jnp.argsort(jnp.where(ids < used, key, N_BUCKETS), stable=True).astype(jnp.int32)
    src = jnp.where(ids < used, order, order[jnp.maximum(used - 1, 0)])
    bucket = tile_bucket[src]
    group, pair = bucket // N_PAIRS, bucket % N_PAIRS
    return src, order, group * E_PER_GROUP + slot0[pair], group * E_PER_GROUP + slot1[pair], used.reshape(1)


def _combine_kernel(pos_ref, x_ref, mod_ref, fg_ref, ys_ref, o_ref, buf, sem, *, final):
    i = pl.program_id(0)
    n = pl.num_programs(0)

    def row_copy(src_row, slot, k):
        return pltpu.make_async_copy(ys_ref.at[pl.ds(src_row, 1), :], buf.at[slot, pl.ds(k, 1), :], sem.at[slot])

    def issue(tile, slot):
        for k in range(TM):
            row_copy(pos_ref[tile * TM + k], slot, k).start(priority=k % 2)

    @pl.when(i == 0)
    def _():
        issue(0, 0)

    for nxt in range(2):
        @pl.when((i + 1 < n) & ((i + 1) % 2 == nxt))
        def _():
            issue(i + 1, nxt)

    slot = i % 2

    pltpu.make_async_copy(ys_ref.at[pl.ds(0, TM), :], buf.at[slot], sem.at[slot]).wait()

    x_new = x_ref[...] + mod_ref[5:6, :] * buf[slot]
    if final:
        x_new = _rms(x_new, fg_ref[...])
    o_ref[...] = x_new


def _combine(pos, x_all, mod, final_g, ys, bsz, with_ctx, final):
    n_tiles, bj, mod_row = _tile_maps(bsz, with_ctx)
    if final:
        out_shape = jax.ShapeDtypeStruct((bsz, SEQ, D_MODEL), jnp.float32)
        out_spec = pl.BlockSpec((None, TM, D_MODEL), lambda i, p: (bj(i)[0], bj(i)[1] - 1, 0))
    else:
        out_shape = jax.ShapeDtypeStruct((bsz, ROWS_PER_B, D_MODEL), jnp.float32)
        out_spec = pl.BlockSpec((None, TM, D_MODEL), lambda i, p: (bj(i)[0], bj(i)[1], 0))
    return pl.pallas_call(
        functools.partial(_combine_kernel, final=final),
        out_shape=out_shape,
        grid_spec=pltpu.PrefetchScalarGridSpec(
            num_scalar_prefetch=1,
            grid=(n_tiles,),
            in_specs=[
                pl.BlockSpec((None, TM, D_MODEL), lambda i, p: (bj(i)[0], bj(i)[1] - (0 if with_ctx else 1), 0)),
                pl.BlockSpec((None, N_MOD, D_MODEL), lambda i, p: (mod_row(i), 0, 0)),
                pl.BlockSpec((1, D_MODEL), lambda i, p: (0, 0)),
                pl.BlockSpec(memory_space=pl.ANY),
            ],
            out_specs=out_spec,
            scratch_shapes=[pltpu.VMEM((2, TM, D_MODEL), jnp.float32), pltpu.SemaphoreType.DMA((2,))],
        ),
        compiler_params=_cparams(("arbitrary",)),
        name="combine",
    )(pos.reshape(-1), x_all, mod, final_g.reshape(1, D_MODEL), ys)


def kernel(x, c, ctx, c_ctx, w_ada, b_ada, norm_mix_g, w_in, lru_conv_w, lru_conv_b, rg_w, rg_b, rg_lam, na_rpb,
           sc_conv_w, sc_conv_b, mix_out_g, w_out, norm_ffn_g, w_router, b_router, w_gate, w_up, w_down, final_g):
    bsz = x.shape[0]
    mod_rows = -(-(bsz + 1) // 8) * 8
    c_all = jnp.zeros((mod_rows, D_MODEL), jnp.float32).at[:bsz].set(c).at[bsz].set(c_ctx)
    mods = _ada(c_all, w_ada, b_ada).reshape(DEPTH, mod_rows, N_MOD, D_MODEL)

    x_all = None
    out = None
    for l in range(DEPTH):
        need_ctx = l < DEPTH - 1
        mod = mods[l]
        srcs = (ctx, x) if l == 0 else (x_all,)
        qkv, rest = _in_proj(srcs, norm_mix_g[l], mod, w_in[l].astype(jnp.bfloat16), bsz)
        yna = _attention(qkv, _attn_bias_table(na_rpb[l]), bsz, need_ctx)
        yls = _lru_sconv(rest, lru_conv_w[l], lru_conv_b[l], rg_w[l], rg_b[l], rg_lam[l], sc_conv_w[l],
                         sc_conv_b[l], bsz, need_ctx)
        x_mid, pos, meta, xs = _out_proj(srcs, yna, yls, mod, mix_out_g[l], w_out[l].astype(jnp.bfloat16),
                                         norm_ffn_g[l], w_router, b_router, bsz, need_ctx)
        ys = _moe(_schedule(meta, xs.shape[0] // TM), xs, w_gate, w_up, w_down, l)
        res = _combine(pos, x_mid, mod, final_g, ys, bsz, need_ctx, final=not need_ctx)
        if need_ctx:
            x_all = res
        else:
            out = res
    return out
```

```python
import functools

import jax
import jax.numpy as jnp
from jax import lax
from jax.experimental import pallas as pl
from jax.experimental.pallas import tpu as pltpu

D_MODEL = 1024
SEQ = 2048
CTX_LEN = 256
ROWS_PER_B = CTX_LEN + SEQ
DEPTH = 2
N_MOD = 6
EPS = 1e-6
NEG_INF = -1e30

GRID_W = 64
GRID_ROWS = SEQ // GRID_W
HEAD_DIM = 64
NA_WIDTH = 512
NA_HEADS = 8
LRU_WIDTH = 256
LRU_HEADS = 4
LRU_BLOCK = 64
SC_WIDTH = 256
QKV_WIDTH = 3 * NA_WIDTH
REST_WIDTH = 2 * LRU_WIDTH + 3 * SC_WIDTH
IN_WIDTH = QKV_WIDTH + REST_WIDTH
RG_C = 8.0
WIN_R = 8
WIN_C = 16
N_EXPERTS = 16
N_GROUPS = 4
E_PER_GROUP = 4
N_PAIRS = 6
N_BUCKETS = N_GROUPS * N_PAIRS
D_EXPERT = 512

TM = 256
TILES_PER_B = ROWS_PER_B // TM
LAT_TILES_PER_B = SEQ // TM
LANES = 128
BUCKET_ROWS = 32
ROW_WIDTH = D_MODEL + LANES
HEADS_PER_STACK = 4
STACK_W = HEADS_PER_STACK * HEAD_DIM
SCAN_ROWS = 8
VMEM_LIMIT = 56 * 1024 * 1024

_HI = lax.Precision.HIGHEST
_NT = (((1,), (1,)), ((), ()))


def _cparams(sem):
    return pltpu.CompilerParams(dimension_semantics=sem, vmem_limit_bytes=VMEM_LIMIT)


def _rms(v, g):
    return v * lax.rsqrt(jnp.mean(v * v, axis=-1, keepdims=True) + EPS) * g


def _ada_kernel(c_ref, w_ref, b_ref, o_ref):
    cond = c_ref[...]
    cond = cond * jax.nn.sigmoid(cond)
    w = w_ref[0]
    c_hi = cond.astype(jnp.bfloat16)
    c_lo = (cond - c_hi.astype(jnp.float32)).astype(jnp.bfloat16)
    w_hi = w.astype(jnp.bfloat16)
    w_lo = (w - w_hi.astype(jnp.float32)).astype(jnp.bfloat16)
    rows = cond.shape[0]
    part = jnp.dot(jnp.concatenate([c_hi, c_lo], axis=0), w_hi, preferred_element_type=jnp.float32)
    o_ref[0] = (part[:rows] + part[rows:] + jnp.dot(c_hi, w_lo, preferred_element_type=jnp.float32)) + b_ref[0]


def _ada(c_all, w_ada, b_ada):
    depth, _, width = w_ada.shape
    rows = c_all.shape[0]
    tn = 1536
    return pl.pallas_call(
        _ada_kernel,
        out_shape=jax.ShapeDtypeStruct((depth, rows, width), jnp.float32),
        grid=(depth, width // tn),
        in_specs=[
            pl.BlockSpec((rows, D_MODEL), lambda l, n: (0, 0)),
            pl.BlockSpec((1, D_MODEL, tn), lambda l, n: (l, 0, n)),
            pl.BlockSpec((1, 1, tn), lambda l, n: (l, 0, n)),
        ],
        out_specs=pl.BlockSpec((1, rows, tn), lambda l, n: (l, 0, n)),
        compiler_params=_cparams(("arbitrary", "arbitrary")),
        name="ada",
    )(c_all, w_ada, b_ada.reshape(depth, 1, width))


def _tile_maps(bsz, with_ctx):
    per_b = TILES_PER_B if with_ctx else LAT_TILES_PER_B
    off = 0 if with_ctx else 1

    def bj(i):
        return i // per_b, i % per_b + off

    def mod_row(i):
        b, j = bj(i)
        return jnp.where(j == 0, bsz, b)

    return per_b * bsz, bj, mod_row


IN_SUB = 3
IN_STEPS_PER_B = TILES_PER_B // IN_SUB


def _in_kernel(*refs, two_src):
    if two_src:
        ctx_ref, *x_refs = refs[:1 + IN_SUB]
        g_ref, modc_ref, modb_ref, w_ref, qkv_ref, rest_ref = refs[1 + IN_SUB:]
    else:
        x_ref, g_ref, modc_ref, modb_ref, w_ref, qkv_ref, rest_ref = refs
    first = pl.program_id(0) % IN_STEPS_PER_B == 0
    for s in range(IN_SUB):
        rows = slice(s * TM, (s + 1) * TM)
        if two_src:
            x = jnp.where(first, ctx_ref[...], x_refs[0][...]) if s == 0 else x_refs[s][...]
        else:
            x = x_ref[rows, :]
        if s == 0:
            shift = jnp.where(first, modc_ref[0:1, :], modb_ref[0:1, :])
            scale = jnp.where(first, modc_ref[1:2, :], modb_ref[1:2, :])
        else:
            shift, scale = modb_ref[0:1, :], modb_ref[1:2, :]
        hb = (_rms(x, g_ref[...]) * (1.0 + scale) + shift).astype(jnp.bfloat16)
        qkv_ref[rows, :] = jnp.dot(hb, w_ref[:, :QKV_WIDTH], preferred_element_type=jnp.float32).astype(jnp.bfloat16)
        rest_ref[rows, :] = jnp.dot(hb, w_ref[:, QKV_WIDTH:], preferred_element_type=jnp.float32)


def _in_proj(srcs, g, mod, w_bf16, bsz):
    two_src = len(srcs) == 2
    bj = lambda i: (i // IN_STEPS_PER_B, i % IN_STEPS_PER_B)
    rows = IN_SUB * TM
    if two_src:
        lat = lambda s: pl.BlockSpec(
            (None, TM, D_MODEL), lambda i: (bj(i)[0], jnp.maximum(IN_SUB * bj(i)[1] + s - 1, 0), 0))
        src_specs = [pl.BlockSpec((None, TM, D_MODEL), lambda i: (bj(i)[0], 0, 0))] + [lat(s) for s in range(IN_SUB)]
        srcs = (srcs[0],) + (srcs[1],) * IN_SUB
    else:
        src_specs = [pl.BlockSpec((None, rows, D_MODEL), lambda i: (bj(i)[0], bj(i)[1], 0))]
    return pl.pallas_call(
        functools.partial(_in_kernel, two_src=two_src),
        out_shape=[
            jax.ShapeDtypeStruct((bsz, ROWS_PER_B, QKV_WIDTH), jnp.bfloat16),
            jax.ShapeDtypeStruct((bsz, ROWS_PER_B, REST_WIDTH), jnp.float32),
        ],
        grid=(bsz * IN_STEPS_PER_B,),
        in_specs=src_specs + [
            pl.BlockSpec((1, D_MODEL), lambda i: (0, 0)),
            pl.BlockSpec((None, N_MOD, D_MODEL), lambda i: (bsz, 0, 0)),
            pl.BlockSpec((None, N_MOD, D_MODEL), lambda i: (bj(i)[0], 0, 0)),
            pl.BlockSpec((D_MODEL, IN_WIDTH), lambda i: (0, 0)),
        ],
        out_specs=[
            pl.BlockSpec((None, rows, QKV_WIDTH), lambda i: (bj(i)[0], bj(i)[1], 0)),
            pl.BlockSpec((None, rows, REST_WIDTH), lambda i: (bj(i)[0], bj(i)[1], 0)),
        ],
        compiler_params=_cparams(("arbitrary",)),
        name="in_proj",
    )(*srcs, g.reshape(1, D_MODEL), mod, mod, w_bf16)


def _attn_kernel(q_ref, k_ref, v_ref, bias_ref, o_ref, *, need_ctx):
    lane_head = lax.broadcasted_iota(jnp.int32, (1, STACK_W), 1) // HEAD_DIM
    n_stacks = NA_WIDTH // STACK_W

    def stack_q(qg):
        zero = jnp.zeros_like(qg)
        return jnp.concatenate([jnp.where(lane_head == h, qg, zero) for h in range(HEADS_PER_STACK)], axis=0)

    def unstack(o):
        out = jnp.zeros((GRID_W, STACK_W), jnp.float32)
        for h in range(HEADS_PER_STACK):
            out = out + jnp.where(lane_head == h, o[h * GRID_W:(h + 1) * GRID_W], 0.0)
        return out

    def attend(q_rows, s, local):
        cols = slice(s * STACK_W, (s + 1) * STACK_W)
        qg = q_ref[pl.ds(q_rows, GRID_W), cols] * jnp.bfloat16(HEAD_DIM ** -0.5)
        qs = stack_q(qg)
        kc = k_ref[0:CTX_LEN, cols]
        vc = v_ref[0:CTX_LEN, cols]
        s_ctx = lax.dot_general(qs, kc, _NT, preferred_element_type=jnp.float32)
        m = jnp.max(s_ctx, axis=-1, keepdims=True)
        if local is not None:
            k_rows, delta = local
            kw = k_ref[pl.ds(k_rows, WIN_R * GRID_W), cols]
            vw = v_ref[pl.ds(k_rows, WIN_R * GRID_W), cols]
            heads = slice(s * HEADS_PER_STACK, (s + 1) * HEADS_PER_STACK)
            bias = jnp.concatenate(
                [bias_ref[heads, pl.ds(2 * m - delta + WIN_R - 1, 1)].reshape(HEADS_PER_STACK * GRID_W, 2 * GRID_W)
                 for m in range(WIN_R // 2)], axis=-1)
            s_loc = lax.dot_general(qs, kw, _NT, preferred_element_type=jnp.float32) + bias
            m = jnp.maximum(m, jnp.max(s_loc, axis=-1, keepdims=True))
            p_loc = jnp.exp(s_loc - m)
        p_ctx = jnp.exp(s_ctx - m)
        denom = jnp.sum(p_ctx, axis=-1, keepdims=True)
        o = jnp.dot(p_ctx.astype(jnp.bfloat16), vc, preferred_element_type=jnp.float32)
        if local is not None:
            denom = denom + jnp.sum(p_loc, axis=-1, keepdims=True)
            o = o + jnp.dot(p_loc.astype(jnp.bfloat16), vw, preferred_element_type=jnp.float32)
        o_ref[pl.ds(q_rows, GRID_W), cols] = unstack(o / denom).astype(o_ref.dtype)

    def lat_row(r, carry):
        r0 = jnp.clip(r - WIN_R // 2, 0, GRID_ROWS - WIN_R)
        q_rows = pl.multiple_of(CTX_LEN + r * GRID_W, GRID_W)
        k_rows = pl.multiple_of(CTX_LEN + r0 * GRID_W, GRID_W)
        for s in range(n_stacks):
            attend(q_rows, s, (k_rows, r - r0))
        return carry

    lax.fori_loop(0, GRID_ROWS, lat_row, 0, unroll=16)

    if need_ctx:
        def ctx_chunk(cq, carry):
            q_rows = pl.multiple_of(cq * GRID_W, GRID_W)
            for s in range(n_stacks):
                attend(q_rows, s, None)
            return carry

        lax.fori_loop(0, CTX_LEN // GRID_W, ctx_chunk, 0, unroll=2)
    else:
        o_ref[0:CTX_LEN, :] = jnp.zeros((CTX_LEN, NA_WIDTH), o_ref.dtype)


def _attn_bias_table(rpb):
    n_rel_c = 2 * WIN_C - 1
    lead = GRID_W - WIN_C
    padded = jnp.pad(rpb, ((0, 0), (0, 0), (lead, 2 * GRID_W - lead - n_rel_c)))
    skew = jnp.tile(padded, (1, 1, GRID_W))[..., :GRID_W * (2 * GRID_W - 1)]
    skew = skew.reshape(NA_HEADS, 2 * WIN_R - 1, GRID_W, 2 * GRID_W - 1)
    toeplitz = skew[..., GRID_W - 1:]
    q_col = jnp.arange(GRID_W)[:, None]
    k_col = jnp.arange(GRID_W)[None, :]
    c_start = jnp.clip(q_col - WIN_C // 2, 0, GRID_W - WIN_C)
    ok = (k_col >= c_start) & (k_col < c_start + WIN_C)
    toeplitz = jnp.where(ok, toeplitz, NEG_INF)
    return jnp.concatenate([toeplitz[:, :-1], toeplitz[:, 1:]], axis=-1)


def _attention(qkv, bias, bsz, need_ctx):
    return pl.pallas_call(
        functools.partial(_attn_kernel, need_ctx=need_ctx),
        out_shape=jax.ShapeDtypeStruct((bsz, ROWS_PER_B, NA_WIDTH), jnp.bfloat16),
        grid=(bsz,),
        in_specs=[
            pl.BlockSpec((None, ROWS_PER_B, NA_WIDTH), lambda b: (b, 0, 0)),
            pl.BlockSpec((None, ROWS_PER_B, NA_WIDTH), lambda b: (b, 0, 1)),
            pl.BlockSpec((None, ROWS_PER_B, NA_WIDTH), lambda b: (b, 0, 2)),
            pl.BlockSpec((NA_HEADS, 2 * WIN_R - 2, GRID_W, 2 * GRID_W), lambda b: (0, 0, 0, 0)),
        ],
        out_specs=pl.BlockSpec((None, ROWS_PER_B, NA_WIDTH), lambda b: (b, 0, 0)),
        compiler_params=_cparams(("arbitrary",)),
        name="attention",
    )(qkv, qkv, qkv, bias)


CONV_PAD = 8


def _pad_base(start):
    return start + CONV_PAD * (1 if start == 0 else 2)


def _dwconv(pad_s, start, length, w_ref, b_ref, left):
    base = _pad_base(start)
    width = w_ref.shape[0]
    y = pad_s[base - left:base - left + length, :] * w_ref[0:1, :] + b_ref[...]
    for k in range(1, width):
        y = y + pad_s[base + k - left:base + k - left + length, :] * w_ref[k:k + 1, :]
    return y


def _lru_kernel(rest_ref, lcw_ref, lcb_ref, wbd_ref, rgb_ref, lam_ref, scw_ref, scb_ref, o_ref,
                xc_s, a_s, b_s, y_s, pad_s, *, need_ctx):
    col_rx, col_rg, col_sb, col_sc, col_sx = (k * LRU_WIDTH for k in range(5))
    segments = ((0, CTX_LEN), (CTX_LEN, SEQ))

    for start, length in segments:
        base = _pad_base(start)
        pad_s[base - CONV_PAD:base, :] = jnp.zeros((CONV_PAD, LRU_WIDTH), jnp.float32)
    pad_s[pad_s.shape[0] - CONV_PAD:, :] = jnp.zeros((CONV_PAD, LRU_WIDTH), jnp.float32)

    for start, length in segments:
        base = _pad_base(start)
        pad_s[base:base + length, :] = rest_ref[start:start + length, col_rx:col_rx + LRU_WIDTH]
    for start, length in segments:
        xc_s[start:start + length, :] = _dwconv(pad_s, start, length, lcw_ref, lcb_ref, 2)

    def coeffs(d, start, length):
        chunk = 256
        sp = jax.nn.softplus(-lam_ref[d:d + 1, :])
        for c0 in range(0, length, chunk):
            xc = xc_s[start + c0:start + c0 + chunk, :]
            pre = jnp.dot(xc.astype(jnp.bfloat16), wbd_ref[:, 2 * d * LRU_WIDTH:(2 * d + 2) * LRU_WIDTH],
                          preferred_element_type=jnp.float32) + rgb_ref[:, 2 * d * LRU_WIDTH:(2 * d + 2) * LRU_WIDTH]
            gate_r = jax.nn.sigmoid(pre[:, :LRU_WIDTH])
            gate_i = jax.nn.sigmoid(pre[:, LRU_WIDTH:])
            log_a = -RG_C * gate_r * sp
            a = jnp.exp(log_a)
            bb = jnp.sqrt(1.0 - a * a) * (gate_i * xc)
            a_s[start + c0:start + c0 + chunk, :] = a
            b_s[start + c0:start + c0 + chunk, :] = bb

    n_ctx_blocks = CTX_LEN // SCAN_ROWS
    n_blocks = ROWS_PER_B // SCAN_ROWS
    sub = lax.broadcasted_iota(jnp.int32, (SCAN_ROWS, LRU_WIDTH), 0)

    def scan(reverse, accumulate):
        def block(i, h_in):
            if reverse:
                blk = jnp.where(i < n_ctx_blocks, n_ctx_blocks - 1 - i, n_blocks + n_ctx_blocks - 1 - i)
            else:
                blk = i
            rows = pl.ds(pl.multiple_of(blk * SCAN_ROWS, SCAN_ROWS), SCAN_ROWS)
            a = a_s[rows, :]
            b = b_s[rows, :]
            for sh in (1, 2, 4):
                if reverse:
                    keep = sub < SCAN_ROWS - sh
                    a_n = pltpu.roll(a, SCAN_ROWS - sh, axis=0)
                    b_n = pltpu.roll(b, SCAN_ROWS - sh, axis=0)
                else:
                    keep = sub >= sh
                    a_n = pltpu.roll(a, sh, axis=0)
                    b_n = pltpu.roll(b, sh, axis=0)
                b = jnp.where(keep, a * b_n + b, b)
                a = jnp.where(keep, a * a_n, a)
            h = a * h_in + b
            y_s[rows, :] = y_s[rows, :] + h if accumulate else h
            return h[0:1, :] if reverse else h[SCAN_ROWS - 1:SCAN_ROWS, :]

        lax.fori_loop(0, n_blocks, block, jnp.zeros((1, LRU_WIDTH), jnp.float32), unroll=8)

    for d, reverse in enumerate((False, True)):
        for start, length in segments:
            coeffs(d, start, length)
        scan(reverse, accumulate=d > 0)

    out_segments = segments if need_ctx else segments[1:]
    for start, length in out_segments:
        rows = slice(start, start + length)
        y_lru = y_s[rows, :] * jax.nn.gelu(rest_ref[rows, col_rg:col_rg + LRU_WIDTH])
        o_ref[rows, 0:LRU_WIDTH] = y_lru.astype(o_ref.dtype)
        base = _pad_base(start)
        pad_s[base:base + length, :] = (rest_ref[rows, col_sc:col_sc + SC_WIDTH]
                                        * rest_ref[rows, col_sx:col_sx + SC_WIDTH])
        y_sc = rest_ref[rows, col_sb:col_sb + SC_WIDTH] * _dwconv(pad_s, start, length, scw_ref, scb_ref, 1)
        o_ref[rows, LRU_WIDTH:LRU_WIDTH + SC_WIDTH] = y_sc.astype(o_ref.dtype)
    if not need_ctx:
        o_ref[0:CTX_LEN, :] = jnp.zeros((CTX_LEN, LRU_WIDTH + SC_WIDTH), o_ref.dtype)


def _block_diag_gates(rg_w):
    eye = jnp.eye(LRU_HEADS, dtype=rg_w.dtype)
    full = jnp.einsum('dgncm,nk->dgnckm', rg_w, eye)
    full = full.reshape(2, 2, LRU_WIDTH, LRU_WIDTH)
    return full.transpose(2, 0, 1, 3).reshape(LRU_WIDTH, 4 * LRU_WIDTH)


def _lru_sconv(rest, lcw, lcb, rg_w, rg_b, rg_lam, scw, scb, bsz, need_ctx):
    wbd = _block_diag_gates(rg_w).astype(jnp.bfloat16)
    const2 = lambda b: (0, 0)
    return pl.pallas_call(
        functools.partial(_lru_kernel, need_ctx=need_ctx),
        out_shape=jax.ShapeDtypeStruct((bsz, ROWS_PER_B, LRU_WIDTH + SC_WIDTH), jnp.bfloat16),
        grid=(bsz,),
        in_specs=[
            pl.BlockSpec((None, ROWS_PER_B, REST_WIDTH), lambda b: (b, 0, 0)),
            pl.BlockSpec(lcw.shape, const2),
            pl.BlockSpec((1, LRU_WIDTH), const2),
            pl.BlockSpec((LRU_WIDTH, 4 * LRU_WIDTH), const2),
            pl.BlockSpec((1, 4 * LRU_WIDTH), const2),
            pl.BlockSpec((2, LRU_WIDTH), const2),
            pl.BlockSpec(scw.shape, const2),
            pl.BlockSpec((1, SC_WIDTH), const2),
        ],
        out_specs=pl.BlockSpec((None, ROWS_PER_B, LRU_WIDTH + SC_WIDTH), lambda b: (b, 0, 0)),
        scratch_shapes=[
            pltpu.VMEM((ROWS_PER_B, LRU_WIDTH), jnp.float32),
            pltpu.VMEM((ROWS_PER_B, LRU_WIDTH), jnp.float32),
            pltpu.VMEM((ROWS_PER_B, LRU_WIDTH), jnp.float32),
            pltpu.VMEM((ROWS_PER_B, LRU_WIDTH), jnp.float32),
            pltpu.VMEM((ROWS_PER_B + 3 * CONV_PAD, LRU_WIDTH), jnp.float32),
        ],
        compiler_params=_cparams(("arbitrary",)),
        name="lru_sconv",
    )(rest, lcw, lcb.reshape(1, LRU_WIDTH), wbd, rg_b.reshape(1, 4 * LRU_WIDTH), rg_lam, scw,
      scb.reshape(1, SC_WIDTH))


def _route(et):
    pe = [et[e:e + 1, :] for e in range(N_EXPERTS)]

    def top2_sum(v):
        best = v[0] + v[1]
        for a in range(E_PER_GROUP):
            for b in range(a + 1, E_PER_GROUP):
                if (a, b) != (0, 1):
                    best = jnp.maximum(best, v[a] + v[b])
        return best

    score = [top2_sum(pe[g * E_PER_GROUP:(g + 1) * E_PER_GROUP]) for g in range(N_GROUPS)]
    g_best, g_sel = score[0], jnp.zeros((1, TM), jnp.int32)
    for g in range(1, N_GROUPS):
        upd = score[g] > g_best
        g_sel = jnp.where(upd, g, g_sel)
        g_best = jnp.where(upd, score[g], g_best)
    p_in = []
    for k in range(E_PER_GROUP):
        v = pe[k]
        for g in range(1, N_GROUPS):
            v = jnp.where(g_sel == g, pe[g * E_PER_GROUP + k], v)
        p_in.append(v)
    m1, i1 = p_in[0], jnp.zeros((1, TM), jnp.int32)
    for k in range(1, E_PER_GROUP):
        upd = p_in[k] > m1
        i1 = jnp.where(upd, k, i1)
        m1 = jnp.where(upd, p_in[k], m1)
    m2, i2 = jnp.full((1, TM), -1.0, jnp.float32), jnp.zeros((1, TM), jnp.int32)
    for k in range(E_PER_GROUP):
        cand = jnp.where(i1 == k, -2.0, p_in[k])
        upd = cand > m2
        i2 = jnp.where(upd, k, i2)
        m2 = jnp.where(upd, cand, m2)
    lo, hi = jnp.minimum(i1, i2), jnp.maximum(i1, i2)
    pair = jnp.where(lo == 0, hi - 1, jnp.where(lo == 1, hi + 1, N_PAIRS - 1))
    bucket = g_sel * N_PAIRS + pair
    w1 = m1 / (m1 + m2)
    w2 = m2 / (m1 + m2)
    return bucket, jnp.where(i1 < i2, w1, w2), jnp.where(i1 < i2, w2, w1)


ROW_BUFS = 4


def _out_kernel(*refs, two_src, n_tiles):
    if two_src:
        ctx_ref, x_ref, *rest = refs
    else:
        x_ref, *rest = refs
    (yna_ref, yls_ref, mod_ref, og_ref, w_ref, fg_ref, wrt_ref, brt_ref, xo_ref, pos_ref, meta_ref, xs_ref,
     cnt_s, cur_s, alloc_s, tb_s, row_s, zero_s, pos_v, pos_sm, sem_p, sem_r, sem_z) = rest
    step = pl.program_id(0)
    shift_tm = TM.bit_length() - 1

    def row_copy(buf, k, dst_row):
        return pltpu.make_async_copy(row_s.at[buf, pl.ds(k, 1), :], xs_ref.at[pl.ds(dst_row, 1), :], sem_r.at[buf])

    def pos_copy(buf):
        return pltpu.make_async_copy(pos_v.at[buf], pos_sm.at[buf], sem_p.at[buf])

    def copy_out(buf):
        pbuf = buf % 2
        pos_copy(pbuf).wait()
        for k in range(TM):
            row_copy(buf, k, pos_sm[pbuf, 0, k]).start(priority=k % 2)

    def drain(buf):
        pltpu.make_async_copy(row_s.at[buf], xs_ref.at[pl.ds(0, TM), :], sem_r.at[buf]).wait()

    @pl.when(step == 0)
    def _():
        cnt_s[...] = jnp.zeros_like(cnt_s)
        cur_s[...] = jnp.zeros_like(cur_s)
        alloc_s[...] = jnp.zeros_like(alloc_s)
        tb_s[...] = jnp.zeros_like(tb_s)

    def project(tile):
        if two_src:
            x = jnp.where(tile % TILES_PER_B == 0, ctx_ref[...], x_ref[...])
        else:
            x = x_ref[...]
        yna = yna_ref[...].astype(jnp.float32)
        yls = yls_ref[...].astype(jnp.float32)
        merged = jnp.concatenate([
            _rms(yna, og_ref[:, :NA_WIDTH]),
            _rms(yls[:, :LRU_WIDTH], og_ref[:, NA_WIDTH:NA_WIDTH + LRU_WIDTH]),
            _rms(yls[:, LRU_WIDTH:], og_ref[:, NA_WIDTH + LRU_WIDTH:]),
        ], axis=-1).astype(jnp.bfloat16)
        y = jnp.dot(merged, w_ref[...], preferred_element_type=jnp.float32)
        x_new = x + mod_ref[2:3, :] * y
        xo_ref[...] = x_new
        h2 = _rms(x_new, fg_ref[...]) * (1.0 + mod_ref[4:5, :]) + mod_ref[3:4, :]
        row_s[tile % ROW_BUFS, :, :D_MODEL] = h2

    def route(tile):
        buf = tile % ROW_BUFS
        h2 = row_s[buf, :, :D_MODEL]
        h_hi = h2.astype(jnp.bfloat16)
        h_lo = (h2 - h_hi.astype(jnp.float32)).astype(jnp.bfloat16)
        part = lax.dot_general(wrt_ref[...], h_hi, _NT, preferred_element_type=jnp.float32)
        lt = (part[:N_EXPERTS] + part[N_EXPERTS:] + brt_ref[...]
              + lax.dot_general(wrt_ref[:N_EXPERTS, :], h_lo, _NT, preferred_element_type=jnp.float32))
        bucket, w_lo, w_hi = _route(jnp.exp(lt - jnp.max(lt, axis=0, keepdims=True)))

        b_iota = lax.broadcasted_iota(jnp.int32, (BUCKET_ROWS, TM), 0)
        onehot = b_iota == bucket
        tri = (lax.broadcasted_iota(jnp.int32, (TM, TM), 0) <= lax.broadcasted_iota(jnp.int32, (TM, TM), 1))
        as_bf16 = lambda mask: jnp.where(mask, 1.0, 0.0).astype(jnp.bfloat16)
        cum = jnp.dot(as_bf16(onehot), as_bf16(tri), preferred_element_type=jnp.float32)
        cnt_new = cum[:, TM - 1:TM].astype(jnp.int32)
        cnt_old = cnt_s[:, 0:1]
        open_id = cur_s[:, 0:1]
        alloc = alloc_s[0:1, 0:1]
        shift = TM.bit_length() - 1
        q_last = (cnt_old + cnt_new - 1) >> shift
        q_prev = (cnt_old - 1) >> shift
        opens = jnp.where(cnt_new > 0, q_last - q_prev, 0)
        lower = (lax.broadcasted_iota(jnp.int32, (BUCKET_ROWS, BUCKET_ROWS), 1)
                 < lax.broadcasted_iota(jnp.int32, (BUCKET_ROWS, BUCKET_ROWS), 0))
        opens_b = jnp.broadcast_to(opens, (BUCKET_ROWS, LANES)).astype(jnp.float32).astype(jnp.bfloat16)
        before = jnp.dot(as_bf16(lower), opens_b, preferred_element_type=jnp.float32)[:, 0:1].astype(jnp.int32)
        new_id = alloc + before
        rank = cnt_old + cum.astype(jnp.int32) - 1
        tile_id = jnp.where((opens > 0) & ((rank >> shift) == q_last), new_id, open_id)
        slot = tile_id * TM + (rank & (TM - 1))
        pos = jnp.sum(jnp.where(onehot, slot, 0).astype(jnp.float32), axis=0, keepdims=True).astype(jnp.int32)
        pos_ref[...] = pos

        lane_id = lax.broadcasted_iota(jnp.int32, (BUCKET_ROWS, TM), 1)
        opened_here = (opens > 0) & (new_id == lane_id)
        opened_bucket = jnp.max(jnp.where(opened_here, b_iota, -1).astype(jnp.float32), axis=0,
                                keepdims=True).astype(jnp.int32)
        tb = jnp.where(opened_bucket >= 0, opened_bucket, tb_s[0:1, :])
        alloc_new = alloc + jnp.sum(opens.astype(jnp.float32), axis=0, keepdims=True).astype(jnp.int32)
        tb_s[...] = jnp.broadcast_to(tb, tb_s.shape)
        cnt_s[...] = jnp.broadcast_to(cnt_old + cnt_new, cnt_s.shape)
        cur_s[...] = jnp.broadcast_to(jnp.where(opens > 0, new_id, open_id), cur_s.shape)
        alloc_s[...] = jnp.broadcast_to(alloc_new, alloc_s.shape)
        meta_ref[0:1, :] = tb
        meta_ref[1:2, :] = jnp.broadcast_to(alloc_new, (1, TM))
        meta_ref[2:8, :] = jnp.zeros((6, TM), jnp.int32)

        payload = jnp.concatenate([w_lo, w_hi, jnp.zeros((LANES - 2, TM), jnp.float32)], axis=0)
        row_s[buf, :, D_MODEL:] = payload.T
        pos_v[tile % 2] = jnp.broadcast_to(pos, (8, TM))

    @pl.when((step >= ROW_BUFS) & (step < n_tiles))
    def _():
        drain(step % ROW_BUFS)
        route(step - 1)
        project(step)
        pos_copy((step - 1) % 2).start()

    @pl.when(((step >= 1) & (step < ROW_BUFS)) | (step == n_tiles))
    def _():
        route(step - 1)
        pos_copy((step - 1) % 2).start()

    @pl.when(step < ROW_BUFS)
    def _():
        project(step)

    for buf in range(ROW_BUFS):
        @pl.when((step >= 2) & ((step - 2) % ROW_BUFS == buf))
        def _():
            copy_out(buf)

    @pl.when(step == n_tiles + 1)
    def _():
        for buf in range(ROW_BUFS):
            drain(buf)

        b_iota = lax.broadcasted_iota(jnp.int32, (BUCKET_ROWS, TM), 0)
        lane_id = lax.broadcasted_iota(jnp.int32, (BUCKET_ROWS, TM), 1)
        on_lanes = lambda col: jnp.sum(jnp.where(b_iota == lane_id, col, 0).astype(jnp.float32), axis=0,
                                       keepdims=True).astype(jnp.int32)
        pos_v[0, 0:1, :] = on_lanes(cnt_s[:, 0:1])
        pos_v[0, 1:2, :] = on_lanes(cur_s[:, 0:1])
        pos_v[0, 2:3, :] = jnp.broadcast_to(alloc_s[0:1, 0:1], (1, TM))
        pos_copy(0).start()
        zero_s[...] = jnp.zeros_like(zero_s)
        pos_copy(0).wait()
        sub = SCAN_ROWS
        tail_bits = [1 << k for k in range(TM.bit_length() - 2, sub.bit_length() - 2, -1)]

        def tail_copies(act):
            def zero_rows(row, n_rows):
                return pltpu.make_async_copy(zero_s.at[pl.ds(0, n_rows), :], xs_ref.at[pl.ds(row, n_rows), :], sem_z)

            def per_bucket(b, c):
                cnt, tile = pos_sm[0, 0, b], pos_sm[0, 1, b]
                fill = cnt - (((cnt - 1) >> shift_tm) << shift_tm)
                aligned = ((fill + sub - 1) >> (sub.bit_length() - 1)) << (sub.bit_length() - 1)
                for j in range(sub - 1):
                    @pl.when((cnt > 0) & (fill + j < aligned))
                    def _():
                        act(zero_rows(tile * TM + fill + j, 1))
                tail = TM - aligned
                off = aligned
                for bit in tail_bits:
                    @pl.when((cnt > 0) & ((tail & bit) != 0))
                    def _():
                        act(zero_rows(pl.multiple_of(tile * TM + off, sub), bit))
                    off = off + (tail & bit)
                return c
            lax.fori_loop(0, N_BUCKETS, per_bucket, 0)

        def unused_tiles(act):
            def per_tile(t, c):
                for part in range(TM // zero_s.shape[0]):
                    row = pl.multiple_of(t * TM + part * zero_s.shape[0], sub)
                    act(pltpu.make_async_copy(zero_s, xs_ref.at[pl.ds(row, zero_s.shape[0]), :], sem_z))
                return c
            lax.fori_loop(pos_sm[0, 2, 0], n_tiles + N_BUCKETS, per_tile, 0)

        tail_copies(lambda cp: cp.start())
        unused_tiles(lambda cp: cp.start())
        tail_copies(lambda cp: cp.wait())
        unused_tiles(lambda cp: cp.wait())


def _out_proj(srcs, yna, yls, mod, out_g, w_bf16, ffn_g, w_router, b_router, bsz, with_ctx):
    two_src = len(srcs) == 2
    n_tiles, bj, mod_row = _tile_maps(bsz, with_ctx)
    assert n_tiles >= ROW_BUFS
    n_sorted = n_tiles + N_BUCKETS
    proj = lambda i: bj(jnp.minimum(i, n_tiles - 1))
    tile = lambda width: pl.BlockSpec((None, TM, width), lambda i: (proj(i)[0], proj(i)[1], 0))
    const2 = lambda i: (0, 0)
    if two_src:
        src_specs = [
            pl.BlockSpec((None, TM, D_MODEL), lambda i: (proj(i)[0], 0, 0)),
            pl.BlockSpec((None, TM, D_MODEL), lambda i: (proj(i)[0], jnp.maximum(proj(i)[1] - 1, 0), 0)),
        ]
    else:
        src_specs = [tile(D_MODEL)]
    wr_hi = w_router.T.astype(jnp.bfloat16)
    wr_lo = (w_router.T - wr_hi.astype(jnp.float32)).astype(jnp.bfloat16)
    wr_split = jnp.concatenate([wr_hi, wr_lo], axis=0)
    return pl.pallas_call(
        functools.partial(_out_kernel, two_src=two_src, n_tiles=n_tiles),
        out_shape=[
            jax.ShapeDtypeStruct((bsz, ROWS_PER_B if with_ctx else SEQ, D_MODEL), jnp.float32),
            jax.ShapeDtypeStruct((n_tiles, 1, TM), jnp.int32),
            jax.ShapeDtypeStruct((8, TM), jnp.int32),
            jax.ShapeDtypeStruct((n_sorted * TM, ROW_WIDTH), jnp.float32),
        ],
        grid=(n_tiles + 2,),
        in_specs=src_specs + [
            tile(NA_WIDTH),
            tile(LRU_WIDTH + SC_WIDTH),
            pl.BlockSpec((None, N_MOD, D_MODEL), lambda i: (mod_row(jnp.minimum(i, n_tiles - 1)), 0, 0)),
            pl.BlockSpec((1, D_MODEL), const2),
            pl.BlockSpec((D_MODEL, D_MODEL), const2),
            pl.BlockSpec((1, D_MODEL), const2),
            pl.BlockSpec((2 * N_EXPERTS, D_MODEL), const2),
            pl.BlockSpec((N_EXPERTS, 1), const2),
        ],
        out_specs=[
            pl.BlockSpec((None, TM, D_MODEL), lambda i: (proj(i)[0], proj(i)[1] - (0 if with_ctx else 1), 0)),
            pl.BlockSpec((None, 1, TM), lambda i: (jnp.clip(i - 1, 0, n_tiles - 1), 0, 0)),
            pl.BlockSpec((8, TM), const2),
            pl.BlockSpec(memory_space=pl.ANY),
        ],
        scratch_shapes=[
            pltpu.VMEM((BUCKET_ROWS, LANES), jnp.int32),
            pltpu.VMEM((BUCKET_ROWS, LANES), jnp.int32),
            pltpu.VMEM((8, LANES), jnp.int32),
            pltpu.VMEM((8, TM), jnp.int32),
            pltpu.VMEM((ROW_BUFS, TM, ROW_WIDTH), jnp.float32),
            pltpu.VMEM((TM // 2, ROW_WIDTH), jnp.float32),
            pltpu.VMEM((2, 8, TM), jnp.int32),
            pltpu.SMEM((2, 8, TM), jnp.int32),
            pltpu.SemaphoreType.DMA((2,)),
            pltpu.SemaphoreType.DMA((ROW_BUFS,)),
            pltpu.SemaphoreType.DMA(()),
        ],
        compiler_params=_cparams(("arbitrary",)),
        name="out_proj_route",
    )(*srcs, yna, yls, mod, out_g.reshape(1, D_MODEL), w_bf16, ffn_g.reshape(1, D_MODEL), wr_split,
      b_router.reshape(N_EXPERTS, 1))


def _moe_kernel(src_ref, dst_ref, e0_ref, e1_ref, used_ref, xs_ref, wg0, wu0, wd0, wg1, wu1, wd1, o_ref,
                wg_s, wu_s, wd_s):
    n = pl.program_id(0)

    @pl.when(n >= used_ref[0])
    def _():
        o_ref[...] = jnp.zeros_like(o_ref)
    prev = jnp.maximum(n - 1, 0)
    for k, (e_ref, g, u, d) in enumerate(((e0_ref, wg0, wu0, wd0), (e1_ref, wg1, wu1, wd1))):
        @pl.when((n == 0) | (e_ref[n] != e_ref[prev]))
        def _():
            wg_s[k] = g[...].astype(jnp.bfloat16)
            wu_s[k] = u[...].astype(jnp.bfloat16)
            wd_s[k] = d[...].astype(jnp.bfloat16)

    @pl.when(n < used_ref[0])
    def _():
        xb = xs_ref[:, :D_MODEL].astype(jnp.bfloat16)
        w_lo, w_hi = xs_ref[:, D_MODEL:D_MODEL + 1], xs_ref[:, D_MODEL + 1:D_MODEL + 2]
        slot0_is_lo = e0_ref[n] < e1_ref[n]
        weights = (jnp.where(slot0_is_lo, w_lo, w_hi), jnp.where(slot0_is_lo, w_hi, w_lo))
        out = jnp.zeros((TM, D_MODEL), jnp.float32)
        for k, wk in enumerate(weights):
            gate = jnp.dot(xb, wg_s[k], preferred_element_type=jnp.float32)
            up = jnp.dot(xb, wu_s[k], preferred_element_type=jnp.float32)
            hid = (gate * jax.nn.sigmoid(gate)) * up
            out = out + wk * jnp.dot(hid.astype(jnp.bfloat16), wd_s[k], preferred_element_type=jnp.float32)
        o_ref[...] = out


def _moe(sched, xs, w_gate, w_up, w_down, layer):
    n_tiles = xs.shape[0] // TM
    src, dst, e0, e1, used = sched
    first = lambda n, s, d, a, b, u: (layer, a[n], 0, 0)
    second = lambda n, s, d, a, b, u: (layer, b[n], 0, 0)
    gate_spec = lambda m: pl.BlockSpec((None, None, D_MODEL, D_EXPERT), m)
    down_spec = lambda m: pl.BlockSpec((None, None, D_EXPERT, D_MODEL), m)
    return pl.pallas_call(
        _moe_kernel,
        out_shape=jax.ShapeDtypeStruct((n_tiles * TM, D_MODEL), jnp.float32),
        grid_spec=pltpu.PrefetchScalarGridSpec(
            num_scalar_prefetch=5,
            grid=(n_tiles,),
            in_specs=[
                pl.BlockSpec((TM, ROW_WIDTH), lambda n, s, d, a, b, u: (s[n], 0)),
                gate_spec(first), gate_spec(first), down_spec(first),
                gate_spec(second), gate_spec(second), down_spec(second),
            ],
            out_specs=pl.BlockSpec((TM, D_MODEL), lambda n, s, d, a, b, u: (d[n], 0)),
            scratch_shapes=[
                pltpu.VMEM((2, D_MODEL, D_EXPERT), jnp.bfloat16),
                pltpu.VMEM((2, D_MODEL, D_EXPERT), jnp.bfloat16),
                pltpu.VMEM((2, D_EXPERT, D_MODEL), jnp.bfloat16),
            ],
        ),
        compiler_params=_cparams(("arbitrary",)),
        name="moe_experts",
    )(src, dst, e0, e1, used, xs, w_gate, w_up, w_down, w_gate, w_up, w_down)


def _schedule(meta, n_tiles):
    visit = jnp.array([0, 2, 3, 1, 4, 5], jnp.int32)
    slot0 = jnp.array([0, 2, 3, 2, 3, 3], jnp.int32)
    slot1 = jnp.array([1, 0, 0, 1, 1, 2], jnp.int32)
    tile_bucket = meta[0, :n_tiles]
    used = meta[1, 0]
    ids = jnp.arange(n_tiles, dtype=jnp.int32)
    key = (tile_bucket // N_PAIRS) * N_PAIRS + visit[tile_bucket % N_PAIRS]
    order = jnp.argsort(jnp.where(ids < used, key, N_BUCKETS), stable=True).astype(jnp.int32)
    src = jnp.where(ids < used, order, order[jnp.maximum(used - 1, 0)])
    bucket = tile_bucket[src]
    group, pair = bucket // N_PAIRS, bucket % N_PAIRS
    return src, order, group * E_PER_GROUP + slot0[pair], group * E_PER_GROUP + slot1[pair], used.reshape(1)


def _combine_kernel(pos_ref, x_ref, modc_ref, modb_ref, fg_ref, ys_ref, o_ref, buf, sem, *, final, sub, with_ctx):
    i = pl.program_id(0)
    n = pl.num_programs(0)
    rows = sub * TM

    def row_copy(src_row, slot, k):
        return pltpu.make_async_copy(ys_ref.at[pl.ds(src_row, 1), :], buf.at[slot, pl.ds(k, 1), :], sem.at[slot])

    def issue(step, slot):
        for k in range(rows):
            row_copy(pos_ref[step * rows + k], slot, k).start(priority=k % 2)

    @pl.when(i == 0)
    def _():
        issue(0, 0)

    for nxt in range(2):
        @pl.when((i + 1 < n) & ((i + 1) % 2 == nxt))
        def _():
            issue(i + 1, nxt)

    slot = i % 2

    pltpu.make_async_copy(ys_ref.at[pl.ds(0, rows), :], buf.at[slot], sem.at[slot]).wait()

    first = i % (TILES_PER_B // sub) == 0
    for s in range(sub):
        tile = slice(s * TM, (s + 1) * TM)
        gate = modb_ref[5:6, :]
        if with_ctx and s == 0:
            gate = jnp.where(first, modc_ref[5:6, :], gate)
        x_new = x_ref[tile, :] + gate * buf[slot, tile, :]
        if final:
            x_new = _rms(x_new, fg_ref[...])
        o_ref[tile, :] = x_new


COMBINE_SUB = 3


def _combine(pos, x_all, mod, final_g, ys, bsz, with_ctx, final):
    sub = COMBINE_SUB if with_ctx else 1
    steps_per_b = (TILES_PER_B if with_ctx else LAT_TILES_PER_B) // sub
    rows = sub * TM
    blk = lambda i, p: (i // steps_per_b, i % steps_per_b, 0)
    return pl.pallas_call(
        functools.partial(_combine_kernel, final=final, sub=sub, with_ctx=with_ctx),
        out_shape=jax.ShapeDtypeStruct(x_all.shape, jnp.float32),
        grid_spec=pltpu.PrefetchScalarGridSpec(
            num_scalar_prefetch=1,
            grid=(bsz * steps_per_b,),
            in_specs=[
                pl.BlockSpec((None, rows, D_MODEL), blk),
                pl.BlockSpec((None, N_MOD, D_MODEL), lambda i, p: (bsz, 0, 0)),
                pl.BlockSpec((None, N_MOD, D_MODEL), lambda i, p: (i // steps_per_b, 0, 0)),
                pl.BlockSpec((1, D_MODEL), lambda i, p: (0, 0)),
                pl.BlockSpec(memory_space=pl.ANY),
            ],
            out_specs=pl.BlockSpec((None, rows, D_MODEL), blk),
            scratch_shapes=[pltpu.VMEM((2, rows, D_MODEL), jnp.float32), pltpu.SemaphoreType.DMA((2,))],
        ),
        compiler_params=_cparams(("arbitrary",)),
        name="combine",
    )(pos.reshape(-1), x_all, mod, mod, final_g.reshape(1, D_MODEL), ys)


def kernel(x, c, ctx, c_ctx, w_ada, b_ada, norm_mix_g, w_in, lru_conv_w, lru_conv_b, rg_w, rg_b, rg_lam, na_rpb,
           sc_conv_w, sc_conv_b, mix_out_g, w_out, norm_ffn_g, w_router, b_router, w_gate, w_up, w_down, final_g):
    bsz = x.shape[0]
    mod_rows = -(-(bsz + 1) // 8) * 8
    c_all = jnp.zeros((mod_rows, D_MODEL), jnp.float32).at[:bsz].set(c).at[bsz].set(c_ctx)
    mods = _ada(c_all, w_ada, b_ada).reshape(DEPTH, mod_rows, N_MOD, D_MODEL)

    x_all = None
    out = None
    for l in range(DEPTH):
        need_ctx = l < DEPTH - 1
        mod = mods[l]
        srcs = (ctx, x) if l == 0 else (x_all,)
        qkv, rest = _in_proj(srcs, norm_mix_g[l], mod, w_in[l].astype(jnp.bfloat16), bsz)
        yna = _attention(qkv, _attn_bias_table(na_rpb[l]), bsz, need_ctx)
        yls = _lru_sconv(rest, lru_conv_w[l], lru_conv_b[l], rg_w[l], rg_b[l], rg_lam[l], sc_conv_w[l],
                         sc_conv_b[l], bsz, need_ctx)
        x_mid, pos, meta, xs = _out_proj(srcs, yna, yls, mod, mix_out_g[l], w_out[l].astype(jnp.bfloat16),
                                         norm_ffn_g[l], w_router, b_router, bsz, need_ctx)
        ys = _moe(_schedule(meta, xs.shape[0] // TM), xs, w_gate, w_up, w_down, l)
        res = _combine(pos, x_mid, mod, final_g, ys, bsz, need_ctx, final=not need_ctx)
        if need_ctx:
            x_all = res
        else:
            out = res
    return out
```

```python
import functools

import jax
import jax.numpy as jnp
from jax import lax
from jax.experimental import pallas as pl
from jax.experimental.pallas import tpu as pltpu

D_MODEL = 1024
SEQ = 2048
CTX_LEN = 256
ROWS_PER_B = CTX_LEN + SEQ
DEPTH = 2
N_MOD = 6
EPS = 1e-6
NEG_INF = -1e30

GRID_W = 64
GRID_ROWS = SEQ // GRID_W
HEAD_DIM = 64
NA_WIDTH = 512
NA_HEADS = 8
LRU_WIDTH = 256
LRU_HEADS = 4
LRU_BLOCK = 64
SC_WIDTH = 256
QKV_WIDTH = 3 * NA_WIDTH
REST_WIDTH = 2 * LRU_WIDTH + 3 * SC_WIDTH
IN_WIDTH = QKV_WIDTH + REST_WIDTH
RG_C = 8.0
WIN_R = 8
WIN_C = 16
N_EXPERTS = 16
N_GROUPS = 4
E_PER_GROUP = 4
N_PAIRS = 6
N_BUCKETS = N_GROUPS * N_PAIRS
D_EXPERT = 512

TM = 256
TILES_PER_B = ROWS_PER_B // TM
LAT_TILES_PER_B = SEQ // TM
LANES = 128
BUCKET_ROWS = 32
ROW_WIDTH = D_MODEL + LANES
HEADS_PER_STACK = 4
STACK_W = HEADS_PER_STACK * HEAD_DIM
SCAN_ROWS = 8
VMEM_LIMIT = 56 * 1024 * 1024

_HI = lax.Precision.HIGHEST
_NT = (((1,), (1,)), ((), ()))


def _cparams(sem):
    return pltpu.CompilerParams(dimension_semantics=sem, vmem_limit_bytes=VMEM_LIMIT)


def _rms(v, g):
    return v * lax.rsqrt(jnp.mean(v * v, axis=-1, keepdims=True) + EPS) * g


def _ada_kernel(c_ref, w_ref, b_ref, o_ref):
    cond = c_ref[...]
    cond = cond * jax.nn.sigmoid(cond)
    w = w_ref[0]
    c_hi = cond.astype(jnp.bfloat16)
    c_lo = (cond - c_hi.astype(jnp.float32)).astype(jnp.bfloat16)
    w_hi = w.astype(jnp.bfloat16)
    w_lo = (w - w_hi.astype(jnp.float32)).astype(jnp.bfloat16)
    rows = cond.shape[0]
    part = jnp.dot(jnp.concatenate([c_hi, c_lo], axis=0), w_hi, preferred_element_type=jnp.float32)
    o_ref[0] = (part[:rows] + part[rows:] + jnp.dot(c_hi, w_lo, preferred_element_type=jnp.float32)) + b_ref[0]


def _ada(c_all, w_ada, b_ada):
    depth, _, width = w_ada.shape
    rows = c_all.shape[0]
    tn = 1536
    return pl.pallas_call(
        _ada_kernel,
        out_shape=jax.ShapeDtypeStruct((depth, rows, width), jnp.float32),
        grid=(depth, width // tn),
        in_specs=[
            pl.BlockSpec((rows, D_MODEL), lambda l, n: (0, 0)),
            pl.BlockSpec((1, D_MODEL, tn), lambda l, n: (l, 0, n)),
            pl.BlockSpec((1, 1, tn), lambda l, n: (l, 0, n)),
        ],
        out_specs=pl.BlockSpec((1, rows, tn), lambda l, n: (l, 0, n)),
        compiler_params=_cparams(("arbitrary", "arbitrary")),
        name="ada",
    )(c_all, w_ada, b_ada.reshape(depth, 1, width))


def _tile_maps(bsz, with_ctx):
    per_b = TILES_PER_B if with_ctx else LAT_TILES_PER_B
    off = 0 if with_ctx else 1

    def bj(i):
        return i // per_b, i % per_b + off

    def mod_row(i):
        b, j = bj(i)
        return jnp.where(j == 0, bsz, b)

    return per_b * bsz, bj, mod_row


IN_SUB = 3
IN_STEPS_PER_B = TILES_PER_B // IN_SUB


def _in_kernel(*refs, two_src):
    if two_src:
        ctx_ref, *x_refs = refs[:1 + IN_SUB]
        g_ref, modc_ref, modb_ref, w_ref, qkv_ref, rest_ref = refs[1 + IN_SUB:]
    else:
        x_ref, g_ref, modc_ref, modb_ref, w_ref, qkv_ref, rest_ref = refs
    first = pl.program_id(0) % IN_STEPS_PER_B == 0
    for s in range(IN_SUB):
        rows = slice(s * TM, (s + 1) * TM)
        if two_src:
            x = jnp.where(first, ctx_ref[...], x_refs[0][...]) if s == 0 else x_refs[s][...]
        else:
            x = x_ref[rows, :]
        if s == 0:
            shift = jnp.where(first, modc_ref[0:1, :], modb_ref[0:1, :])
            scale = jnp.where(first, modc_ref[1:2, :], modb_ref[1:2, :])
        else:
            shift, scale = modb_ref[0:1, :], modb_ref[1:2, :]
        hb = (_rms(x, g_ref[...]) * (1.0 + scale) + shift).astype(jnp.bfloat16)
        qkv_ref[rows, :] = jnp.dot(hb, w_ref[:, :QKV_WIDTH], preferred_element_type=jnp.float32).astype(jnp.bfloat16)
        rest_ref[rows, :] = jnp.dot(hb, w_ref[:, QKV_WIDTH:], preferred_element_type=jnp.float32)


def _in_proj(srcs, g, mod, w_bf16, bsz):
    two_src = len(srcs) == 2
    bj = lambda i: (i // IN_STEPS_PER_B, i % IN_STEPS_PER_B)
    rows = IN_SUB * TM
    if two_src:
        lat = lambda s: pl.BlockSpec(
            (None, TM, D_MODEL), lambda i: (bj(i)[0], jnp.maximum(IN_SUB * bj(i)[1] + s - 1, 0), 0))
        src_specs = [pl.BlockSpec((None, TM, D_MODEL), lambda i: (bj(i)[0], 0, 0))] + [lat(s) for s in range(IN_SUB)]
        srcs = (srcs[0],) + (srcs[1],) * IN_SUB
    else:
        src_specs = [pl.BlockSpec((None, rows, D_MODEL), lambda i: (bj(i)[0], bj(i)[1], 0))]
    return pl.pallas_call(
        functools.partial(_in_kernel, two_src=two_src),
        out_shape=[
            jax.ShapeDtypeStruct((bsz, ROWS_PER_B, QKV_WIDTH), jnp.bfloat16),
            jax.ShapeDtypeStruct((bsz, ROWS_PER_B, REST_WIDTH), jnp.float32),
        ],
        grid=(bsz * IN_STEPS_PER_B,),
        in_specs=src_specs + [
            pl.BlockSpec((1, D_MODEL), lambda i: (0, 0)),
            pl.BlockSpec((None, N_MOD, D_MODEL), lambda i: (bsz, 0, 0)),
            pl.BlockSpec((None, N_MOD, D_MODEL), lambda i: (bj(i)[0], 0, 0)),
            pl.BlockSpec((D_MODEL, IN_WIDTH), lambda i: (0, 0)),
        ],
        out_specs=[
            pl.BlockSpec((None, rows, QKV_WIDTH), lambda i: (bj(i)[0], bj(i)[1], 0)),
            pl.BlockSpec((None, rows, REST_WIDTH), lambda i: (bj(i)[0], bj(i)[1], 0)),
        ],
        compiler_params=_cparams(("arbitrary",)),
        name="in_proj",
    )(*srcs, g.reshape(1, D_MODEL), mod, mod, w_bf16)


def _attn_kernel(q_ref, k_ref, v_ref, bias_ref, o_ref, *, need_ctx):
    lane_head = lax.broadcasted_iota(jnp.int32, (1, STACK_W), 1) // HEAD_DIM
    n_stacks = NA_WIDTH // STACK_W

    def stack_q(qg):
        zero = jnp.zeros_like(qg)
        return jnp.concatenate([jnp.where(lane_head == h, qg, zero) for h in range(HEADS_PER_STACK)], axis=0)

    def unstack(o):
        out = jnp.zeros((GRID_W, STACK_W), jnp.float32)
        for h in range(HEADS_PER_STACK):
            out = out + jnp.where(lane_head == h, o[h * GRID_W:(h + 1) * GRID_W], 0.0)
        return out

    def attend(q_rows, s, local):
        cols = slice(s * STACK_W, (s + 1) * STACK_W)
        qg = q_ref[pl.ds(q_rows, GRID_W), cols] * jnp.bfloat16(HEAD_DIM ** -0.5)
        qs = stack_q(qg)
        kc = k_ref[0:CTX_LEN, cols]
        vc = v_ref[0:CTX_LEN, cols]
        s_ctx = lax.dot_general(qs, kc, _NT, preferred_element_type=jnp.float32)
        m = jnp.max(s_ctx, axis=-1, keepdims=True)
        if local is not None:
            k_rows, delta = local
            kw = k_ref[pl.ds(k_rows, WIN_R * GRID_W), cols]
            vw = v_ref[pl.ds(k_rows, WIN_R * GRID_W), cols]
            heads = slice(s * HEADS_PER_STACK, (s + 1) * HEADS_PER_STACK)
            bias = jnp.concatenate(
                [bias_ref[heads, pl.ds(2 * m - delta + WIN_R - 1, 1)].reshape(HEADS_PER_STACK * GRID_W, 2 * GRID_W)
                 for m in range(WIN_R // 2)], axis=-1)
            s_loc = lax.dot_general(qs, kw, _NT, preferred_element_type=jnp.float32) + bias
            m = jnp.maximum(m, jnp.max(s_loc, axis=-1, keepdims=True))
            p_loc = jnp.exp(s_loc - m)
        p_ctx = jnp.exp(s_ctx - m)
        denom = jnp.sum(p_ctx, axis=-1, keepdims=True)
        o = jnp.dot(p_ctx.astype(jnp.bfloat16), vc, preferred_element_type=jnp.float32)
        if local is not None:
            denom = denom + jnp.sum(p_loc, axis=-1, keepdims=True)
            o = o + jnp.dot(p_loc.astype(jnp.bfloat16), vw, preferred_element_type=jnp.float32)
        o_ref[pl.ds(q_rows, GRID_W), cols] = unstack(o / denom).astype(o_ref.dtype)

    def lat_row(r, carry):
        r0 = jnp.clip(r - WIN_R // 2, 0, GRID_ROWS - WIN_R)
        q_rows = pl.multiple_of(CTX_LEN + r * GRID_W, GRID_W)
        k_rows = pl.multiple_of(CTX_LEN + r0 * GRID_W, GRID_W)
        for s in range(n_stacks):
            attend(q_rows, s, (k_rows, r - r0))
        return carry

    lax.fori_loop(0, GRID_ROWS, lat_row, 0, unroll=16)

    if need_ctx:
        def ctx_chunk(cq, carry):
            q_rows = pl.multiple_of(cq * GRID_W, GRID_W)
            for s in range(n_stacks):
                attend(q_rows, s, None)
            return carry

        lax.fori_loop(0, CTX_LEN // GRID_W, ctx_chunk, 0, unroll=2)
    else:
        o_ref[0:CTX_LEN, :] = jnp.zeros((CTX_LEN, NA_WIDTH), o_ref.dtype)


def _attn_bias_table(rpb):
    n_rel_c = 2 * WIN_C - 1
    lead = GRID_W - WIN_C
    padded = jnp.pad(rpb, ((0, 0), (0, 0), (lead, 2 * GRID_W - lead - n_rel_c)))
    skew = jnp.tile(padded, (1, 1, GRID_W))[..., :GRID_W * (2 * GRID_W - 1)]
    skew = skew.reshape(NA_HEADS, 2 * WIN_R - 1, GRID_W, 2 * GRID_W - 1)
    toeplitz = skew[..., GRID_W - 1:]
    q_col = jnp.arange(GRID_W)[:, None]
    k_col = jnp.arange(GRID_W)[None, :]
    c_start = jnp.clip(q_col - WIN_C // 2, 0, GRID_W - WIN_C)
    ok = (k_col >= c_start) & (k_col < c_start + WIN_C)
    toeplitz = jnp.where(ok, toeplitz, NEG_INF)
    return jnp.concatenate([toeplitz[:, :-1], toeplitz[:, 1:]], axis=-1)


def _attention(qkv, bias, bsz, need_ctx):
    return pl.pallas_call(
        functools.partial(_attn_kernel, need_ctx=need_ctx),
        out_shape=jax.ShapeDtypeStruct((bsz, ROWS_PER_B, NA_WIDTH), jnp.bfloat16),
        grid=(bsz,),
        in_specs=[
            pl.BlockSpec((None, ROWS_PER_B, NA_WIDTH), lambda b: (b, 0, 0)),
            pl.BlockSpec((None, ROWS_PER_B, NA_WIDTH), lambda b: (b, 0, 1)),
            pl.BlockSpec((None, ROWS_PER_B, NA_WIDTH), lambda b: (b, 0, 2)),
            pl.BlockSpec((NA_HEADS, 2 * WIN_R - 2, GRID_W, 2 * GRID_W), lambda b: (0, 0, 0, 0)),
        ],
        out_specs=pl.BlockSpec((None, ROWS_PER_B, NA_WIDTH), lambda b: (b, 0, 0)),
        compiler_params=_cparams(("arbitrary",)),
        name="attention",
    )(qkv, qkv, qkv, bias)


CONV_PAD = 8


def _pad_base(start):
    return start + CONV_PAD * (1 if start == 0 else 2)


def _dwconv(pad_s, start, length, w_ref, b_ref, left):
    base = _pad_base(start)
    width = w_ref.shape[0]
    y = pad_s[base - left:base - left + length, :] * w_ref[0:1, :] + b_ref[...]
    for k in range(1, width):
        y = y + pad_s[base + k - left:base + k - left + length, :] * w_ref[k:k + 1, :]
    return y


def _lru_kernel(rest_ref, lcw_ref, lcb_ref, wbd_ref, rgb_ref, lam_ref, scw_ref, scb_ref, o_ref,
                xc_s, a_s, b_s, y_s, pad_s, *, need_ctx):
    col_rx, col_rg, col_sb, col_sc, col_sx = (k * LRU_WIDTH for k in range(5))
    segments = ((0, CTX_LEN), (CTX_LEN, SEQ))

    for start, length in segments:
        base = _pad_base(start)
        pad_s[base - CONV_PAD:base, :] = jnp.zeros((CONV_PAD, LRU_WIDTH), jnp.float32)
    pad_s[pad_s.shape[0] - CONV_PAD:, :] = jnp.zeros((CONV_PAD, LRU_WIDTH), jnp.float32)

    for start, length in segments:
        base = _pad_base(start)
        pad_s[base:base + length, :] = rest_ref[start:start + length, col_rx:col_rx + LRU_WIDTH]
    for start, length in segments:
        xc_s[start:start + length, :] = _dwconv(pad_s, start, length, lcw_ref, lcb_ref, 2)

    def coeffs(d, start, length):
        chunk = 256
        sp = jax.nn.softplus(-lam_ref[d:d + 1, :])
        for c0 in range(0, length, chunk):
            xc = xc_s[start + c0:start + c0 + chunk, :]
            pre = jnp.dot(xc.astype(jnp.bfloat16), wbd_ref[:, 2 * d * LRU_WIDTH:(2 * d + 2) * LRU_WIDTH],
                          preferred_element_type=jnp.float32) + rgb_ref[:, 2 * d * LRU_WIDTH:(2 * d + 2) * LRU_WIDTH]
            gate_r = jax.nn.sigmoid(pre[:, :LRU_WIDTH])
            gate_i = jax.nn.sigmoid(pre[:, LRU_WIDTH:])
            log_a = -RG_C * gate_r * sp
            a = jnp.exp(log_a)
            bb = jnp.sqrt(1.0 - a * a) * (gate_i * xc)
            a_s[start + c0:start + c0 + chunk, :] = a
            b_s[start + c0:start + c0 + chunk, :] = bb

    n_ctx_blocks = CTX_LEN // SCAN_ROWS
    n_blocks = ROWS_PER_B // SCAN_ROWS
    sub = lax.broadcasted_iota(jnp.int32, (SCAN_ROWS, LRU_WIDTH), 0)

    def scan(reverse, accumulate):
        def block(i, h_in):
            if reverse:
                blk = jnp.where(i < n_ctx_blocks, n_ctx_blocks - 1 - i, n_blocks + n_ctx_blocks - 1 - i)
            else:
                blk = i
            rows = pl.ds(pl.multiple_of(blk * SCAN_ROWS, SCAN_ROWS), SCAN_ROWS)
            a = a_s[rows, :]
            b = b_s[rows, :]
            for sh in (1, 2, 4):
                if reverse:
                    keep = sub < SCAN_ROWS - sh
                    a_n = pltpu.roll(a, SCAN_ROWS - sh, axis=0)
                    b_n = pltpu.roll(b, SCAN_ROWS - sh, axis=0)
                else:
                    keep = sub >= sh
                    a_n = pltpu.roll(a, sh, axis=0)
                    b_n = pltpu.roll(b, sh, axis=0)
                b = jnp.where(keep, a * b_n + b, b)
                a = jnp.where(keep, a * a_n, a)
            h = a * h_in + b
            y_s[rows, :] = y_s[rows, :] + h if accumulate else h
            return h[0:1, :] if reverse else h[SCAN_ROWS - 1:SCAN_ROWS, :]

        lax.fori_loop(0, n_blocks, block, jnp.zeros((1, LRU_WIDTH), jnp.float32), unroll=8)

    for d, reverse in enumerate((False, True)):
        for start, length in segments:
            coeffs(d, start, length)
        scan(reverse, accumulate=d > 0)

    out_segments = segments if need_ctx else segments[1:]
    for start, length in out_segments:
        rows = slice(start, start + length)
        y_lru = y_s[rows, :] * jax.nn.gelu(rest_ref[rows, col_rg:col_rg + LRU_WIDTH])
        o_ref[rows, 0:LRU_WIDTH] = y_lru.astype(o_ref.dtype)
        base = _pad_base(start)
        pad_s[base:base + length, :] = (rest_ref[rows, col_sc:col_sc + SC_WIDTH]
                                        * rest_ref[rows, col_sx:col_sx + SC_WIDTH])
        y_sc = rest_ref[rows, col_sb:col_sb + SC_WIDTH] * _dwconv(pad_s, start, length, scw_ref, scb_ref, 1)
        o_ref[rows, LRU_WIDTH:LRU_WIDTH + SC_WIDTH] = y_sc.astype(o_ref.dtype)
    if not need_ctx:
        o_ref[0:CTX_LEN, :] = jnp.zeros((CTX_LEN, LRU_WIDTH + SC_WIDTH), o_ref.dtype)


def _block_diag_gates(rg_w):
    eye = jnp.eye(LRU_HEADS, dtype=rg_w.dtype)
    full = jnp.einsum('dgncm,nk->dgnckm', rg_w, eye)
    full = full.reshape(2, 2, LRU_WIDTH, LRU_WIDTH)
    return full.transpose(2, 0, 1, 3).reshape(LRU_WIDTH, 4 * LRU_WIDTH)


def _lru_sconv(rest, lcw, lcb, rg_w, rg_b, rg_lam, scw, scb, bsz, need_ctx):
    wbd = _block_diag_gates(rg_w).astype(jnp.bfloat16)
    const2 = lambda b: (0, 0)
    return pl.pallas_call(
        functools.partial(_lru_kernel, need_ctx=need_ctx),
        out_shape=jax.ShapeDtypeStruct((bsz, ROWS_PER_B, LRU_WIDTH + SC_WIDTH), jnp.bfloat16),
        grid=(bsz,),
        in_specs=[
            pl.BlockSpec((None, ROWS_PER_B, REST_WIDTH), lambda b: (b, 0, 0)),
            pl.BlockSpec(lcw.shape, const2),
            pl.BlockSpec((1, LRU_WIDTH), const2),
            pl.BlockSpec((LRU_WIDTH, 4 * LRU_WIDTH), const2),
            pl.BlockSpec((1, 4 * LRU_WIDTH), const2),
            pl.BlockSpec((2, LRU_WIDTH), const2),
            pl.BlockSpec(scw.shape, const2),
            pl.BlockSpec((1, SC_WIDTH), const2),
        ],
        out_specs=pl.BlockSpec((None, ROWS_PER_B, LRU_WIDTH + SC_WIDTH), lambda b: (b, 0, 0)),
        scratch_shapes=[
            pltpu.VMEM((ROWS_PER_B, LRU_WIDTH), jnp.float32),
            pltpu.VMEM((ROWS_PER_B, LRU_WIDTH), jnp.float32),
            pltpu.VMEM((ROWS_PER_B, LRU_WIDTH), jnp.float32),
            pltpu.VMEM((ROWS_PER_B, LRU_WIDTH), jnp.float32),
            pltpu.VMEM((ROWS_PER_B + 3 * CONV_PAD, LRU_WIDTH), jnp.float32),
        ],
        compiler_params=_cparams(("arbitrary",)),
        name="lru_sconv",
    )(rest, lcw, lcb.reshape(1, LRU_WIDTH), wbd, rg_b.reshape(1, 4 * LRU_WIDTH), rg_lam, scw,
      scb.reshape(1, SC_WIDTH))


def _route(et):
    pe = [et[e:e + 1, :] for e in range(N_EXPERTS)]

    def top2_sum(v):
        best = v[0] + v[1]
        for a in range(E_PER_GROUP):
            for b in range(a + 1, E_PER_GROUP):
                if (a, b) != (0, 1):
                    best = jnp.maximum(best, v[a] + v[b])
        return best

    score = [top2_sum(pe[g * E_PER_GROUP:(g + 1) * E_PER_GROUP]) for g in range(N_GROUPS)]
    g_best, g_sel = score[0], jnp.zeros((1, TM), jnp.int32)
    for g in range(1, N_GROUPS):
        upd = score[g] > g_best
        g_sel = jnp.where(upd, g, g_sel)
        g_best = jnp.where(upd, score[g], g_best)
    p_in = []
    for k in range(E_PER_GROUP):
        v = pe[k]
        for g in range(1, N_GROUPS):
            v = jnp.where(g_sel == g, pe[g * E_PER_GROUP + k], v)
        p_in.append(v)
    m1, i1 = p_in[0], jnp.zeros((1, TM), jnp.int32)
    for k in range(1, E_PER_GROUP):
        upd = p_in[k] > m1
        i1 = jnp.where(upd, k, i1)
        m1 = jnp.where(upd, p_in[k], m1)
    m2, i2 = jnp.full((1, TM), -1.0, jnp.float32), jnp.zeros((1, TM), jnp.int32)
    for k in range(E_PER_GROUP):
        cand = jnp.where(i1 == k, -2.0, p_in[k])
        upd = cand > m2
        i2 = jnp.where(upd, k, i2)
        m2 = jnp.where(upd, cand, m2)
    lo, hi = jnp.minimum(i1, i2), jnp.maximum(i1, i2)
    pair = jnp.where(lo == 0, hi - 1, jnp.where(lo == 1, hi + 1, N_PAIRS - 1))
    bucket = g_sel * N_PAIRS + pair
    w1 = m1 / (m1 + m2)
    w2 = m2 / (m1 + m2)
    return bucket, jnp.where(i1 < i2, w1, w2), jnp.where(i1 < i2, w2, w1)


ROW_BUFS = 4


def _out_kernel(*refs, two_src, n_tiles):
    if two_src:
        ctx_ref, x_ref, *rest = refs
    else:
        x_ref, *rest = refs
    (yna_ref, yls_ref, mod_ref, og_ref, w_ref, fg_ref, wrt_ref, brt_ref, xo_ref, pos_ref, meta_ref, xs_ref,
     cnt_s, cur_s, alloc_s, tb_s, row_s, zero_s, pos_v, pos_sm, sem_p, sem_r, sem_z) = rest
    step = pl.program_id(0)
    shift_tm = TM.bit_length() - 1

    def row_copy(buf, k, dst_row):
        return pltpu.make_async_copy(row_s.at[buf, pl.ds(k, 1), :], xs_ref.at[pl.ds(dst_row, 1), :], sem_r.at[buf])

    def pos_copy(buf):
        return pltpu.make_async_copy(pos_v.at[buf], pos_sm.at[buf], sem_p.at[buf])

    def copy_out(buf):
        pbuf = buf % 2
        pos_copy(pbuf).wait()
        for k in range(TM):
            row_copy(buf, k, pos_sm[pbuf, 0, k]).start(priority=k % 2)

    def drain(buf):
        pltpu.make_async_copy(row_s.at[buf], xs_ref.at[pl.ds(0, TM), :], sem_r.at[buf]).wait()

    @pl.when(step == 0)
    def _():
        cnt_s[...] = jnp.zeros_like(cnt_s)
        cur_s[...] = jnp.zeros_like(cur_s)
        alloc_s[...] = jnp.zeros_like(alloc_s)
        tb_s[...] = jnp.zeros_like(tb_s)

    def project(tile):
        if two_src:
            x = jnp.where(tile % TILES_PER_B == 0, ctx_ref[...], x_ref[...])
        else:
            x = x_ref[...]
        yna = yna_ref[...].astype(jnp.float32)
        yls = yls_ref[...].astype(jnp.float32)
        merged = jnp.concatenate([
            _rms(yna, og_ref[:, :NA_WIDTH]),
            _rms(yls[:, :LRU_WIDTH], og_ref[:, NA_WIDTH:NA_WIDTH + LRU_WIDTH]),
            _rms(yls[:, LRU_WIDTH:], og_ref[:, NA_WIDTH + LRU_WIDTH:]),
        ], axis=-1).astype(jnp.bfloat16)
        y = jnp.dot(merged, w_ref[...], preferred_element_type=jnp.float32)
        x_new = x + mod_ref[2:3, :] * y
        xo_ref[...] = x_new
        h2 = _rms(x_new, fg_ref[...]) * (1.0 + mod_ref[4:5, :]) + mod_ref[3:4, :]
        row_s[tile % ROW_BUFS, :, :D_MODEL] = h2

    def route(tile):
        buf = tile % ROW_BUFS
        h2 = row_s[buf, :, :D_MODEL]
        h_hi = h2.astype(jnp.bfloat16)
        h_lo = (h2 - h_hi.astype(jnp.float32)).astype(jnp.bfloat16)
        part = lax.dot_general(wrt_ref[...], h_hi, _NT, preferred_element_type=jnp.float32)
        lt = (part[:N_EXPERTS] + part[N_EXPERTS:] + brt_ref[...]
              + lax.dot_general(wrt_ref[:N_EXPERTS, :], h_lo, _NT, preferred_element_type=jnp.float32))
        bucket, w_lo, w_hi = _route(jnp.exp(lt - jnp.max(lt, axis=0, keepdims=True)))

        b_iota = lax.broadcasted_iota(jnp.int32, (BUCKET_ROWS, TM), 0)
        onehot = b_iota == bucket
        tri = (lax.broadcasted_iota(jnp.int32, (TM, TM), 0) <= lax.broadcasted_iota(jnp.int32, (TM, TM), 1))
        as_bf16 = lambda mask: jnp.where(mask, 1.0, 0.0).astype(jnp.bfloat16)
        cum = jnp.dot(as_bf16(onehot), as_bf16(tri), preferred_element_type=jnp.float32)
        cnt_new = cum[:, TM - 1:TM].astype(jnp.int32)
        cnt_old = cnt_s[:, 0:1]
        open_id = cur_s[:, 0:1]
        alloc = alloc_s[0:1, 0:1]
        shift = TM.bit_length() - 1
        q_last = (cnt_old + cnt_new - 1) >> shift
        q_prev = (cnt_old - 1) >> shift
        opens = jnp.where(cnt_new > 0, q_last - q_prev, 0)
        lower = (lax.broadcasted_iota(jnp.int32, (BUCKET_ROWS, BUCKET_ROWS), 1)
                 < lax.broadcasted_iota(jnp.int32, (BUCKET_ROWS, BUCKET_ROWS), 0))
        opens_b = jnp.broadcast_to(opens, (BUCKET_ROWS, LANES)).astype(jnp.float32).astype(jnp.bfloat16)
        before = jnp.dot(as_bf16(lower), opens_b, preferred_element_type=jnp.float32)[:, 0:1].astype(jnp.int32)
        new_id = alloc + before
        rank = cnt_old + cum.astype(jnp.int32) - 1
        tile_id = jnp.where((opens > 0) & ((rank >> shift) == q_last), new_id, open_id)
        slot = tile_id * TM + (rank & (TM - 1))
        pos = jnp.sum(jnp.where(onehot, slot, 0).astype(jnp.float32), axis=0, keepdims=True).astype(jnp.int32)
        pos_ref[...] = pos

        lane_id = lax.broadcasted_iota(jnp.int32, (BUCKET_ROWS, TM), 1)
        opened_here = (opens > 0) & (new_id == lane_id)
        opened_bucket = jnp.max(jnp.where(opened_here, b_iota, -1).astype(jnp.float32), axis=0,
                                keepdims=True).astype(jnp.int32)
        tb = jnp.where(opened_bucket >= 0, opened_bucket, tb_s[0:1, :])
        alloc_new = alloc + jnp.sum(opens.astype(jnp.float32), axis=0, keepdims=True).astype(jnp.int32)
        tb_s[...] = jnp.broadcast_to(tb, tb_s.shape)
        cnt_s[...] = jnp.broadcast_to(cnt_old + cnt_new, cnt_s.shape)
        cur_s[...] = jnp.broadcast_to(jnp.where(opens > 0, new_id, open_id), cur_s.shape)
        alloc_s[...] = jnp.broadcast_to(alloc_new, alloc_s.shape)
        meta_ref[0:1, :] = tb
        meta_ref[1:2, :] = jnp.broadcast_to(alloc_new, (1, TM))
        meta_ref[2:8, :] = jnp.zeros((6, TM), jnp.int32)

        payload = jnp.concatenate([w_lo, w_hi, jnp.zeros((LANES - 2, TM), jnp.float32)], axis=0)
        row_s[buf, :, D_MODEL:] = payload.T
        pos_v[tile % 2] = jnp.broadcast_to(pos, (8, TM))

    @pl.when((step >= ROW_BUFS) & (step < n_tiles))
    def _():
        drain(step % ROW_BUFS)
        route(step - 1)
        project(step)
        pos_copy((step - 1) % 2).start()

    @pl.when(((step >= 1) & (step < ROW_BUFS)) | (step == n_tiles))
    def _():
        route(step - 1)
        pos_copy((step - 1) % 2).start()

    @pl.when(step < ROW_BUFS)
    def _():
        project(step)

    for buf in range(ROW_BUFS):
        @pl.when((step >= 2) & ((step - 2) % ROW_BUFS == buf))
        def _():
            copy_out(buf)

    @pl.when(step == n_tiles + 1)
    def _():
        for buf in range(ROW_BUFS):
            drain(buf)

        b_iota = lax.broadcasted_iota(jnp.int32, (BUCKET_ROWS, TM), 0)
        lane_id = lax.broadcasted_iota(jnp.int32, (BUCKET_ROWS, TM), 1)
        on_lanes = lambda col: jnp.sum(jnp.where(b_iota == lane_id, col, 0).astype(jnp.float32), axis=0,
                                       keepdims=True).astype(jnp.int32)
        pos_v[0, 0:1, :] = on_lanes(cnt_s[:, 0:1])
        pos_v[0, 1:2, :] = on_lanes(cur_s[:, 0:1])
        pos_v[0, 2:3, :] = jnp.broadcast_to(alloc_s[0:1, 0:1], (1, TM))
        pos_copy(0).start()
        zero_s[...] = jnp.zeros_like(zero_s)
        pos_copy(0).wait()
        sub = SCAN_ROWS
        tail_bits = [1 << k for k in range(TM.bit_length() - 2, sub.bit_length() - 2, -1)]

        def tail_copies(act):
            def zero_rows(row, n_rows):
                return pltpu.make_async_copy(zero_s.at[pl.ds(0, n_rows), :], xs_ref.at[pl.ds(row, n_rows), :], sem_z)

            def per_bucket(b, c):
                cnt, tile = pos_sm[0, 0, b], pos_sm[0, 1, b]
                fill = cnt - (((cnt - 1) >> shift_tm) << shift_tm)
                aligned = ((fill + sub - 1) >> (sub.bit_length() - 1)) << (sub.bit_length() - 1)
                for j in range(sub - 1):
                    @pl.when((cnt > 0) & (fill + j < aligned))
                    def _():
                        act(zero_rows(tile * TM + fill + j, 1))
                tail = TM - aligned
                off = aligned
                for bit in tail_bits:
                    @pl.when((cnt > 0) & ((tail & bit) != 0))
                    def _():
                        act(zero_rows(pl.multiple_of(tile * TM + off, sub), bit))
                    off = off + (tail & bit)
                return c
            lax.fori_loop(0, N_BUCKETS, per_bucket, 0)

        def unused_tiles(act):
            def per_tile(t, c):
                for part in range(TM // zero_s.shape[0]):
                    row = pl.multiple_of(t * TM + part * zero_s.shape[0], sub)
                    act(pltpu.make_async_copy(zero_s, xs_ref.at[pl.ds(row, zero_s.shape[0]), :], sem_z))
                return c
            lax.fori_loop(pos_sm[0, 2, 0], n_tiles + N_BUCKETS, per_tile, 0)

        tail_copies(lambda cp: cp.start())
        unused_tiles(lambda cp: cp.start())
        tail_copies(lambda cp: cp.wait())
        unused_tiles(lambda cp: cp.wait())


def _out_proj(srcs, yna, yls, mod, out_g, w_bf16, ffn_g, w_router, b_router, bsz, with_ctx):
    two_src = len(srcs) == 2
    n_tiles, bj, mod_row = _tile_maps(bsz, with_ctx)
    assert n_tiles >= ROW_BUFS
    n_sorted = n_tiles + N_BUCKETS
    proj = lambda i: bj(jnp.minimum(i, n_tiles - 1))
    tile = lambda width: pl.BlockSpec((None, TM, width), lambda i: (proj(i)[0], proj(i)[1], 0))
    const2 = lambda i: (0, 0)
    if two_src:
        src_specs = [
            pl.BlockSpec((None, TM, D_MODEL), lambda i: (proj(i)[0], 0, 0)),
            pl.BlockSpec((None, TM, D_MODEL), lambda i: (proj(i)[0], jnp.maximum(proj(i)[1] - 1, 0), 0)),
        ]
    else:
        src_specs = [tile(D_MODEL)]
    wr_hi = w_router.T.astype(jnp.bfloat16)
    wr_lo = (w_router.T - wr_hi.astype(jnp.float32)).astype(jnp.bfloat16)
    wr_split = jnp.concatenate([wr_hi, wr_lo], axis=0)
    return pl.pallas_call(
        functools.partial(_out_kernel, two_src=two_src, n_tiles=n_tiles),
        out_shape=[
            jax.ShapeDtypeStruct((bsz, ROWS_PER_B if with_ctx else SEQ, D_MODEL), jnp.float32),
            jax.ShapeDtypeStruct((n_tiles, 1, TM), jnp.int32),
            jax.ShapeDtypeStruct((8, TM), jnp.int32),
            jax.ShapeDtypeStruct((n_sorted * TM, ROW_WIDTH), jnp.float32),
        ],
        grid=(n_tiles + 2,),
        in_specs=src_specs + [
            tile(NA_WIDTH),
            tile(LRU_WIDTH + SC_WIDTH),
            pl.BlockSpec((None, N_MOD, D_MODEL), lambda i: (mod_row(jnp.minimum(i, n_tiles - 1)), 0, 0)),
            pl.BlockSpec((1, D_MODEL), const2),
            pl.BlockSpec((D_MODEL, D_MODEL), const2),
            pl.BlockSpec((1, D_MODEL), const2),
            pl.BlockSpec((2 * N_EXPERTS, D_MODEL), const2),
            pl.BlockSpec((N_EXPERTS, 1), const2),
        ],
        out_specs=[
            pl.BlockSpec((None, TM, D_MODEL), lambda i: (proj(i)[0], proj(i)[1] - (0 if with_ctx else 1), 0)),
            pl.BlockSpec((None, 1, TM), lambda i: (jnp.clip(i - 1, 0, n_tiles - 1), 0, 0)),
            pl.BlockSpec((8, TM), const2),
            pl.BlockSpec(memory_space=pl.ANY),
        ],
        scratch_shapes=[
            pltpu.VMEM((BUCKET_ROWS, LANES), jnp.int32),
            pltpu.VMEM((BUCKET_ROWS, LANES), jnp.int32),
            pltpu.VMEM((8, LANES), jnp.int32),
            pltpu.VMEM((8, TM), jnp.int32),
            pltpu.VMEM((ROW_BUFS, TM, ROW_WIDTH), jnp.float32),
            pltpu.VMEM((TM // 2, ROW_WIDTH), jnp.float32),
            pltpu.VMEM((2, 8, TM), jnp.int32),
            pltpu.SMEM((2, 8, TM), jnp.int32),
            pltpu.SemaphoreType.DMA((2,)),
            pltpu.SemaphoreType.DMA((ROW_BUFS,)),
            pltpu.SemaphoreType.DMA(()),
        ],
        compiler_params=_cparams(("arbitrary",)),
        name="out_proj_route",
    )(*srcs, yna, yls, mod, out_g.reshape(1, D_MODEL), w_bf16, ffn_g.reshape(1, D_MODEL), wr_split,
      b_router.reshape(N_EXPERTS, 1))


def _moe_kernel(src_ref, dst_ref, e0_ref, e1_ref, used_ref, xs_ref, wg0, wu0, wd0, wg1, wu1, wd1, o_ref,
                wg_s, wu_s, wd_s):
    n = pl.program_id(0)

    @pl.when(n >= used_ref[0])
    def _():
        o_ref[...] = jnp.zeros_like(o_ref)
    prev = jnp.maximum(n - 1, 0)
    for k, (e_ref, g, u, d) in enumerate(((e0_ref, wg0, wu0, wd0), (e1_ref, wg1, wu1, wd1))):
        @pl.when((n == 0) | (e_ref[n] != e_ref[prev]))
        def _():
            wg_s[k] = g[...].astype(jnp.bfloat16)
            wu_s[k] = u[...].astype(jnp.bfloat16)
            wd_s[k] = d[...].astype(jnp.bfloat16)

    @pl.when(n < used_ref[0])
    def _():
        xb = xs_ref[:, :D_MODEL].astype(jnp.bfloat16)
        w_lo, w_hi = xs_ref[:, D_MODEL:D_MODEL + 1], xs_ref[:, D_MODEL + 1:D_MODEL + 2]
        slot0_is_lo = e0_ref[n] < e1_ref[n]
        weights = (jnp.where(slot0_is_lo, w_lo, w_hi), jnp.where(slot0_is_lo, w_hi, w_lo))
        out = jnp.zeros((TM, D_MODEL), jnp.float32)
        for k, wk in enumerate(weights):
            gate = jnp.dot(xb, wg_s[k], preferred_element_type=jnp.float32)
            up = jnp.dot(xb, wu_s[k], preferred_element_type=jnp.float32)
            hid = (gate * jax.nn.sigmoid(gate)) * up
            out = out + wk * jnp.dot(hid.astype(jnp.bfloat16), wd_s[k], preferred_element_type=jnp.float32)
        o_ref[...] = out


def _moe(sched, xs, w_gate, w_up, w_down, layer):
    n_tiles = xs.shape[0] // TM
    src, dst, e0, e1, used = sched
    first = lambda n, s, d, a, b, u: (layer, a[n], 0, 0)
    second = lambda n, s, d, a, b, u: (layer, b[n], 0, 0)
    gate_spec = lambda m: pl.BlockSpec((None, None, D_MODEL, D_EXPERT), m)
    down_spec = lambda m: pl.BlockSpec((None, None, D_EXPERT, D_MODEL), m)
    return pl.pallas_call(
        _moe_kernel,
        out_shape=jax.ShapeDtypeStruct((n_tiles * TM, D_MODEL), jnp.float32),
        grid_spec=pltpu.PrefetchScalarGridSpec(
            num_scalar_prefetch=5,
            grid=(n_tiles,),
            in_specs=[
                pl.BlockSpec((TM, ROW_WIDTH), lambda n, s, d, a, b, u: (s[n], 0)),
                gate_spec(first), gate_spec(first), down_spec(first),
                gate_spec(second), gate_spec(second), down_spec(second),
            ],
            out_specs=pl.BlockSpec((TM, D_MODEL), lambda n, s, d, a, b, u: (d[n], 0)),
            scratch_shapes=[
                pltpu.VMEM((2, D_MODEL, D_EXPERT), jnp.bfloat16),
                pltpu.VMEM((2, D_MODEL, D_EXPERT), jnp.bfloat16),
                pltpu.VMEM((2, D_EXPERT, D_MODEL), jnp.bfloat16),
            ],
        ),
        compiler_params=_cparams(("arbitrary",)),
        name="moe_experts",
    )(src, dst, e0, e1, used, xs, w_gate, w_up, w_down, w_gate, w_up, w_down)


def _schedule(meta, n_tiles):
    visit = jnp.array([0, 2, 3, 1, 4, 5], jnp.int32)
    slot0 = jnp.array([0, 2, 3, 2, 3, 3], jnp.int32)
    slot1 = jnp.array([1, 0, 0, 1, 1, 2], jnp.int32)
    tile_bucket = meta[0, :n_tiles]
    used = meta[1, 0]
    ids = jnp.arange(n_tiles, dtype=jnp.int32)
    key = (tile_bucket // N_PAIRS) * N_PAIRS + visit[tile_bucket % N_PAIRS]
    order = jnp.argsort(jnp.where(ids < used, key, N_BUCKETS), stable=True).astype(jnp.int32)
    src = jnp.where(ids < used, order, order[jnp.maximum(used - 1, 0)])
    bucket = tile_bucket[src]
    group, pair = bucket // N_PAIRS, bucket % N_PAIRS
    return src, order, group * E_PER_GROUP + slot0[pair], group * E_PER_GROUP + slot1[pair], used.reshape(1)


def _combine_kernel(pos_ref, x_ref, modc_ref, modb_ref, fg_ref, ys_ref, o_ref, buf, sem, *, final, sub, with_ctx):
    i = pl.program_id(0)
    n = pl.num_programs(0)
    rows = sub * TM

    def row_copy(src_row, slot, k):
        return pltpu.make_async_copy(ys_ref.at[pl.ds(src_row, 1), :], buf.at[slot, pl.ds(k, 1), :], sem.at[slot])

    def issue(step, slot):
        for k in range(rows):
            row_copy(pos_ref[step * rows + k], slot, k).start(priority=k % 2)

    @pl.when(i == 0)
    def _():
        issue(0, 0)

    for nxt in range(2):
        @pl.when((i + 1 < n) & ((i + 1) % 2 == nxt))
        def _():
            issue(i + 1, nxt)

    slot = i % 2

    pltpu.make_async_copy(ys_ref.at[pl.ds(0, rows), :], buf.at[slot], sem.at[slot]).wait()

    first = i % (TILES_PER_B // sub) == 0
    for s in range(sub):
        tile = slice(s * TM, (s + 1) * TM)
        gate = modb_ref[5:6, :]
        if with_ctx and s == 0:
            gate = jnp.where(first, modc_ref[5:6, :], gate)
        x_new = x_ref[tile, :] + gate * buf[slot, tile, :]
        if final:
            x_new = _rms(x_new, fg_ref[...])
        o_ref[tile, :] = x_new


COMBINE_SUB = 3
COMBINE_SUB_LAT = 4


def _combine(pos, x_all, mod, final_g, ys, bsz, with_ctx, final):
    sub = COMBINE_SUB if with_ctx else COMBINE_SUB_LAT
    steps_per_b = (TILES_PER_B if with_ctx else LAT_TILES_PER_B) // sub
    rows = sub * TM
    blk = lambda i, p: (i // steps_per_b, i % steps_per_b, 0)
    return pl.pallas_call(
        functools.partial(_combine_kernel, final=final, sub=sub, with_ctx=with_ctx),
        out_shape=jax.ShapeDtypeStruct(x_all.shape, jnp.float32),
        grid_spec=pltpu.PrefetchScalarGridSpec(
            num_scalar_prefetch=1,
            grid=(bsz * steps_per_b,),
            in_specs=[
                pl.BlockSpec((None, rows, D_MODEL), blk),
                pl.BlockSpec((None, N_MOD, D_MODEL), lambda i, p: (bsz, 0, 0)),
                pl.BlockSpec((None, N_MOD, D_MODEL), lambda i, p: (i // steps_per_b, 0, 0)),
                pl.BlockSpec((1, D_MODEL), lambda i, p: (0, 0)),
                pl.BlockSpec(memory_space=pl.ANY),
            ],
            out_specs=pl.BlockSpec((None, rows, D_MODEL), blk),
            scratch_shapes=[pltpu.VMEM((2, rows, D_MODEL), jnp.float32), pltpu.SemaphoreType.DMA((2,))],
        ),
        compiler_params=_cparams(("arbitrary",)),
        name="combine",
    )(pos.reshape(-1), x_all, mod, mod, final_g.reshape(1, D_MODEL), ys)


def kernel(x, c, ctx, c_ctx, w_ada, b_ada, norm_mix_g, w_in, lru_conv_w, lru_conv_b, rg_w, rg_b, rg_lam, na_rpb,
           sc_conv_w, sc_conv_b, mix_out_g, w_out, norm_ffn_g, w_router, b_router, w_gate, w_up, w_down, final_g):
    bsz = x.shape[0]
    mod_rows = -(-(bsz + 1) // 8) * 8
    c_all = jnp.zeros((mod_rows, D_MODEL), jnp.float32).at[:bsz].set(c).at[bsz].set(c_ctx)
    mods = _ada(c_all, w_ada, b_ada).reshape(DEPTH, mod_rows, N_MOD, D_MODEL)

    x_all = None
    out = None
    for l in range(DEPTH):
        need_ctx = l < DEPTH - 1
        mod = mods[l]
        srcs = (ctx, x) if l == 0 else (x_all,)
        qkv, rest = _in_proj(srcs, norm_mix_g[l], mod, w_in[l].astype(jnp.bfloat16), bsz)
        yna = _attention(qkv, _attn_bias_table(na_rpb[l]), bsz, need_ctx)
        yls = _lru_sconv(rest, lru_conv_w[l], lru_conv_b[l], rg_w[l], rg_b[l], rg_lam[l], sc_conv_w[l],
                         sc_conv_b[l], bsz, need_ctx)
        x_mid, pos, meta, xs = _out_proj(srcs, yna, yls, mod, mix_out_g[l], w_out[l].astype(jnp.bfloat16),
                                         norm_ffn_g[l], w_router, b_router, bsz, need_ctx)
        ys = _moe(_schedule(meta, xs.shape[0] // TM), xs, w_gate, w_up, w_down, l)
        res = _combine(pos, x_mid, mod, final_g, ys, bsz, need_ctx, final=not need_ctx)
        if need_ctx:
            x_all = res
        else:
            out = res
    return out
```
